```python
import jax, jax.numpy as jnp
from jax import lax
import numpy as np


D_MODEL = 1024
BATCH = 4
SEQ = 8192
DEPTH = 2

MEM_TOKENS = 256
N_BRANCH = 4
BRANCH_DIM = 512
CONV_DIM = 512
CONV_WIDTH = 3
DSA_HEADS = 8
DSA_HEAD_DIM = 64
IDX_HEADS = 8
IDX_DIM = 64
TOPK_MAX = 256
Q_BLOCK = 128
HG_HEADS = 4
HG_DK = 128
HG_DV = 128
HG_CHUNK = 64
MEM_HEADS = 4
MEM_HEAD_DIM = 128
ROPE_THETA = 500000.0
ROT_DIM = DSA_HEAD_DIM // 4
FFN_DIM = ((8 * D_MODEL // 3 + 255) // 256) * 256
EPS = 1e-6

SPLIT_SIZES = (CONV_DIM, CONV_DIM, CONV_DIM,
               DSA_HEADS * DSA_HEAD_DIM, DSA_HEAD_DIM, DSA_HEAD_DIM,
               IDX_HEADS * IDX_DIM, IDX_DIM, IDX_HEADS,
               HG_HEADS * HG_DK, HG_HEADS * HG_DK, HG_HEADS * HG_DV, HG_HEADS * HG_DV,
               MEM_HEADS * MEM_HEAD_DIM,
               N_BRANCH * D_MODEL)
IN_COLS = sum(SPLIT_SIZES)

kernel_name = 'hybrid_gated_conv_dsa_hgrn2_mem'


def rmsnorm(x, g):
    xf = x.astype(jnp.float32)
    y = xf * lax.rsqrt(jnp.mean(xf * xf, axis=-1, keepdims=True) + EPS)
    return (y * g.astype(jnp.float32)).astype(x.dtype)


def rope_tables(positions, dtype):
    inv_freq = 1.0 / (ROPE_THETA ** (jnp.arange(0, ROT_DIM, 2, dtype=jnp.float32) / ROT_DIM))
    ang = positions.astype(jnp.float32)[..., None] * inv_freq
    return jnp.cos(ang)[:, :, None, :].astype(dtype), jnp.sin(ang)[:, :, None, :].astype(dtype)


def partial_rope(x, cos, sin):
    half = ROT_DIM // 2
    x1, x2, xp = x[..., :half], x[..., half:ROT_DIM], x[..., ROT_DIM:]
    return jnp.concatenate([x1 * cos - x2 * sin, x1 * sin + x2 * cos, xp], axis=-1)


def split_cols(z):
    idx = [int(i) for i in np.cumsum(SPLIT_SIZES)[:-1]]
    return jnp.split(z, idx, axis=-1)


def causal_conv3(u, w):
    up = jnp.pad(u, ((0, 0), (CONV_WIDTH - 1, 0), (0, 0)))
    s = u.shape[1]
    return up[:, 0:s] * w[0] + up[:, 1:s + 1] * w[1] + up[:, 2:s + 2] * w[2]


def dsa_attention(q, k, v, iq, ik, iw):
    b, s = q.shape[0], q.shape[1]
    nb = s // Q_BLOCK
    topk = min(TOPK_MAX, s // 4)
    key_pos = jnp.arange(s)
    idx_scale = IDX_DIM ** -0.5
    att_scale = DSA_HEAD_DIM ** -0.5

    def blockify(a):
        return a.reshape(b, nb, Q_BLOCK, *a.shape[2:]).swapaxes(0, 1)

    def one_block(args):
        qb, iqb, iwb, qpos = args
        rel = jax.nn.relu(jnp.einsum('bthd,bsd->bths', iqb, ik) * idx_scale)
        score = jnp.einsum('bths,bth->bts', rel, iwb).astype(jnp.float32)
        causal = key_pos[None, :] <= qpos[:, None]
        score = jnp.where(causal[None], score, -jnp.inf)
        _, sel = lax.top_k(score, topk)
        valid = sel <= qpos[None, :, None]
        k_sel = jax.vmap(lambda kb, ib: kb[ib])(k, sel)
        v_sel = jax.vmap(lambda vb, ib: vb[ib])(v, sel)
        logits = jnp.einsum('bthd,btkd->bthk', qb, k_sel).astype(jnp.float32) * att_scale
        logits = jnp.where(valid[:, :, None, :], logits, -jnp.inf)
        p = jax.nn.softmax(logits, axis=-1).astype(v.dtype)
        return jnp.einsum('bthk,btkd->bthd', p, v_sel)

    out = lax.map(one_block, (blockify(q), blockify(iq), blockify(iw),
                              jnp.arange(s).reshape(nb, Q_BLOCK)))
    return out.swapaxes(0, 1).reshape(b, s, q.shape[2], q.shape[3])


def hgrn2_chunkwise(q, k, v, log_f):
    q, k, v, log_f = (a.astype(jnp.float32) for a in (q, k, v, log_f))
    b, s, h, dk = q.shape
    dv = v.shape[-1]
    nc = s // HG_CHUNK

    def to_chunks(a):
        return a.reshape(b, nc, HG_CHUNK, h, a.shape[-1]).transpose(1, 0, 3, 2, 4)

    qc, kc, vc, gc = (to_chunks(a) for a in (q, k, v, log_f))
    cum = jnp.cumsum(gc, axis=3)
    ref = cum[:, :, :, HG_CHUNK // 2 - 1:HG_CHUNK // 2, :]
    last = cum[:, :, :, -1:, :]
    tri = jnp.tril(jnp.ones((HG_CHUNK, HG_CHUNK), dtype=bool))
    a_intra = jnp.einsum('nbhtk,nbhsk->nbhts', qc * jnp.exp(cum - ref), kc * jnp.exp(ref - cum))
    o_intra = jnp.einsum('nbhts,nbhsv->nbhtv', jnp.where(tri, a_intra, 0.0), vc)
    q_inter = qc * jnp.exp(cum)
    k_state = kc * jnp.exp(last - cum)
    decay_last = jnp.exp(last[:, :, :, 0, :])

    def step(state, inp):
        qi, ks, vv, dl = inp
        o = jnp.einsum('bhtk,bhkv->bhtv', qi, state)
        state = dl[..., None] * state + jnp.einsum('bhtk,bhtv->bhkv', ks, vv)
        return state, o

    s0 = jnp.zeros((b, h, dk, dv), jnp.float32)
    _, o_inter = lax.scan(step, s0, (q_inter, k_state, vc, decay_last))
    o = o_intra + o_inter
    return o.transpose(1, 0, 3, 2, 4).reshape(b, s, h, dv)


def hybrid_layer(x, mem_n, cos, sin, lb, norm_mix, w_in, conv_w, dsa_q_norm, dsa_k_norm,
                 hgrn_out_norm, mem_w_kv, mem_q_norm, mem_k_norm, w_lift, w_out,
                 norm_ffn, ffn_w_up, ffn_w_down):
    b, s, _ = x.shape
    h = rmsnorm(x, norm_mix)
    (a_x, a_b, a_c, d_q, d_k, d_v, i_q, i_k, i_w,
     g_q, g_f, g_i, g_g, m_q, gate_logits) = split_cols(h @ w_in)

    y_a = a_b * causal_conv3(a_c * a_x, conv_w)

    q = partial_rope(rmsnorm(d_q.reshape(b, s, DSA_HEADS, DSA_HEAD_DIM), dsa_q_norm), cos, sin)
    k = partial_rope(rmsnorm(d_k.reshape(b, s, 1, DSA_HEAD_DIM), dsa_k_norm), cos, sin)[:, :, 0]
    iq = partial_rope(i_q.reshape(b, s, IDX_HEADS, IDX_DIM), cos, sin)
    ik = partial_rope(i_k.reshape(b, s, 1, IDX_DIM), cos, sin)[:, :, 0]
    iw = i_w * (IDX_HEADS ** -0.5)
    y_b = dsa_attention(q, k, d_v, iq, ik, iw).reshape(b, s, DSA_HEADS * DSA_HEAD_DIM)

    hq = jax.nn.silu(g_q).reshape(b, s, HG_HEADS, HG_DK)
    zf = g_f.astype(jnp.float32).reshape(b, s, HG_HEADS, HG_DK)
    lbh = lb.reshape(HG_HEADS, HG_DK)
    log_f = jnp.logaddexp(jnp.log(lbh), jnp.log1p(-lbh) + jax.nn.log_sigmoid(zf))
    hk = -jnp.expm1(log_f)
    hv = g_i.reshape(b, s, HG_HEADS, HG_DV)
    o_c = hgrn2_chunkwise(hq, hk, hv, log_f).astype(x.dtype)
    o_c = rmsnorm(o_c, hgrn_out_norm) * jax.nn.silu(g_g.reshape(b, s, HG_HEADS, HG_DV))
    y_c = o_c.reshape(b, s, HG_HEADS * HG_DV)

    mk, mv = jnp.split(mem_n @ mem_w_kv, 2, axis=-1)
    mk = rmsnorm(mk.reshape(b, MEM_TOKENS, MEM_HEADS, MEM_HEAD_DIM), mem_k_norm)
    mv = mv.reshape(b, MEM_TOKENS, MEM_HEADS, MEM_HEAD_DIM)
    mq = rmsnorm(m_q.reshape(b, s, MEM_HEADS, MEM_HEAD_DIM), mem_q_norm)
    ml = jnp.einsum('bshd,bmhd->bhsm', mq, mk).astype(jnp.float32) * (MEM_HEAD_DIM ** -0.5)
    mp = jax.nn.softmax(ml, axis=-1).astype(mv.dtype)
    y_m = jnp.einsum('bhsm,bmhd->bshd', mp, mv).reshape(b, s, MEM_HEADS * MEM_HEAD_DIM)

    gates = jax.nn.sigmoid(gate_logits).reshape(b, s, N_BRANCH, D_MODEL)
    branches = (y_a, y_b, y_c, y_m)
    merged = gates[:, :, 0] * (branches[0] @ w_lift[0])
    for n in range(1, N_BRANCH):
        merged = merged + gates[:, :, n] * (branches[n] @ w_lift[n])
    x = x + merged @ w_out

    hf = rmsnorm(x, norm_ffn)
    gate, up = jnp.split(hf @ ffn_w_up, 2, axis=-1)
    return x + (jax.nn.silu(gate) * up) @ ffn_w_down


def setup_inputs(seed: int = 0) -> dict:
    key = jax.random.key(seed)
    ks = jax.random.split(key, 20)
    f32 = jnp.float32

    def nrm(k, shape, scale):
        return jax.random.normal(k, shape, f32) * scale

    def gain(k, shape):
        return 1.0 + 0.02 * jax.random.normal(k, shape, f32)

    return {
        'x': nrm(ks[0], (BATCH, SEQ, D_MODEL), 1.0),
        'mem': nrm(ks[1], (BATCH, MEM_TOKENS, D_MODEL), 1.0),
        'positions': jnp.broadcast_to(jnp.arange(SEQ, dtype=jnp.int32), (BATCH, SEQ)),
        'norm_mix': gain(ks[2], (DEPTH, D_MODEL)),
        'w_in': nrm(ks[3], (DEPTH, D_MODEL, IN_COLS), D_MODEL ** -0.5),
        'conv_w': nrm(ks[4], (DEPTH, CONV_WIDTH, CONV_DIM), CONV_WIDTH ** -0.5),
        'dsa_q_norm': gain(ks[5], (DEPTH, DSA_HEAD_DIM)),
        'dsa_k_norm': gain(ks[6], (DEPTH, DSA_HEAD_DIM)),
        'hgrn_lower_bounds': nrm(ks[7], (DEPTH, HG_HEADS * HG_DK), 0.5),
        'hgrn_out_norm': gain(ks[8], (DEPTH, HG_DV)),
        'mem_norm': gain(ks[9], (D_MODEL,)),
        'mem_w_kv': nrm(ks[10], (DEPTH, D_MODEL, 2 * MEM_HEADS * MEM_HEAD_DIM), D_MODEL ** -0.5),
        'mem_q_norm': gain(ks[11], (DEPTH, MEM_HEAD_DIM)),
        'mem_k_norm': gain(ks[12], (DEPTH, MEM_HEAD_DIM)),
        'w_lift': nrm(ks[13], (DEPTH, N_BRANCH, BRANCH_DIM, D_MODEL), BRANCH_DIM ** -0.5),
        'w_out': nrm(ks[14], (DEPTH, D_MODEL, D_MODEL), D_MODEL ** -0.5),
        'norm_ffn': gain(ks[15], (DEPTH, D_MODEL)),
        'ffn_w_up': nrm(ks[16], (DEPTH, D_MODEL, 2 * FFN_DIM), D_MODEL ** -0.5),
        'ffn_w_down': nrm(ks[17], (DEPTH, FFN_DIM, D_MODEL), FFN_DIM ** -0.5),
    }


def reference(x, mem, positions, norm_mix, w_in, conv_w, dsa_q_norm, dsa_k_norm,
              hgrn_lower_bounds, hgrn_out_norm, mem_norm, mem_w_kv, mem_q_norm, mem_k_norm,
              w_lift, w_out, norm_ffn, ffn_w_up, ffn_w_down):
    cos, sin = rope_tables(positions, x.dtype)
    mem_n = rmsnorm(mem, mem_norm)
    lbs = jnp.cumsum(jax.nn.softmax(hgrn_lower_bounds.astype(jnp.float32), axis=0), axis=0)
    lbs = lbs - lbs[0:1]
    for l in range(DEPTH):
        x = hybrid_layer(x, mem_n, cos, sin, lbs[l], norm_mix[l], w_in[l], conv_w[l],
                         dsa_q_norm[l], dsa_k_norm[l], hgrn_out_norm[l], mem_w_kv[l],
                         mem_q_norm[l], mem_k_norm[l], w_lift[l], w_out[l],
                         norm_ffn[l], ffn_w_up[l], ffn_w_down[l])
    return x
```

```python
import functools

import jax
import jax.numpy as jnp
import numpy as np
from jax import lax
from jax.experimental import pallas as pl
from jax.experimental.pallas import tpu as pltpu

D_MODEL = 1024
MEM_TOKENS = 256
N_BRANCH = 4
BRANCH_DIM = 512
CONV_DIM = 512
CONV_WIDTH = 3
DSA_HEADS = 8
DSA_HEAD_DIM = 64
IDX_HEADS = 8
IDX_DIM = 64
TOPK_MAX = 256
HG_HEADS = 4
HG_DK = 128
HG_DV = 128
HG_CHUNK = 64
MEM_HEADS = 4
MEM_HEAD_DIM = 128
ROPE_THETA = 500000.0
ROT_DIM = DSA_HEAD_DIM // 4
ROT_HALF = ROT_DIM // 2
FFN_DIM = ((8 * D_MODEL // 3 + 255) // 256) * 256
EPS = 1e-6

SPLIT_SIZES = (CONV_DIM, CONV_DIM, CONV_DIM,
               DSA_HEADS * DSA_HEAD_DIM, DSA_HEAD_DIM, DSA_HEAD_DIM,
               IDX_HEADS * IDX_DIM, IDX_DIM, IDX_HEADS,
               HG_HEADS * HG_DK, HG_HEADS * HG_DK, HG_HEADS * HG_DV, HG_HEADS * HG_DV,
               MEM_HEADS * MEM_HEAD_DIM,
               N_BRANCH * D_MODEL)

LANES = 128
SUBLANES = 8
VMEM_LIMIT_BYTES = 56 * 1024 * 1024

PROJ_TILE = 256
DSA_Q_TILE = 256
DSA_K_TILE = 512
HG_TILE = 512
MERGE_TILE = 512
FFN_TILE = 512
FFN_CHUNK = FFN_DIM // 2

GRP_A = 3 * CONV_DIM
GRP_Q = DSA_HEADS * DSA_HEAD_DIM
GRP_KV = 2 * LANES
GRP_IQ = IDX_HEADS * IDX_DIM
GRP_H = 4 * HG_HEADS * HG_DK
GRP_M = MEM_HEADS * MEM_HEAD_DIM
OFF_A = 0
OFF_Q = OFF_A + GRP_A
OFF_KV = OFF_Q + GRP_Q
OFF_IQ = OFF_KV + GRP_KV
OFF_H = OFF_IQ + GRP_IQ
OFF_M = OFF_H + GRP_H
PROJ_COLS = OFF_M + GRP_M

NEG_BIG = -1e30
INT_MIN = -(2 ** 31)

BF16 = jnp.bfloat16
F32 = jnp.float32


def _mm(a, b):
    return jnp.dot(a, b, preferred_element_type=F32)


def _mm_nt(a, b):
    return lax.dot_general(a, b, (((1,), (1,)), ((), ())), preferred_element_type=F32)


def _mm_tn(a, b):
    return lax.dot_general(a, b, (((0,), (0,)), ((), ())), preferred_element_type=F32)


def _rms_scale(x):
    return lax.rsqrt(jnp.mean(x * x, axis=-1, keepdims=True) + EPS)


def _params(n_grid):
    return pltpu.CompilerParams(dimension_semantics=("arbitrary",) * n_grid,
                                vmem_limit_bytes=VMEM_LIMIT_BYTES)


def _mem_kv_kernel(mem_ref, mem_norm_ref, wkv_ref, knorm_ref, mkt_ref, mv_ref):
    m = mem_ref[0]
    mn = (m * _rms_scale(m) * mem_norm_ref[...]).astype(BF16)
    kv = _mm(mn, wkv_ref[...])
    for h in range(MEM_HEADS):
        kh = kv[:, h * MEM_HEAD_DIM:(h + 1) * MEM_HEAD_DIM]
        kh = kh * _rms_scale(kh) * knorm_ref[...]
        mkt_ref[0, h] = kh.T.astype(BF16)
        off = MEM_HEADS * MEM_HEAD_DIM + h * MEM_HEAD_DIM
        mv_ref[0, h] = kv[:, off:off + MEM_HEAD_DIM].astype(BF16)


def _mem_kv(mem, mem_norm, wkv, knorm):
    b = mem.shape[0]
    return pl.pallas_call(
        _mem_kv_kernel,
        grid=(b,),
        in_specs=[
            pl.BlockSpec((1, MEM_TOKENS, D_MODEL), lambda i: (i, 0, 0)),
            pl.BlockSpec((1, D_MODEL), lambda i: (0, 0)),
            pl.BlockSpec((D_MODEL, 2 * GRP_M), lambda i: (0, 0)),
            pl.BlockSpec((1, MEM_HEAD_DIM), lambda i: (0, 0)),
        ],
        out_specs=[
            pl.BlockSpec((1, MEM_HEADS, MEM_HEAD_DIM, MEM_TOKENS), lambda i: (i, 0, 0, 0)),
            pl.BlockSpec((1, MEM_HEADS, MEM_TOKENS, MEM_HEAD_DIM), lambda i: (i, 0, 0, 0)),
        ],
        out_shape=[
            jax.ShapeDtypeStruct((b, MEM_HEADS, MEM_HEAD_DIM, MEM_TOKENS), BF16),
            jax.ShapeDtypeStruct((b, MEM_HEADS, MEM_TOKENS, MEM_HEAD_DIM), BF16),
        ],
        compiler_params=_params(1),
        name="mem_kv",
    )(mem, mem_norm, wkv, knorm)


def _rope_slab(s, c, sa, sb):
    return s * c + pltpu.roll(s, LANES - ROT_HALF, 1) * sa + pltpu.roll(s, ROT_HALF, 1) * sb


def _proj_kernel(x_ref, nmix_ref, w_ref, rc_ref, rsa_ref, rsb_ref, convw_ref, qg_ref, kg_ref,
                 g64_ref, mkt_ref, mv_ref, mqg_ref,
                 ya_ref, q_ref, iq_ref, kk_ref, vw_ref, zh_ref, ym_ref,
                 carry_ref):
    i = pl.program_id(1)
    t = x_ref.shape[1]
    x = x_ref[0]
    hb = (x * _rms_scale(x) * nmix_ref[...]).astype(BF16)
    rc, rsa, rsb = rc_ref[0], rsa_ref[0], rsb_ref[0]

    za = _mm(hb, w_ref[:, OFF_A:OFF_A + GRP_A])
    a_x, a_b, a_c = za[:, :CONV_DIM], za[:, CONV_DIM:2 * CONV_DIM], za[:, 2 * CONV_DIM:]
    u = a_c * a_x

    @pl.when(i == 0)
    def _():
        carry_ref[...] = jnp.zeros_like(carry_ref)

    carry = carry_ref[...]
    row8 = lax.broadcasted_iota(jnp.int32, (SUBLANES, CONV_DIM), 0)
    r1 = pltpu.roll(u, 1, 0)
    r2 = pltpu.roll(u, 2, 0)
    top1 = jnp.where(row8 < 1, pltpu.roll(carry, 1, 0), r1[:SUBLANES])
    top2 = jnp.where(row8 < 2, pltpu.roll(carry, 2, 0), r2[:SUBLANES])
    u1 = jnp.concatenate([top1, r1[SUBLANES:]], axis=0)
    u2 = jnp.concatenate([top2, r2[SUBLANES:]], axis=0)
    carry_ref[...] = u[t - SUBLANES:]
    cw = convw_ref[...]
    ya_ref[0] = (a_b * (u2 * cw[0:1] + u1 * cw[1:2] + u * cw[2:3])).astype(BF16)

    zq = _mm(hb, w_ref[:, OFF_Q:OFF_Q + GRP_Q])
    msq = _mm((zq * zq).astype(BF16), g64_ref[...])
    qn = zq * lax.rsqrt(msq + EPS) * qg_ref[...]
    for p in range(GRP_Q // LANES):
        sl = slice(p * LANES, (p + 1) * LANES)
        q_ref[0, :, sl] = (_rope_slab(qn[:, sl], rc, rsa, rsb) * (DSA_HEAD_DIM ** -0.5)).astype(BF16)

    zkv = _mm(hb, w_ref[:, OFF_KV:OFF_KV + GRP_KV])
    s0, s1 = zkv[:, :LANES], zkv[:, LANES:]
    lane = lax.broadcasted_iota(jnp.int32, (t, LANES), 1)
    is_k = lane < DSA_HEAD_DIM
    kms = jnp.sum(jnp.where(is_k, s0 * s0, 0.0), axis=-1, keepdims=True) * (1.0 / DSA_HEAD_DIM)
    s0 = s0 * jnp.where(is_k, lax.rsqrt(kms + EPS) * kg_ref[...], 1.0)
    kk_ref[0] = _rope_slab(s0, rc, rsa, rsb)
    w_scale = jnp.where(lane >= DSA_HEAD_DIM, jnp.where(lane < DSA_HEAD_DIM + IDX_HEADS, IDX_HEADS ** -0.5, 1.0), 1.0)
    vw_ref[0] = s1 * w_scale

    ziq = _mm(hb, w_ref[:, OFF_IQ:OFF_IQ + GRP_IQ])
    for p in range(GRP_IQ // LANES):
        sl = slice(p * LANES, (p + 1) * LANES)
        iq_ref[0, :, sl] = (_rope_slab(ziq[:, sl], rc, rsa, rsb) * (IDX_DIM ** -0.5)).astype(BF16)

    zh_ref[0] = _mm(hb, w_ref[:, OFF_H:OFF_H + GRP_H])

    zm = _mm(hb, w_ref[:, OFF_M:OFF_M + GRP_M])
    for h in range(MEM_HEADS):
        sl = slice(h * MEM_HEAD_DIM, (h + 1) * MEM_HEAD_DIM)
        mq = zm[:, sl]
        mq = mq * _rms_scale(mq) * mqg_ref[...] * (MEM_HEAD_DIM ** -0.5)
        lg = _mm(mq.astype(BF16), mkt_ref[0, h])
        pe = jnp.exp(lg - jnp.max(lg, axis=-1, keepdims=True))
        den = jnp.sum(pe, axis=-1, keepdims=True)
        ym_ref[0, :, sl] = (_mm(pe.astype(BF16), mv_ref[0, h]) / den).astype(BF16)


def _proj(x, nmix, w1, rc, rsa, rsb, convw, qg, kg, g64, mkt, mv, mqg):
    b, s, _ = x.shape
    t = min(PROJ_TILE, s)
    tok = lambda width: pl.BlockSpec((1, t, width), lambda bi, i: (bi, i, 0))
    const2 = lambda shape: pl.BlockSpec(shape, lambda bi, i: (0, 0))
    per_b4 = lambda shape: pl.BlockSpec((1,) + shape, lambda bi, i: (bi, 0, 0, 0))
    out_widths = (CONV_DIM, GRP_Q, GRP_IQ, LANES, LANES, GRP_H, GRP_M)
    out_dtypes = (BF16, BF16, BF16, F32, F32, F32, BF16)
    return pl.pallas_call(
        _proj_kernel,
        grid=(b, s // t),
        in_specs=[
            tok(D_MODEL), const2((1, D_MODEL)), const2((D_MODEL, PROJ_COLS)),
            tok(LANES), tok(LANES), tok(LANES),
            const2((SUBLANES, CONV_DIM)), const2((1, GRP_Q)), const2((1, LANES)),
            const2((GRP_Q, GRP_Q)),
            per_b4((MEM_HEADS, MEM_HEAD_DIM, MEM_TOKENS)), per_b4((MEM_HEADS, MEM_TOKENS, MEM_HEAD_DIM)),
            const2((1, MEM_HEAD_DIM)),
        ],
        out_specs=[tok(w) for w in out_widths],
        out_shape=[jax.ShapeDtypeStruct((b, s, w), d) for w, d in zip(out_widths, out_dtypes)],
        scratch_shapes=[pltpu.VMEM((SUBLANES, CONV_DIM), F32)],
        compiler_params=_params(2),
        name="proj",
    )(x, nmix, w1, rc, rsa, rsb, convw, qg, kg, g64, mkt, mv, mqg)


def _dsa_kernel(q_ref, iq_ref, vw_ref, kbd_ref, ikbd_ref, vbd_ref, o_ref,
                key_ref, m_ref, l_ref, acc_ref, *, topk, seq_len):
    i = pl.program_id(1)
    t = q_ref.shape[1]
    tk = key_ref.shape[2]
    n_pairs = DSA_HEADS // 2
    q0 = i * t
    nj = (q0 + t - 1) // tk + 1

    row = lax.broadcasted_iota(jnp.int32, (t, 1), 0) + q0
    n_valid = row + 1
    lane_k = lax.broadcasted_iota(jnp.int32, (t, tk), 1)

    def fold_lanes(w):
        acc = w[:, :LANES]
        for c in range(1, tk // LANES):
            acc = acc + w[:, c * LANES:(c + 1) * LANES]
        return acc

    iw = vw_ref[0][:, DSA_HEAD_DIM:DSA_HEAD_DIM + IDX_HEADS]

    def score_tile(j, carry):
        sc = jnp.zeros((t, tk), F32)
        for p in range(n_pairs):
            r = _mm(iq_ref[0, :, p * LANES:(p + 1) * LANES], ikbd_ref[0, j])
            sc = sc + jnp.maximum(r[:, :tk], 0.0) * iw[:, 2 * p:2 * p + 1]
            sc = sc + jnp.maximum(r[:, tk:], 0.0) * iw[:, 2 * p + 1:2 * p + 2]
        sc = jnp.where(sc == 0.0, 0.0, sc)
        bits = pltpu.bitcast(sc, jnp.int32)
        skey = bits ^ ((bits >> 31) & 0x7FFFFFFF)
        key_ref[j] = jnp.where(lane_k + j * tk <= row, skey, INT_MIN)
        return carry

    lax.fori_loop(0, nj, score_tile, 0)

    def count(ind_fn):
        def body(j, acc):
            return acc + fold_lanes(ind_fn(key_ref[j], j))
        acc = lax.fori_loop(0, nj, body, jnp.zeros((t, LANES), F32))
        return jnp.sum(acc, axis=-1, keepdims=True)

    def bisect(bit, prefix):
        cand = prefix | (jnp.int32(1) << (31 - bit))
        cand_s = cand ^ INT_MIN
        cnt = count(lambda blk, j: jnp.where(blk >= cand_s, 1.0, 0.0))
        return jnp.where(cnt >= topk, cand, prefix)

    prefix = lax.fori_loop(0, 32, bisect, jnp.zeros((t, 1), jnp.int32))
    thr = prefix ^ INT_MIN

    cnt_gt = count(lambda blk, j: jnp.where(blk > thr, 1.0, 0.0))
    cnt_ge = count(lambda blk, j: jnp.where(blk >= thr, 1.0, 0.0))
    need = topk - cnt_gt
    has_tie = jnp.where(n_valid > topk, jnp.where(cnt_ge > topk, 1.0, 0.0), 0.0) > 0.5
    m_ref[0] = jnp.broadcast_to(jnp.where(thr == INT_MIN, -1.0, float(seq_len)), (t, LANES))

    @pl.when(jnp.max(jnp.where(has_tie, 1.0, 0.0)) > 0.0)
    def _():
        n_bits = max(1, int(seq_len - 1).bit_length())

        def bisect_idx(bit, pfx):
            cand = pfx | (jnp.int32(1) << (n_bits - 1 - bit))
            cnt = count(lambda blk, j: jnp.where(blk == thr, jnp.where(lane_k + j * tk < cand, 1.0, 0.0), 0.0))
            return jnp.where(cnt < need, cand, pfx)

        cut = lax.fori_loop(0, n_bits, bisect_idx, jnp.zeros((t, 1), jnp.int32))
        cut_f = jnp.where(has_tie, cut.astype(F32), m_ref[0][:, :1])
        m_ref[0] = jnp.broadcast_to(cut_f, (t, LANES))

    cut = m_ref[0][:, :1].astype(jnp.int32)

    def to_bias(j, carry):
        blk = key_ref[j]
        keep_eq = jnp.where(lane_k + j * tk <= cut, 0.0, NEG_BIG)
        bias = jnp.where(blk > thr, 0.0, jnp.where(blk == thr, keep_eq, NEG_BIG))
        key_ref[j] = pltpu.bitcast(bias, jnp.int32)
        return carry

    lax.fori_loop(0, nj, to_bias, 0)

    m_ref[...] = jnp.full(m_ref.shape, NEG_BIG, F32)
    l_ref[...] = jnp.zeros(l_ref.shape, F32)
    acc_ref[...] = jnp.zeros(acc_ref.shape, F32)
    lane = lax.broadcasted_iota(jnp.int32, (t, LANES), 1)
    first_head = lane < DSA_HEAD_DIM

    def attend(j, carry):
        bias = pltpu.bitcast(key_ref[j], F32)
        for p in range(n_pairs):
            lg = _mm(q_ref[0, :, p * LANES:(p + 1) * LANES], kbd_ref[0, j])
            probs, alphas = [], []
            for e in range(2):
                h = 2 * p + e
                s = lg[:, e * tk:(e + 1) * tk] + bias
                m_old = m_ref[h][:, :1]
                m_new = jnp.maximum(m_old, jnp.max(s, axis=-1, keepdims=True))
                alpha = jnp.exp(m_old - m_new)
                pe = jnp.exp(s - m_new)
                l_ref[h] = jnp.broadcast_to(
                    alpha * l_ref[h][:, :1] + jnp.sum(fold_lanes(pe), axis=-1, keepdims=True), (t, LANES))
                m_ref[h] = jnp.broadcast_to(m_new, (t, LANES))
                probs.append(pe.astype(BF16))
                alphas.append(alpha)
            pv = _mm(jnp.concatenate(probs, axis=-1), vbd_ref[0, j])
            acc_ref[p] = acc_ref[p] * jnp.where(first_head, alphas[0], alphas[1]) + pv
        return carry

    lax.fori_loop(0, nj, attend, 0)

    for p in range(n_pairs):
        den = jnp.where(first_head, l_ref[2 * p][:, :1], l_ref[2 * p + 1][:, :1])
        o_ref[0, :, p * LANES:(p + 1) * LANES] = (acc_ref[p] / den).astype(BF16)


def _dsa(q, iq, vw, kbd, ikbd, vbd, topk):
    b, s, _ = q.shape
    t = min(DSA_Q_TILE, s)
    nj, _, tk2 = kbd.shape[1:]
    tk = tk2 // 2
    tok = lambda width: pl.BlockSpec((1, t, width), lambda bi, i: (bi, i, 0))
    per_b = lambda shape: pl.BlockSpec((1,) + shape, lambda bi, i: (bi, 0, 0, 0))
    return pl.pallas_call(
        functools.partial(_dsa_kernel, topk=topk, seq_len=s),
        grid=(b, s // t),
        in_specs=[tok(GRP_Q), tok(GRP_IQ), tok(LANES),
                  per_b((nj, LANES, tk2)), per_b((nj, LANES, tk2)), per_b((nj, tk2, LANES))],
        out_specs=tok(GRP_Q),
        out_shape=jax.ShapeDtypeStruct((b, s, GRP_Q), BF16),
        scratch_shapes=[
            pltpu.VMEM((nj, t, tk), jnp.int32),
            pltpu.VMEM((DSA_HEADS, t, LANES), F32),
            pltpu.VMEM((DSA_HEADS, t, LANES), F32),
            pltpu.VMEM((DSA_HEADS // 2, t, LANES), F32),
        ],
        compiler_params=_params(2),
        name="dsa",
    )(q, iq, vw, kbd, ikbd, vbd)


def _block_diag_keys(kmat, tk):
    b, s, d = kmat.shape
    kt = kmat.reshape(b, s // tk, tk, d).transpose(0, 1, 3, 2)
    z = jnp.zeros_like(kt)
    return jnp.concatenate([jnp.concatenate([kt, z], axis=-1), jnp.concatenate([z, kt], axis=-1)], axis=2)


def _block_diag_values(vmat, tk):
    b, s, d = vmat.shape
    vt = vmat.reshape(b, s // tk, tk, d)
    z = jnp.zeros_like(vt)
    return jnp.concatenate([jnp.concatenate([vt, z], axis=-1), jnp.concatenate([z, vt], axis=-1)], axis=2)


def _split3_bf16(x):
    hi = x.astype(BF16)
    r1 = x - hi.astype(F32)
    mid = r1.astype(BF16)
    lo = (r1 - mid.astype(F32)).astype(BF16)
    return hi, mid, lo


def _hgrn_kernel(gq_ref, gf_ref, gi_ref, gg_ref, lb_ref, onorm_ref, o_ref, state_ref):
    i = pl.program_id(2)
    t = gq_ref.shape[1]
    c = HG_CHUNK

    @pl.when(i == 0)
    def _():
        state_ref[...] = jnp.zeros_like(state_ref)

    lb = lb_ref[...]
    log_lb, log1m_lb, one_m_lb = lb[0:1], lb[1:2], lb[2:3]
    rr = lax.broadcasted_iota(jnp.int32, (c, c), 0)
    cc = lax.broadcasted_iota(jnp.int32, (c, c), 1)
    tril = rr >= cc
    tril_b = jnp.where(tril, 1.0, 0.0).astype(BF16)

    for n in range(t // c):
        rows = slice(n * c, (n + 1) * c)
        zq, zf, v, zg = gq_ref[0, rows], gf_ref[0, rows], gi_ref[0, rows], gg_ref[0, rows]
        log_sig = jnp.minimum(zf, 0.0) - jnp.log1p(jnp.exp(-jnp.abs(zf)))
        b_ = log1m_lb + log_sig
        log_f = jnp.maximum(log_lb, b_) + jnp.log1p(jnp.exp(-jnp.abs(log_lb - b_)))
        hk = one_m_lb * jax.nn.sigmoid(-zf)
        hq = zq * jax.nn.sigmoid(zq)
        hi, mid, lo = _split3_bf16(log_f)
        cum = _mm(tril_b, hi) + _mm(tril_b, mid) + _mm(tril_b, lo)
        ref = cum[c // 2 - 1:c // 2]
        last = cum[c - 1:c]
        vb = v.astype(BF16)
        a = _mm_nt((hq * jnp.exp(cum - ref)).astype(BF16), (hk * jnp.exp(ref - cum)).astype(BF16))
        o = _mm(jnp.where(tril, a, 0.0).astype(BF16), vb)
        st = state_ref[...]
        o = o + _mm_nt((hq * jnp.exp(cum)).astype(BF16), st.astype(BF16))
        ks = (hk * jnp.exp(last - cum)).astype(BF16)
        state_ref[...] = st * jnp.exp(last) + _mm_tn(vb, ks)
        o = o * _rms_scale(o) * onorm_ref[...]
        o_ref[0, rows] = (o * (zg * jax.nn.sigmoid(zg))).astype(BF16)


def _hgrn(zh, lb_rows, onorm):
    b, s, _ = zh.shape
    t = min(HG_TILE, s)
    gate = lambda g: pl.BlockSpec((1, t, HG_DK), lambda bi, h, i: (bi, i, g * HG_HEADS + h))
    return pl.pallas_call(
        _hgrn_kernel,
        grid=(b, HG_HEADS, s // t),
        in_specs=[gate(0), gate(1), gate(2), gate(3),
                  pl.BlockSpec((SUBLANES, HG_DK), lambda bi, h, i: (0, h)),
                  pl.BlockSpec((1, HG_DV), lambda bi, h, i: (0, 0))],
        out_specs=pl.BlockSpec((1, t, HG_DV), lambda bi, h, i: (bi, i, h)),
        out_shape=jax.ShapeDtypeStruct((b, s, HG_HEADS * HG_DV), BF16),
        scratch_shapes=[pltpu.VMEM((HG_DV, HG_DK), F32)],
        compiler_params=_params(3),
        name="hgrn",
    )(zh, zh, zh, zh, lb_rows, onorm)


def _merge_kernel(x_ref, nmix_ref, ya_ref, yb_ref, yc_ref, ym_ref, wg_ref, wl_ref, wo_ref, o_ref):
    x = x_ref[0]
    hb = (x * _rms_scale(x) * nmix_ref[...]).astype(BF16)
    merged = None
    for n, y_ref in enumerate((ya_ref, yb_ref, yc_ref, ym_ref)):
        gate = jax.nn.sigmoid(_mm(hb, wg_ref[:, n * D_MODEL:(n + 1) * D_MODEL]))
        term = gate * _mm(y_ref[0], wl_ref[n])
        merged = term if merged is None else merged + term
    o_ref[0] = x + _mm(merged.astype(BF16), wo_ref[...])


def _merge(x, nmix, ya, yb, yc, ym, wg, wl, wo):
    b, s, _ = x.shape
    t = min(MERGE_TILE, s)
    tok = lambda width: pl.BlockSpec((1, t, width), lambda bi, i: (bi, i, 0))
    return pl.pallas_call(
        _merge_kernel,
        grid=(b, s // t),
        in_specs=[tok(D_MODEL), pl.BlockSpec((1, D_MODEL), lambda bi, i: (0, 0)),
                  tok(BRANCH_DIM), tok(BRANCH_DIM), tok(BRANCH_DIM), tok(BRANCH_DIM),
                  pl.BlockSpec((D_MODEL, N_BRANCH * D_MODEL), lambda bi, i: (0, 0)),
                  pl.BlockSpec((N_BRANCH, BRANCH_DIM, D_MODEL), lambda bi, i: (0, 0, 0)),
                  pl.BlockSpec((D_MODEL, D_MODEL), lambda bi, i: (0, 0))],
        out_specs=tok(D_MODEL),
        out_shape=jax.ShapeDtypeStruct((b, s, D_MODEL), F32),
        compiler_params=_params(2),
        name="merge",
    )(x, nmix, ya, yb, yc, ym, wg, wl, wo)


def _ffn_kernel(x_ref, nffn_ref, wup_ref, wdn_ref, o_ref):
    x = x_ref[0]
    hb = (x * _rms_scale(x) * nffn_ref[...]).astype(BF16)
    out = x
    for n in range(FFN_DIM // FFN_CHUNK):
        lo = n * FFN_CHUNK
        gate = _mm(hb, wup_ref[:, lo:lo + FFN_CHUNK])
        up = _mm(hb, wup_ref[:, FFN_DIM + lo:FFN_DIM + lo + FFN_CHUNK])
        act = (gate * jax.nn.sigmoid(gate) * up).astype(BF16)
        out = out + _mm(act, wdn_ref[lo:lo + FFN_CHUNK, :])
    o_ref[0] = out


def _ffn(x, nffn, wup, wdn):
    b, s, _ = x.shape
    t = min(FFN_TILE, s)
    tok = pl.BlockSpec((1, t, D_MODEL), lambda bi, i: (bi, i, 0))
    return pl.pallas_call(
        _ffn_kernel,
        grid=(b, s // t),
        in_specs=[tok, pl.BlockSpec((1, D_MODEL), lambda bi, i: (0, 0)),
                  pl.BlockSpec((D_MODEL, 2 * FFN_DIM), lambda bi, i: (0, 0)),
                  pl.BlockSpec((FFN_DIM, D_MODEL), lambda bi, i: (0, 0))],
        out_specs=tok,
        out_shape=jax.ShapeDtypeStruct((b, s, D_MODEL), F32),
        compiler_params=_params(2),
        name="ffn",
    )(x, nffn, wup, wdn)


def _rope_tables(positions):
    inv_freq = 1.0 / (ROPE_THETA ** (jnp.arange(0, ROT_DIM, 2, dtype=F32) / ROT_DIM))
    ang = positions.astype(F32)[..., None] * inv_freq
    cos, sin = jnp.cos(ang), jnp.sin(ang)
    rest = DSA_HEAD_DIM - ROT_DIM
    ones = jnp.ones(cos.shape[:-1] + (rest,), F32)
    zeros = jnp.zeros(cos.shape[:-1] + (rest,), F32)
    z8 = jnp.zeros_like(sin)
    c = jnp.concatenate([cos, cos, ones], axis=-1)
    sa = jnp.concatenate([-sin, z8, zeros], axis=-1)
    sb = jnp.concatenate([z8, sin, zeros], axis=-1)
    two = lambda a: jnp.concatenate([a, a], axis=-1)
    return two(c), two(sa), two(sb)


def _relayout_w_in(w_in):
    offs = np.concatenate([[0], np.cumsum(SPLIT_SIZES)])
    col = lambda n: w_in[:, int(offs[n]):int(offs[n + 1])]
    (a_x, a_b, a_c, d_q, d_k, d_v, i_q, i_k, i_w, g_q, g_f, g_i, g_g, m_q, gates) = (col(n) for n in range(15))
    pad = jnp.zeros((w_in.shape[0], LANES - DSA_HEAD_DIM - IDX_HEADS), w_in.dtype)
    w1 = jnp.concatenate([a_x, a_b, a_c, d_q, d_k, i_k, d_v, i_w, pad, i_q, g_q, g_f, g_i, g_g, m_q], axis=1)
    return w1.astype(BF16), gates.astype(BF16)


def kernel(x, mem, positions, norm_mix, w_in, conv_w, dsa_q_norm, dsa_k_norm, hgrn_lower_bounds,
           hgrn_out_norm, mem_norm, mem_w_kv, mem_q_norm, mem_k_norm, w_lift, w_out, norm_ffn,
           ffn_w_up, ffn_w_down):
    b, s, d = x.shape
    depth = w_in.shape[0]
    assert d == D_MODEL and w_in.shape[2] == sum(SPLIT_SIZES)
    assert s % HG_CHUNK == 0 and s % min(DSA_K_TILE, s) == 0
    topk = min(TOPK_MAX, s // 4)
    tk = min(DSA_K_TILE, s)

    rc, rsa, rsb = _rope_tables(positions)
    lbs = jnp.cumsum(jax.nn.softmax(hgrn_lower_bounds.astype(F32), axis=0), axis=0)
    lbs = lbs - lbs[0:1]
    g64 = jnp.asarray(np.kron(np.eye(DSA_HEADS), np.full((DSA_HEAD_DIM, DSA_HEAD_DIM), 1.0 / DSA_HEAD_DIM)), BF16)
    row = lambda v: v.reshape(1, -1).astype(F32)

    for l in range(depth):
        w1, wg = _relayout_w_in(w_in[l])
        lb = lbs[l]
        lb_rows = jnp.concatenate([jnp.stack([jnp.log(lb), jnp.log1p(-lb), 1.0 - lb]),
                                   jnp.zeros((SUBLANES - 3, lb.shape[0]), F32)])
        convw = jnp.concatenate([conv_w[l], jnp.zeros((SUBLANES - CONV_WIDTH, CONV_DIM), F32)])
        mkt, mv = _mem_kv(mem, row(mem_norm), mem_w_kv[l].astype(BF16), row(mem_k_norm[l]))
        ya, q, iq, kk, vw, zh, ym = _proj(
            x, row(norm_mix[l]), w1, rc, rsa, rsb, convw,
            row(jnp.tile(dsa_q_norm[l], DSA_HEADS)),
            row(jnp.concatenate([dsa_k_norm[l], jnp.ones((LANES - DSA_HEAD_DIM,), F32)])),
            g64, mkt, mv, row(mem_q_norm[l]))
        kbd = _block_diag_keys(kk[..., :DSA_HEAD_DIM].astype(BF16), tk)
        ikbd = _block_diag_keys(kk[..., DSA_HEAD_DIM:].astype(BF16), tk)
        vbd = _block_diag_values(vw[..., :DSA_HEAD_DIM].astype(BF16), tk)
        yb = _dsa(q, iq, vw, kbd, ikbd, vbd, topk)
        yc = _hgrn(zh, lb_rows, row(hgrn_out_norm[l]))
        x = _merge(x, row(norm_mix[l]), ya, yb, yc, ym, wg, w_lift[l].astype(BF16), w_out[l].astype(BF16))
        x = _ffn(x, row(norm_ffn[l]), ffn_w_up[l].astype(BF16), ffn_w_down[l].astype(BF16))
    return x
```

```python
import functools

import jax
import jax.numpy as jnp
import numpy as np
from jax import lax
from jax.experimental import pallas as pl
from jax.experimental.pallas import tpu as pltpu

D_MODEL = 1024
MEM_TOKENS = 256
N_BRANCH = 4
BRANCH_DIM = 512
CONV_DIM = 512
CONV_WIDTH = 3
DSA_HEADS = 8
DSA_HEAD_DIM = 64
IDX_HEADS = 8
IDX_DIM = 64
TOPK_MAX = 256
HG_HEADS = 4
HG_DK = 128
HG_DV = 128
HG_CHUNK = 64
MEM_HEADS = 4
MEM_HEAD_DIM = 128
ROPE_THETA = 500000.0
ROT_DIM = DSA_HEAD_DIM // 4
ROT_HALF = ROT_DIM // 2
FFN_DIM = ((8 * D_MODEL // 3 + 255) // 256) * 256
EPS = 1e-6

SPLIT_SIZES = (CONV_DIM, CONV_DIM, CONV_DIM,
               DSA_HEADS * DSA_HEAD_DIM, DSA_HEAD_DIM, DSA_HEAD_DIM,
               IDX_HEADS * IDX_DIM, IDX_DIM, IDX_HEADS,
               HG_HEADS * HG_DK, HG_HEADS * HG_DK, HG_HEADS * HG_DV, HG_HEADS * HG_DV,
               MEM_HEADS * MEM_HEAD_DIM,
               N_BRANCH * D_MODEL)

LANES = 128
SUBLANES = 8
VMEM_LIMIT_BYTES = 56 * 1024 * 1024

PROJ_TILE = 256
DSA_Q_TILE = 256
DSA_K_TILE = 512
HG_TILE = 512
MERGE_TILE = 512
FFN_TILE = 512
FFN_CHUNK = FFN_DIM // 2

GRP_A = 3 * CONV_DIM
GRP_Q = DSA_HEADS * DSA_HEAD_DIM
GRP_KV = 2 * LANES
GRP_IQ = IDX_HEADS * IDX_DIM
GRP_H = 4 * HG_HEADS * HG_DK
GRP_M = MEM_HEADS * MEM_HEAD_DIM
OFF_A = 0
OFF_Q = OFF_A + GRP_A
OFF_KV = OFF_Q + GRP_Q
OFF_IQ = OFF_KV + GRP_KV
OFF_H = OFF_IQ + GRP_IQ
OFF_M = OFF_H + GRP_H
PROJ_COLS = OFF_M + GRP_M

NEG_BIG = -1e30
INT_MIN = -(2 ** 31)
LOG2E = 1.4426950408889634
Q_SCALE = DSA_HEAD_DIM ** -0.5 * LOG2E
SAFE_LOGIT_BOUND = 40.0
SEL_ROWS = 64

BF16 = jnp.bfloat16
F32 = jnp.float32


def _mm(a, b):
    return jnp.dot(a, b, preferred_element_type=F32)


def _mm_nt(a, b):
    return lax.dot_general(a, b, (((1,), (1,)), ((), ())), preferred_element_type=F32)


def _mm_tn(a, b):
    return lax.dot_general(a, b, (((0,), (0,)), ((), ())), preferred_element_type=F32)


def _rms_scale(x):
    return lax.rsqrt(jnp.mean(x * x, axis=-1, keepdims=True) + EPS)


def _params(n_grid):
    return pltpu.CompilerParams(dimension_semantics=("arbitrary",) * n_grid,
                                vmem_limit_bytes=VMEM_LIMIT_BYTES)


def _mem_kv_kernel(mem_ref, mem_norm_ref, wkv_ref, knorm_ref, mkt_ref, mv_ref):
    m = mem_ref[0]
    mn = (m * _rms_scale(m) * mem_norm_ref[...]).astype(BF16)
    kv = _mm(mn, wkv_ref[...])
    for h in range(MEM_HEADS):
        kh = kv[:, h * MEM_HEAD_DIM:(h + 1) * MEM_HEAD_DIM]
        kh = kh * _rms_scale(kh) * knorm_ref[...]
        mkt_ref[0, h] = kh.T.astype(BF16)
        off = MEM_HEADS * MEM_HEAD_DIM + h * MEM_HEAD_DIM
        mv_ref[0, h] = kv[:, off:off + MEM_HEAD_DIM].astype(BF16)


def _mem_kv(mem, mem_norm, wkv, knorm):
    b = mem.shape[0]
    return pl.pallas_call(
        _mem_kv_kernel,
        grid=(b,),
        in_specs=[
            pl.BlockSpec((1, MEM_TOKENS, D_MODEL), lambda i: (i, 0, 0)),
            pl.BlockSpec((1, D_MODEL), lambda i: (0, 0)),
            pl.BlockSpec((D_MODEL, 2 * GRP_M), lambda i: (0, 0)),
            pl.BlockSpec((1, MEM_HEAD_DIM), lambda i: (0, 0)),
        ],
        out_specs=[
            pl.BlockSpec((1, MEM_HEADS, MEM_HEAD_DIM, MEM_TOKENS), lambda i: (i, 0, 0, 0)),
            pl.BlockSpec((1, MEM_HEADS, MEM_TOKENS, MEM_HEAD_DIM), lambda i: (i, 0, 0, 0)),
        ],
        out_shape=[
            jax.ShapeDtypeStruct((b, MEM_HEADS, MEM_HEAD_DIM, MEM_TOKENS), BF16),
            jax.ShapeDtypeStruct((b, MEM_HEADS, MEM_TOKENS, MEM_HEAD_DIM), BF16),
        ],
        compiler_params=_params(1),
        name="mem_kv",
    )(mem, mem_norm, wkv, knorm)


def _rope_slab(s, c, sa, sb):
    return s * c + pltpu.roll(s, LANES - ROT_HALF, 1) * sa + pltpu.roll(s, ROT_HALF, 1) * sb


def _proj_kernel(x_ref, nmix_ref, w_ref, rc_ref, rsa_ref, rsb_ref, convw_ref, qg_ref, kg_ref,
                 g64_ref, mkt_ref, mv_ref, mqg_ref,
                 ya_ref, q_ref, iq_ref, kk_ref, vw_ref, v1_ref, zh_ref, ym_ref,
                 carry_ref):
    i = pl.program_id(1)
    t = x_ref.shape[1]
    x = x_ref[0]
    hb = (x * _rms_scale(x) * nmix_ref[...]).astype(BF16)
    rc, rsa, rsb = rc_ref[0], rsa_ref[0], rsb_ref[0]

    za = _mm(hb, w_ref[:, OFF_A:OFF_A + GRP_A])
    a_x, a_b, a_c = za[:, :CONV_DIM], za[:, CONV_DIM:2 * CONV_DIM], za[:, 2 * CONV_DIM:]
    u = a_c * a_x

    @pl.when(i == 0)
    def _():
        carry_ref[...] = jnp.zeros_like(carry_ref)

    carry = carry_ref[...]
    row8 = lax.broadcasted_iota(jnp.int32, (SUBLANES, CONV_DIM), 0)
    r1 = pltpu.roll(u, 1, 0)
    r2 = pltpu.roll(u, 2, 0)
    top1 = jnp.where(row8 < 1, pltpu.roll(carry, 1, 0), r1[:SUBLANES])
    top2 = jnp.where(row8 < 2, pltpu.roll(carry, 2, 0), r2[:SUBLANES])
    u1 = jnp.concatenate([top1, r1[SUBLANES:]], axis=0)
    u2 = jnp.concatenate([top2, r2[SUBLANES:]], axis=0)
    carry_ref[...] = u[t - SUBLANES:]
    cw = convw_ref[...]
    ya_ref[0] = (a_b * (u2 * cw[0:1] + u1 * cw[1:2] + u * cw[2:3])).astype(BF16)

    zq = _mm(hb, w_ref[:, OFF_Q:OFF_Q + GRP_Q])
    msq = _mm((zq * zq).astype(BF16), g64_ref[...])
    qn = zq * lax.rsqrt(msq + EPS) * qg_ref[...]
    for p in range(GRP_Q // LANES):
        sl = slice(p * LANES, (p + 1) * LANES)
        q_ref[0, :, sl] = (_rope_slab(qn[:, sl], rc, rsa, rsb) * Q_SCALE).astype(BF16)

    zkv = _mm(hb, w_ref[:, OFF_KV:OFF_KV + GRP_KV])
    s0, s1 = zkv[:, :LANES], zkv[:, LANES:]
    lane = lax.broadcasted_iota(jnp.int32, (t, LANES), 1)
    is_k = lane < DSA_HEAD_DIM
    kms = jnp.sum(jnp.where(is_k, s0 * s0, 0.0), axis=-1, keepdims=True) * (1.0 / DSA_HEAD_DIM)
    s0 = s0 * jnp.where(is_k, lax.rsqrt(kms + EPS) * kg_ref[...], 1.0)
    kk_ref[0] = _rope_slab(s0, rc, rsa, rsb)
    w_scale = jnp.where(lane >= DSA_HEAD_DIM, jnp.where(lane < DSA_HEAD_DIM + IDX_HEADS, IDX_HEADS ** -0.5, 1.0), 1.0)
    vw_ref[0] = s1 * w_scale
    v1_ref[0] = jnp.where(is_k, s1, jnp.where(lane == DSA_HEAD_DIM, 1.0, 0.0)).astype(BF16)

    ziq = _mm(hb, w_ref[:, OFF_IQ:OFF_IQ + GRP_IQ])
    for p in range(GRP_IQ // LANES):
        sl = slice(p * LANES, (p + 1) * LANES)
        iq_ref[0, :, sl] = (_rope_slab(ziq[:, sl], rc, rsa, rsb) * (IDX_DIM ** -0.5)).astype(BF16)

    zh_ref[0] = _mm(hb, w_ref[:, OFF_H:OFF_H + GRP_H])

    zm = _mm(hb, w_ref[:, OFF_M:OFF_M + GRP_M])
    for h in range(MEM_HEADS):
        sl = slice(h * MEM_HEAD_DIM, (h + 1) * MEM_HEAD_DIM)
        mq = zm[:, sl]
        mq = mq * _rms_scale(mq) * mqg_ref[...] * (MEM_HEAD_DIM ** -0.5)
        lg = _mm(mq.astype(BF16), mkt_ref[0, h])
        pe = jnp.exp(lg - jnp.max(lg, axis=-1, keepdims=True))
        den = jnp.sum(pe, axis=-1, keepdims=True)
        ym_ref[0, :, sl] = (_mm(pe.astype(BF16), mv_ref[0, h]) / den).astype(BF16)


def _proj(x, nmix, w1, rc, rsa, rsb, convw, qg, kg, g64, mkt, mv, mqg):
    b, s, _ = x.shape
    t = min(PROJ_TILE, s)
    tok = lambda width: pl.BlockSpec((1, t, width), lambda bi, i: (bi, i, 0))
    const2 = lambda shape: pl.BlockSpec(shape, lambda bi, i: (0, 0))
    per_b4 = lambda shape: pl.BlockSpec((1,) + shape, lambda bi, i: (bi, 0, 0, 0))
    out_widths = (CONV_DIM, GRP_Q, GRP_IQ, LANES, LANES, LANES, GRP_H, GRP_M)
    out_dtypes = (BF16, BF16, BF16, F32, F32, BF16, F32, BF16)
    return pl.pallas_call(
        _proj_kernel,
        grid=(b, s // t),
        in_specs=[
            tok(D_MODEL), const2((1, D_MODEL)), const2((D_MODEL, PROJ_COLS)),
            tok(LANES), tok(LANES), tok(LANES),
            const2((SUBLANES, CONV_DIM)), const2((1, GRP_Q)), const2((1, LANES)),
            const2((GRP_Q, GRP_Q)),
            per_b4((MEM_HEADS, MEM_HEAD_DIM, MEM_TOKENS)), per_b4((MEM_HEADS, MEM_TOKENS, MEM_HEAD_DIM)),
            const2((1, MEM_HEAD_DIM)),
        ],
        out_specs=[tok(w) for w in out_widths],
        out_shape=[jax.ShapeDtypeStruct((b, s, w), d) for w, d in zip(out_widths, out_dtypes)],
        scratch_shapes=[pltpu.VMEM((SUBLANES, CONV_DIM), F32)],
        compiler_params=_params(2),
        name="proj",
    )(x, nmix, w1, rc, rsa, rsb, convw, qg, kg, g64, mkt, mv, mqg)


def _dsa_kernel(fast_ref, q_ref, iq_ref, vw_ref, kbd_ref, ikbd_ref, v1_ref, o_ref,
                key_ref, cut_ref, m_ref, acc_ref, *, topk, seq_len):
    i = pl.program_id(1)
    t = q_ref.shape[1]
    tk = key_ref.shape[2]
    rb = min(SEL_ROWS, t)
    n_pairs = DSA_HEADS // 2
    q0 = i * t
    nj = (q0 + t - 1) // tk + 1

    row = lax.broadcasted_iota(jnp.int32, (t, 1), 0) + q0
    lane_k = lax.broadcasted_iota(jnp.int32, (t, tk), 1)

    def fold_lanes(w):
        acc = w[:, :LANES]
        for c in range(1, tk // LANES):
            acc = acc + w[:, c * LANES:(c + 1) * LANES]
        return acc

    iw = vw_ref[0][:, DSA_HEAD_DIM:DSA_HEAD_DIM + IDX_HEADS]

    def score_tile(j, carry):
        sc = jnp.zeros((t, tk), F32)
        for p in range(n_pairs):
            r = _mm(iq_ref[0, :, p * LANES:(p + 1) * LANES], ikbd_ref[0, j])
            sc = sc + jnp.maximum(r[:, :tk], 0.0) * iw[:, 2 * p:2 * p + 1]
            sc = sc + jnp.maximum(r[:, tk:], 0.0) * iw[:, 2 * p + 1:2 * p + 2]
        sc = jnp.where(sc == 0.0, 0.0, sc)
        bits = pltpu.bitcast(sc, jnp.int32)
        skey = bits ^ ((bits >> 31) & 0x7FFFFFFF)
        key_ref[j] = jnp.where(lane_k + j * tk <= row, skey, INT_MIN)
        return carry

    lax.fori_loop(0, nj, score_tile, 0)

    n_valid = row + 1
    lane_rb = lax.broadcasted_iota(jnp.int32, (rb, tk), 1)

    def count(ind_fn, *row_args):
        parts = []
        for r0 in range(0, t, rb):
            args = [a[r0:r0 + rb] for a in row_args]

            def body(j, acc, r0=r0, args=args):
                return acc + fold_lanes(ind_fn(key_ref[j, r0:r0 + rb, :], j, *args))

            parts.append(lax.fori_loop(0, nj, body, jnp.zeros((rb, LANES), F32)))
        return jnp.sum(jnp.concatenate(parts, axis=0), axis=-1, keepdims=True)

    def bisect(bit, state):
        prefix, cnt_ge = state
        cand = prefix | (jnp.int32(1) << (31 - bit))
        cnt = count(lambda blk, j, c: jnp.where(blk >= c, 1.0, 0.0), cand ^ INT_MIN)
        ok = cnt >= topk
        return jnp.where(ok, cand, prefix), jnp.where(ok, cnt, cnt_ge)

    prefix, cnt_ge = lax.fori_loop(
        0, 32, bisect, (jnp.zeros((t, 1), jnp.int32), jnp.full((t, 1), float(seq_len), F32)))
    thr = prefix ^ INT_MIN

    has_tie = jnp.where(n_valid > topk, jnp.where(cnt_ge > topk, 1.0, 0.0), 0.0)
    cut_ref[...] = jnp.broadcast_to(jnp.where(thr == INT_MIN, -1, seq_len), (t, LANES))

    @pl.when(jnp.max(has_tie) > 0.0)
    def _():
        n_bits = max(1, int(seq_len - 1).bit_length())
        need = topk - count(lambda blk, j, th: jnp.where(blk > th, 1.0, 0.0), thr)

        def bisect_idx(bit, pfx):
            cand = pfx | (jnp.int32(1) << (n_bits - 1 - bit))
            cnt = count(lambda blk, j, th, c: jnp.where(
                blk == th, jnp.where(lane_rb + j * tk < c, 1.0, 0.0), 0.0), thr, cand)
            return jnp.where(cnt < need, cand, pfx)

        cut_t = lax.fori_loop(0, n_bits, bisect_idx, jnp.zeros((t, 1), jnp.int32))
        cut_ref[...] = jnp.broadcast_to(jnp.where(has_tie > 0.0, cut_t, cut_ref[:, :1]), (t, LANES))

    cut = cut_ref[:, :1]

    def to_bias(j, carry):
        blk = key_ref[j]
        keep_eq = jnp.where(lane_k + j * tk <= cut, 0.0, NEG_BIG)
        bias = jnp.where(blk > thr, 0.0, jnp.where(blk == thr, keep_eq, NEG_BIG))
        key_ref[j] = pltpu.bitcast(bias, jnp.int32)
        return carry

    lax.fori_loop(0, nj, to_bias, 0)

    acc_ref[...] = jnp.zeros(acc_ref.shape, F32)
    use_fast = fast_ref[0] > 0

    @pl.when(use_fast)
    def _():
        def attend(j, carry):
            bias = pltpu.bitcast(key_ref[j], F32)
            for p in range(n_pairs):
                lg = _mm(q_ref[0, :, p * LANES:(p + 1) * LANES], kbd_ref[0, j])
                for e in range(2):
                    pe = jnp.exp2(lg[:, e * tk:(e + 1) * tk] + bias).astype(BF16)
                    acc_ref[2 * p + e] += _mm(pe, v1_ref[0, j])
            return carry

        lax.fori_loop(0, nj, attend, 0)

    @pl.when(jnp.logical_not(use_fast))
    def _():
        m_ref[...] = jnp.full(m_ref.shape, NEG_BIG, F32)

        def attend(j, carry):
            bias = pltpu.bitcast(key_ref[j], F32)
            for p in range(n_pairs):
                lg = _mm(q_ref[0, :, p * LANES:(p + 1) * LANES], kbd_ref[0, j])
                for e in range(2):
                    h = 2 * p + e
                    s = lg[:, e * tk:(e + 1) * tk] + bias
                    m_old = m_ref[h][:, :1]
                    m_new = jnp.maximum(m_old, jnp.max(s, axis=-1, keepdims=True))
                    pe = jnp.exp2(s - m_new).astype(BF16)
                    acc_ref[h] = acc_ref[h] * jnp.exp2(m_old - m_new) + _mm(pe, v1_ref[0, j])
                    m_ref[h] = jnp.broadcast_to(m_new, (t, LANES))
            return carry

        lax.fori_loop(0, nj, attend, 0)

    first_head = lax.broadcasted_iota(jnp.int32, (t, LANES), 1) < DSA_HEAD_DIM
    for p in range(n_pairs):
        a0, a1 = acc_ref[2 * p], acc_ref[2 * p + 1]
        o0 = a0 / a0[:, DSA_HEAD_DIM:DSA_HEAD_DIM + 1]
        o1 = a1 / a1[:, DSA_HEAD_DIM:DSA_HEAD_DIM + 1]
        o_ref[0, :, p * LANES:(p + 1) * LANES] = jnp.where(
            first_head, o0, pltpu.roll(o1, DSA_HEAD_DIM, 1)).astype(BF16)


def _dsa(fast, q, iq, vw, kbd, ikbd, v1, topk):
    b, s, _ = q.shape
    t = min(DSA_Q_TILE, s)
    nj, _, tk2 = kbd.shape[1:]
    tk = tk2 // 2
    tok = lambda width: pl.BlockSpec((1, t, width), lambda bi, i: (bi, i, 0))
    per_b = lambda shape: pl.BlockSpec((1,) + shape, lambda bi, i: (bi, 0, 0, 0))
    return pl.pallas_call(
        functools.partial(_dsa_kernel, topk=topk, seq_len=s),
        grid=(b, s // t),
        in_specs=[pl.BlockSpec(memory_space=pltpu.SMEM),
                  tok(GRP_Q), tok(GRP_IQ), tok(LANES),
                  per_b((nj, LANES, tk2)), per_b((nj, LANES, tk2)), per_b((nj, tk, LANES))],
        out_specs=tok(GRP_Q),
        out_shape=jax.ShapeDtypeStruct((b, s, GRP_Q), BF16),
        scratch_shapes=[
            pltpu.VMEM((nj, t, tk), jnp.int32),
            pltpu.VMEM((t, LANES), jnp.int32),
            pltpu.VMEM((DSA_HEADS, t, LANES), F32),
            pltpu.VMEM((DSA_HEADS, t, LANES), F32),
        ],
        compiler_params=_params(2),
        name="dsa",
    )(fast, q, iq, vw, kbd, ikbd, v1.reshape(b, nj, tk, LANES))


def _block_diag_keys(kmat, tk):
    b, s, d = kmat.shape
    kt = kmat.reshape(b, s // tk, tk, d).transpose(0, 1, 3, 2)
    z = jnp.zeros_like(kt)
    return jnp.concatenate([jnp.concatenate([kt, z], axis=-1), jnp.concatenate([z, kt], axis=-1)], axis=2)


def _split3_bf16(x):
    hi = x.astype(BF16)
    r1 = x - hi.astype(F32)
    mid = r1.astype(BF16)
    lo = (r1 - mid.astype(F32)).astype(BF16)
    return hi, mid, lo


def _hgrn_kernel(gq_ref, gf_ref, gi_ref, gg_ref, lb_ref, onorm_ref, o_ref, state_ref):
    i = pl.program_id(2)
    t = gq_ref.shape[1]
    c = HG_CHUNK

    @pl.when(i == 0)
    def _():
        state_ref[...] = jnp.zeros_like(state_ref)

    lb = lb_ref[...]
    log_lb, log1m_lb, one_m_lb = lb[0:1], lb[1:2], lb[2:3]
    rr = lax.broadcasted_iota(jnp.int32, (c, c), 0)
    cc = lax.broadcasted_iota(jnp.int32, (c, c), 1)
    tril = rr >= cc
    tril_b = jnp.where(tril, 1.0, 0.0).astype(BF16)

    for n in range(t // c):
        rows = slice(n * c, (n + 1) * c)
        zq, zf, v, zg = gq_ref[0, rows], gf_ref[0, rows], gi_ref[0, rows], gg_ref[0, rows]
        log_sig = jnp.minimum(zf, 0.0) - jnp.log1p(jnp.exp(-jnp.abs(zf)))
        b_ = log1m_lb + log_sig
        log_f = jnp.maximum(log_lb, b_) + jnp.log1p(jnp.exp(-jnp.abs(log_lb - b_)))
        hk = one_m_lb * jax.nn.sigmoid(-zf)
        hq = zq * jax.nn.sigmoid(zq)
        hi, mid, lo = _split3_bf16(log_f)
        cum = _mm(tril_b, hi) + _mm(tril_b, mid) + _mm(tril_b, lo)
        ref = cum[c // 2 - 1:c // 2]
        last = cum[c - 1:c]
        vb = v.astype(BF16)
        a = _mm_nt((hq * jnp.exp(cum - ref)).astype(BF16), (hk * jnp.exp(ref - cum)).astype(BF16))
        o = _mm(jnp.where(tril, a, 0.0).astype(BF16), vb)
        st = state_ref[...]
        o = o + _mm_nt((hq * jnp.exp(cum)).astype(BF16), st.astype(BF16))
        ks = (hk * jnp.exp(last - cum)).astype(BF16)
        state_ref[...] = st * jnp.exp(last) + _mm_tn(vb, ks)
        o = o * _rms_scale(o) * onorm_ref[...]
        o_ref[0, rows] = (o * (zg * jax.nn.sigmoid(zg))).astype(BF16)


def _hgrn(zh, lb_rows, onorm):
    b, s, _ = zh.shape
    t = min(HG_TILE, s)
    gate = lambda g: pl.BlockSpec((1, t, HG_DK), lambda bi, h, i: (bi, i, g * HG_HEADS + h))
    return pl.pallas_call(
        _hgrn_kernel,
        grid=(b, HG_HEADS, s // t),
        in_specs=[gate(0), gate(1), gate(2), gate(3),
                  pl.BlockSpec((SUBLANES, HG_DK), lambda bi, h, i: (0, h)),
                  pl.BlockSpec((1, HG_DV), lambda bi, h, i: (0, 0))],
        out_specs=pl.BlockSpec((1, t, HG_DV), lambda bi, h, i: (bi, i, h)),
        out_shape=jax.ShapeDtypeStruct((b, s, HG_HEADS * HG_DV), BF16),
        scratch_shapes=[pltpu.VMEM((HG_DV, HG_DK), F32)],
        compiler_params=_params(3),
        name="hgrn",
    )(zh, zh, zh, zh, lb_rows, onorm)


def _merge_kernel(x_ref, nmix_ref, ya_ref, yb_ref, yc_ref, ym_ref, wg_ref, wl_ref, wo_ref, o_ref):
    x = x_ref[0]
    hb = (x * _rms_scale(x) * nmix_ref[...]).astype(BF16)
    merged = None
    for n, y_ref in enumerate((ya_ref, yb_ref, yc_ref, ym_ref)):
        gate = jax.nn.sigmoid(_mm(hb, wg_ref[:, n * D_MODEL:(n + 1) * D_MODEL]))
        term = gate * _mm(y_ref[0], wl_ref[n])
        merged = term if merged is None else merged + term
    o_ref[0] = x + _mm(merged.astype(BF16), wo_ref[...])


def _merge(x, nmix, ya, yb, yc, ym, wg, wl, wo):
    b, s, _ = x.shape
    t = min(MERGE_TILE, s)
    tok = lambda width: pl.BlockSpec((1, t, width), lambda bi, i: (bi, i, 0))
    return pl.pallas_call(
        _merge_kernel,
        grid=(b, s // t),
        in_specs=[tok(D_MODEL), pl.BlockSpec((1, D_MODEL), lambda bi, i: (0, 0)),
                  tok(BRANCH_DIM), tok(BRANCH_DIM), tok(BRANCH_DIM), tok(BRANCH_DIM),
                  pl.BlockSpec((D_MODEL, N_BRANCH * D_MODEL), lambda bi, i: (0, 0)),
                  pl.BlockSpec((N_BRANCH, BRANCH_DIM, D_MODEL), lambda bi, i: (0, 0, 0)),
                  pl.BlockSpec((D_MODEL, D_MODEL), lambda bi, i: (0, 0))],
        out_specs=tok(D_MODEL),
        out_shape=jax.ShapeDtypeStruct((b, s, D_MODEL), F32),
        compiler_params=_params(2),
        name="merge",
    )(x, nmix, ya, yb, yc, ym, wg, wl, wo)


def _ffn_kernel(x_ref, nffn_ref, wup_ref, wdn_ref, o_ref):
    x = x_ref[0]
    hb = (x * _rms_scale(x) * nffn_ref[...]).astype(BF16)
    out = x
    for n in range(FFN_DIM // FFN_CHUNK):
        lo = n * FFN_CHUNK
        gate = _mm(hb, wup_ref[:, lo:lo + FFN_CHUNK])
        up = _mm(hb, wup_ref[:, FFN_DIM + lo:FFN_DIM + lo + FFN_CHUNK])
        act = (gate * jax.nn.sigmoid(gate) * up).astype(BF16)
        out = out + _mm(act, wdn_ref[lo:lo + FFN_CHUNK, :])
    o_ref[0] = out


def _ffn(x, nffn, wup, wdn):
    b, s, _ = x.shape
    t = min(FFN_TILE, s)
    tok = pl.BlockSpec((1, t, D_MODEL), lambda bi, i: (bi, i, 0))
    return pl.pallas_call(
        _ffn_kernel,
        grid=(b, s // t),
        in_specs=[tok, pl.BlockSpec((1, D_MODEL), lambda bi, i: (0, 0)),
                  pl.BlockSpec((D_MODEL, 2 * FFN_DIM), lambda bi, i: (0, 0)),
                  pl.BlockSpec((FFN_DIM, D_MODEL), lambda bi, i: (0, 0))],
        out_specs=tok,
        out_shape=jax.ShapeDtypeStruct((b, s, D_MODEL), F32),
        compiler_params=_params(2),
        name="ffn",
    )(x, nffn, wup, wdn)


def _rope_tables(positions):
    inv_freq = 1.0 / (ROPE_THETA ** (jnp.arange(0, ROT_DIM, 2, dtype=F32) / ROT_DIM))
    ang = positions.astype(F32)[..., None] * inv_freq
    cos, sin = jnp.cos(ang), jnp.sin(ang)
    rest = DSA_HEAD_DIM - ROT_DIM
    ones = jnp.ones(cos.shape[:-1] + (rest,), F32)
    zeros = jnp.zeros(cos.shape[:-1] + (rest,), F32)
    z8 = jnp.zeros_like(sin)
    c = jnp.concatenate([cos, cos, ones], axis=-1)
    sa = jnp.concatenate([-sin, z8, zeros], axis=-1)
    sb = jnp.concatenate([z8, sin, zeros], axis=-1)
    two = lambda a: jnp.concatenate([a, a], axis=-1)
    return two(c), two(sa), two(sb)


def _relayout_w_in(w_in):
    offs = np.concatenate([[0], np.cumsum(SPLIT_SIZES)])
    col = lambda n: w_in[:, int(offs[n]):int(offs[n + 1])]
    (a_x, a_b, a_c, d_q, d_k, d_v, i_q, i_k, i_w, g_q, g_f, g_i, g_g, m_q, gates) = (col(n) for n in range(15))
    pad = jnp.zeros((w_in.shape[0], LANES - DSA_HEAD_DIM - IDX_HEADS), w_in.dtype)
    w1 = jnp.concatenate([a_x, a_b, a_c, d_q, d_k, i_k, d_v, i_w, pad, i_q, g_q, g_f, g_i, g_g, m_q], axis=1)
    return w1.astype(BF16), gates.astype(BF16)


def kernel(x, mem, positions, norm_mix, w_in, conv_w, dsa_q_norm, dsa_k_norm, hgrn_lower_bounds,
           hgrn_out_norm, mem_norm, mem_w_kv, mem_q_norm, mem_k_norm, w_lift, w_out, norm_ffn,
           ffn_w_up, ffn_w_down):
    b, s, d = x.shape
    depth = w_in.shape[0]
    assert d == D_MODEL and w_in.shape[2] == sum(SPLIT_SIZES)
    assert s % HG_CHUNK == 0 and s % min(DSA_K_TILE, s) == 0
    topk = min(TOPK_MAX, s // 4)
    tk = min(DSA_K_TILE, s)

    rc, rsa, rsb = _rope_tables(positions)
    lbs = jnp.cumsum(jax.nn.softmax(hgrn_lower_bounds.astype(F32), axis=0), axis=0)
    lbs = lbs - lbs[0:1]
    g64 = jnp.asarray(np.kron(np.eye(DSA_HEADS), np.full((DSA_HEAD_DIM, DSA_HEAD_DIM), 1.0 / DSA_HEAD_DIM)), BF16)
    row = lambda v: v.reshape(1, -1).astype(F32)

    for l in range(depth):
        w1, wg = _relayout_w_in(w_in[l])
        lb = lbs[l]
        lb_rows = jnp.concatenate([jnp.stack([jnp.log(lb), jnp.log1p(-lb), 1.0 - lb]),
                                   jnp.zeros((SUBLANES - 3, lb.shape[0]), F32)])
        convw = jnp.concatenate([conv_w[l], jnp.zeros((SUBLANES - CONV_WIDTH, CONV_DIM), F32)])
        mkt, mv = _mem_kv(mem, row(mem_norm), mem_w_kv[l].astype(BF16), row(mem_k_norm[l]))
        ya, q, iq, kk, vw, v1, zh, ym = _proj(
            x, row(norm_mix[l]), w1, rc, rsa, rsb, convw,
            row(jnp.tile(dsa_q_norm[l], DSA_HEADS)),
            row(jnp.concatenate([dsa_k_norm[l], jnp.ones((LANES - DSA_HEAD_DIM,), F32)])),
            g64, mkt, mv, row(mem_q_norm[l]))
        kbd = _block_diag_keys(kk[..., :DSA_HEAD_DIM].astype(BF16), tk)
        ikbd = _block_diag_keys(kk[..., DSA_HEAD_DIM:].astype(BF16), tk)
        logit_bound = DSA_HEAD_DIM ** 0.5 * jnp.max(jnp.abs(dsa_q_norm[l])) * jnp.max(jnp.abs(dsa_k_norm[l]))
        fast = (logit_bound <= SAFE_LOGIT_BOUND).astype(jnp.int32).reshape(1)
        yb = _dsa(fast, q, iq, vw, kbd, ikbd, v1, topk)
        yc = _hgrn(zh, lb_rows, row(hgrn_out_norm[l]))
        x = _merge(x, row(norm_mix[l]), ya, yb, yc, ym, wg, w_lift[l].astype(BF16), w_out[l].astype(BF16))
        x = _ffn(x, row(norm_ffn[l]), ffn_w_up[l].astype(BF16), ffn_w_down[l].astype(BF16))
    return x
```

```python
import functools

import jax
import jax.numpy as jnp
import numpy as np
from jax import lax
from jax.experimental import pallas as pl
from jax.experimental.pallas import tpu as pltpu

D_MODEL = 1024
MEM_TOKENS = 256
N_BRANCH = 4
BRANCH_DIM = 512
CONV_DIM = 512
CONV_WIDTH = 3
DSA_HEADS = 8
DSA_HEAD_DIM = 64
IDX_HEADS = 8
IDX_DIM = 64
TOPK_MAX = 256
HG_HEADS = 4
HG_DK = 128
HG_DV = 128
HG_CHUNK = 64
MEM_HEADS = 4
MEM_HEAD_DIM = 128
ROPE_THETA = 500000.0
ROT_DIM = DSA_HEAD_DIM // 4
ROT_HALF = ROT_DIM // 2
FFN_DIM = ((8 * D_MODEL // 3 + 255) // 256) * 256
EPS = 1e-6

SPLIT_SIZES = (CONV_DIM, CONV_DIM, CONV_DIM,
               DSA_HEADS * DSA_HEAD_DIM, DSA_HEAD_DIM, DSA_HEAD_DIM,
               IDX_HEADS * IDX_DIM, IDX_DIM, IDX_HEADS,
               HG_HEADS * HG_DK, HG_HEADS * HG_DK, HG_HEADS * HG_DV, HG_HEADS * HG_DV,
               MEM_HEADS * MEM_HEAD_DIM,
               N_BRANCH * D_MODEL)

LANES = 128
SUBLANES = 8
VMEM_LIMIT_BYTES = 56 * 1024 * 1024

PROJ_TILE = 256
DSA_Q_TILE = 256
DSA_K_TILE = 512
HG_TILE = 512
MERGE_TILE = 512
FFN_TILE = 512
FFN_CHUNK = FFN_DIM // 2

GRP_A = 3 * CONV_DIM
GRP_Q = DSA_HEADS * DSA_HEAD_DIM
GRP_KV = 2 * LANES
GRP_IQ = IDX_HEADS * IDX_DIM
GRP_H = 4 * HG_HEADS * HG_DK
GRP_M = MEM_HEADS * MEM_HEAD_DIM
OFF_A = 0
OFF_Q = OFF_A + GRP_A
OFF_KV = OFF_Q + GRP_Q
OFF_IQ = OFF_KV + GRP_KV
OFF_H = OFF_IQ + GRP_IQ
OFF_M = OFF_H + GRP_H
PROJ_COLS = OFF_M + GRP_M

NEG_BIG = -1e30
INT_MIN = -(2 ** 31)
LOG2E = 1.4426950408889634
Q_SCALE = DSA_HEAD_DIM ** -0.5 * LOG2E
SAFE_LOGIT_BOUND = 40.0

BF16 = jnp.bfloat16
F32 = jnp.float32


def _mm(a, b):
    return jnp.dot(a, b, preferred_element_type=F32)


def _mm_nt(a, b):
    return lax.dot_general(a, b, (((1,), (1,)), ((), ())), preferred_element_type=F32)


def _mm_tn(a, b):
    return lax.dot_general(a, b, (((0,), (0,)), ((), ())), preferred_element_type=F32)


def _rms_scale(x):
    return lax.rsqrt(jnp.mean(x * x, axis=-1, keepdims=True) + EPS)


def _params(n_grid):
    return pltpu.CompilerParams(dimension_semantics=("arbitrary",) * n_grid,
                                vmem_limit_bytes=VMEM_LIMIT_BYTES)


def _mem_kv_kernel(mem_ref, mem_norm_ref, wkv_ref, knorm_ref, mkt_ref, mv_ref):
    m = mem_ref[0]
    mn = (m * _rms_scale(m) * mem_norm_ref[...]).astype(BF16)
    kv = _mm(mn, wkv_ref[...])
    for h in range(MEM_HEADS):
        kh = kv[:, h * MEM_HEAD_DIM:(h + 1) * MEM_HEAD_DIM]
        kh = kh * _rms_scale(kh) * knorm_ref[...]
        mkt_ref[0, h] = kh.T.astype(BF16)
        off = MEM_HEADS * MEM_HEAD_DIM + h * MEM_HEAD_DIM
        mv_ref[0, h] = kv[:, off:off + MEM_HEAD_DIM].astype(BF16)


def _mem_kv(mem, mem_norm, wkv, knorm):
    b = mem.shape[0]
    return pl.pallas_call(
        _mem_kv_kernel,
        grid=(b,),
        in_specs=[
            pl.BlockSpec((1, MEM_TOKENS, D_MODEL), lambda i: (i, 0, 0)),
            pl.BlockSpec((1, D_MODEL), lambda i: (0, 0)),
            pl.BlockSpec((D_MODEL, 2 * GRP_M), lambda i: (0, 0)),
            pl.BlockSpec((1, MEM_HEAD_DIM), lambda i: (0, 0)),
        ],
        out_specs=[
            pl.BlockSpec((1, MEM_HEADS, MEM_HEAD_DIM, MEM_TOKENS), lambda i: (i, 0, 0, 0)),
            pl.BlockSpec((1, MEM_HEADS, MEM_TOKENS, MEM_HEAD_DIM), lambda i: (i, 0, 0, 0)),
        ],
        out_shape=[
            jax.ShapeDtypeStruct((b, MEM_HEADS, MEM_HEAD_DIM, MEM_TOKENS), BF16),
            jax.ShapeDtypeStruct((b, MEM_HEADS, MEM_TOKENS, MEM_HEAD_DIM), BF16),
        ],
        compiler_params=_params(1),
        name="mem_kv",
    )(mem, mem_norm, wkv, knorm)


def _rope_slab(s, c, sa, sb):
    return s * c + pltpu.roll(s, LANES - ROT_HALF, 1) * sa + pltpu.roll(s, ROT_HALF, 1) * sb


def _proj_kernel(x_ref, nmix_ref, w_ref, rc_ref, rsa_ref, rsb_ref, convw_ref, qg_ref, kg_ref,
                 g64_ref, mkt_ref, mv_ref, mqg_ref,
                 ya_ref, q_ref, iq_ref, kk_ref, vw_ref, v1_ref, zh_ref, ym_ref,
                 carry_ref):
    i = pl.program_id(1)
    t = x_ref.shape[1]
    x = x_ref[0]
    hb = (x * _rms_scale(x) * nmix_ref[...]).astype(BF16)
    rc, rsa, rsb = rc_ref[0], rsa_ref[0], rsb_ref[0]

    za = _mm(hb, w_ref[:, OFF_A:OFF_A + GRP_A])
    a_x, a_b, a_c = za[:, :CONV_DIM], za[:, CONV_DIM:2 * CONV_DIM], za[:, 2 * CONV_DIM:]
    u = a_c * a_x

    @pl.when(i == 0)
    def _():
        carry_ref[...] = jnp.zeros_like(carry_ref)

    carry = carry_ref[...]
    row8 = lax.broadcasted_iota(jnp.int32, (SUBLANES, CONV_DIM), 0)
    r1 = pltpu.roll(u, 1, 0)
    r2 = pltpu.roll(u, 2, 0)
    top1 = jnp.where(row8 < 1, pltpu.roll(carry, 1, 0), r1[:SUBLANES])
    top2 = jnp.where(row8 < 2, pltpu.roll(carry, 2, 0), r2[:SUBLANES])
    u1 = jnp.concatenate([top1, r1[SUBLANES:]], axis=0)
    u2 = jnp.concatenate([top2, r2[SUBLANES:]], axis=0)
    carry_ref[...] = u[t - SUBLANES:]
    cw = convw_ref[...]
    ya_ref[0] = (a_b * (u2 * cw[0:1] + u1 * cw[1:2] + u * cw[2:3])).astype(BF16)

    zq = _mm(hb, w_ref[:, OFF_Q:OFF_Q + GRP_Q])
    msq = _mm((zq * zq).astype(BF16), g64_ref[...])
    qn = zq * lax.rsqrt(msq + EPS) * qg_ref[...]
    for p in range(GRP_Q // LANES):
        sl = slice(p * LANES, (p + 1) * LANES)
        q_ref[0, :, sl] = (_rope_slab(qn[:, sl], rc, rsa, rsb) * Q_SCALE).astype(BF16)

    zkv = _mm(hb, w_ref[:, OFF_KV:OFF_KV + GRP_KV])
    s0, s1 = zkv[:, :LANES], zkv[:, LANES:]
    lane = lax.broadcasted_iota(jnp.int32, (t, LANES), 1)
    is_k = lane < DSA_HEAD_DIM
    kms = jnp.sum(jnp.where(is_k, s0 * s0, 0.0), axis=-1, keepdims=True) * (1.0 / DSA_HEAD_DIM)
    s0 = s0 * jnp.where(is_k, lax.rsqrt(kms + EPS) * kg_ref[...], 1.0)
    kk_ref[0] = _rope_slab(s0, rc, rsa, rsb)
    w_scale = jnp.where(lane >= DSA_HEAD_DIM, jnp.where(lane < DSA_HEAD_DIM + IDX_HEADS, IDX_HEADS ** -0.5, 1.0), 1.0)
    vw_ref[0] = s1 * w_scale
    v1_ref[0] = jnp.where(is_k, s1, jnp.where(lane == DSA_HEAD_DIM, 1.0, 0.0)).astype(BF16)

    ziq = _mm(hb, w_ref[:, OFF_IQ:OFF_IQ + GRP_IQ])
    for p in range(GRP_IQ // LANES):
        sl = slice(p * LANES, (p + 1) * LANES)
        iq_ref[0, :, sl] = (_rope_slab(ziq[:, sl], rc, rsa, rsb) * (IDX_DIM ** -0.5)).astype(BF16)

    zh_ref[0] = _mm(hb, w_ref[:, OFF_H:OFF_H + GRP_H])

    zm = _mm(hb, w_ref[:, OFF_M:OFF_M + GRP_M])
    for h in range(MEM_HEADS):
        sl = slice(h * MEM_HEAD_DIM, (h + 1) * MEM_HEAD_DIM)
        mq = zm[:, sl]
        mq = mq * _rms_scale(mq) * mqg_ref[...] * (MEM_HEAD_DIM ** -0.5)
        lg = _mm(mq.astype(BF16), mkt_ref[0, h])
        pe = jnp.exp(lg - jnp.max(lg, axis=-1, keepdims=True))
        den = jnp.sum(pe, axis=-1, keepdims=True)
        ym_ref[0, :, sl] = (_mm(pe.astype(BF16), mv_ref[0, h]) / den).astype(BF16)


def _proj(x, nmix, w1, rc, rsa, rsb, convw, qg, kg, g64, mkt, mv, mqg):
    b, s, _ = x.shape
    t = min(PROJ_TILE, s)
    tok = lambda width: pl.BlockSpec((1, t, width), lambda bi, i: (bi, i, 0))
    const2 = lambda shape: pl.BlockSpec(shape, lambda bi, i: (0, 0))
    per_b4 = lambda shape: pl.BlockSpec((1,) + shape, lambda bi, i: (bi, 0, 0, 0))
    out_widths = (CONV_DIM, GRP_Q, GRP_IQ, LANES, LANES, LANES, GRP_H, GRP_M)
    out_dtypes = (BF16, BF16, BF16, F32, F32, BF16, F32, BF16)
    return pl.pallas_call(
        _proj_kernel,
        grid=(b, s // t),
        in_specs=[
            tok(D_MODEL), const2((1, D_MODEL)), const2((D_MODEL, PROJ_COLS)),
            tok(LANES), tok(LANES), tok(LANES),
            const2((SUBLANES, CONV_DIM)), const2((1, GRP_Q)), const2((1, LANES)),
            const2((GRP_Q, GRP_Q)),
            per_b4((MEM_HEADS, MEM_HEAD_DIM, MEM_TOKENS)), per_b4((MEM_HEADS, MEM_TOKENS, MEM_HEAD_DIM)),
            const2((1, MEM_HEAD_DIM)),
        ],
        out_specs=[tok(w) for w in out_widths],
        out_shape=[jax.ShapeDtypeStruct((b, s, w), d) for w, d in zip(out_widths, out_dtypes)],
        scratch_shapes=[pltpu.VMEM((SUBLANES, CONV_DIM), F32)],
        compiler_params=_params(2),
        name="proj",
    )(x, nmix, w1, rc, rsa, rsb, convw, qg, kg, g64, mkt, mv, mqg)


def _bit_transpose32(words):
    a = list(words)
    for j, m in ((16, 0x0000FFFF), (8, 0x00FF00FF), (4, 0x0F0F0F0F), (2, 0x33333333), (1, 0x55555555)):
        for k in range(32):
            if k & j:
                continue
            tmp = (a[k] ^ lax.shift_right_logical(a[k + j], jnp.int32(j))) & m
            a[k] = a[k] ^ tmp
            a[k + j] = a[k + j] ^ lax.shift_left(tmp, jnp.int32(j))
    return a


def _low_bits(n):
    return jnp.where(n >= 32, -1, jnp.where(n <= 0, 0, lax.shift_left(jnp.int32(1), n) - 1))


def _dsa_kernel(fast_ref, q_ref, iq_ref, vw_ref, kbd_ref, ikbd_ref, v1_ref, o_ref,
                key_ref, plane_ref, cand_ref, sel_ref, m_ref, acc_ref, *, topk, seq_len):
    i = pl.program_id(1)
    t = q_ref.shape[1]
    gt, _, tk = key_ref.shape
    n_groups = plane_ref.shape[0]
    cpt = tk // LANES
    n_pairs = DSA_HEADS // 2
    q0 = i * t
    nj = (q0 + t - 1) // tk + 1
    ng = (nj + gt - 1) // gt

    @pl.when(jnp.logical_and(pl.program_id(0) == 0, i == 0))
    def _():
        plane_ref[...] = jnp.zeros_like(plane_ref)

    iw = vw_ref[0][:, DSA_HEAD_DIM:DSA_HEAD_DIM + IDX_HEADS]

    def score_group(g, carry):
        nt = jnp.minimum(nj - g * gt, gt)

        def score_tile(jj, c2):
            sc = jnp.zeros((t, tk), F32)
            for p in range(n_pairs):
                r = _mm(iq_ref[0, :, p * LANES:(p + 1) * LANES], ikbd_ref[0, g * gt + jj])
                sc = sc + jnp.maximum(r[:, :tk], 0.0) * iw[:, 2 * p:2 * p + 1]
                sc = sc + jnp.maximum(r[:, tk:], 0.0) * iw[:, 2 * p + 1:2 * p + 2]
            sc = jnp.where(sc == 0.0, 0.0, sc)
            bits = pltpu.bitcast(sc, jnp.int32)
            key_ref[jj] = bits ^ ((bits >> 31) | INT_MIN)
            return c2

        lax.fori_loop(0, nt, score_tile, 0)

        def clear_tile(jj, c2):
            key_ref[jj] = jnp.zeros((t, tk), jnp.int32)
            return c2

        lax.fori_loop(nt, gt, clear_tile, 0)

        def to_planes(rg, c2):
            rows = pl.ds(pl.multiple_of(rg * SUBLANES, SUBLANES), SUBLANES)
            words = []
            for k in range(32):
                ch = 31 - k
                words.append(key_ref[ch // cpt, rows, (ch % cpt) * LANES:(ch % cpt + 1) * LANES])
            for p, w in enumerate(_bit_transpose32(words)):
                plane_ref[g, p, rows, :] = w
            return c2

        lax.fori_loop(0, t // SUBLANES, to_planes, 0)
        return carry

    lax.fori_loop(0, ng, score_group, 0)

    lane = lax.broadcasted_iota(jnp.int32, (t, LANES), 1)
    qpos = lax.broadcasted_iota(jnp.int32, (t, LANES), 0) + q0
    chunks_valid = ((qpos - lane) >> 7) + 1

    def count_bits(words):
        tot = lax.population_count(words[0]).astype(F32)
        for w in words[1:]:
            tot = tot + lax.population_count(w).astype(F32)
        return jnp.sum(tot, axis=-1, keepdims=True)

    for g in range(n_groups):
        cand_ref[g] = _low_bits(chunks_valid - 32 * g)
        sel_ref[g] = jnp.zeros((t, LANES), jnp.int32)

    def radix(p, need):
        ones = [cand_ref[g] & plane_ref[g, p] for g in range(n_groups)]
        c1 = count_bits(ones)
        take = c1 >= need
        for g in range(n_groups):
            cand = cand_ref[g]
            cand_ref[g] = jnp.where(take, ones[g], cand ^ ones[g])
            sel_ref[g] = jnp.where(take, sel_ref[g], sel_ref[g] | ones[g])
        return jnp.where(take, need, need - c1)

    need = lax.fori_loop(0, 32, radix, jnp.full((t, 1), float(topk), F32))

    has_tie = jnp.where(count_bits([cand_ref[g] for g in range(n_groups)]) > need, 1.0, 0.0)
    any_tie = jnp.max(has_tie) > 0.0

    @pl.when(jnp.logical_not(any_tie))
    def _():
        for g in range(n_groups):
            sel_ref[g] = sel_ref[g] | cand_ref[g]

    @pl.when(any_tie)
    def _():
        n_bits = max(1, int(seq_len - 1).bit_length())

        def below(g, cut):
            return cand_ref[g] & _low_bits(((cut - lane + (LANES - 1)) >> 7) - 32 * g)

        def bisect_idx(bit, pfx):
            cut = pfx | (jnp.int32(1) << (n_bits - 1 - bit))
            cnt = count_bits([below(g, cut) for g in range(n_groups)])
            return jnp.where(cnt < need, cut, pfx)

        cut = lax.fori_loop(0, n_bits, bisect_idx, jnp.zeros((t, 1), jnp.int32))
        cut = jnp.where(has_tie > 0.0, cut + 1, seq_len)
        for g in range(n_groups):
            sel_ref[g] = sel_ref[g] | below(g, cut)

    neg_bits = int(np.float32(NEG_BIG).view(np.int32))

    def tile_bias(j):
        words = sel_ref[j // gt]
        parts = []
        for c in range(cpt):
            k = (j % gt) * cpt + c
            picked = lax.shift_left(words, 31 - k) >> 31
            parts.append(pltpu.bitcast(neg_bits & ~picked, F32))
        return jnp.concatenate(parts, axis=-1)

    acc_ref[...] = jnp.zeros(acc_ref.shape, F32)
    use_fast = fast_ref[0] > 0

    @pl.when(use_fast)
    def _():
        def attend(j, carry):
            bias = tile_bias(j)
            for p in range(n_pairs):
                lg = _mm(q_ref[0, :, p * LANES:(p + 1) * LANES], kbd_ref[0, j])
                for e in range(2):
                    pe = jnp.exp2(lg[:, e * tk:(e + 1) * tk] + bias).astype(BF16)
                    acc_ref[2 * p + e] += _mm(pe, v1_ref[0, j])
            return carry

        lax.fori_loop(0, nj, attend, 0)

    @pl.when(jnp.logical_not(use_fast))
    def _():
        m_ref[...] = jnp.full(m_ref.shape, NEG_BIG, F32)

        def attend(j, carry):
            bias = tile_bias(j)
            for p in range(n_pairs):
                lg = _mm(q_ref[0, :, p * LANES:(p + 1) * LANES], kbd_ref[0, j])
                for e in range(2):
                    h = 2 * p + e
                    s = lg[:, e * tk:(e + 1) * tk] + bias
                    m_old = m_ref[h][:, :1]
                    m_new = jnp.maximum(m_old, jnp.max(s, axis=-1, keepdims=True))
                    pe = jnp.exp2(s - m_new).astype(BF16)
                    acc_ref[h] = acc_ref[h] * jnp.exp2(m_old - m_new) + _mm(pe, v1_ref[0, j])
                    m_ref[h] = jnp.broadcast_to(m_new, (t, LANES))
            return carry

        lax.fori_loop(0, nj, attend, 0)

    first_head = lax.broadcasted_iota(jnp.int32, (t, LANES), 1) < DSA_HEAD_DIM
    for p in range(n_pairs):
        a0, a1 = acc_ref[2 * p], acc_ref[2 * p + 1]
        o0 = a0 / a0[:, DSA_HEAD_DIM:DSA_HEAD_DIM + 1]
        o1 = a1 / a1[:, DSA_HEAD_DIM:DSA_HEAD_DIM + 1]
        o_ref[0, :, p * LANES:(p + 1) * LANES] = jnp.where(
            first_head, o0, pltpu.roll(o1, DSA_HEAD_DIM, 1)).astype(BF16)


def _dsa(fast, q, iq, vw, kbd, ikbd, v1, topk):
    b, s, _ = q.shape
    t = min(DSA_Q_TILE, s)
    nj, _, tk2 = kbd.shape[1:]
    tk = tk2 // 2
    gt = 32 * LANES // tk
    n_groups = -(-nj // gt)
    tok = lambda width: pl.BlockSpec((1, t, width), lambda bi, i: (bi, i, 0))
    per_b = lambda shape: pl.BlockSpec((1,) + shape, lambda bi, i: (bi, 0, 0, 0))
    return pl.pallas_call(
        functools.partial(_dsa_kernel, topk=topk, seq_len=s),
        grid=(b, s // t),
        in_specs=[pl.BlockSpec(memory_space=pltpu.SMEM),
                  tok(GRP_Q), tok(GRP_IQ), tok(LANES),
                  per_b((nj, LANES, tk2)), per_b((nj, LANES, tk2)), per_b((nj, tk, LANES))],
        out_specs=tok(GRP_Q),
        out_shape=jax.ShapeDtypeStruct((b, s, GRP_Q), BF16),
        scratch_shapes=[
            pltpu.VMEM((gt, t, tk), jnp.int32),
            pltpu.VMEM((n_groups, 32, t, LANES), jnp.int32),
            pltpu.VMEM((n_groups, t, LANES), jnp.int32),
            pltpu.VMEM((n_groups, t, LANES), jnp.int32),
            pltpu.VMEM((DSA_HEADS, t, LANES), F32),
            pltpu.VMEM((DSA_HEADS, t, LANES), F32),
        ],
        compiler_params=_params(2),
        name="dsa",
    )(fast, q, iq, vw, kbd, ikbd, v1.reshape(b, nj, tk, LANES))


def _block_diag_keys(kmat, tk):
    b, s, d = kmat.shape
    kt = kmat.reshape(b, s // tk, tk, d).transpose(0, 1, 3, 2)
    z = jnp.zeros_like(kt)
    return jnp.concatenate([jnp.concatenate([kt, z], axis=-1), jnp.concatenate([z, kt], axis=-1)], axis=2)


def _split3_bf16(x):
    hi = x.astype(BF16)
    r1 = x - hi.astype(F32)
    mid = r1.astype(BF16)
    lo = (r1 - mid.astype(F32)).astype(BF16)
    return hi, mid, lo


def _hgrn_kernel(gq_ref, gf_ref, gi_ref, gg_ref, lb_ref, onorm_ref, o_ref, state_ref):
    i = pl.program_id(2)
    t = gq_ref.shape[1]
    c = HG_CHUNK

    @pl.when(i == 0)
    def _():
        state_ref[...] = jnp.zeros_like(state_ref)

    lb = lb_ref[...]
    log_lb, log1m_lb, one_m_lb = lb[0:1], lb[1:2], lb[2:3]
    rr = lax.broadcasted_iota(jnp.int32, (c, c), 0)
    cc = lax.broadcasted_iota(jnp.int32, (c, c), 1)
    tril = rr >= cc
    tril_b = jnp.where(tril, 1.0, 0.0).astype(BF16)

    for n in range(t // c):
        rows = slice(n * c, (n + 1) * c)
        zq, zf, v, zg = gq_ref[0, rows], gf_ref[0, rows], gi_ref[0, rows], gg_ref[0, rows]
        log_sig = jnp.minimum(zf, 0.0) - jnp.log1p(jnp.exp(-jnp.abs(zf)))
        b_ = log1m_lb + log_sig
        log_f = jnp.maximum(log_lb, b_) + jnp.log1p(jnp.exp(-jnp.abs(log_lb - b_)))
        hk = one_m_lb * jax.nn.sigmoid(-zf)
        hq = zq * jax.nn.sigmoid(zq)
        hi, mid, lo = _split3_bf16(log_f)
        cum = _mm(tril_b, hi) + _mm(tril_b, mid) + _mm(tril_b, lo)
        ref = cum[c // 2 - 1:c // 2]
        last = cum[c - 1:c]
        vb = v.astype(BF16)
        a = _mm_nt((hq * jnp.exp(cum - ref)).astype(BF16), (hk * jnp.exp(ref - cum)).astype(BF16))
        o = _mm(jnp.where(tril, a, 0.0).astype(BF16), vb)
        st = state_ref[...]
        o = o + _mm_nt((hq * jnp.exp(cum)).astype(BF16), st.astype(BF16))
        ks = (hk * jnp.exp(last - cum)).astype(BF16)
        state_ref[...] = st * jnp.exp(last) + _mm_tn(vb, ks)
        o = o * _rms_scale(o) * onorm_ref[...]
        o_ref[0, rows] = (o * (zg * jax.nn.sigmoid(zg))).astype(BF16)


def _hgrn(zh, lb_rows, onorm):
    b, s, _ = zh.shape
    t = min(HG_TILE, s)
    gate = lambda g: pl.BlockSpec((1, t, HG_DK), lambda bi, h, i: (bi, i, g * HG_HEADS + h))
    return pl.pallas_call(
        _hgrn_kernel,
        grid=(b, HG_HEADS, s // t),
        in_specs=[gate(0), gate(1), gate(2), gate(3),
                  pl.BlockSpec((SUBLANES, HG_DK), lambda bi, h, i: (0, h)),
                  pl.BlockSpec((1, HG_DV), lambda bi, h, i: (0, 0))],
        out_specs=pl.BlockSpec((1, t, HG_DV), lambda bi, h, i: (bi, i, h)),
        out_shape=jax.ShapeDtypeStruct((b, s, HG_HEADS * HG_DV), BF16),
        scratch_shapes=[pltpu.VMEM((HG_DV, HG_DK), F32)],
        compiler_params=_params(3),
        name="hgrn",
    )(zh, zh, zh, zh, lb_rows, onorm)


def _merge_kernel(x_ref, nmix_ref, ya_ref, yb_ref, yc_ref, ym_ref, wg_ref, wl_ref, wo_ref, o_ref):
    x = x_ref[0]
    hb = (x * _rms_scale(x) * nmix_ref[...]).astype(BF16)
    merged = None
    for n, y_ref in enumerate((ya_ref, yb_ref, yc_ref, ym_ref)):
        gate = jax.nn.sigmoid(_mm(hb, wg_ref[:, n * D_MODEL:(n + 1) * D_MODEL]))
        term = gate * _mm(y_ref[0], wl_ref[n])
        merged = term if merged is None else merged + term
    o_ref[0] = x + _mm(merged.astype(BF16), wo_ref[...])


def _merge(x, nmix, ya, yb, yc, ym, wg, wl, wo):
    b, s, _ = x.shape
    t = min(MERGE_TILE, s)
    tok = lambda width: pl.BlockSpec((1, t, width), lambda bi, i: (bi, i, 0))
    return pl.pallas_call(
        _merge_kernel,
        grid=(b, s // t),
        in_specs=[tok(D_MODEL), pl.BlockSpec((1, D_MODEL), lambda bi, i: (0, 0)),
                  tok(BRANCH_DIM), tok(BRANCH_DIM), tok(BRANCH_DIM), tok(BRANCH_DIM),
                  pl.BlockSpec((D_MODEL, N_BRANCH * D_MODEL), lambda bi, i: (0, 0)),
                  pl.BlockSpec((N_BRANCH, BRANCH_DIM, D_MODEL), lambda bi, i: (0, 0, 0)),
                  pl.BlockSpec((D_MODEL, D_MODEL), lambda bi, i: (0, 0))],
        out_specs=tok(D_MODEL),
        out_shape=jax.ShapeDtypeStruct((b, s, D_MODEL), F32),
        compiler_params=_params(2),
        name="merge",
    )(x, nmix, ya, yb, yc, ym, wg, wl, wo)


def _ffn_kernel(x_ref, nffn_ref, wup_ref, wdn_ref, o_ref):
    x = x_ref[0]
    hb = (x * _rms_scale(x) * nffn_ref[...]).astype(BF16)
    out = x
    for n in range(FFN_DIM // FFN_CHUNK):
        lo = n * FFN_CHUNK
        gate = _mm(hb, wup_ref[:, lo:lo + FFN_CHUNK])
        up = _mm(hb, wup_ref[:, FFN_DIM + lo:FFN_DIM + lo + FFN_CHUNK])
        act = (gate * jax.nn.sigmoid(gate) * up).astype(BF16)
        out = out + _mm(act, wdn_ref[lo:lo + FFN_CHUNK, :])
    o_ref[0] = out


def _ffn(x, nffn, wup, wdn):
    b, s, _ = x.shape
    t = min(FFN_TILE, s)
    tok = pl.BlockSpec((1, t, D_MODEL), lambda bi, i: (bi, i, 0))
    return pl.pallas_call(
        _ffn_kernel,
        grid=(b, s // t),
        in_specs=[tok, pl.BlockSpec((1, D_MODEL), lambda bi, i: (0, 0)),
                  pl.BlockSpec((D_MODEL, 2 * FFN_DIM), lambda bi, i: (0, 0)),
                  pl.BlockSpec((FFN_DIM, D_MODEL), lambda bi, i: (0, 0))],
        out_specs=tok,
        out_shape=jax.ShapeDtypeStruct((b, s, D_MODEL), F32),
        compiler_params=_params(2),
        name="ffn",
    )(x, nffn, wup, wdn)


def _rope_tables(positions):
    inv_freq = 1.0 / (ROPE_THETA ** (jnp.arange(0, ROT_DIM, 2, dtype=F32) / ROT_DIM))
    ang = positions.astype(F32)[..., None] * inv_freq
    cos, sin = jnp.cos(ang), jnp.sin(ang)
    rest = DSA_HEAD_DIM - ROT_DIM
    ones = jnp.ones(cos.shape[:-1] + (rest,), F32)
    zeros = jnp.zeros(cos.shape[:-1] + (rest,), F32)
    z8 = jnp.zeros_like(sin)
    c = jnp.concatenate([cos, cos, ones], axis=-1)
    sa = jnp.concatenate([-sin, z8, zeros], axis=-1)
    sb = jnp.concatenate([z8, sin, zeros], axis=-1)
    two = lambda a: jnp.concatenate([a, a], axis=-1)
    return two(c), two(sa), two(sb)


def _relayout_w_in(w_in):
    offs = np.concatenate([[0], np.cumsum(SPLIT_SIZES)])
    col = lambda n: w_in[:, int(offs[n]):int(offs[n + 1])]
    (a_x, a_b, a_c, d_q, d_k, d_v, i_q, i_k, i_w, g_q, g_f, g_i, g_g, m_q, gates) = (col(n) for n in range(15))
    pad = jnp.zeros((w_in.shape[0], LANES - DSA_HEAD_DIM - IDX_HEADS), w_in.dtype)
    w1 = jnp.concatenate([a_x, a_b, a_c, d_q, d_k, i_k, d_v, i_w, pad, i_q, g_q, g_f, g_i, g_g, m_q], axis=1)
    return w1.astype(BF16), gates.astype(BF16)


def kernel(x, mem, positions, norm_mix, w_in, conv_w, dsa_q_norm, dsa_k_norm, hgrn_lower_bounds,
           hgrn_out_norm, mem_norm, mem_w_kv, mem_q_norm, mem_k_norm, w_lift, w_out, norm_ffn,
           ffn_w_up, ffn_w_down):
    b, s, d = x.shape
    depth = w_in.shape[0]
    assert d == D_MODEL and w_in.shape[2] == sum(SPLIT_SIZES)
    assert s % HG_CHUNK == 0 and s % min(DSA_K_TILE, s) == 0
    topk = min(TOPK_MAX, s // 4)
    tk = min(DSA_K_TILE, s)

    rc, rsa, rsb = _rope_tables(positions)
    lbs = jnp.cumsum(jax.nn.softmax(hgrn_lower_bounds.astype(F32), axis=0), axis=0)
    lbs = lbs - lbs[0:1]
    g64 = jnp.asarray(np.kron(np.eye(DSA_HEADS), np.full((DSA_HEAD_DIM, DSA_HEAD_DIM), 1.0 / DSA_HEAD_DIM)), BF16)
    row = lambda v: v.reshape(1, -1).astype(F32)

    for l in range(depth):
        w1, wg = _relayout_w_in(w_in[l])
        lb = lbs[l]
        lb_rows = jnp.concatenate([jnp.stack([jnp.log(lb), jnp.log1p(-lb), 1.0 - lb]),
                                   jnp.zeros((SUBLANES - 3, lb.shape[0]), F32)])
        convw = jnp.concatenate([conv_w[l], jnp.zeros((SUBLANES - CONV_WIDTH, CONV_DIM), F32)])
        mkt, mv = _mem_kv(mem, row(mem_norm), mem_w_kv[l].astype(BF16), row(mem_k_norm[l]))
        ya, q, iq, kk, vw, v1, zh, ym = _proj(
            x, row(norm_mix[l]), w1, rc, rsa, rsb, convw,
            row(jnp.tile(dsa_q_norm[l], DSA_HEADS)),
            row(jnp.concatenate([dsa_k_norm[l], jnp.ones((LANES - DSA_HEAD_DIM,), F32)])),
            g64, mkt, mv, row(mem_q_norm[l]))
        kbd = _block_diag_keys(kk[..., :DSA_HEAD_DIM].astype(BF16), tk)
        ikbd = _block_diag_keys(kk[..., DSA_HEAD_DIM:].astype(BF16), tk)
        logit_bound = DSA_HEAD_DIM ** 0.5 * jnp.max(jnp.abs(dsa_q_norm[l])) * jnp.max(jnp.abs(dsa_k_norm[l]))
        fast = (logit_bound <= SAFE_LOGIT_BOUND).astype(jnp.int32).reshape(1)
        yb = _dsa(fast, q, iq, vw, kbd, ikbd, v1, topk)
        yc = _hgrn(zh, lb_rows, row(hgrn_out_norm[l]))
        x = _merge(x, row(norm_mix[l]), ya, yb, yc, ym, wg, w_lift[l].astype(BF16), w_out[l].astype(BF16))
        x = _ffn(x, row(norm_ffn[l]), ffn_w_up[l].astype(BF16), ffn_w_down[l].astype(BF16))
    return x
```

```python
import functools

import jax
import jax.numpy as jnp
import numpy as np
from jax import lax
from jax.experimental import pallas as pl
from jax.experimental.pallas import tpu as pltpu

D_MODEL = 1024
MEM_TOKENS = 256
N_BRANCH = 4
BRANCH_DIM = 512
CONV_DIM = 512
CONV_WIDTH = 3
DSA_HEADS = 8
DSA_HEAD_DIM = 64
IDX_HEADS = 8
IDX_DIM = 64
TOPK_MAX = 256
HG_HEADS = 4
HG_DK = 128
HG_DV = 128
HG_CHUNK = 64
MEM_HEADS = 4
MEM_HEAD_DIM = 128
ROPE_THETA = 500000.0
ROT_DIM = DSA_HEAD_DIM // 4
ROT_HALF = ROT_DIM // 2
FFN_DIM = ((8 * D_MODEL // 3 + 255) // 256) * 256
EPS = 1e-6

SPLIT_SIZES = (CONV_DIM, CONV_DIM, CONV_DIM,
               DSA_HEADS * DSA_HEAD_DIM, DSA_HEAD_DIM, DSA_HEAD_DIM,
               IDX_HEADS * IDX_DIM, IDX_DIM, IDX_HEADS,
               HG_HEADS * HG_DK, HG_HEADS * HG_DK, HG_HEADS * HG_DV, HG_HEADS * HG_DV,
               MEM_HEADS * MEM_HEAD_DIM,
               N_BRANCH * D_MODEL)

LANES = 128
SUBLANES = 8
VMEM_LIMIT_BYTES = 56 * 1024 * 1024

PROJ_TILE = 256
DSA_Q_TILE = 512
DSA_K_TILE = 512
HG_TILE = 512
MERGE_TILE = 512
FFN_TILE = 512
FFN_CHUNK = FFN_DIM // 2

GRP_A = 3 * CONV_DIM
GRP_Q = DSA_HEADS * DSA_HEAD_DIM
GRP_KV = 2 * LANES
GRP_IQ = IDX_HEADS * IDX_DIM
GRP_H = 4 * HG_HEADS * HG_DK
GRP_M = MEM_HEADS * MEM_HEAD_DIM
OFF_A = 0
OFF_Q = OFF_A + GRP_A
OFF_KV = OFF_Q + GRP_Q
OFF_IQ = OFF_KV + GRP_KV
OFF_H = OFF_IQ + GRP_IQ
OFF_M = OFF_H + GRP_H
PROJ_COLS = OFF_M + GRP_M

NEG_BIG = -1e30
INT_MIN = -(2 ** 31)
LOG2E = 1.4426950408889634
Q_SCALE = DSA_HEAD_DIM ** -0.5 * LOG2E
SAFE_LOGIT_BOUND = 40.0

BF16 = jnp.bfloat16
F32 = jnp.float32


def _mm(a, b):
    return jnp.dot(a, b, preferred_element_type=F32)


def _mm_nt(a, b):
    return lax.dot_general(a, b, (((1,), (1,)), ((), ())), preferred_element_type=F32)


def _mm_tn(a, b):
    return lax.dot_general(a, b, (((0,), (0,)), ((), ())), preferred_element_type=F32)


def _rms_scale(x):
    return lax.rsqrt(jnp.mean(x * x, axis=-1, keepdims=True) + EPS)


def _params(n_grid):
    return pltpu.CompilerParams(dimension_semantics=("arbitrary",) * n_grid,
                                vmem_limit_bytes=VMEM_LIMIT_BYTES)


def _mem_kv_kernel(mem_ref, mem_norm_ref, wkv_ref, knorm_ref, mkt_ref, mv_ref):
    m = mem_ref[0]
    mn = (m * _rms_scale(m) * mem_norm_ref[...]).astype(BF16)
    kv = _mm(mn, wkv_ref[...])
    for h in range(MEM_HEADS):
        kh = kv[:, h * MEM_HEAD_DIM:(h + 1) * MEM_HEAD_DIM]
        kh = kh * _rms_scale(kh) * knorm_ref[...]
        mkt_ref[0, h] = kh.T.astype(BF16)
        off = MEM_HEADS * MEM_HEAD_DIM + h * MEM_HEAD_DIM
        mv_ref[0, h] = kv[:, off:off + MEM_HEAD_DIM].astype(BF16)


def _mem_kv(mem, mem_norm, wkv, knorm):
    b = mem.shape[0]
    return pl.pallas_call(
        _mem_kv_kernel,
        grid=(b,),
        in_specs=[
            pl.BlockSpec((1, MEM_TOKENS, D_MODEL), lambda i: (i, 0, 0)),
            pl.BlockSpec((1, D_MODEL), lambda i: (0, 0)),
            pl.BlockSpec((D_MODEL, 2 * GRP_M), lambda i: (0, 0)),
            pl.BlockSpec((1, MEM_HEAD_DIM), lambda i: (0, 0)),
        ],
        out_specs=[
            pl.BlockSpec((1, MEM_HEADS, MEM_HEAD_DIM, MEM_TOKENS), lambda i: (i, 0, 0, 0)),
            pl.BlockSpec((1, MEM_HEADS, MEM_TOKENS, MEM_HEAD_DIM), lambda i: (i, 0, 0, 0)),
        ],
        out_shape=[
            jax.ShapeDtypeStruct((b, MEM_HEADS, MEM_HEAD_DIM, MEM_TOKENS), BF16),
            jax.ShapeDtypeStruct((b, MEM_HEADS, MEM_TOKENS, MEM_HEAD_DIM), BF16),
        ],
        compiler_params=_params(1),
        name="mem_kv",
    )(mem, mem_norm, wkv, knorm)


def _rope_slab(s, c, sa, sb):
    return s * c + pltpu.roll(s, LANES - ROT_HALF, 1) * sa + pltpu.roll(s, ROT_HALF, 1) * sb


def _proj_kernel(x_ref, nmix_ref, w_ref, cs_ref, rplace_ref, rones_ref, convw_ref, qg_ref, kg_ref,
                 g64_ref, mkt_ref, mv_ref, mqg_ref,
                 ya_ref, q_ref, iq_ref, kk_ref, vw_ref, v1_ref, zh_ref, ym_ref,
                 carry_ref):
    i = pl.program_id(1)
    t = x_ref.shape[1]
    x = x_ref[0]
    hb = (x * _rms_scale(x) * nmix_ref[...]).astype(BF16)

    cs_hi, cs_mid, cs_lo = _split3_bf16(cs_ref[0])
    place = rplace_ref[...]
    tabs = _mm(cs_hi, place) + _mm(cs_mid, place) + _mm(cs_lo, place)
    rc = tabs[:, :LANES] + rones_ref[...]
    rsa = tabs[:, LANES:2 * LANES]
    rsb = tabs[:, 2 * LANES:]

    za = _mm(hb, w_ref[:, OFF_A:OFF_A + GRP_A])
    a_x, a_b, a_c = za[:, :CONV_DIM], za[:, CONV_DIM:2 * CONV_DIM], za[:, 2 * CONV_DIM:]
    u = a_c * a_x

    @pl.when(i == 0)
    def _():
        carry_ref[...] = jnp.zeros_like(carry_ref)

    carry = carry_ref[...]
    row8 = lax.broadcasted_iota(jnp.int32, (SUBLANES, CONV_DIM), 0)
    r1 = pltpu.roll(u, 1, 0)
    r2 = pltpu.roll(u, 2, 0)
    top1 = jnp.where(row8 < 1, pltpu.roll(carry, 1, 0), r1[:SUBLANES])
    top2 = jnp.where(row8 < 2, pltpu.roll(carry, 2, 0), r2[:SUBLANES])
    u1 = jnp.concatenate([top1, r1[SUBLANES:]], axis=0)
    u2 = jnp.concatenate([top2, r2[SUBLANES:]], axis=0)
    carry_ref[...] = u[t - SUBLANES:]
    cw = convw_ref[...]
    ya_ref[0] = (a_b * (u2 * cw[0:1] + u1 * cw[1:2] + u * cw[2:3])).astype(BF16)

    zq = _mm(hb, w_ref[:, OFF_Q:OFF_Q + GRP_Q])
    msq = _mm((zq * zq).astype(BF16), g64_ref[...])
    qn = zq * lax.rsqrt(msq + EPS) * qg_ref[...]
    for p in range(GRP_Q // LANES):
        sl = slice(p * LANES, (p + 1) * LANES)
        q_ref[0, :, sl] = (_rope_slab(qn[:, sl], rc, rsa, rsb) * Q_SCALE).astype(BF16)

    zkv = _mm(hb, w_ref[:, OFF_KV:OFF_KV + GRP_KV])
    s0, s1 = zkv[:, :LANES], zkv[:, LANES:]
    lane = lax.broadcasted_iota(jnp.int32, (t, LANES), 1)
    is_k = lane < DSA_HEAD_DIM
    kms = jnp.sum(jnp.where(is_k, s0 * s0, 0.0), axis=-1, keepdims=True) * (1.0 / DSA_HEAD_DIM)
    s0 = s0 * jnp.where(is_k, lax.rsqrt(kms + EPS) * kg_ref[...], 1.0)
    kk_ref[0] = _rope_slab(s0, rc, rsa, rsb)
    w_scale = jnp.where(lane >= DSA_HEAD_DIM, jnp.where(lane < DSA_HEAD_DIM + IDX_HEADS, IDX_HEADS ** -0.5, 1.0), 1.0)
    vw_ref[0] = s1 * w_scale
    v1_ref[0] = jnp.where(is_k, s1, jnp.where(lane == DSA_HEAD_DIM, 1.0, 0.0)).astype(BF16)

    ziq = _mm(hb, w_ref[:, OFF_IQ:OFF_IQ + GRP_IQ])
    for p in range(GRP_IQ // LANES):
        sl = slice(p * LANES, (p + 1) * LANES)
        iq_ref[0, :, sl] = (_rope_slab(ziq[:, sl], rc, rsa, rsb) * (IDX_DIM ** -0.5)).astype(BF16)

    zh_ref[0] = _mm(hb, w_ref[:, OFF_H:OFF_H + GRP_H])

    zm = _mm(hb, w_ref[:, OFF_M:OFF_M + GRP_M])
    for h in range(MEM_HEADS):
        sl = slice(h * MEM_HEAD_DIM, (h + 1) * MEM_HEAD_DIM)
        mq = zm[:, sl]
        mq = mq * _rms_scale(mq) * mqg_ref[...] * (MEM_HEAD_DIM ** -0.5)
        lg = _mm(mq.astype(BF16), mkt_ref[0, h])
        pe = jnp.exp(lg - jnp.max(lg, axis=-1, keepdims=True))
        den = jnp.sum(pe, axis=-1, keepdims=True)
        ym_ref[0, :, sl] = (_mm(pe.astype(BF16), mv_ref[0, h]) / den).astype(BF16)


def _proj(x, nmix, w1, cs, rplace, rones, convw, qg, kg, g64, mkt, mv, mqg):
    b, s, _ = x.shape
    t = min(PROJ_TILE, s)
    tok = lambda width: pl.BlockSpec((1, t, width), lambda bi, i: (bi, i, 0))
    const2 = lambda shape: pl.BlockSpec(shape, lambda bi, i: (0, 0))
    per_b4 = lambda shape: pl.BlockSpec((1,) + shape, lambda bi, i: (bi, 0, 0, 0))
    out_widths = (CONV_DIM, GRP_Q, GRP_IQ, LANES, LANES, LANES, GRP_H, GRP_M)
    out_dtypes = (BF16, BF16, BF16, F32, F32, BF16, F32, BF16)
    return pl.pallas_call(
        _proj_kernel,
        grid=(b, s // t),
        in_specs=[
            tok(D_MODEL), const2((1, D_MODEL)), const2((D_MODEL, PROJ_COLS)),
            tok(ROT_DIM), const2((ROT_DIM, 3 * LANES)), const2((1, LANES)),
            const2((SUBLANES, CONV_DIM)), const2((1, GRP_Q)), const2((1, LANES)),
            const2((GRP_Q, GRP_Q)),
            per_b4((MEM_HEADS, MEM_HEAD_DIM, MEM_TOKENS)), per_b4((MEM_HEADS, MEM_TOKENS, MEM_HEAD_DIM)),
            const2((1, MEM_HEAD_DIM)),
        ],
        out_specs=[tok(w) for w in out_widths],
        out_shape=[jax.ShapeDtypeStruct((b, s, w), d) for w, d in zip(out_widths, out_dtypes)],
        scratch_shapes=[pltpu.VMEM((SUBLANES, CONV_DIM), F32)],
        compiler_params=_params(2),
        name="proj",
    )(x, nmix, w1, cs, rplace, rones, convw, qg, kg, g64, mkt, mv, mqg)


def _bit_transpose32(words):
    a = list(words)
    for j, m in ((16, 0x0000FFFF), (8, 0x00FF00FF), (4, 0x0F0F0F0F), (2, 0x33333333), (1, 0x55555555)):
        for k in range(32):
            if k & j:
                continue
            tmp = (a[k] ^ lax.shift_right_logical(a[k + j], jnp.int32(j))) & m
            a[k] = a[k] ^ tmp
            a[k + j] = a[k + j] ^ lax.shift_left(tmp, jnp.int32(j))
    return a


def _low_bits(n):
    return jnp.where(n >= 32, -1, jnp.where(n <= 0, 0, lax.shift_left(jnp.int32(1), n) - 1))


def _dsa_kernel(fast_ref, q_ref, iq_ref, vw_ref, kbd_ref, ikbd_ref, v1_ref, o_ref,
                key_ref, plane_ref, cand_ref, sel_ref, m_ref, acc_ref, *, topk, seq_len):
    i = pl.program_id(1)
    t = q_ref.shape[1]
    gt, _, tk = key_ref.shape
    n_groups = plane_ref.shape[0]
    cpt = tk // LANES
    n_pairs = DSA_HEADS // 2
    q0 = i * t
    nj = (q0 + t - 1) // tk + 1
    ng = (nj + gt - 1) // gt

    @pl.when(jnp.logical_and(pl.program_id(0) == 0, i == 0))
    def _():
        plane_ref[...] = jnp.zeros_like(plane_ref)

    iw = vw_ref[0][:, DSA_HEAD_DIM:DSA_HEAD_DIM + IDX_HEADS]

    def score_group(g, carry):
        nt = jnp.minimum(nj - g * gt, gt)

        def score_tile(jj, c2):
            sc = jnp.zeros((t, tk), F32)
            for p in range(n_pairs):
                r = _mm(iq_ref[0, :, p * LANES:(p + 1) * LANES], ikbd_ref[0, g * gt + jj])
                sc = sc + jnp.maximum(r[:, :tk], 0.0) * iw[:, 2 * p:2 * p + 1]
                sc = sc + jnp.maximum(r[:, tk:], 0.0) * iw[:, 2 * p + 1:2 * p + 2]
            sc = jnp.where(sc == 0.0, 0.0, sc)
            bits = pltpu.bitcast(sc, jnp.int32)
            key_ref[jj] = bits ^ ((bits >> 31) | INT_MIN)
            return c2

        lax.fori_loop(0, nt, score_tile, 0)

        def clear_tile(jj, c2):
            key_ref[jj] = jnp.zeros((t, tk), jnp.int32)
            return c2

        lax.fori_loop(nt, gt, clear_tile, 0)

        def to_planes(rg, c2):
            rows = pl.ds(pl.multiple_of(rg * SUBLANES, SUBLANES), SUBLANES)
            words = []
            for k in range(32):
                ch = 31 - k
                words.append(key_ref[ch // cpt, rows, (ch % cpt) * LANES:(ch % cpt + 1) * LANES])
            for p, w in enumerate(_bit_transpose32(words)):
                plane_ref[g, p, rows, :] = w
            return c2

        lax.fori_loop(0, t // SUBLANES, to_planes, 0)
        return carry

    lax.fori_loop(0, ng, score_group, 0)

    lane = lax.broadcasted_iota(jnp.int32, (t, LANES), 1)
    qpos = lax.broadcasted_iota(jnp.int32, (t, LANES), 0) + q0
    chunks_valid = ((qpos - lane) >> 7) + 1

    def count_bits(words):
        tot = lax.population_count(words[0]).astype(F32)
        for w in words[1:]:
            tot = tot + lax.population_count(w).astype(F32)
        return jnp.sum(tot, axis=-1, keepdims=True)

    for g in range(n_groups):
        cand_ref[g] = _low_bits(chunks_valid - 32 * g)
        sel_ref[g] = jnp.zeros((t, LANES), jnp.int32)

    def radix(p, need):
        ones = [cand_ref[g] & plane_ref[g, p] for g in range(n_groups)]
        c1 = count_bits(ones)
        take = c1 >= need
        for g in range(n_groups):
            cand = cand_ref[g]
            cand_ref[g] = jnp.where(take, ones[g], cand ^ ones[g])
            sel_ref[g] = jnp.where(take, sel_ref[g], sel_ref[g] | ones[g])
        return jnp.where(take, need, need - c1)

    need = lax.fori_loop(0, 32, radix, jnp.full((t, 1), float(topk), F32))

    has_tie = jnp.where(count_bits([cand_ref[g] for g in range(n_groups)]) > need, 1.0, 0.0)
    any_tie = jnp.max(has_tie) > 0.0

    @pl.when(jnp.logical_not(any_tie))
    def _():
        for g in range(n_groups):
            sel_ref[g] = sel_ref[g] | cand_ref[g]

    @pl.when(any_tie)
    def _():
        n_bits = max(1, int(seq_len - 1).bit_length())

        def below(g, cut):
            return cand_ref[g] & _low_bits(((cut - lane + (LANES - 1)) >> 7) - 32 * g)

        def bisect_idx(bit, pfx):
            cut = pfx | (jnp.int32(1) << (n_bits - 1 - bit))
            cnt = count_bits([below(g, cut) for g in range(n_groups)])
            return jnp.where(cnt < need, cut, pfx)

        cut = lax.fori_loop(0, n_bits, bisect_idx, jnp.zeros((t, 1), jnp.int32))
        cut = jnp.where(has_tie > 0.0, cut + 1, seq_len)
        for g in range(n_groups):
            sel_ref[g] = sel_ref[g] | below(g, cut)

    neg_bits = int(np.float32(NEG_BIG).view(np.int32))

    def tile_bias(j):
        words = sel_ref[j // gt]
        parts = []
        for c in range(cpt):
            k = (j % gt) * cpt + c
            picked = lax.shift_left(words, 31 - k) >> 31
            parts.append(pltpu.bitcast(neg_bits & ~picked, F32))
        return jnp.concatenate(parts, axis=-1)

    acc_ref[...] = jnp.zeros(acc_ref.shape, F32)
    use_fast = fast_ref[0] > 0

    @pl.when(use_fast)
    def _():
        def attend(j, carry):
            bias = tile_bias(j)
            for p in range(n_pairs):
                lg = _mm(q_ref[0, :, p * LANES:(p + 1) * LANES], kbd_ref[0, j])
                for e in range(2):
                    pe = jnp.exp2(lg[:, e * tk:(e + 1) * tk] + bias).astype(BF16)
                    acc_ref[2 * p + e] += _mm(pe, v1_ref[0, j])
            return carry

        lax.fori_loop(0, nj, attend, 0)

    @pl.when(jnp.logical_not(use_fast))
    def _():
        m_ref[...] = jnp.full(m_ref.shape, NEG_BIG, F32)

        def attend(j, carry):
            bias = tile_bias(j)
            for p in range(n_pairs):
                lg = _mm(q_ref[0, :, p * LANES:(p + 1) * LANES], kbd_ref[0, j])
                for e in range(2):
                    h = 2 * p + e
                    s = lg[:, e * tk:(e + 1) * tk] + bias
                    m_old = m_ref[h][:, :1]
                    m_new = jnp.maximum(m_old, jnp.max(s, axis=-1, keepdims=True))
                    pe = jnp.exp2(s - m_new).astype(BF16)
                    acc_ref[h] = acc_ref[h] * jnp.exp2(m_old - m_new) + _mm(pe, v1_ref[0, j])
                    m_ref[h] = jnp.broadcast_to(m_new, (t, LANES))
            return carry

        lax.fori_loop(0, nj, attend, 0)

    first_head = lax.broadcasted_iota(jnp.int32, (t, LANES), 1) < DSA_HEAD_DIM
    for p in range(n_pairs):
        a0, a1 = acc_ref[2 * p], acc_ref[2 * p + 1]
        o0 = a0 / a0[:, DSA_HEAD_DIM:DSA_HEAD_DIM + 1]
        o1 = a1 / a1[:, DSA_HEAD_DIM:DSA_HEAD_DIM + 1]
        o_ref[0, :, p * LANES:(p + 1) * LANES] = jnp.where(
            first_head, o0, pltpu.roll(o1, DSA_HEAD_DIM, 1)).astype(BF16)


def _dsa(fast, q, iq, vw, kbd, ikbd, v1, topk):
    b, s, _ = q.shape
    t = min(DSA_Q_TILE, s)
    nj, _, tk2 = kbd.shape[1:]
    tk = tk2 // 2
    gt = 32 * LANES // tk
    n_groups = -(-nj // gt)
    tok = lambda width: pl.BlockSpec((1, t, width), lambda bi, i: (bi, i, 0))
    per_b = lambda shape: pl.BlockSpec((1,) + shape, lambda bi, i: (bi, 0, 0, 0), pipeline_mode=pl.Buffered(1))
    return pl.pallas_call(
        functools.partial(_dsa_kernel, topk=topk, seq_len=s),
        grid=(b, s // t),
        in_specs=[pl.BlockSpec(memory_space=pltpu.SMEM),
                  tok(GRP_Q), tok(GRP_IQ), tok(LANES),
                  per_b((nj, LANES, tk2)), per_b((nj, LANES, tk2)), per_b((nj, tk, LANES))],
        out_specs=tok(GRP_Q),
        out_shape=jax.ShapeDtypeStruct((b, s, GRP_Q), BF16),
        scratch_shapes=[
            pltpu.VMEM((gt, t, tk), jnp.int32),
            pltpu.VMEM((n_groups, 32, t, LANES), jnp.int32),
            pltpu.VMEM((n_groups, t, LANES), jnp.int32),
            pltpu.VMEM((n_groups, t, LANES), jnp.int32),
            pltpu.VMEM((DSA_HEADS, t, LANES), F32),
            pltpu.VMEM((DSA_HEADS, t, LANES), F32),
        ],
        compiler_params=_params(2),
        name="dsa",
    )(fast, q, iq, vw, kbd, ikbd, v1.reshape(b, nj, tk, LANES))


def _block_diag_keys(kmat, tk):
    b, s, d = kmat.shape
    kt = kmat.reshape(b, s // tk, tk, d).transpose(0, 1, 3, 2)
    z = jnp.zeros_like(kt)
    return jnp.concatenate([jnp.concatenate([kt, z], axis=-1), jnp.concatenate([z, kt], axis=-1)], axis=2)


def _split3_bf16(x):
    hi = x.astype(BF16)
    r1 = x - hi.astype(F32)
    mid = r1.astype(BF16)
    lo = (r1 - mid.astype(F32)).astype(BF16)
    return hi, mid, lo


def _hgrn_kernel(zh_ref, lb_ref, onorm_ref, o_ref, state_ref):
    i = pl.program_id(1)
    t = zh_ref.shape[1]
    c = HG_CHUNK
    width = HG_HEADS * HG_DK

    @pl.when(i == 0)
    def _():
        state_ref[...] = jnp.zeros_like(state_ref)

    rr = lax.broadcasted_iota(jnp.int32, (c, c), 0)
    cc = lax.broadcasted_iota(jnp.int32, (c, c), 1)
    tril = rr >= cc
    tril_b = jnp.where(tril, 1.0, 0.0).astype(BF16)

    n_chunks = t // c
    chunk = lambda n: slice(n * c, (n + 1) * c)
    head = lambda h: slice(h * HG_DK, (h + 1) * HG_DK)
    zq, zf, v, zg = (zh_ref[0, :, g * width:(g + 1) * width] for g in range(4))
    lb = lb_ref[...]
    log_lb, log1m_lb, one_m_lb = lb[0:1], lb[1:2], lb[2:3]

    e = jnp.exp(-jnp.abs(zf))
    b_ = log1m_lb + jnp.minimum(zf, 0.0) - jnp.log1p(e)
    log_f = jnp.maximum(log_lb, b_) + jnp.log1p(jnp.exp(-jnp.abs(log_lb - b_)))
    hk = one_m_lb * jnp.where(zf >= 0.0, e, 1.0) / (1.0 + e)
    hq = zq * jax.nn.sigmoid(zq)
    vb = v.astype(BF16)
    hi, mid, lo = _split3_bf16(log_f)
    cums = [_mm(tril_b, hi[chunk(n)]) + _mm(tril_b, mid[chunk(n)]) + _mm(tril_b, lo[chunk(n)])
            for n in range(n_chunks)]
    q_intra, k_intra, q_inter, k_state, decay = [], [], [], [], []
    for n in range(n_chunks):
        cum = cums[n]
        ref = cum[c // 2 - 1:c // 2]
        last = cum[c - 1:c]
        q_intra.append((hq[chunk(n)] * jnp.exp(cum - ref)).astype(BF16))
        k_intra.append((hk[chunk(n)] * jnp.exp(ref - cum)).astype(BF16))
        q_inter.append((hq[chunk(n)] * jnp.exp(cum)).astype(BF16))
        k_state.append((hk[chunk(n)] * jnp.exp(last - cum)).astype(BF16))
        decay.append(jnp.exp(last))
    att = [[_mm_nt(q_intra[n][:, head(h)], k_intra[n][:, head(h)]) for h in range(HG_HEADS)]
           for n in range(n_chunks)]
    o_intra = [[_mm(jnp.where(tril, att[n][h], 0.0).astype(BF16), vb[chunk(n), head(h)])
                for h in range(HG_HEADS)] for n in range(n_chunks)]
    states = [state_ref[h] for h in range(HG_HEADS)]
    o_inter = []
    for n in range(n_chunks):
        o_inter.append([_mm_nt(q_inter[n][:, head(h)], states[h].astype(BF16)) for h in range(HG_HEADS)])
        states = [states[h] * decay[n][:, head(h)] + _mm_tn(vb[chunk(n), head(h)], k_state[n][:, head(h)])
                  for h in range(HG_HEADS)]
    for h in range(HG_HEADS):
        state_ref[h] = states[h]
    gate = zg * jax.nn.sigmoid(zg)
    for n in range(n_chunks):
        for h in range(HG_HEADS):
            o = o_intra[n][h] + o_inter[n][h]
            o = o * _rms_scale(o) * onorm_ref[...]
            o_ref[0, chunk(n), head(h)] = (o * gate[chunk(n), head(h)]).astype(BF16)


def _hgrn(zh, lb_rows, onorm):
    b, s, _ = zh.shape
    t = min(HG_TILE, s)
    return pl.pallas_call(
        _hgrn_kernel,
        grid=(b, s // t),
        in_specs=[pl.BlockSpec((1, t, GRP_H), lambda bi, i: (bi, i, 0)),
                  pl.BlockSpec((SUBLANES, HG_HEADS * HG_DK), lambda bi, i: (0, 0)),
                  pl.BlockSpec((1, HG_DV), lambda bi, i: (0, 0))],
        out_specs=pl.BlockSpec((1, t, HG_HEADS * HG_DV), lambda bi, i: (bi, i, 0)),
        out_shape=jax.ShapeDtypeStruct((b, s, HG_HEADS * HG_DV), BF16),
        scratch_shapes=[pltpu.VMEM((HG_HEADS, HG_DV, HG_DK), F32)],
        compiler_params=_params(2),
        name="hgrn",
    )(zh, lb_rows, onorm)


def _merge_kernel(x_ref, nmix_ref, ya_ref, yb_ref, yc_ref, ym_ref, wg_ref, wl_ref, wo_ref, o_ref):
    x = x_ref[0]
    hb = (x * _rms_scale(x) * nmix_ref[...]).astype(BF16)
    merged = None
    for n, y_ref in enumerate((ya_ref, yb_ref, yc_ref, ym_ref)):
        gate = jax.nn.sigmoid(_mm(hb, wg_ref[:, n * D_MODEL:(n + 1) * D_MODEL]))
        term = gate * _mm(y_ref[0], wl_ref[n])
        merged = term if merged is None else merged + term
    o_ref[0] = x + _mm(merged.astype(BF16), wo_ref[...])


def _merge(x, nmix, ya, yb, yc, ym, wg, wl, wo):
    b, s, _ = x.shape
    t = min(MERGE_TILE, s)
    tok = lambda width: pl.BlockSpec((1, t, width), lambda bi, i: (bi, i, 0))
    return pl.pallas_call(
        _merge_kernel,
        grid=(b, s // t),
        in_specs=[tok(D_MODEL), pl.BlockSpec((1, D_MODEL), lambda bi, i: (0, 0)),
                  tok(BRANCH_DIM), tok(BRANCH_DIM), tok(BRANCH_DIM), tok(BRANCH_DIM),
                  pl.BlockSpec((D_MODEL, N_BRANCH * D_MODEL), lambda bi, i: (0, 0)),
                  pl.BlockSpec((N_BRANCH, BRANCH_DIM, D_MODEL), lambda bi, i: (0, 0, 0)),
                  pl.BlockSpec((D_MODEL, D_MODEL), lambda bi, i: (0, 0))],
        out_specs=tok(D_MODEL),
        out_shape=jax.ShapeDtypeStruct((b, s, D_MODEL), F32),
        compiler_params=_params(2),
        name="merge",
    )(x, nmix, ya, yb, yc, ym, wg, wl, wo)


def _ffn_kernel(x_ref, nffn_ref, wup_ref, wdn_ref, o_ref):
    x = x_ref[0]
    hb = (x * _rms_scale(x) * nffn_ref[...]).astype(BF16)
    out = x
    for n in range(FFN_DIM // FFN_CHUNK):
        lo = n * FFN_CHUNK
        gate = _mm(hb, wup_ref[:, lo:lo + FFN_CHUNK])
        up = _mm(hb, wup_ref[:, FFN_DIM + lo:FFN_DIM + lo + FFN_CHUNK])
        act = (gate * jax.nn.sigmoid(gate) * up).astype(BF16)
        out = out + _mm(act, wdn_ref[lo:lo + FFN_CHUNK, :])
    o_ref[0] = out


def _ffn(x, nffn, wup, wdn):
    b, s, _ = x.shape
    t = min(FFN_TILE, s)
    tok = pl.BlockSpec((1, t, D_MODEL), lambda bi, i: (bi, i, 0))
    return pl.pallas_call(
        _ffn_kernel,
        grid=(b, s // t),
        in_specs=[tok, pl.BlockSpec((1, D_MODEL), lambda bi, i: (0, 0)),
                  pl.BlockSpec((D_MODEL, 2 * FFN_DIM), lambda bi, i: (0, 0)),
                  pl.BlockSpec((FFN_DIM, D_MODEL), lambda bi, i: (0, 0))],
        out_specs=tok,
        out_shape=jax.ShapeDtypeStruct((b, s, D_MODEL), F32),
        compiler_params=_params(2),
        name="ffn",
    )(x, nffn, wup, wdn)


def _rope_constants():
    inv_freq = 1.0 / (ROPE_THETA ** (jnp.arange(0, ROT_DIM, 2, dtype=F32) / ROT_DIM))
    invf = jnp.concatenate([inv_freq, inv_freq]).reshape(1, ROT_DIM)
    place = np.zeros((ROT_DIM, 3 * LANES), np.float32)
    ones = np.zeros((1, LANES), np.float32)
    for lane in range(LANES):
        d = lane % DSA_HEAD_DIM
        if d < ROT_HALF:
            place[d, lane] = 1.0
            place[ROT_HALF + d, LANES + lane] = -1.0
        elif d < ROT_DIM:
            place[d - ROT_HALF, lane] = 1.0
            place[d, 2 * LANES + lane] = 1.0
        else:
            ones[0, lane] = 1.0
    return invf, jnp.asarray(place, BF16), jnp.asarray(ones)


def _relayout_w_in(w_in):
    offs = np.concatenate([[0], np.cumsum(SPLIT_SIZES)])
    col = lambda n: w_in[:, int(offs[n]):int(offs[n + 1])]
    (a_x, a_b, a_c, d_q, d_k, d_v, i_q, i_k, i_w, g_q, g_f, g_i, g_g, m_q, gates) = (col(n) for n in range(15))
    pad = jnp.zeros((w_in.shape[0], LANES - DSA_HEAD_DIM - IDX_HEADS), w_in.dtype)
    w1 = jnp.concatenate([a_x, a_b, a_c, d_q, d_k, i_k, d_v, i_w, pad, i_q, g_q, g_f, g_i, g_g, m_q], axis=1)
    return w1.astype(BF16), gates.astype(BF16)


def kernel(x, mem, positions, norm_mix, w_in, conv_w, dsa_q_norm, dsa_k_norm, hgrn_lower_bounds,
           hgrn_out_norm, mem_norm, mem_w_kv, mem_q_norm, mem_k_norm, w_lift, w_out, norm_ffn,
           ffn_w_up, ffn_w_down):
    b, s, d = x.shape
    depth = w_in.shape[0]
    assert d == D_MODEL and w_in.shape[2] == sum(SPLIT_SIZES)
    assert s % HG_CHUNK == 0 and s % min(DSA_K_TILE, s) == 0
    topk = min(TOPK_MAX, s // 4)
    tk = min(DSA_K_TILE, s)

    invf, rplace, rones = _rope_constants()
    ang = positions.astype(F32)[..., None] * invf
    cs = jnp.where(jnp.arange(ROT_DIM) < ROT_HALF, jnp.cos(ang), jnp.sin(ang))
    lbs = jnp.cumsum(jax.nn.softmax(hgrn_lower_bounds.astype(F32), axis=0), axis=0)
    lbs = lbs - lbs[0:1]
    g64 = jnp.asarray(np.kron(np.eye(DSA_HEADS), np.full((DSA_HEAD_DIM, DSA_HEAD_DIM), 1.0 / DSA_HEAD_DIM)), BF16)
    row = lambda v: v.reshape(1, -1).astype(F32)

    for l in range(depth):
        w1, wg = _relayout_w_in(w_in[l])
        lb = lbs[l]
        lb_rows = jnp.concatenate([jnp.stack([jnp.log(lb), jnp.log1p(-lb), 1.0 - lb]),
                                   jnp.zeros((SUBLANES - 3, lb.shape[0]), F32)])
        convw = jnp.concatenate([conv_w[l], jnp.zeros((SUBLANES - CONV_WIDTH, CONV_DIM), F32)])
        mkt, mv = _mem_kv(mem, row(mem_norm), mem_w_kv[l].astype(BF16), row(mem_k_norm[l]))
        ya, q, iq, kk, vw, v1, zh, ym = _proj(
            x, row(norm_mix[l]), w1, cs, rplace, rones, convw,
            row(jnp.tile(dsa_q_norm[l], DSA_HEADS)),
            row(jnp.concatenate([dsa_k_norm[l], jnp.ones((LANES - DSA_HEAD_DIM,), F32)])),
            g64, mkt, mv, row(mem_q_norm[l]))
        kbd = _block_diag_keys(kk[..., :DSA_HEAD_DIM].astype(BF16), tk)
        ikbd = _block_diag_keys(kk[..., DSA_HEAD_DIM:].astype(BF16), tk)
        logit_bound = DSA_HEAD_DIM ** 0.5 * jnp.max(jnp.abs(dsa_q_norm[l])) * jnp.max(jnp.abs(dsa_k_norm[l]))
        fast = (logit_bound <= SAFE_LOGIT_BOUND).astype(jnp.int32).reshape(1)
        yb = _dsa(fast, q, iq, vw, kbd, ikbd, v1, topk)
        yc = _hgrn(zh, lb_rows, row(hgrn_out_norm[l]))
        x = _merge(x, row(norm_mix[l]), ya, yb, yc, ym, wg, w_lift[l].astype(BF16), w_out[l].astype(BF16))
        x = _ffn(x, row(norm_ffn[l]), ffn_w_up[l].astype(BF16), ffn_w_down[l].astype(BF16))
    return x
```

```python
import functools

import jax
import jax.numpy as jnp
import numpy as np
from jax import lax
from jax.experimental import pallas as pl
from jax.experimental.pallas import tpu as pltpu

D_MODEL = 1024
MEM_TOKENS = 256
N_BRANCH = 4
BRANCH_DIM = 512
CONV_DIM = 512
CONV_WIDTH = 3
DSA_HEADS = 8
DSA_HEAD_DIM = 64
IDX_HEADS = 8
IDX_DIM = 64
TOPK_MAX = 256
HG_HEADS = 4
HG_DK = 128
HG_DV = 128
HG_CHUNK = 64
MEM_HEADS = 4
MEM_HEAD_DIM = 128
ROPE_THETA = 500000.0
ROT_DIM = DSA_HEAD_DIM // 4
ROT_HALF = ROT_DIM // 2
FFN_DIM = ((8 * D_MODEL // 3 + 255) // 256) * 256
EPS = 1e-6

SPLIT_SIZES = (CONV_DIM, CONV_DIM, CONV_DIM,
               DSA_HEADS * DSA_HEAD_DIM, DSA_HEAD_DIM, DSA_HEAD_DIM,
               IDX_HEADS * IDX_DIM, IDX_DIM, IDX_HEADS,
               HG_HEADS * HG_DK, HG_HEADS * HG_DK, HG_HEADS * HG_DV, HG_HEADS * HG_DV,
               MEM_HEADS * MEM_HEAD_DIM,
               N_BRANCH * D_MODEL)

LANES = 128
SUBLANES = 8
VMEM_LIMIT_BYTES = 56 * 1024 * 1024

PROJ_TILE = 512
DSA_Q_TILE = 512
DSA_K_TILE = 512
HG_TILE = 512
MERGE_TILE = 512
FFN_TILE = 512
FFN_CHUNK = FFN_DIM // 2

GRP_A = 3 * CONV_DIM
GRP_Q = DSA_HEADS * DSA_HEAD_DIM
GRP_KV = 2 * LANES
GRP_IQ = IDX_HEADS * IDX_DIM
GRP_H = 4 * HG_HEADS * HG_DK
GRP_M = MEM_HEADS * MEM_HEAD_DIM
OFF_A = 0
OFF_Q = OFF_A + GRP_A
OFF_KV = OFF_Q + GRP_Q
OFF_IQ = OFF_KV + GRP_KV
OFF_H = OFF_IQ + GRP_IQ
OFF_M = OFF_H + GRP_H
PROJ_COLS = OFF_M + GRP_M

NEG_BIG = -1e30
INT_MIN = -(2 ** 31)
LOG2E = 1.4426950408889634
Q_SCALE = DSA_HEAD_DIM ** -0.5 * LOG2E
SAFE_LOGIT_BOUND = 40.0

BF16 = jnp.bfloat16
F32 = jnp.float32


def _mm(a, b):
    return jnp.dot(a, b, preferred_element_type=F32)


def _mm_nt(a, b):
    return lax.dot_general(a, b, (((1,), (1,)), ((), ())), preferred_element_type=F32)


def _mm_tn(a, b):
    return lax.dot_general(a, b, (((0,), (0,)), ((), ())), preferred_element_type=F32)


def _rms_scale(x):
    return lax.rsqrt(jnp.mean(x * x, axis=-1, keepdims=True) + EPS)


def _params(n_grid):
    return pltpu.CompilerParams(dimension_semantics=("arbitrary",) * n_grid,
                                vmem_limit_bytes=VMEM_LIMIT_BYTES)


def _mem_kv_kernel(mem_ref, mem_norm_ref, wkv_ref, knorm_ref, mkt_ref, mv_ref):
    m = mem_ref[0]
    mn = (m * _rms_scale(m) * mem_norm_ref[...]).astype(BF16)
    kv = _mm(mn, wkv_ref[...])
    for h in range(MEM_HEADS):
        kh = kv[:, h * MEM_HEAD_DIM:(h + 1) * MEM_HEAD_DIM]
        kh = kh * _rms_scale(kh) * knorm_ref[...]
        mkt_ref[0, h] = kh.T.astype(BF16)
        off = MEM_HEADS * MEM_HEAD_DIM + h * MEM_HEAD_DIM
        mv_ref[0, h] = kv[:, off:off + MEM_HEAD_DIM].astype(BF16)


def _mem_kv(mem, mem_norm, wkv, knorm):
    b = mem.shape[0]
    return pl.pallas_call(
        _mem_kv_kernel,
        grid=(b,),
        in_specs=[
            pl.BlockSpec((1, MEM_TOKENS, D_MODEL), lambda i: (i, 0, 0)),
            pl.BlockSpec((1, D_MODEL), lambda i: (0, 0)),
            pl.BlockSpec((D_MODEL, 2 * GRP_M), lambda i: (0, 0)),
            pl.BlockSpec((1, MEM_HEAD_DIM), lambda i: (0, 0)),
        ],
        out_specs=[
            pl.BlockSpec((1, MEM_HEADS, MEM_HEAD_DIM, MEM_TOKENS), lambda i: (i, 0, 0, 0)),
            pl.BlockSpec((1, MEM_HEADS, MEM_TOKENS, MEM_HEAD_DIM), lambda i: (i, 0, 0, 0)),
        ],
        out_shape=[
            jax.ShapeDtypeStruct((b, MEM_HEADS, MEM_HEAD_DIM, MEM_TOKENS), BF16),
            jax.ShapeDtypeStruct((b, MEM_HEADS, MEM_TOKENS, MEM_HEAD_DIM), BF16),
        ],
        compiler_params=_params(1),
        name="mem_kv",
    )(mem, mem_norm, wkv, knorm)


def _rope_slab(s, c, sa, sb):
    return s * c + pltpu.roll(s, LANES - ROT_HALF, 1) * sa + pltpu.roll(s, ROT_HALF, 1) * sb


def _proj_kernel(x_ref, nmix_ref, w_ref, cs_ref, rplace_ref, rones_ref, convw_ref, qg_ref, kg_ref,
                 g64_ref, mkt_ref, mv_ref, mqg_ref,
                 ya_ref, q_ref, iq_ref, kk_ref, vw_ref, v1_ref, zh_ref, ym_ref,
                 carry_ref):
    i = pl.program_id(1)
    t = x_ref.shape[1]
    x = x_ref[0]
    hb = (x * _rms_scale(x) * nmix_ref[...]).astype(BF16)

    cs_hi, cs_mid, cs_lo = _split3_bf16(cs_ref[0])
    place = rplace_ref[...]
    tabs = _mm(cs_hi, place) + _mm(cs_mid, place) + _mm(cs_lo, place)
    rc = tabs[:, :LANES] + rones_ref[...]
    rsa = tabs[:, LANES:2 * LANES]
    rsb = tabs[:, 2 * LANES:]

    za = _mm(hb, w_ref[:, OFF_A:OFF_A + GRP_A])
    a_x, a_b, a_c = za[:, :CONV_DIM], za[:, CONV_DIM:2 * CONV_DIM], za[:, 2 * CONV_DIM:]
    u = a_c * a_x

    @pl.when(i == 0)
    def _():
        carry_ref[...] = jnp.zeros_like(carry_ref)

    carry = carry_ref[...]
    row8 = lax.broadcasted_iota(jnp.int32, (SUBLANES, CONV_DIM), 0)
    r1 = pltpu.roll(u, 1, 0)
    r2 = pltpu.roll(u, 2, 0)
    top1 = jnp.where(row8 < 1, pltpu.roll(carry, 1, 0), r1[:SUBLANES])
    top2 = jnp.where(row8 < 2, pltpu.roll(carry, 2, 0), r2[:SUBLANES])
    u1 = jnp.concatenate([top1, r1[SUBLANES:]], axis=0)
    u2 = jnp.concatenate([top2, r2[SUBLANES:]], axis=0)
    carry_ref[...] = u[t - SUBLANES:]
    cw = convw_ref[...]
    ya_ref[0] = (a_b * (u2 * cw[0:1] + u1 * cw[1:2] + u * cw[2:3])).astype(BF16)

    zq = _mm(hb, w_ref[:, OFF_Q:OFF_Q + GRP_Q])
    msq = _mm((zq * zq).astype(BF16), g64_ref[...])
    qn = zq * lax.rsqrt(msq + EPS) * qg_ref[...]
    for p in range(GRP_Q // LANES):
        sl = slice(p * LANES, (p + 1) * LANES)
        q_ref[0, :, sl] = (_rope_slab(qn[:, sl], rc, rsa, rsb) * Q_SCALE).astype(BF16)

    zkv = _mm(hb, w_ref[:, OFF_KV:OFF_KV + GRP_KV])
    s0, s1 = zkv[:, :LANES], zkv[:, LANES:]
    lane = lax.broadcasted_iota(jnp.int32, (t, LANES), 1)
    is_k = lane < DSA_HEAD_DIM
    kms = jnp.sum(jnp.where(is_k, s0 * s0, 0.0), axis=-1, keepdims=True) * (1.0 / DSA_HEAD_DIM)
    s0 = s0 * jnp.where(is_k, lax.rsqrt(kms + EPS) * kg_ref[...], 1.0)
    kk_ref[0] = _rope_slab(s0, rc, rsa, rsb)
    w_scale = jnp.where(lane >= DSA_HEAD_DIM, jnp.where(lane < DSA_HEAD_DIM + IDX_HEADS, IDX_HEADS ** -0.5, 1.0), 1.0)
    vw_ref[0] = s1 * w_scale
    v1_ref[0] = jnp.where(is_k, s1, jnp.where(lane == DSA_HEAD_DIM, 1.0, 0.0)).astype(BF16)

    ziq = _mm(hb, w_ref[:, OFF_IQ:OFF_IQ + GRP_IQ])
    for p in range(GRP_IQ // LANES):
        sl = slice(p * LANES, (p + 1) * LANES)
        iq_ref[0, :, sl] = (_rope_slab(ziq[:, sl], rc, rsa, rsb) * (IDX_DIM ** -0.5)).astype(BF16)

    zh_ref[0] = _mm(hb, w_ref[:, OFF_H:OFF_H + GRP_H])

    zm = _mm(hb, w_ref[:, OFF_M:OFF_M + GRP_M])
    for h in range(MEM_HEADS):
        sl = slice(h * MEM_HEAD_DIM, (h + 1) * MEM_HEAD_DIM)
        mq = zm[:, sl]
        mq = mq * _rms_scale(mq) * mqg_ref[...] * (MEM_HEAD_DIM ** -0.5)
        lg = _mm(mq.astype(BF16), mkt_ref[0, h])
        pe = jnp.exp(lg - jnp.max(lg, axis=-1, keepdims=True))
        den = jnp.sum(pe, axis=-1, keepdims=True)
        ym_ref[0, :, sl] = (_mm(pe.astype(BF16), mv_ref[0, h]) / den).astype(BF16)


def _proj(x, nmix, w1, cs, rplace, rones, convw, qg, kg, g64, mkt, mv, mqg):
    b, s, _ = x.shape
    t = min(PROJ_TILE, s)
    tok = lambda width: pl.BlockSpec((1, t, width), lambda bi, i: (bi, i, 0))
    const2 = lambda shape: pl.BlockSpec(shape, lambda bi, i: (0, 0))
    per_b4 = lambda shape: pl.BlockSpec((1,) + shape, lambda bi, i: (bi, 0, 0, 0))
    out_widths = (CONV_DIM, GRP_Q, GRP_IQ, LANES, LANES, LANES, GRP_H, GRP_M)
    out_dtypes = (BF16, BF16, BF16, F32, F32, BF16, F32, BF16)
    return pl.pallas_call(
        _proj_kernel,
        grid=(b, s // t),
        in_specs=[
            tok(D_MODEL), const2((1, D_MODEL)), const2((D_MODEL, PROJ_COLS)),
            tok(ROT_DIM), const2((ROT_DIM, 3 * LANES)), const2((1, LANES)),
            const2((SUBLANES, CONV_DIM)), const2((1, GRP_Q)), const2((1, LANES)),
            const2((GRP_Q, GRP_Q)),
            per_b4((MEM_HEADS, MEM_HEAD_DIM, MEM_TOKENS)), per_b4((MEM_HEADS, MEM_TOKENS, MEM_HEAD_DIM)),
            const2((1, MEM_HEAD_DIM)),
        ],
        out_specs=[tok(w) for w in out_widths],
        out_shape=[jax.ShapeDtypeStruct((b, s, w), d) for w, d in zip(out_widths, out_dtypes)],
        scratch_shapes=[pltpu.VMEM((SUBLANES, CONV_DIM), F32)],
        compiler_params=_params(2),
        name="proj",
    )(x, nmix, w1, cs, rplace, rones, convw, qg, kg, g64, mkt, mv, mqg)


def _bit_transpose32(words):
    a = list(words)
    for j, m in ((16, 0x0000FFFF), (8, 0x00FF00FF), (4, 0x0F0F0F0F), (2, 0x33333333), (1, 0x55555555)):
        for k in range(32):
            if k & j:
                continue
            tmp = (a[k] ^ lax.shift_right_logical(a[k + j], jnp.int32(j))) & m
            a[k] = a[k] ^ tmp
            a[k + j] = a[k + j] ^ lax.shift_left(tmp, jnp.int32(j))
    return a


def _low_bits(n):
    return jnp.where(n >= 32, -1, jnp.where(n <= 0, 0, lax.shift_left(jnp.int32(1), n) - 1))


def _dsa_kernel(fast_ref, q_ref, iq_ref, vw_ref, kbd_ref, ikbd_ref, v1_ref, o_ref,
                key_ref, plane_ref, cand_ref, sel_ref, m_ref, acc_ref, *, topk, seq_len):
    i = pl.program_id(1)
    t = q_ref.shape[1]
    gt, _, tk = key_ref.shape
    n_groups = plane_ref.shape[0]
    cpt = tk // LANES
    n_pairs = DSA_HEADS // 2
    q0 = i * t
    nj = (q0 + t - 1) // tk + 1
    ng = (nj + gt - 1) // gt

    @pl.when(jnp.logical_and(pl.program_id(0) == 0, i == 0))
    def _():
        plane_ref[...] = jnp.zeros_like(plane_ref)

    iw = vw_ref[0][:, DSA_HEAD_DIM:DSA_HEAD_DIM + IDX_HEADS]

    def score_group(g, carry):
        nt = jnp.minimum(nj - g * gt, gt)

        def score_tile(jj, c2):
            sc = jnp.zeros((t, tk), F32)
            for p in range(n_pairs):
                r = _mm(iq_ref[0, :, p * LANES:(p + 1) * LANES], ikbd_ref[0, g * gt + jj])
                sc = sc + jnp.maximum(r[:, :tk], 0.0) * iw[:, 2 * p:2 * p + 1]
                sc = sc + jnp.maximum(r[:, tk:], 0.0) * iw[:, 2 * p + 1:2 * p + 2]
            sc = jnp.where(sc == 0.0, 0.0, sc)
            bits = pltpu.bitcast(sc, jnp.int32)
            key_ref[jj] = bits ^ ((bits >> 31) | INT_MIN)
            return c2

        lax.fori_loop(0, nt, score_tile, 0)

        def clear_tile(jj, c2):
            key_ref[jj] = jnp.zeros((t, tk), jnp.int32)
            return c2

        lax.fori_loop(nt, gt, clear_tile, 0)

        def to_planes(rg, c2):
            rows = pl.ds(pl.multiple_of(rg * SUBLANES, SUBLANES), SUBLANES)
            words = []
            for k in range(32):
                ch = 31 - k
                words.append(key_ref[ch // cpt, rows, (ch % cpt) * LANES:(ch % cpt + 1) * LANES])
            for p, w in enumerate(_bit_transpose32(words)):
                plane_ref[g, p, rows, :] = w
            return c2

        lax.fori_loop(0, t // SUBLANES, to_planes, 0)
        return carry

    lax.fori_loop(0, ng, score_group, 0)

    lane = lax.broadcasted_iota(jnp.int32, (t, LANES), 1)
    qpos = lax.broadcasted_iota(jnp.int32, (t, LANES), 0) + q0
    chunks_valid = ((qpos - lane) >> 7) + 1

    def count_bits(words):
        tot = lax.population_count(words[0]).astype(F32)
        for w in words[1:]:
            tot = tot + lax.population_count(w).astype(F32)
        return jnp.sum(tot, axis=-1, keepdims=True)

    for g in range(n_groups):
        cand_ref[g] = _low_bits(chunks_valid - 32 * g)
        sel_ref[g] = jnp.zeros((t, LANES), jnp.int32)

    def radix_select(groups):
        def radix(p, need):
            ones = [cand_ref[g] & plane_ref[g, p] for g in groups]
            c1 = count_bits(ones)
            take = c1 >= need
            for g, one in zip(groups, ones):
                cand = cand_ref[g]
                cand_ref[g] = jnp.where(take, one, cand ^ one)
                sel_ref[g] = jnp.where(take, sel_ref[g], sel_ref[g] | one)
            return jnp.where(take, need, need - c1)

        return lax.fori_loop(0, 32, radix, jnp.full((t, 1), float(topk), F32))

    if n_groups == 1:
        need = radix_select(range(1))
    else:
        need = lax.switch(ng - 1, [functools.partial(radix_select, range(n)) for n in range(1, n_groups + 1)])

    has_tie = jnp.where(count_bits([cand_ref[g] for g in range(n_groups)]) > need, 1.0, 0.0)
    any_tie = jnp.max(has_tie) > 0.0

    @pl.when(jnp.logical_not(any_tie))
    def _():
        for g in range(n_groups):
            sel_ref[g] = sel_ref[g] | cand_ref[g]

    @pl.when(any_tie)
    def _():
        n_bits = max(1, int(seq_len - 1).bit_length())

        def below(g, cut):
            return cand_ref[g] & _low_bits(((cut - lane + (LANES - 1)) >> 7) - 32 * g)

        def bisect_idx(bit, pfx):
            cut = pfx | (jnp.int32(1) << (n_bits - 1 - bit))
            cnt = count_bits([below(g, cut) for g in range(n_groups)])
            return jnp.where(cnt < need, cut, pfx)

        cut = lax.fori_loop(0, n_bits, bisect_idx, jnp.zeros((t, 1), jnp.int32))
        cut = jnp.where(has_tie > 0.0, cut + 1, seq_len)
        for g in range(n_groups):
            sel_ref[g] = sel_ref[g] | below(g, cut)

    neg_bits = int(np.float32(NEG_BIG).view(np.int32))

    def tile_bias(j):
        words = sel_ref[j // gt]
        parts = []
        for c in range(cpt):
            k = (j % gt) * cpt + c
            picked = lax.shift_left(words, 31 - k) >> 31
            parts.append(pltpu.bitcast(neg_bits & ~picked, F32))
        return jnp.concatenate(parts, axis=-1)

    acc_ref[...] = jnp.zeros(acc_ref.shape, F32)
    use_fast = fast_ref[0] > 0

    @pl.when(use_fast)
    def _():
        def attend(j, carry):
            bias = tile_bias(j)
            for p in range(n_pairs):
                lg = _mm(q_ref[0, :, p * LANES:(p + 1) * LANES], kbd_ref[0, j])
                for e in range(2):
                    pe = jnp.exp2(lg[:, e * tk:(e + 1) * tk] + bias).astype(BF16)
                    acc_ref[2 * p + e] += _mm(pe, v1_ref[0, j])
            return carry

        lax.fori_loop(0, nj, attend, 0)

    @pl.when(jnp.logical_not(use_fast))
    def _():
        m_ref[...] = jnp.full(m_ref.shape, NEG_BIG, F32)

        def attend(j, carry):
            bias = tile_bias(j)
            for p in range(n_pairs):
                lg = _mm(q_ref[0, :, p * LANES:(p + 1) * LANES], kbd_ref[0, j])
                for e in range(2):
                    h = 2 * p + e
                    s = lg[:, e * tk:(e + 1) * tk] + bias
                    m_old = m_ref[h][:, :1]
                    m_new = jnp.maximum(m_old, jnp.max(s, axis=-1, keepdims=True))
                    pe = jnp.exp2(s - m_new).astype(BF16)
                    acc_ref[h] = acc_ref[h] * jnp.exp2(m_old - m_new) + _mm(pe, v1_ref[0, j])
                    m_ref[h] = jnp.broadcast_to(m_new, (t, LANES))
            return carry

        lax.fori_loop(0, nj, attend, 0)

    first_head = lax.broadcasted_iota(jnp.int32, (t, LANES), 1) < DSA_HEAD_DIM
    for p in range(n_pairs):
        a0, a1 = acc_ref[2 * p], acc_ref[2 * p + 1]
        o0 = a0 / a0[:, DSA_HEAD_DIM:DSA_HEAD_DIM + 1]
        o1 = a1 / a1[:, DSA_HEAD_DIM:DSA_HEAD_DIM + 1]
        o_ref[0, :, p * LANES:(p + 1) * LANES] = jnp.where(
            first_head, o0, pltpu.roll(o1, DSA_HEAD_DIM, 1)).astype(BF16)


def _dsa(fast, q, iq, vw, kbd, ikbd, v1, topk):
    b, s, _ = q.shape
    t = min(DSA_Q_TILE, s)
    nj, _, tk2 = kbd.shape[1:]
    tk = tk2 // 2
    gt = 32 * LANES // tk
    n_groups = -(-nj // gt)
    tok = lambda width: pl.BlockSpec((1, t, width), lambda bi, i: (bi, i, 0))
    per_b = lambda shape: pl.BlockSpec((1,) + shape, lambda bi, i: (bi, 0, 0, 0), pipeline_mode=pl.Buffered(1))
    return pl.pallas_call(
        functools.partial(_dsa_kernel, topk=topk, seq_len=s),
        grid=(b, s // t),
        in_specs=[pl.BlockSpec(memory_space=pltpu.SMEM),
                  tok(GRP_Q), tok(GRP_IQ), tok(LANES),
                  per_b((nj, LANES, tk2)), per_b((nj, LANES, tk2)), per_b((nj, tk, LANES))],
        out_specs=tok(GRP_Q),
        out_shape=jax.ShapeDtypeStruct((b, s, GRP_Q), BF16),
        scratch_shapes=[
            pltpu.VMEM((gt, t, tk), jnp.int32),
            pltpu.VMEM((n_groups, 32, t, LANES), jnp.int32),
            pltpu.VMEM((n_groups, t, LANES), jnp.int32),
            pltpu.VMEM((n_groups, t, LANES), jnp.int32),
            pltpu.VMEM((DSA_HEADS, t, LANES), F32),
            pltpu.VMEM((DSA_HEADS, t, LANES), F32),
        ],
        compiler_params=_params(2),
        name="dsa",
    )(fast, q, iq, vw, kbd, ikbd, v1.reshape(b, nj, tk, LANES))


def _block_diag_keys(kmat, tk):
    b, s, d = kmat.shape
    kt = kmat.reshape(b, s // tk, tk, d).transpose(0, 1, 3, 2)
    z = jnp.zeros_like(kt)
    return jnp.concatenate([jnp.concatenate([kt, z], axis=-1), jnp.concatenate([z, kt], axis=-1)], axis=2)


def _split3_bf16(x):
    hi = x.astype(BF16)
    r1 = x - hi.astype(F32)
    mid = r1.astype(BF16)
    lo = (r1 - mid.astype(F32)).astype(BF16)
    return hi, mid, lo


def _hgrn_kernel(zh_ref, lb_ref, onorm_ref, o_ref, state_ref):
    i = pl.program_id(1)
    t = zh_ref.shape[1]
    c = HG_CHUNK
    width = HG_HEADS * HG_DK

    @pl.when(i == 0)
    def _():
        state_ref[...] = jnp.zeros_like(state_ref)

    rr = lax.broadcasted_iota(jnp.int32, (c, c), 0)
    cc = lax.broadcasted_iota(jnp.int32, (c, c), 1)
    tril = rr >= cc
    tril_b = jnp.where(tril, 1.0, 0.0).astype(BF16)

    n_chunks = t // c
    chunk = lambda n: slice(n * c, (n + 1) * c)
    head = lambda h: slice(h * HG_DK, (h + 1) * HG_DK)
    zq, zf, v, zg = (zh_ref[0, :, g * width:(g + 1) * width] for g in range(4))
    lb = lb_ref[...]
    log_lb, log1m_lb, one_m_lb = lb[0:1], lb[1:2], lb[2:3]

    e = jnp.exp(-jnp.abs(zf))
    b_ = log1m_lb + jnp.minimum(zf, 0.0) - jnp.log1p(e)
    log_f = jnp.maximum(log_lb, b_) + jnp.log1p(jnp.exp(-jnp.abs(log_lb - b_)))
    hk = one_m_lb * jnp.where(zf >= 0.0, e, 1.0) / (1.0 + e)
    hq = zq * jax.nn.sigmoid(zq)
    vb = v.astype(BF16)
    hi, mid, lo = _split3_bf16(log_f)
    cums = [_mm(tril_b, hi[chunk(n)]) + _mm(tril_b, mid[chunk(n)]) + _mm(tril_b, lo[chunk(n)])
            for n in range(n_chunks)]
    q_intra, k_intra, q_inter, k_state, decay = [], [], [], [], []
    for n in range(n_chunks):
        cum = cums[n]
        ref = cum[c // 2 - 1:c // 2]
        last = cum[c - 1:c]
        q_intra.append((hq[chunk(n)] * jnp.exp(cum - ref)).astype(BF16))
        k_intra.append((hk[chunk(n)] * jnp.exp(ref - cum)).astype(BF16))
        q_inter.append((hq[chunk(n)] * jnp.exp(cum)).astype(BF16))
        k_state.append((hk[chunk(n)] * jnp.exp(last - cum)).astype(BF16))
        decay.append(jnp.exp(last))
    att = [[_mm_nt(q_intra[n][:, head(h)], k_intra[n][:, head(h)]) for h in range(HG_HEADS)]
           for n in range(n_chunks)]
    o_intra = [[_mm(jnp.where(tril, att[n][h], 0.0).astype(BF16), vb[chunk(n), head(h)])
                for h in range(HG_HEADS)] for n in range(n_chunks)]
    states = [state_ref[h] for h in range(HG_HEADS)]
    o_inter = []
    for n in range(n_chunks):
        o_inter.append([_mm_nt(q_inter[n][:, head(h)], states[h].astype(BF16)) for h in range(HG_HEADS)])
        states = [states[h] * decay[n][:, head(h)] + _mm_tn(vb[chunk(n), head(h)], k_state[n][:, head(h)])
                  for h in range(HG_HEADS)]
    for h in range(HG_HEADS):
        state_ref[h] = states[h]
    gate = zg * jax.nn.sigmoid(zg)
    for n in range(n_chunks):
        for h in range(HG_HEADS):
            o = o_intra[n][h] + o_inter[n][h]
            o = o * _rms_scale(o) * onorm_ref[...]
            o_ref[0, chunk(n), head(h)] = (o * gate[chunk(n), head(h)]).astype(BF16)


def _hgrn(zh, lb_rows, onorm):
    b, s, _ = zh.shape
    t = min(HG_TILE, s)
    return pl.pallas_call(
        _hgrn_kernel,
        grid=(b, s // t),
        in_specs=[pl.BlockSpec((1, t, GRP_H), lambda bi, i: (bi, i, 0)),
                  pl.BlockSpec((SUBLANES, HG_HEADS * HG_DK), lambda bi, i: (0, 0)),
                  pl.BlockSpec((1, HG_DV), lambda bi, i: (0, 0))],
        out_specs=pl.BlockSpec((1, t, HG_HEADS * HG_DV), lambda bi, i: (bi, i, 0)),
        out_shape=jax.ShapeDtypeStruct((b, s, HG_HEADS * HG_DV), BF16),
        scratch_shapes=[pltpu.VMEM((HG_HEADS, HG_DV, HG_DK), F32)],
        compiler_params=_params(2),
        name="hgrn",
    )(zh, lb_rows, onorm)


def _merge_kernel(x_ref, nmix_ref, ya_ref, yb_ref, yc_ref, ym_ref, wg_ref, wl_ref, wo_ref, o_ref):
    x = x_ref[0]
    hb = (x * _rms_scale(x) * nmix_ref[...]).astype(BF16)
    merged = None
    for n, y_ref in enumerate((ya_ref, yb_ref, yc_ref, ym_ref)):
        gate = jax.nn.sigmoid(_mm(hb, wg_ref[:, n * D_MODEL:(n + 1) * D_MODEL]))
        term = gate * _mm(y_ref[0], wl_ref[n])
        merged = term if merged is None else merged + term
    o_ref[0] = x + _mm(merged.astype(BF16), wo_ref[...])


def _merge(x, nmix, ya, yb, yc, ym, wg, wl, wo):
    b, s, _ = x.shape
    t = min(MERGE_TILE, s)
    tok = lambda width: pl.BlockSpec((1, t, width), lambda bi, i: (bi, i, 0))
    return pl.pallas_call(
        _merge_kernel,
        grid=(b, s // t),
        in_specs=[tok(D_MODEL), pl.BlockSpec((1, D_MODEL), lambda bi, i: (0, 0)),
                  tok(BRANCH_DIM), tok(BRANCH_DIM), tok(BRANCH_DIM), tok(BRANCH_DIM),
                  pl.BlockSpec((D_MODEL, N_BRANCH * D_MODEL), lambda bi, i: (0, 0)),
                  pl.BlockSpec((N_BRANCH, BRANCH_DIM, D_MODEL), lambda bi, i: (0, 0, 0)),
                  pl.BlockSpec((D_MODEL, D_MODEL), lambda bi, i: (0, 0))],
        out_specs=tok(D_MODEL),
        out_shape=jax.ShapeDtypeStruct((b, s, D_MODEL), F32),
        compiler_params=_params(2),
        name="merge",
    )(x, nmix, ya, yb, yc, ym, wg, wl, wo)


def _ffn_kernel(x_ref, nffn_ref, wup_ref, wdn_ref, o_ref):
    x = x_ref[0]
    hb = (x * _rms_scale(x) * nffn_ref[...]).astype(BF16)
    out = x
    for n in range(FFN_DIM // FFN_CHUNK):
        lo = n * FFN_CHUNK
        gate = _mm(hb, wup_ref[:, lo:lo + FFN_CHUNK])
        up = _mm(hb, wup_ref[:, FFN_DIM + lo:FFN_DIM + lo + FFN_CHUNK])
        act = (gate * jax.nn.sigmoid(gate) * up).astype(BF16)
        out = out + _mm(act, wdn_ref[lo:lo + FFN_CHUNK, :])
    o_ref[0] = out


def _ffn(x, nffn, wup, wdn):
    b, s, _ = x.shape
    t = min(FFN_TILE, s)
    tok = pl.BlockSpec((1, t, D_MODEL), lambda bi, i: (bi, i, 0))
    return pl.pallas_call(
        _ffn_kernel,
        grid=(b, s // t),
        in_specs=[tok, pl.BlockSpec((1, D_MODEL), lambda bi, i: (0, 0)),
                  pl.BlockSpec((D_MODEL, 2 * FFN_DIM), lambda bi, i: (0, 0)),
                  pl.BlockSpec((FFN_DIM, D_MODEL), lambda bi, i: (0, 0))],
        out_specs=tok,
        out_shape=jax.ShapeDtypeStruct((b, s, D_MODEL), F32),
        compiler_params=_params(2),
        name="ffn",
    )(x, nffn, wup, wdn)


def _rope_constants():
    inv_freq = 1.0 / (ROPE_THETA ** (jnp.arange(0, ROT_DIM, 2, dtype=F32) / ROT_DIM))
    invf = jnp.concatenate([inv_freq, inv_freq]).reshape(1, ROT_DIM)
    place = np.zeros((ROT_DIM, 3 * LANES), np.float32)
    ones = np.zeros((1, LANES), np.float32)
    for lane in range(LANES):
        d = lane % DSA_HEAD_DIM
        if d < ROT_HALF:
            place[d, lane] = 1.0
            place[ROT_HALF + d, LANES + lane] = -1.0
        elif d < ROT_DIM:
            place[d - ROT_HALF, lane] = 1.0
            place[d, 2 * LANES + lane] = 1.0
        else:
            ones[0, lane] = 1.0
    return invf, jnp.asarray(place, BF16), jnp.asarray(ones)


def _relayout_w_in(w_in):
    offs = np.concatenate([[0], np.cumsum(SPLIT_SIZES)])
    col = lambda n: w_in[:, int(offs[n]):int(offs[n + 1])]
    (a_x, a_b, a_c, d_q, d_k, d_v, i_q, i_k, i_w, g_q, g_f, g_i, g_g, m_q, gates) = (col(n) for n in range(15))
    pad = jnp.zeros((w_in.shape[0], LANES - DSA_HEAD_DIM - IDX_HEADS), w_in.dtype)
    w1 = jnp.concatenate([a_x, a_b, a_c, d_q, d_k, i_k, d_v, i_w, pad, i_q, g_q, g_f, g_i, g_g, m_q], axis=1)
    return w1.astype(BF16), gates.astype(BF16)


def kernel(x, mem, positions, norm_mix, w_in, conv_w, dsa_q_norm, dsa_k_norm, hgrn_lower_bounds,
           hgrn_out_norm, mem_norm, mem_w_kv, mem_q_norm, mem_k_norm, w_lift, w_out, norm_ffn,
           ffn_w_up, ffn_w_down):
    b, s, d = x.shape
    depth = w_in.shape[0]
    assert d == D_MODEL and w_in.shape[2] == sum(SPLIT_SIZES)
    assert s % HG_CHUNK == 0 and s % min(DSA_K_TILE, s) == 0
    topk = min(TOPK_MAX, s // 4)
    tk = min(DSA_K_TILE, s)

    invf, rplace, rones = _rope_constants()
    ang = positions.astype(F32)[..., None] * invf
    cs = jnp.where(jnp.arange(ROT_DIM) < ROT_HALF, jnp.cos(ang), jnp.sin(ang))
    lbs = jnp.cumsum(jax.nn.softmax(hgrn_lower_bounds.astype(F32), axis=0), axis=0)
    lbs = lbs - lbs[0:1]
    g64 = jnp.asarray(np.kron(np.eye(DSA_HEADS), np.full((DSA_HEAD_DIM, DSA_HEAD_DIM), 1.0 / DSA_HEAD_DIM)), BF16)
    row = lambda v: v.reshape(1, -1).astype(F32)

    for l in range(depth):
        w1, wg = _relayout_w_in(w_in[l])
        lb = lbs[l]
        lb_rows = jnp.concatenate([jnp.stack([jnp.log(lb), jnp.log1p(-lb), 1.0 - lb]),
                                   jnp.zeros((SUBLANES - 3, lb.shape[0]), F32)])
        convw = jnp.concatenate([conv_w[l], jnp.zeros((SUBLANES - CONV_WIDTH, CONV_DIM), F32)])
        mkt, mv = _mem_kv(mem, row(mem_norm), mem_w_kv[l].astype(BF16), row(mem_k_norm[l]))
        ya, q, iq, kk, vw, v1, zh, ym = _proj(
            x, row(norm_mix[l]), w1, cs, rplace, rones, convw,
            row(jnp.tile(dsa_q_norm[l], DSA_HEADS)),
            row(jnp.concatenate([dsa_k_norm[l], jnp.ones((LANES - DSA_HEAD_DIM,), F32)])),
            g64, mkt, mv, row(mem_q_norm[l]))
        kbd = _block_diag_keys(kk[..., :DSA_HEAD_DIM].astype(BF16), tk)
        ikbd = _block_diag_keys(kk[..., DSA_HEAD_DIM:].astype(BF16), tk)
        logit_bound = DSA_HEAD_DIM ** 0.5 * jnp.max(jnp.abs(dsa_q_norm[l])) * jnp.max(jnp.abs(dsa_k_norm[l]))
        fast = (logit_bound <= SAFE_LOGIT_BOUND).astype(jnp.int32).reshape(1)
        yb = _dsa(fast, q, iq, vw, kbd, ikbd, v1, topk)
        yc = _hgrn(zh, lb_rows, row(hgrn_out_norm[l]))
        x = _merge(x, row(norm_mix[l]), ya, yb, yc, ym, wg, w_lift[l].astype(BF16), w_out[l].astype(BF16))
        x = _ffn(x, row(norm_ffn[l]), ffn_w_up[l].astype(BF16), ffn_w_down[l].astype(BF16))
    return x
```

```python
import functools

import jax
import jax.numpy as jnp
import numpy as np
from jax import lax
from jax.experimental import pallas as pl
from jax.experimental.pallas import tpu as pltpu

D_MODEL = 1024
MEM_TOKENS = 256
N_BRANCH = 4
BRANCH_DIM = 512
CONV_DIM = 512
CONV_WIDTH = 3
DSA_HEADS = 8
DSA_HEAD_DIM = 64
IDX_HEADS = 8
IDX_DIM = 64
TOPK_MAX = 256
HG_HEADS = 4
HG_DK = 128
HG_DV = 128
HG_CHUNK = 64
MEM_HEADS = 4
MEM_HEAD_DIM = 128
ROPE_THETA = 500000.0
ROT_DIM = DSA_HEAD_DIM // 4
ROT_HALF = ROT_DIM // 2
FFN_DIM = ((8 * D_MODEL // 3 + 255) // 256) * 256
EPS = 1e-6

SPLIT_SIZES = (CONV_DIM, CONV_DIM, CONV_DIM,
               DSA_HEADS * DSA_HEAD_DIM, DSA_HEAD_DIM, DSA_HEAD_DIM,
               IDX_HEADS * IDX_DIM, IDX_DIM, IDX_HEADS,
               HG_HEADS * HG_DK, HG_HEADS * HG_DK, HG_HEADS * HG_DV, HG_HEADS * HG_DV,
               MEM_HEADS * MEM_HEAD_DIM,
               N_BRANCH * D_MODEL)

LANES = 128
SUBLANES = 8
VMEM_LIMIT_BYTES = 56 * 1024 * 1024

PROJ_TILE = 512
DSA_Q_TILE = 512
DSA_K_TILE = 512
HG_TILE = 512
MERGE_TILE = 512
FFN_TILE = 512
FFN_CHUNK = FFN_DIM // 2

GRP_A = 3 * CONV_DIM
GRP_Q = DSA_HEADS * DSA_HEAD_DIM
GRP_KV = 2 * LANES
GRP_IQ = IDX_HEADS * IDX_DIM
GRP_H = 4 * HG_HEADS * HG_DK
GRP_M = MEM_HEADS * MEM_HEAD_DIM
OFF_A = 0
OFF_Q = OFF_A + GRP_A
OFF_KV = OFF_Q + GRP_Q
OFF_IQ = OFF_KV + GRP_KV
OFF_H = OFF_IQ + GRP_IQ
OFF_M = OFF_H + GRP_H
PROJ_COLS = OFF_M + GRP_M

NEG_BIG = -1e30
INT_MIN = -(2 ** 31)
LOG2E = 1.4426950408889634
Q_SCALE = DSA_HEAD_DIM ** -0.5 * LOG2E
SAFE_LOGIT_BOUND = 40.0

BF16 = jnp.bfloat16
F32 = jnp.float32


def _mm(a, b):
    return jnp.dot(a, b, preferred_element_type=F32)


def _mm_nt(a, b):
    return lax.dot_general(a, b, (((1,), (1,)), ((), ())), preferred_element_type=F32)


def _mm_tn(a, b):
    return lax.dot_general(a, b, (((0,), (0,)), ((), ())), preferred_element_type=F32)


def _rms_scale(x):
    return lax.rsqrt(jnp.mean(x * x, axis=-1, keepdims=True) + EPS)


def _params(n_grid):
    return pltpu.CompilerParams(dimension_semantics=("arbitrary",) * n_grid,
                                vmem_limit_bytes=VMEM_LIMIT_BYTES)


def _mem_kv_kernel(mem_ref, mem_norm_ref, wkv_ref, knorm_ref, mkt_ref, mv_ref):
    m = mem_ref[0]
    mn = (m * _rms_scale(m) * mem_norm_ref[...]).astype(BF16)
    kv = _mm(mn, wkv_ref[...])
    for h in range(MEM_HEADS):
        kh = kv[:, h * MEM_HEAD_DIM:(h + 1) * MEM_HEAD_DIM]
        kh = kh * _rms_scale(kh) * knorm_ref[...]
        mkt_ref[0, h] = kh.T.astype(BF16)
        off = MEM_HEADS * MEM_HEAD_DIM + h * MEM_HEAD_DIM
        mv_ref[0, h] = kv[:, off:off + MEM_HEAD_DIM].astype(BF16)


def _mem_kv(layer, mem, mem_norm, wkv, knorm):
    b = mem.shape[0]
    return pl.pallas_call(
        _mem_kv_kernel,
        grid=(b,),
        in_specs=[
            pl.BlockSpec((1, MEM_TOKENS, D_MODEL), lambda i: (i, 0, 0)),
            pl.BlockSpec((1, D_MODEL), lambda i: (0, 0)),
            pl.BlockSpec((None, D_MODEL, 2 * GRP_M), lambda i: (layer, 0, 0)),
            pl.BlockSpec((1, MEM_HEAD_DIM), lambda i: (0, 0)),
        ],
        out_specs=[
            pl.BlockSpec((1, MEM_HEADS, MEM_HEAD_DIM, MEM_TOKENS), lambda i: (i, 0, 0, 0)),
            pl.BlockSpec((1, MEM_HEADS, MEM_TOKENS, MEM_HEAD_DIM), lambda i: (i, 0, 0, 0)),
        ],
        out_shape=[
            jax.ShapeDtypeStruct((b, MEM_HEADS, MEM_HEAD_DIM, MEM_TOKENS), BF16),
            jax.ShapeDtypeStruct((b, MEM_HEADS, MEM_TOKENS, MEM_HEAD_DIM), BF16),
        ],
        compiler_params=_params(1),
        name="mem_kv",
    )(mem, mem_norm, wkv, knorm)


def _rope_slab(s, c, sa, sb):
    return s * c + pltpu.roll(s, LANES - ROT_HALF, 1) * sa + pltpu.roll(s, ROT_HALF, 1) * sb


def _proj_kernel(x_ref, nmix_ref, w_ref, cs_ref, rplace_ref, rones_ref, convw_ref, qg_ref, kg_ref,
                 g64_ref, mkt_ref, mv_ref, mqg_ref,
                 ya_ref, q_ref, iq_ref, kkt_ref, vw_ref, v1_ref, zh_ref, ym_ref,
                 carry_ref):
    i = pl.program_id(1)
    t = x_ref.shape[1]
    x = x_ref[0]
    hb = (x * _rms_scale(x) * nmix_ref[...]).astype(BF16)

    cs_hi, cs_mid, cs_lo = _split3_bf16(cs_ref[0])
    place = rplace_ref[...]
    tabs = _mm(cs_hi, place) + _mm(cs_mid, place) + _mm(cs_lo, place)
    rc = tabs[:, :LANES] + rones_ref[...]
    rsa = tabs[:, LANES:2 * LANES]
    rsb = tabs[:, 2 * LANES:]

    za = _mm(hb, w_ref[:, OFF_A:OFF_A + GRP_A])
    a_x, a_b, a_c = za[:, :CONV_DIM], za[:, CONV_DIM:2 * CONV_DIM], za[:, 2 * CONV_DIM:]
    u = a_c * a_x

    @pl.when(i == 0)
    def _():
        carry_ref[...] = jnp.zeros_like(carry_ref)

    carry = carry_ref[...]
    row8 = lax.broadcasted_iota(jnp.int32, (SUBLANES, CONV_DIM), 0)
    r1 = pltpu.roll(u, 1, 0)
    r2 = pltpu.roll(u, 2, 0)
    top1 = jnp.where(row8 < 1, pltpu.roll(carry, 1, 0), r1[:SUBLANES])
    top2 = jnp.where(row8 < 2, pltpu.roll(carry, 2, 0), r2[:SUBLANES])
    u1 = jnp.concatenate([top1, r1[SUBLANES:]], axis=0)
    u2 = jnp.concatenate([top2, r2[SUBLANES:]], axis=0)
    carry_ref[...] = u[t - SUBLANES:]
    cw = convw_ref[...]
    ya_ref[0] = (a_b * (u2 * cw[0:1] + u1 * cw[1:2] + u * cw[2:3])).astype(BF16)

    zq = _mm(hb, w_ref[:, OFF_Q:OFF_Q + GRP_Q])
    msq = _mm((zq * zq).astype(BF16), g64_ref[...])
    qn = zq * lax.rsqrt(msq + EPS) * qg_ref[...]
    for p in range(GRP_Q // LANES):
        sl = slice(p * LANES, (p + 1) * LANES)
        q_ref[0, :, sl] = (_rope_slab(qn[:, sl], rc, rsa, rsb) * Q_SCALE).astype(BF16)

    zkv = _mm(hb, w_ref[:, OFF_KV:OFF_KV + GRP_KV])
    s0, s1 = zkv[:, :LANES], zkv[:, LANES:]
    lane = lax.broadcasted_iota(jnp.int32, (t, LANES), 1)
    is_k = lane < DSA_HEAD_DIM
    kms = jnp.sum(jnp.where(is_k, s0 * s0, 0.0), axis=-1, keepdims=True) * (1.0 / DSA_HEAD_DIM)
    s0 = s0 * jnp.where(is_k, lax.rsqrt(kms + EPS) * kg_ref[...], 1.0)
    kkt_ref[0] = _rope_slab(s0, rc, rsa, rsb).T.astype(BF16)
    w_scale = jnp.where(lane >= DSA_HEAD_DIM, jnp.where(lane < DSA_HEAD_DIM + IDX_HEADS, IDX_HEADS ** -0.5, 1.0), 1.0)
    vw_ref[0] = s1 * w_scale
    v1_ref[0] = jnp.where(is_k, s1, jnp.where(lane == DSA_HEAD_DIM, 1.0, 0.0)).astype(BF16)

    ziq = _mm(hb, w_ref[:, OFF_IQ:OFF_IQ + GRP_IQ])
    for p in range(GRP_IQ // LANES):
        sl = slice(p * LANES, (p + 1) * LANES)
        iq_ref[0, :, sl] = (_rope_slab(ziq[:, sl], rc, rsa, rsb) * (IDX_DIM ** -0.5)).astype(BF16)

    zh_ref[0] = _mm(hb, w_ref[:, OFF_H:OFF_H + GRP_H])

    zm = _mm(hb, w_ref[:, OFF_M:OFF_M + GRP_M])
    for h in range(MEM_HEADS):
        sl = slice(h * MEM_HEAD_DIM, (h + 1) * MEM_HEAD_DIM)
        mq = zm[:, sl]
        mq = mq * _rms_scale(mq) * mqg_ref[...] * (MEM_HEAD_DIM ** -0.5)
        lg = _mm(mq.astype(BF16), mkt_ref[0, h])
        pe = jnp.exp(lg - jnp.max(lg, axis=-1, keepdims=True))
        den = jnp.sum(pe, axis=-1, keepdims=True)
        ym_ref[0, :, sl] = (_mm(pe.astype(BF16), mv_ref[0, h]) / den).astype(BF16)


def _proj(layer, x, nmix, w1, cs, rplace, rones, convw, qg, kg, g64, mkt, mv, mqg):
    b, s, _ = x.shape
    t = min(PROJ_TILE, s)
    tok = lambda width: pl.BlockSpec((1, t, width), lambda bi, i: (bi, i, 0))
    const2 = lambda shape: pl.BlockSpec(shape, lambda bi, i: (0, 0))
    per_b4 = lambda shape: pl.BlockSpec((1,) + shape, lambda bi, i: (bi, 0, 0, 0))
    out_widths = (CONV_DIM, GRP_Q, GRP_IQ, LANES, LANES, LANES, GRP_H, GRP_M)
    out_dtypes = (BF16, BF16, BF16, BF16, F32, BF16, F32, BF16)
    return pl.pallas_call(
        _proj_kernel,
        grid=(b, s // t),
        in_specs=[
            tok(D_MODEL), const2((1, D_MODEL)),
            pl.BlockSpec((None, D_MODEL, PROJ_COLS), lambda bi, i: (layer, 0, 0)),
            tok(ROT_DIM), const2((ROT_DIM, 3 * LANES)), const2((1, LANES)),
            const2((SUBLANES, CONV_DIM)), const2((1, GRP_Q)), const2((1, LANES)),
            const2((GRP_Q, GRP_Q)),
            per_b4((MEM_HEADS, MEM_HEAD_DIM, MEM_TOKENS)), per_b4((MEM_HEADS, MEM_TOKENS, MEM_HEAD_DIM)),
            const2((1, MEM_HEAD_DIM)),
        ],
        out_specs=[tok(w) if n != 3 else pl.BlockSpec((1, LANES, t), lambda bi, i: (bi, 0, i))
                   for n, w in enumerate(out_widths)],
        out_shape=[jax.ShapeDtypeStruct((b, s, w) if n != 3 else (b, LANES, s), d)
                   for n, (w, d) in enumerate(zip(out_widths, out_dtypes))],
        scratch_shapes=[pltpu.VMEM((SUBLANES, CONV_DIM), F32)],
        compiler_params=_params(2),
        name="proj",
    )(x, nmix, w1, cs, rplace, rones, convw, qg, kg, g64, mkt, mv, mqg)


def _bit_transpose32(words):
    a = list(words)
    for j, m in ((16, 0x0000FFFF), (8, 0x00FF00FF), (4, 0x0F0F0F0F), (2, 0x33333333), (1, 0x55555555)):
        for k in range(32):
            if k & j:
                continue
            tmp = (a[k] ^ lax.shift_right_logical(a[k + j], jnp.int32(j))) & m
            a[k] = a[k] ^ tmp
            a[k + j] = a[k + j] ^ lax.shift_left(tmp, jnp.int32(j))
    return a


def _low_bits(n):
    return jnp.where(n >= 32, -1, jnp.where(n <= 0, 0, lax.shift_left(jnp.int32(1), n) - 1))


def _dsa_kernel(fast_ref, q_ref, iq_ref, vw_ref, kkt_ref, v1_ref, o_ref,
                key_ref, plane_ref, cand_ref, sel_ref, m_ref, acc_ref, kbd_ref, ikbd_ref, *, topk, seq_len):
    i = pl.program_id(1)
    t = q_ref.shape[1]
    gt, _, tk = key_ref.shape
    n_groups = plane_ref.shape[0]
    cpt = tk // LANES
    n_pairs = DSA_HEADS // 2
    q0 = i * t
    nj = (q0 + t - 1) // tk + 1
    ng = (nj + gt - 1) // gt

    @pl.when(jnp.logical_and(pl.program_id(0) == 0, i == 0))
    def _():
        plane_ref[...] = jnp.zeros_like(plane_ref)

    @pl.when(i == 0)
    def _():
        kbd_ref[...] = jnp.zeros_like(kbd_ref)
        ikbd_ref[...] = jnp.zeros_like(ikbd_ref)
        d = DSA_HEAD_DIM
        for j in range(kbd_ref.shape[0]):
            kt = kkt_ref[0, :, j * tk:(j + 1) * tk]
            kbd_ref[j, :d, :tk] = kt[:d]
            kbd_ref[j, d:, tk:] = kt[:d]
            ikbd_ref[j, :d, :tk] = kt[d:]
            ikbd_ref[j, d:, tk:] = kt[d:]

    iw = vw_ref[0][:, DSA_HEAD_DIM:DSA_HEAD_DIM + IDX_HEADS]

    def score_group(g, carry):
        nt = jnp.minimum(nj - g * gt, gt)

        def score_tile(jj):
            sc = jnp.zeros((t, tk), F32)
            for p in range(n_pairs):
                r = _mm(iq_ref[0, :, p * LANES:(p + 1) * LANES], ikbd_ref[g * gt + jj])
                sc = sc + jnp.maximum(r[:, :tk], 0.0) * iw[:, 2 * p:2 * p + 1]
                sc = sc + jnp.maximum(r[:, tk:], 0.0) * iw[:, 2 * p + 1:2 * p + 2]
            sc = jnp.where(sc == 0.0, 0.0, sc)
            bits = pltpu.bitcast(sc, jnp.int32)
            key_ref[jj] = bits ^ ((bits >> 31) | INT_MIN)

        def score_two(k, c2):
            score_tile(2 * k)
            score_tile(2 * k + 1)
            return c2

        lax.fori_loop(0, nt // 2, score_two, 0)

        @pl.when(nt % 2 == 1)
        def _():
            score_tile(nt - 1)

        def clear_tile(jj, c2):
            key_ref[jj] = jnp.zeros((t, tk), jnp.int32)
            return c2

        lax.fori_loop(nt, gt, clear_tile, 0)

        def to_planes(rg, c2):
            rows = pl.ds(pl.multiple_of(rg * SUBLANES, SUBLANES), SUBLANES)
            words = []
            for k in range(32):
                ch = 31 - k
                words.append(key_ref[ch // cpt, rows, (ch % cpt) * LANES:(ch % cpt + 1) * LANES])
            for p, w in enumerate(_bit_transpose32(words)):
                plane_ref[g, p, rows, :] = w
            return c2

        lax.fori_loop(0, t // SUBLANES, to_planes, 0)
        return carry

    lax.fori_loop(0, ng, score_group, 0)

    lane = lax.broadcasted_iota(jnp.int32, (t, LANES), 1)
    qpos = lax.broadcasted_iota(jnp.int32, (t, LANES), 0) + q0
    chunks_valid = ((qpos - lane) >> 7) + 1

    def count_bits(words):
        tot = lax.population_count(words[0]).astype(F32)
        for w in words[1:]:
            tot = tot + lax.population_count(w).astype(F32)
        return jnp.sum(tot, axis=-1, keepdims=True)

    for g in range(n_groups):
        cand_ref[g] = _low_bits(chunks_valid - 32 * g)
        sel_ref[g] = jnp.zeros((t, LANES), jnp.int32)

    def radix_select(groups):
        def radix(p, need):
            ones = [cand_ref[g] & plane_ref[g, p] for g in groups]
            c1 = count_bits(ones)
            take = c1 >= need
            for g, one in zip(groups, ones):
                cand = cand_ref[g]
                cand_ref[g] = jnp.where(take, one, cand ^ one)
                sel_ref[g] = jnp.where(take, sel_ref[g], sel_ref[g] | one)
            return jnp.where(take, need, need - c1)

        return lax.fori_loop(0, 32, radix, jnp.full((t, 1), float(topk), F32))

    if n_groups == 1:
        need = radix_select(range(1))
    else:
        need = lax.switch(ng - 1, [functools.partial(radix_select, range(n)) for n in range(1, n_groups + 1)])

    has_tie = jnp.where(count_bits([cand_ref[g] for g in range(n_groups)]) > need, 1.0, 0.0)
    any_tie = jnp.max(has_tie) > 0.0

    @pl.when(jnp.logical_not(any_tie))
    def _():
        for g in range(n_groups):
            sel_ref[g] = sel_ref[g] | cand_ref[g]

    @pl.when(any_tie)
    def _():
        n_bits = max(1, int(seq_len - 1).bit_length())

        def below(g, cut):
            return cand_ref[g] & _low_bits(((cut - lane + (LANES - 1)) >> 7) - 32 * g)

        def bisect_idx(bit, pfx):
            cut = pfx | (jnp.int32(1) << (n_bits - 1 - bit))
            cnt = count_bits([below(g, cut) for g in range(n_groups)])
            return jnp.where(cnt < need, cut, pfx)

        cut = lax.fori_loop(0, n_bits, bisect_idx, jnp.zeros((t, 1), jnp.int32))
        cut = jnp.where(has_tie > 0.0, cut + 1, seq_len)
        for g in range(n_groups):
            sel_ref[g] = sel_ref[g] | below(g, cut)

    neg_bits = int(np.float32(NEG_BIG).view(np.int32))

    def tile_bias(j):
        words = sel_ref[j // gt]
        parts = []
        for c in range(cpt):
            k = (j % gt) * cpt + c
            picked = lax.shift_left(words, 31 - k) >> 31
            parts.append(pltpu.bitcast(neg_bits & ~picked, F32))
        return jnp.concatenate(parts, axis=-1)

    acc_ref[...] = jnp.zeros(acc_ref.shape, F32)
    use_fast = fast_ref[0] > 0

    @pl.when(use_fast)
    def _():
        def attend(j, carry):
            bias = tile_bias(j)
            for p in range(n_pairs):
                lg = _mm(q_ref[0, :, p * LANES:(p + 1) * LANES], kbd_ref[j])
                for e in range(2):
                    pe = jnp.exp2(lg[:, e * tk:(e + 1) * tk] + bias).astype(BF16)
                    acc_ref[2 * p + e] += _mm(pe, v1_ref[0, j])
            return carry

        lax.fori_loop(0, nj, attend, 0)

    @pl.when(jnp.logical_not(use_fast))
    def _():
        m_ref[...] = jnp.full(m_ref.shape, NEG_BIG, F32)

        def attend(j, carry):
            bias = tile_bias(j)
            for p in range(n_pairs):
                lg = _mm(q_ref[0, :, p * LANES:(p + 1) * LANES], kbd_ref[j])
                for e in range(2):
                    h = 2 * p + e
                    s = lg[:, e * tk:(e + 1) * tk] + bias
                    m_old = m_ref[h][:, :1]
                    m_new = jnp.maximum(m_old, jnp.max(s, axis=-1, keepdims=True))
                    pe = jnp.exp2(s - m_new).astype(BF16)
                    acc_ref[h] = acc_ref[h] * jnp.exp2(m_old - m_new) + _mm(pe, v1_ref[0, j])
                    m_ref[h] = jnp.broadcast_to(m_new, (t, LANES))
            return carry

        lax.fori_loop(0, nj, attend, 0)

    first_head = lax.broadcasted_iota(jnp.int32, (t, LANES), 1) < DSA_HEAD_DIM
    for p in range(n_pairs):
        a0, a1 = acc_ref[2 * p], acc_ref[2 * p + 1]
        o0 = a0 / a0[:, DSA_HEAD_DIM:DSA_HEAD_DIM + 1]
        o1 = a1 / a1[:, DSA_HEAD_DIM:DSA_HEAD_DIM + 1]
        o_ref[0, :, p * LANES:(p + 1) * LANES] = jnp.where(
            first_head, o0, pltpu.roll(o1, DSA_HEAD_DIM, 1)).astype(BF16)


def _dsa(fast, q, iq, vw, kkt, v1, topk):
    b, s, _ = q.shape
    t = min(DSA_Q_TILE, s)
    tk = min(DSA_K_TILE, s)
    nj = s // tk
    gt = 32 * LANES // tk
    n_groups = -(-nj // gt)
    tok = lambda width: pl.BlockSpec((1, t, width), lambda bi, i: (bi, i, 0))
    per_b = lambda shape: pl.BlockSpec((1,) + shape, lambda bi, i: (bi,) + (0,) * len(shape),
                                       pipeline_mode=pl.Buffered(1))
    return pl.pallas_call(
        functools.partial(_dsa_kernel, topk=topk, seq_len=s),
        grid=(b, s // t),
        in_specs=[pl.BlockSpec(memory_space=pltpu.SMEM),
                  tok(GRP_Q), tok(GRP_IQ), tok(LANES),
                  per_b((LANES, s)), per_b((nj, tk, LANES))],
        out_specs=tok(GRP_Q),
        out_shape=jax.ShapeDtypeStruct((b, s, GRP_Q), BF16),
        scratch_shapes=[
            pltpu.VMEM((gt, t, tk), jnp.int32),
            pltpu.VMEM((n_groups, 32, t, LANES), jnp.int32),
            pltpu.VMEM((n_groups, t, LANES), jnp.int32),
            pltpu.VMEM((n_groups, t, LANES), jnp.int32),
            pltpu.VMEM((DSA_HEADS, t, LANES), F32),
            pltpu.VMEM((DSA_HEADS, t, LANES), F32),
            pltpu.VMEM((nj, LANES, 2 * tk), BF16),
            pltpu.VMEM((nj, LANES, 2 * tk), BF16),
        ],
        compiler_params=_params(2),
        name="dsa",
    )(fast, q, iq, vw, kkt, v1.reshape(b, nj, tk, LANES))


def _split3_bf16(x):
    hi = x.astype(BF16)
    r1 = x - hi.astype(F32)
    mid = r1.astype(BF16)
    lo = (r1 - mid.astype(F32)).astype(BF16)
    return hi, mid, lo


def _hgrn_kernel(zh_ref, lb_ref, onorm_ref, o_ref, state_ref):
    i = pl.program_id(1)
    t = zh_ref.shape[1]
    c = HG_CHUNK
    width = HG_HEADS * HG_DK

    @pl.when(i == 0)
    def _():
        state_ref[...] = jnp.zeros_like(state_ref)

    rr = lax.broadcasted_iota(jnp.int32, (c, c), 0)
    cc = lax.broadcasted_iota(jnp.int32, (c, c), 1)
    tril = rr >= cc
    tril_b = jnp.where(tril, 1.0, 0.0).astype(BF16)

    n_chunks = t // c
    chunk = lambda n: slice(n * c, (n + 1) * c)
    head = lambda h: slice(h * HG_DK, (h + 1) * HG_DK)
    zq, zf, v, zg = (zh_ref[0, :, g * width:(g + 1) * width] for g in range(4))
    lb = lb_ref[...]
    log_lb, log1m_lb, one_m_lb = lb[0:1], lb[1:2], lb[2:3]

    e = jnp.exp(-jnp.abs(zf))
    b_ = log1m_lb + jnp.minimum(zf, 0.0) - jnp.log1p(e)
    log_f = jnp.maximum(log_lb, b_) + jnp.log1p(jnp.exp(-jnp.abs(log_lb - b_)))
    hk = one_m_lb * jnp.where(zf >= 0.0, e, 1.0) / (1.0 + e)
    hq = zq * jax.nn.sigmoid(zq)
    vb = v.astype(BF16)
    hi, mid, lo = _split3_bf16(log_f)
    cums = [_mm(tril_b, hi[chunk(n)]) + _mm(tril_b, mid[chunk(n)]) + _mm(tril_b, lo[chunk(n)])
            for n in range(n_chunks)]
    q_intra, k_intra, q_inter, k_state, decay = [], [], [], [], []
    for n in range(n_chunks):
        cum = cums[n]
        ref = cum[c // 2 - 1:c // 2]
        last = cum[c - 1:c]
        q_intra.append((hq[chunk(n)] * jnp.exp(cum - ref)).astype(BF16))
        k_intra.append((hk[chunk(n)] * jnp.exp(ref - cum)).astype(BF16))
        q_inter.append((hq[chunk(n)] * jnp.exp(cum)).astype(BF16))
        k_state.append((hk[chunk(n)] * jnp.exp(last - cum)).astype(BF16))
        decay.append(jnp.exp(last))
    att = [[_mm_nt(q_intra[n][:, head(h)], k_intra[n][:, head(h)]) for h in range(HG_HEADS)]
           for n in range(n_chunks)]
    o_intra = [[_mm(jnp.where(tril, att[n][h], 0.0).astype(BF16), vb[chunk(n), head(h)])
                for h in range(HG_HEADS)] for n in range(n_chunks)]
    states = [state_ref[h] for h in range(HG_HEADS)]
    o_inter = []
    for n in range(n_chunks):
        o_inter.append([_mm_nt(q_inter[n][:, head(h)], states[h].astype(BF16)) for h in range(HG_HEADS)])
        states = [states[h] * decay[n][:, head(h)] + _mm_tn(vb[chunk(n), head(h)], k_state[n][:, head(h)])
                  for h in range(HG_HEADS)]
    for h in range(HG_HEADS):
        state_ref[h] = states[h]
    gate = zg * jax.nn.sigmoid(zg)
    for n in range(n_chunks):
        for h in range(HG_HEADS):
            o = o_intra[n][h] + o_inter[n][h]
            o = o * _rms_scale(o) * onorm_ref[...]
            o_ref[0, chunk(n), head(h)] = (o * gate[chunk(n), head(h)]).astype(BF16)


def _hgrn(zh, lb_rows, onorm):
    b, s, _ = zh.shape
    t = min(HG_TILE, s)
    return pl.pallas_call(
        _hgrn_kernel,
        grid=(b, s // t),
        in_specs=[pl.BlockSpec((1, t, GRP_H), lambda bi, i: (bi, i, 0)),
                  pl.BlockSpec((SUBLANES, HG_HEADS * HG_DK), lambda bi, i: (0, 0)),
                  pl.BlockSpec((1, HG_DV), lambda bi, i: (0, 0))],
        out_specs=pl.BlockSpec((1, t, HG_HEADS * HG_DV), lambda bi, i: (bi, i, 0)),
        out_shape=jax.ShapeDtypeStruct((b, s, HG_HEADS * HG_DV), BF16),
        scratch_shapes=[pltpu.VMEM((HG_HEADS, HG_DV, HG_DK), F32)],
        compiler_params=_params(2),
        name="hgrn",
    )(zh, lb_rows, onorm)


def _merge_kernel(x_ref, nmix_ref, ya_ref, yb_ref, yc_ref, ym_ref, wg_ref, wl_ref, wo_ref, o_ref):
    x = x_ref[0]
    hb = (x * _rms_scale(x) * nmix_ref[...]).astype(BF16)
    merged = None
    for n, y_ref in enumerate((ya_ref, yb_ref, yc_ref, ym_ref)):
        gate = jax.nn.sigmoid(_mm(hb, wg_ref[:, n * D_MODEL:(n + 1) * D_MODEL]))
        term = gate * _mm(y_ref[0], wl_ref[n])
        merged = term if merged is None else merged + term
    o_ref[0] = x + _mm(merged.astype(BF16), wo_ref[...])


def _merge(layer, x, nmix, ya, yb, yc, ym, wg, wl, wo):
    b, s, _ = x.shape
    t = min(MERGE_TILE, s)
    tok = lambda width: pl.BlockSpec((1, t, width), lambda bi, i: (bi, i, 0))
    return pl.pallas_call(
        _merge_kernel,
        grid=(b, s // t),
        in_specs=[tok(D_MODEL), pl.BlockSpec((1, D_MODEL), lambda bi, i: (0, 0)),
                  tok(BRANCH_DIM), tok(BRANCH_DIM), tok(BRANCH_DIM), tok(BRANCH_DIM),
                  pl.BlockSpec((None, D_MODEL, N_BRANCH * D_MODEL), lambda bi, i: (layer, 0, 0)),
                  pl.BlockSpec((None, N_BRANCH, BRANCH_DIM, D_MODEL), lambda bi, i: (layer, 0, 0, 0)),
                  pl.BlockSpec((None, D_MODEL, D_MODEL), lambda bi, i: (layer, 0, 0))],
        out_specs=tok(D_MODEL),
        out_shape=jax.ShapeDtypeStruct((b, s, D_MODEL), F32),
        compiler_params=_params(2),
        name="merge",
    )(x, nmix, ya, yb, yc, ym, wg, wl, wo)


def _ffn_kernel(x_ref, nffn_ref, wup_ref, wdn_ref, o_ref):
    x = x_ref[0]
    hb = (x * _rms_scale(x) * nffn_ref[...]).astype(BF16)
    out = x
    for n in range(FFN_DIM // FFN_CHUNK):
        lo = n * FFN_CHUNK
        gate = _mm(hb, wup_ref[:, lo:lo + FFN_CHUNK])
        up = _mm(hb, wup_ref[:, FFN_DIM + lo:FFN_DIM + lo + FFN_CHUNK])
        act = (gate * jax.nn.sigmoid(gate) * up).astype(BF16)
        out = out + _mm(act, wdn_ref[lo:lo + FFN_CHUNK, :])
    o_ref[0] = out


def _ffn(layer, x, nffn, wup, wdn):
    b, s, _ = x.shape
    t = min(FFN_TILE, s)
    tok = pl.BlockSpec((1, t, D_MODEL), lambda bi, i: (bi, i, 0))
    return pl.pallas_call(
        _ffn_kernel,
        grid=(b, s // t),
        in_specs=[tok, pl.BlockSpec((1, D_MODEL), lambda bi, i: (0, 0)),
                  pl.BlockSpec((None, D_MODEL, 2 * FFN_DIM), lambda bi, i: (layer, 0, 0)),
                  pl.BlockSpec((None, FFN_DIM, D_MODEL), lambda bi, i: (layer, 0, 0))],
        out_specs=tok,
        out_shape=jax.ShapeDtypeStruct((b, s, D_MODEL), F32),
        compiler_params=_params(2),
        name="ffn",
    )(x, nffn, wup, wdn)


def _rope_constants():
    inv_freq = 1.0 / (ROPE_THETA ** (jnp.arange(0, ROT_DIM, 2, dtype=F32) / ROT_DIM))
    invf = jnp.concatenate([inv_freq, inv_freq]).reshape(1, ROT_DIM)
    place = np.zeros((ROT_DIM, 3 * LANES), np.float32)
    ones = np.zeros((1, LANES), np.float32)
    for lane in range(LANES):
        d = lane % DSA_HEAD_DIM
        if d < ROT_HALF:
            place[d, lane] = 1.0
            place[ROT_HALF + d, LANES + lane] = -1.0
        elif d < ROT_DIM:
            place[d - ROT_HALF, lane] = 1.0
            place[d, 2 * LANES + lane] = 1.0
        else:
            ones[0, lane] = 1.0
    return invf, jnp.asarray(place, BF16), jnp.asarray(ones)


def _relayout_w_in(w_in):
    offs = np.concatenate([[0], np.cumsum(SPLIT_SIZES)])
    col = lambda n: w_in[..., int(offs[n]):int(offs[n + 1])]
    (a_x, a_b, a_c, d_q, d_k, d_v, i_q, i_k, i_w, g_q, g_f, g_i, g_g, m_q, gates) = (col(n) for n in range(15))
    pad = jnp.zeros(w_in.shape[:-1] + (LANES - DSA_HEAD_DIM - IDX_HEADS,), w_in.dtype)
    w1 = jnp.concatenate([a_x, a_b, a_c, d_q, d_k, i_k, d_v, i_w, pad, i_q, g_q, g_f, g_i, g_g, m_q], axis=-1)
    return w1.astype(BF16), gates.astype(BF16)


def kernel(x, mem, positions, norm_mix, w_in, conv_w, dsa_q_norm, dsa_k_norm, hgrn_lower_bounds,
           hgrn_out_norm, mem_norm, mem_w_kv, mem_q_norm, mem_k_norm, w_lift, w_out, norm_ffn,
           ffn_w_up, ffn_w_down):
    b, s, d = x.shape
    depth = w_in.shape[0]
    assert d == D_MODEL and w_in.shape[2] == sum(SPLIT_SIZES)
    assert s % HG_CHUNK == 0 and s % min(DSA_K_TILE, s) == 0
    topk = min(TOPK_MAX, s // 4)

    invf, rplace, rones = _rope_constants()
    ang = positions.astype(F32)[..., None] * invf
    cs = jnp.where(jnp.arange(ROT_DIM) < ROT_HALF, jnp.cos(ang), jnp.sin(ang))
    lbs = jnp.cumsum(jax.nn.softmax(hgrn_lower_bounds.astype(F32), axis=0), axis=0)
    lbs = lbs - lbs[0:1]
    g64 = jnp.asarray(np.kron(np.eye(DSA_HEADS), np.full((DSA_HEAD_DIM, DSA_HEAD_DIM), 1.0 / DSA_HEAD_DIM)), BF16)
    row = lambda v: v.reshape(1, -1).astype(F32)
    w1, wg = _relayout_w_in(w_in)
    wkv, wl, wo = mem_w_kv.astype(BF16), w_lift.astype(BF16), w_out.astype(BF16)
    wup, wdn = ffn_w_up.astype(BF16), ffn_w_down.astype(BF16)

    for l in range(depth):
        lb = lbs[l]
        lb_rows = jnp.concatenate([jnp.stack([jnp.log(lb), jnp.log1p(-lb), 1.0 - lb]),
                                   jnp.zeros((SUBLANES - 3, lb.shape[0]), F32)])
        convw = jnp.concatenate([conv_w[l], jnp.zeros((SUBLANES - CONV_WIDTH, CONV_DIM), F32)])
        mkt, mv = _mem_kv(l, mem, row(mem_norm), wkv, row(mem_k_norm[l]))
        ya, q, iq, kkt, vw, v1, zh, ym = _proj(
            l, x, row(norm_mix[l]), w1, cs, rplace, rones, convw,
            row(jnp.tile(dsa_q_norm[l], DSA_HEADS)),
            row(jnp.concatenate([dsa_k_norm[l], jnp.ones((LANES - DSA_HEAD_DIM,), F32)])),
            g64, mkt, mv, row(mem_q_norm[l]))
        logit_bound = DSA_HEAD_DIM ** 0.5 * jnp.max(jnp.abs(dsa_q_norm[l])) * jnp.max(jnp.abs(dsa_k_norm[l]))
        fast = (logit_bound <= SAFE_LOGIT_BOUND).astype(jnp.int32).reshape(1)
        yb = _dsa(fast, q, iq, vw, kkt, v1, topk)
        yc = _hgrn(zh, lb_rows, row(hgrn_out_norm[l]))
        x = _merge(l, x, row(norm_mix[l]), ya, yb, yc, ym, wg, wl, wo)
        x = _ffn(l, x, row(norm_ffn[l]), wup, wdn)
    return x
```

```python
import functools

import jax
import jax.numpy as jnp
import numpy as np
from jax import lax
from jax.experimental import pallas as pl
from jax.experimental.pallas import tpu as pltpu

D_MODEL = 1024
MEM_TOKENS = 256
N_BRANCH = 4
BRANCH_DIM = 512
CONV_DIM = 512
CONV_WIDTH = 3
DSA_HEADS = 8
DSA_HEAD_DIM = 64
IDX_HEADS = 8
IDX_DIM = 64
TOPK_MAX = 256
HG_HEADS = 4
HG_DK = 128
HG_DV = 128
HG_CHUNK = 64
MEM_HEADS = 4
MEM_HEAD_DIM = 128
ROPE_THETA = 500000.0
ROT_DIM = DSA_HEAD_DIM // 4
ROT_HALF = ROT_DIM // 2
FFN_DIM = ((8 * D_MODEL // 3 + 255) // 256) * 256
EPS = 1e-6

SPLIT_SIZES = (CONV_DIM, CONV_DIM, CONV_DIM,
               DSA_HEADS * DSA_HEAD_DIM, DSA_HEAD_DIM, DSA_HEAD_DIM,
               IDX_HEADS * IDX_DIM, IDX_DIM, IDX_HEADS,
               HG_HEADS * HG_DK, HG_HEADS * HG_DK, HG_HEADS * HG_DV, HG_HEADS * HG_DV,
               MEM_HEADS * MEM_HEAD_DIM,
               N_BRANCH * D_MODEL)

LANES = 128
SUBLANES = 8
VMEM_LIMIT_BYTES = 56 * 1024 * 1024

PROJ_TILE = 512
DSA_Q_TILE = 512
DSA_K_TILE = 512
HG_TILE = 512
MERGE_TILE = 512
FFN_TILE = 512
FFN_CHUNK = FFN_DIM // 2

GRP_A = 3 * CONV_DIM
GRP_Q = DSA_HEADS * DSA_HEAD_DIM
GRP_KV = 2 * LANES
GRP_IQ = IDX_HEADS * IDX_DIM
GRP_H = 4 * HG_HEADS * HG_DK
GRP_M = MEM_HEADS * MEM_HEAD_DIM
OFF_A = 0
OFF_Q = OFF_A + GRP_A
OFF_KV = OFF_Q + GRP_Q
OFF_IQ = OFF_KV + GRP_KV
OFF_H = OFF_IQ + GRP_IQ
OFF_M = OFF_H + GRP_H
PROJ_COLS = OFF_M + GRP_M

NEG_BIG = -1e30
INT_MIN = -(2 ** 31)
LOG2E = 1.4426950408889634
Q_SCALE = DSA_HEAD_DIM ** -0.5 * LOG2E
SAFE_LOGIT_BOUND = 40.0

BF16 = jnp.bfloat16
F32 = jnp.float32


def _mm(a, b):
    return jnp.dot(a, b, preferred_element_type=F32)


def _mm_nt(a, b):
    return lax.dot_general(a, b, (((1,), (1,)), ((), ())), preferred_element_type=F32)


def _mm_tn(a, b):
    return lax.dot_general(a, b, (((0,), (0,)), ((), ())), preferred_element_type=F32)


def _rms_scale(x):
    return lax.rsqrt(jnp.mean(x * x, axis=-1, keepdims=True) + EPS)


def _params(n_grid):
    return pltpu.CompilerParams(dimension_semantics=("arbitrary",) * n_grid,
                                vmem_limit_bytes=VMEM_LIMIT_BYTES)


def _mem_kv_kernel(mem_ref, mem_norm_ref, wkv_ref, knorm_ref, mkt_ref, mv_ref):
    m = mem_ref[0]
    mn = (m * _rms_scale(m) * mem_norm_ref[...]).astype(BF16)
    kv = _mm(mn, wkv_ref[...])
    for h in range(MEM_HEADS):
        kh = kv[:, h * MEM_HEAD_DIM:(h + 1) * MEM_HEAD_DIM]
        kh = kh * _rms_scale(kh) * knorm_ref[...]
        mkt_ref[0, h] = kh.T.astype(BF16)
        off = MEM_HEADS * MEM_HEAD_DIM + h * MEM_HEAD_DIM
        mv_ref[0, h] = kv[:, off:off + MEM_HEAD_DIM].astype(BF16)


def _mem_kv(layer, mem, mem_norm, wkv, knorm):
    b = mem.shape[0]
    return pl.pallas_call(
        _mem_kv_kernel,
        grid=(b,),
        in_specs=[
            pl.BlockSpec((1, MEM_TOKENS, D_MODEL), lambda i: (i, 0, 0)),
            pl.BlockSpec((1, D_MODEL), lambda i: (0, 0)),
            pl.BlockSpec((None, D_MODEL, 2 * GRP_M), lambda i: (layer, 0, 0)),
            pl.BlockSpec((1, MEM_HEAD_DIM), lambda i: (0, 0)),
        ],
        out_specs=[
            pl.BlockSpec((1, MEM_HEADS, MEM_HEAD_DIM, MEM_TOKENS), lambda i: (i, 0, 0, 0)),
            pl.BlockSpec((1, MEM_HEADS, MEM_TOKENS, MEM_HEAD_DIM), lambda i: (i, 0, 0, 0)),
        ],
        out_shape=[
            jax.ShapeDtypeStruct((b, MEM_HEADS, MEM_HEAD_DIM, MEM_TOKENS), BF16),
            jax.ShapeDtypeStruct((b, MEM_HEADS, MEM_TOKENS, MEM_HEAD_DIM), BF16),
        ],
        compiler_params=_params(1),
        name="mem_kv",
    )(mem, mem_norm, wkv, knorm)


def _rope_slab(s, c, sa, sb):
    return s * c + pltpu.roll(s, LANES - ROT_HALF, 1) * sa + pltpu.roll(s, ROT_HALF, 1) * sb


def _proj_kernel(x_ref, nmix_ref, w_ref, cs_ref, rplace_ref, rones_ref, convw_ref, qg_ref, kg_ref,
                 g64_ref, mkt_ref, mv_ref, mqg_ref,
                 ya_ref, q_ref, iq_ref, kkt_ref, vw_ref, v1_ref, zh_ref, ym_ref,
                 carry_ref):
    i = pl.program_id(1)
    t = x_ref.shape[1]
    x = x_ref[0]
    hb = (x * _rms_scale(x) * nmix_ref[...]).astype(BF16)

    cs_hi, cs_mid, cs_lo = _split3_bf16(cs_ref[0])
    place = rplace_ref[...]
    tabs = _mm(cs_hi, place) + _mm(cs_mid, place) + _mm(cs_lo, place)
    rc = tabs[:, :LANES] + rones_ref[...]
    rsa = tabs[:, LANES:2 * LANES]
    rsb = tabs[:, 2 * LANES:]

    za = _mm(hb, w_ref[:, OFF_A:OFF_A + GRP_A])
    a_x, a_b, a_c = za[:, :CONV_DIM], za[:, CONV_DIM:2 * CONV_DIM], za[:, 2 * CONV_DIM:]
    u = a_c * a_x

    @pl.when(i == 0)
    def _():
        carry_ref[...] = jnp.zeros_like(carry_ref)

    carry = carry_ref[...]
    row8 = lax.broadcasted_iota(jnp.int32, (SUBLANES, CONV_DIM), 0)
    r1 = pltpu.roll(u, 1, 0)
    r2 = pltpu.roll(u, 2, 0)
    top1 = jnp.where(row8 < 1, pltpu.roll(carry, 1, 0), r1[:SUBLANES])
    top2 = jnp.where(row8 < 2, pltpu.roll(carry, 2, 0), r2[:SUBLANES])
    u1 = jnp.concatenate([top1, r1[SUBLANES:]], axis=0)
    u2 = jnp.concatenate([top2, r2[SUBLANES:]], axis=0)
    carry_ref[...] = u[t - SUBLANES:]
    cw = convw_ref[...]
    ya_ref[0] = (a_b * (u2 * cw[0:1] + u1 * cw[1:2] + u * cw[2:3])).astype(BF16)

    zq = _mm(hb, w_ref[:, OFF_Q:OFF_Q + GRP_Q])
    msq = _mm((zq * zq).astype(BF16), g64_ref[...])
    qn = zq * lax.rsqrt(msq + EPS) * qg_ref[...]
    for p in range(GRP_Q // LANES):
        sl = slice(p * LANES, (p + 1) * LANES)
        q_ref[0, :, sl] = (_rope_slab(qn[:, sl], rc, rsa, rsb) * Q_SCALE).astype(BF16)

    zkv = _mm(hb, w_ref[:, OFF_KV:OFF_KV + GRP_KV])
    s0, s1 = zkv[:, :LANES], zkv[:, LANES:]
    lane = lax.broadcasted_iota(jnp.int32, (t, LANES), 1)
    is_k = lane < DSA_HEAD_DIM
    kms = jnp.sum(jnp.where(is_k, s0 * s0, 0.0), axis=-1, keepdims=True) * (1.0 / DSA_HEAD_DIM)
    s0 = s0 * jnp.where(is_k, lax.rsqrt(kms + EPS) * kg_ref[...], 1.0)
    kkt_ref[0] = _rope_slab(s0, rc, rsa, rsb).T.astype(BF16)
    w_scale = jnp.where(lane >= DSA_HEAD_DIM, jnp.where(lane < DSA_HEAD_DIM + IDX_HEADS, IDX_HEADS ** -0.5, 1.0), 1.0)
    vw_ref[0] = s1 * w_scale
    v1_ref[0] = jnp.where(is_k, s1, jnp.where(lane == DSA_HEAD_DIM, 1.0, 0.0)).astype(BF16)

    ziq = _mm(hb, w_ref[:, OFF_IQ:OFF_IQ + GRP_IQ])
    for p in range(GRP_IQ // LANES):
        sl = slice(p * LANES, (p + 1) * LANES)
        iq_ref[0, :, sl] = (_rope_slab(ziq[:, sl], rc, rsa, rsb) * (IDX_DIM ** -0.5)).astype(BF16)

    zm = _mm(hb, w_ref[:, OFF_M:OFF_M + GRP_M])
    for h in range(MEM_HEADS):
        sl = slice(h * MEM_HEAD_DIM, (h + 1) * MEM_HEAD_DIM)
        mq = zm[:, sl]
        mq = mq * _rms_scale(mq) * mqg_ref[...] * (MEM_HEAD_DIM ** -0.5)
        lg = _mm(mq.astype(BF16), mkt_ref[0, h])
        pe = jnp.exp(lg - jnp.max(lg, axis=-1, keepdims=True))
        den = jnp.sum(pe, axis=-1, keepdims=True)
        ym_ref[0, :, sl] = (_mm(pe.astype(BF16), mv_ref[0, h]) / den).astype(BF16)

    zh_ref[0] = _mm(hb, w_ref[:, OFF_H:OFF_H + GRP_H])


def _proj(layer, x, nmix, w1, cs, rplace, rones, convw, qg, kg, g64, mkt, mv, mqg):
    b, s, _ = x.shape
    t = min(PROJ_TILE, s)
    tok = lambda width: pl.BlockSpec((1, t, width), lambda bi, i: (bi, i, 0))
    const2 = lambda shape: pl.BlockSpec(shape, lambda bi, i: (0, 0))
    per_b4 = lambda shape: pl.BlockSpec((1,) + shape, lambda bi, i: (bi, 0, 0, 0))
    out_widths = (CONV_DIM, GRP_Q, GRP_IQ, LANES, LANES, LANES, GRP_H, GRP_M)
    out_dtypes = (BF16, BF16, BF16, BF16, F32, BF16, F32, BF16)
    return pl.pallas_call(
        _proj_kernel,
        grid=(b, s // t),
        in_specs=[
            tok(D_MODEL), const2((1, D_MODEL)),
            pl.BlockSpec((None, D_MODEL, PROJ_COLS), lambda bi, i: (layer, 0, 0)),
            tok(ROT_DIM), const2((ROT_DIM, 3 * LANES)), const2((1, LANES)),
            const2((SUBLANES, CONV_DIM)), const2((1, GRP_Q)), const2((1, LANES)),
            const2((GRP_Q, GRP_Q)),
            per_b4((MEM_HEADS, MEM_HEAD_DIM, MEM_TOKENS)), per_b4((MEM_HEADS, MEM_TOKENS, MEM_HEAD_DIM)),
            const2((1, MEM_HEAD_DIM)),
        ],
        out_specs=[tok(w) if n != 3 else pl.BlockSpec((1, LANES, t), lambda bi, i: (bi, 0, i))
                   for n, w in enumerate(out_widths)],
        out_shape=[jax.ShapeDtypeStruct((b, s, w) if n != 3 else (b, LANES, s), d)
                   for n, (w, d) in enumerate(zip(out_widths, out_dtypes))],
        scratch_shapes=[pltpu.VMEM((SUBLANES, CONV_DIM), F32)],
        compiler_params=_params(2),
        name="proj",
    )(x, nmix, w1, cs, rplace, rones, convw, qg, kg, g64, mkt, mv, mqg)


def _bit_transpose32(words):
    a = list(words)
    for j, m in ((16, 0x0000FFFF), (8, 0x00FF00FF), (4, 0x0F0F0F0F), (2, 0x33333333), (1, 0x55555555)):
        for k in range(32):
            if k & j:
                continue
            tmp = (a[k] ^ lax.shift_right_logical(a[k + j], jnp.int32(j))) & m
            a[k] = a[k] ^ tmp
            a[k + j] = a[k + j] ^ lax.shift_left(tmp, jnp.int32(j))
    return a


def _low_bits(n):
    return jnp.where(n >= 32, -1, jnp.where(n <= 0, 0, lax.shift_left(jnp.int32(1), n) - 1))


def _dsa_kernel(fast_ref, q_ref, iq_ref, vw_ref, kkt_ref, v1_ref, o_ref,
                key_ref, plane_ref, cand_ref, sel_ref, m_ref, acc_ref, kbd_ref, ikbd_ref, *, topk, seq_len):
    i = pl.program_id(1)
    t = q_ref.shape[1]
    gt, _, tk = key_ref.shape
    n_groups = plane_ref.shape[0]
    cpt = tk // LANES
    n_pairs = DSA_HEADS // 2
    q0 = i * t
    nj = (q0 + t - 1) // tk + 1
    ng = (nj + gt - 1) // gt

    @pl.when(jnp.logical_and(pl.program_id(0) == 0, i == 0))
    def _():
        plane_ref[...] = jnp.zeros_like(plane_ref)

    @pl.when(i == 0)
    def _():
        kbd_ref[...] = jnp.zeros_like(kbd_ref)
        ikbd_ref[...] = jnp.zeros_like(ikbd_ref)
        d = DSA_HEAD_DIM
        for j in range(kbd_ref.shape[0]):
            kt = kkt_ref[0, :, j * tk:(j + 1) * tk]
            kbd_ref[j, :d, :tk] = kt[:d]
            kbd_ref[j, d:, tk:] = kt[:d]
            ikbd_ref[j, :d, :tk] = kt[d:]
            ikbd_ref[j, d:, tk:] = kt[d:]

    iw = vw_ref[0][:, DSA_HEAD_DIM:DSA_HEAD_DIM + IDX_HEADS]

    def score_group(g, carry):
        nt = jnp.minimum(nj - g * gt, gt)

        def score_tile(jj):
            sc = jnp.zeros((t, tk), F32)
            for p in range(n_pairs):
                r = _mm(iq_ref[0, :, p * LANES:(p + 1) * LANES], ikbd_ref[g * gt + jj])
                sc = sc + jnp.maximum(r[:, :tk], 0.0) * iw[:, 2 * p:2 * p + 1]
                sc = sc + jnp.maximum(r[:, tk:], 0.0) * iw[:, 2 * p + 1:2 * p + 2]
            sc = jnp.where(sc == 0.0, 0.0, sc)
            bits = pltpu.bitcast(sc, jnp.int32)
            key_ref[jj] = bits ^ ((bits >> 31) | INT_MIN)

        def score_two(k, c2):
            score_tile(2 * k)
            score_tile(2 * k + 1)
            return c2

        lax.fori_loop(0, nt // 2, score_two, 0)

        @pl.when(nt % 2 == 1)
        def _():
            score_tile(nt - 1)

        def clear_tile(jj, c2):
            key_ref[jj] = jnp.zeros((t, tk), jnp.int32)
            return c2

        lax.fori_loop(nt, gt, clear_tile, 0)

        def to_planes(rg, c2):
            rows = pl.ds(pl.multiple_of(rg * SUBLANES, SUBLANES), SUBLANES)
            words = []
            for k in range(32):
                ch = 31 - k
                words.append(key_ref[ch // cpt, rows, (ch % cpt) * LANES:(ch % cpt + 1) * LANES])
            for p, w in enumerate(_bit_transpose32(words)):
                plane_ref[g, p, rows, :] = w
            return c2

        lax.fori_loop(0, t // SUBLANES, to_planes, 0)
        return carry

    lax.fori_loop(0, ng, score_group, 0)

    lane = lax.broadcasted_iota(jnp.int32, (t, LANES), 1)
    qpos = lax.broadcasted_iota(jnp.int32, (t, LANES), 0) + q0
    chunks_valid = ((qpos - lane) >> 7) + 1

    def count_bits(words):
        tot = lax.population_count(words[0]).astype(F32)
        for w in words[1:]:
            tot = tot + lax.population_count(w).astype(F32)
        return jnp.sum(tot, axis=-1, keepdims=True)

    for g in range(n_groups):
        cand_ref[g] = _low_bits(chunks_valid - 32 * g)
        sel_ref[g] = jnp.zeros((t, LANES), jnp.int32)

    def radix_select(groups):
        def radix(p, need):
            ones = [cand_ref[g] & plane_ref[g, p] for g in groups]
            c1 = count_bits(ones)
            take = c1 >= need
            for g, one in zip(groups, ones):
                cand = cand_ref[g]
                cand_ref[g] = jnp.where(take, one, cand ^ one)
                sel_ref[g] = jnp.where(take, sel_ref[g], sel_ref[g] | one)
            return jnp.where(take, need, need - c1)

        return lax.fori_loop(0, 32, radix, jnp.full((t, 1), float(topk), F32))

    if n_groups == 1:
        need = radix_select(range(1))
    else:
        need = lax.switch(ng - 1, [functools.partial(radix_select, range(n)) for n in range(1, n_groups + 1)])

    has_tie = jnp.where(count_bits([cand_ref[g] for g in range(n_groups)]) > need, 1.0, 0.0)
    any_tie = jnp.max(has_tie) > 0.0

    @pl.when(jnp.logical_not(any_tie))
    def _():
        for g in range(n_groups):
            sel_ref[g] = sel_ref[g] | cand_ref[g]

    @pl.when(any_tie)
    def _():
        n_bits = max(1, int(seq_len - 1).bit_length())

        def below(g, cut):
            return cand_ref[g] & _low_bits(((cut - lane + (LANES - 1)) >> 7) - 32 * g)

        def bisect_idx(bit, pfx):
            cut = pfx | (jnp.int32(1) << (n_bits - 1 - bit))
            cnt = count_bits([below(g, cut) for g in range(n_groups)])
            return jnp.where(cnt < need, cut, pfx)

        cut = lax.fori_loop(0, n_bits, bisect_idx, jnp.zeros((t, 1), jnp.int32))
        cut = jnp.where(has_tie > 0.0, cut + 1, seq_len)
        for g in range(n_groups):
            sel_ref[g] = sel_ref[g] | below(g, cut)

    neg_bits = int(np.float32(NEG_BIG).view(np.int32))

    def tile_bias(j):
        words = sel_ref[j // gt]
        parts = []
        for c in range(cpt):
            k = (j % gt) * cpt + c
            picked = lax.shift_left(words, 31 - k) >> 31
            parts.append(pltpu.bitcast(neg_bits & ~picked, F32))
        return jnp.concatenate(parts, axis=-1)

    acc_ref[...] = jnp.zeros(acc_ref.shape, F32)
    use_fast = fast_ref[0] > 0

    @pl.when(use_fast)
    def _():
        def attend(j):
            bias = tile_bias(j)
            for p in range(n_pairs):
                lg = _mm(q_ref[0, :, p * LANES:(p + 1) * LANES], kbd_ref[j])
                for e in range(2):
                    pe = jnp.exp2(lg[:, e * tk:(e + 1) * tk] + bias).astype(BF16)
                    acc_ref[2 * p + e] += _mm(pe, v1_ref[0, j])

        def attend_two(k, carry):
            attend(2 * k)
            attend(2 * k + 1)
            return carry

        lax.fori_loop(0, nj // 2, attend_two, 0)

        @pl.when(nj % 2 == 1)
        def _():
            attend(nj - 1)

    @pl.when(jnp.logical_not(use_fast))
    def _():
        m_ref[...] = jnp.full(m_ref.shape, NEG_BIG, F32)

        def attend(j, carry):
            bias = tile_bias(j)
            for p in range(n_pairs):
                lg = _mm(q_ref[0, :, p * LANES:(p + 1) * LANES], kbd_ref[j])
                for e in range(2):
                    h = 2 * p + e
                    s = lg[:, e * tk:(e + 1) * tk] + bias
                    m_old = m_ref[h][:, :1]
                    m_new = jnp.maximum(m_old, jnp.max(s, axis=-1, keepdims=True))
                    pe = jnp.exp2(s - m_new).astype(BF16)
                    acc_ref[h] = acc_ref[h] * jnp.exp2(m_old - m_new) + _mm(pe, v1_ref[0, j])
                    m_ref[h] = jnp.broadcast_to(m_new, (t, LANES))
            return carry

        lax.fori_loop(0, nj, attend, 0)

    first_head = lax.broadcasted_iota(jnp.int32, (t, LANES), 1) < DSA_HEAD_DIM
    for p in range(n_pairs):
        a0, a1 = acc_ref[2 * p], acc_ref[2 * p + 1]
        o0 = a0 / a0[:, DSA_HEAD_DIM:DSA_HEAD_DIM + 1]
        o1 = a1 / a1[:, DSA_HEAD_DIM:DSA_HEAD_DIM + 1]
        o_ref[0, :, p * LANES:(p + 1) * LANES] = jnp.where(
            first_head, o0, pltpu.roll(o1, DSA_HEAD_DIM, 1)).astype(BF16)


def _dsa(fast, q, iq, vw, kkt, v1, topk):
    b, s, _ = q.shape
    t = min(DSA_Q_TILE, s)
    tk = min(DSA_K_TILE, s)
    nj = s // tk
    gt = 32 * LANES // tk
    n_groups = -(-nj // gt)
    tok = lambda width: pl.BlockSpec((1, t, width), lambda bi, i: (bi, i, 0))
    per_b = lambda shape: pl.BlockSpec((1,) + shape, lambda bi, i: (bi,) + (0,) * len(shape),
                                       pipeline_mode=pl.Buffered(1))
    return pl.pallas_call(
        functools.partial(_dsa_kernel, topk=topk, seq_len=s),
        grid=(b, s // t),
        in_specs=[pl.BlockSpec(memory_space=pltpu.SMEM),
                  tok(GRP_Q), tok(GRP_IQ), tok(LANES),
                  per_b((LANES, s)), per_b((nj, tk, LANES))],
        out_specs=tok(GRP_Q),
        out_shape=jax.ShapeDtypeStruct((b, s, GRP_Q), BF16),
        scratch_shapes=[
            pltpu.VMEM((gt, t, tk), jnp.int32),
            pltpu.VMEM((n_groups, 32, t, LANES), jnp.int32),
            pltpu.VMEM((n_groups, t, LANES), jnp.int32),
            pltpu.VMEM((n_groups, t, LANES), jnp.int32),
            pltpu.VMEM((DSA_HEADS, t, LANES), F32),
            pltpu.VMEM((DSA_HEADS, t, LANES), F32),
            pltpu.VMEM((nj, LANES, 2 * tk), BF16),
            pltpu.VMEM((nj, LANES, 2 * tk), BF16),
        ],
        compiler_params=_params(2),
        name="dsa",
    )(fast, q, iq, vw, kkt, v1.reshape(b, nj, tk, LANES))


def _split3_bf16(x):
    hi = x.astype(BF16)
    r1 = x - hi.astype(F32)
    mid = r1.astype(BF16)
    lo = (r1 - mid.astype(F32)).astype(BF16)
    return hi, mid, lo


def _hgrn_kernel(zh_ref, lb_ref, onorm_ref, o_ref, state_ref):
    i = pl.program_id(1)
    t = zh_ref.shape[1]
    c = HG_CHUNK
    width = HG_HEADS * HG_DK

    @pl.when(i == 0)
    def _():
        state_ref[...] = jnp.zeros_like(state_ref)

    rr = lax.broadcasted_iota(jnp.int32, (c, c), 0)
    cc = lax.broadcasted_iota(jnp.int32, (c, c), 1)
    tril = rr >= cc
    tril_b = jnp.where(tril, 1.0, 0.0).astype(BF16)

    n_chunks = t // c
    chunk = lambda n: slice(n * c, (n + 1) * c)
    head = lambda h: slice(h * HG_DK, (h + 1) * HG_DK)
    zq, zf, v, zg = (zh_ref[0, :, g * width:(g + 1) * width] for g in range(4))
    lb = lb_ref[...]
    log_lb, log1m_lb, one_m_lb = lb[0:1], lb[1:2], lb[2:3]

    e = jnp.exp(-jnp.abs(zf))
    b_ = log1m_lb + jnp.minimum(zf, 0.0) - jnp.log1p(e)
    log_f = jnp.maximum(log_lb, b_) + jnp.log1p(jnp.exp(-jnp.abs(log_lb - b_)))
    hk = one_m_lb * jnp.where(zf >= 0.0, e, 1.0) / (1.0 + e)
    hq = zq * jax.nn.sigmoid(zq)
    vb = v.astype(BF16)
    hi, mid, lo = _split3_bf16(log_f)
    cums = [_mm(tril_b, hi[chunk(n)]) + _mm(tril_b, mid[chunk(n)]) + _mm(tril_b, lo[chunk(n)])
            for n in range(n_chunks)]
    q_intra, k_intra, q_inter, k_state, decay = [], [], [], [], []
    for n in range(n_chunks):
        cum = cums[n]
        ref = cum[c // 2 - 1:c // 2]
        last = cum[c - 1:c]
        q_intra.append((hq[chunk(n)] * jnp.exp(cum - ref)).astype(BF16))
        k_intra.append((hk[chunk(n)] * jnp.exp(ref - cum)).astype(BF16))
        q_inter.append((hq[chunk(n)] * jnp.exp(cum)).astype(BF16))
        k_state.append((hk[chunk(n)] * jnp.exp(last - cum)).astype(BF16))
        decay.append(jnp.exp(last))
    att = [[_mm_nt(q_intra[n][:, head(h)], k_intra[n][:, head(h)]) for h in range(HG_HEADS)]
           for n in range(n_chunks)]
    o_intra = [[_mm(jnp.where(tril, att[n][h], 0.0).astype(BF16), vb[chunk(n), head(h)])
                for h in range(HG_HEADS)] for n in range(n_chunks)]
    states = [state_ref[h] for h in range(HG_HEADS)]
    o_inter = []
    for n in range(n_chunks):
        o_inter.append([_mm_nt(q_inter[n][:, head(h)], states[h].astype(BF16)) for h in range(HG_HEADS)])
        states = [states[h] * decay[n][:, head(h)] + _mm_tn(vb[chunk(n), head(h)], k_state[n][:, head(h)])
                  for h in range(HG_HEADS)]
    for h in range(HG_HEADS):
        state_ref[h] = states[h]
    gate = zg * jax.nn.sigmoid(zg)
    for n in range(n_chunks):
        for h in range(HG_HEADS):
            o = o_intra[n][h] + o_inter[n][h]
            o = o * _rms_scale(o) * onorm_ref[...]
            o_ref[0, chunk(n), head(h)] = (o * gate[chunk(n), head(h)]).astype(BF16)


def _hgrn(zh, lb_rows, onorm):
    b, s, _ = zh.shape
    t = min(HG_TILE, s)
    return pl.pallas_call(
        _hgrn_kernel,
        grid=(b, s // t),
        in_specs=[pl.BlockSpec((1, t, GRP_H), lambda bi, i: (bi, i, 0)),
                  pl.BlockSpec((SUBLANES, HG_HEADS * HG_DK), lambda bi, i: (0, 0)),
                  pl.BlockSpec((1, HG_DV), lambda bi, i: (0, 0))],
        out_specs=pl.BlockSpec((1, t, HG_HEADS * HG_DV), lambda bi, i: (bi, i, 0)),
        out_shape=jax.ShapeDtypeStruct((b, s, HG_HEADS * HG_DV), BF16),
        scratch_shapes=[pltpu.VMEM((HG_HEADS, HG_DV, HG_DK), F32)],
        compiler_params=_params(2),
        name="hgrn",
    )(zh, lb_rows, onorm)


def _merge_kernel(x_ref, nmix_ref, ya_ref, yb_ref, yc_ref, ym_ref, wg_ref, wl_ref, wo_ref, o_ref):
    x = x_ref[0]
    hb = (x * _rms_scale(x) * nmix_ref[...]).astype(BF16)
    merged = None
    for n, y_ref in enumerate((ya_ref, yb_ref, yc_ref, ym_ref)):
        gate = jax.nn.sigmoid(_mm(hb, wg_ref[:, n * D_MODEL:(n + 1) * D_MODEL]))
        term = gate * _mm(y_ref[0], wl_ref[n])
        merged = term if merged is None else merged + term
    o_ref[0] = x + _mm(merged.astype(BF16), wo_ref[...])


def _merge(layer, x, nmix, ya, yb, yc, ym, wg, wl, wo):
    b, s, _ = x.shape
    t = min(MERGE_TILE, s)
    tok = lambda width: pl.BlockSpec((1, t, width), lambda bi, i: (bi, i, 0))
    return pl.pallas_call(
        _merge_kernel,
        grid=(b, s // t),
        in_specs=[tok(D_MODEL), pl.BlockSpec((1, D_MODEL), lambda bi, i: (0, 0)),
                  tok(BRANCH_DIM), tok(BRANCH_DIM), tok(BRANCH_DIM), tok(BRANCH_DIM),
                  pl.BlockSpec((None, D_MODEL, N_BRANCH * D_MODEL), lambda bi, i: (layer, 0, 0)),
                  pl.BlockSpec((None, N_BRANCH, BRANCH_DIM, D_MODEL), lambda bi, i: (layer, 0, 0, 0)),
                  pl.BlockSpec((None, D_MODEL, D_MODEL), lambda bi, i: (layer, 0, 0))],
        out_specs=tok(D_MODEL),
        out_shape=jax.ShapeDtypeStruct((b, s, D_MODEL), F32),
        compiler_params=_params(2),
        name="merge",
    )(x, nmix, ya, yb, yc, ym, wg, wl, wo)


def _ffn_kernel(x_ref, nffn_ref, wup_ref, wdn_ref, o_ref):
    x = x_ref[0]
    hb = (x * _rms_scale(x) * nffn_ref[...]).astype(BF16)
    out = x
    for n in range(FFN_DIM // FFN_CHUNK):
        lo = n * FFN_CHUNK
        gate = _mm(hb, wup_ref[:, lo:lo + FFN_CHUNK])
        up = _mm(hb, wup_ref[:, FFN_DIM + lo:FFN_DIM + lo + FFN_CHUNK])
        act = (gate * jax.nn.sigmoid(gate) * up).astype(BF16)
        out = out + _mm(act, wdn_ref[lo:lo + FFN_CHUNK, :])
    o_ref[0] = out


def _ffn(layer, x, nffn, wup, wdn):
    b, s, _ = x.shape
    t = min(FFN_TILE, s)
    tok = pl.BlockSpec((1, t, D_MODEL), lambda bi, i: (bi, i, 0))
    return pl.pallas_call(
        _ffn_kernel,
        grid=(b, s // t),
        in_specs=[tok, pl.BlockSpec((1, D_MODEL), lambda bi, i: (0, 0)),
                  pl.BlockSpec((None, D_MODEL, 2 * FFN_DIM), lambda bi, i: (layer, 0, 0)),
                  pl.BlockSpec((None, FFN_DIM, D_MODEL), lambda bi, i: (layer, 0, 0))],
        out_specs=tok,
        out_shape=jax.ShapeDtypeStruct((b, s, D_MODEL), F32),
        compiler_params=_params(2),
        name="ffn",
    )(x, nffn, wup, wdn)


def _rope_constants():
    inv_freq = 1.0 / (ROPE_THETA ** (jnp.arange(0, ROT_DIM, 2, dtype=F32) / ROT_DIM))
    invf = jnp.concatenate([inv_freq, inv_freq]).reshape(1, ROT_DIM)
    place = np.zeros((ROT_DIM, 3 * LANES), np.float32)
    ones = np.zeros((1, LANES), np.float32)
    for lane in range(LANES):
        d = lane % DSA_HEAD_DIM
        if d < ROT_HALF:
            place[d, lane] = 1.0
            place[ROT_HALF + d, LANES + lane] = -1.0
        elif d < ROT_DIM:
            place[d - ROT_HALF, lane] = 1.0
            place[d, 2 * LANES + lane] = 1.0
        else:
            ones[0, lane] = 1.0
    return invf, jnp.asarray(place, BF16), jnp.asarray(ones)


def _relayout_w_in(w_in):
    offs = np.concatenate([[0], np.cumsum(SPLIT_SIZES)])
    col = lambda n: w_in[..., int(offs[n]):int(offs[n + 1])]
    (a_x, a_b, a_c, d_q, d_k, d_v, i_q, i_k, i_w, g_q, g_f, g_i, g_g, m_q, gates) = (col(n) for n in range(15))
    pad = jnp.zeros(w_in.shape[:-1] + (LANES - DSA_HEAD_DIM - IDX_HEADS,), w_in.dtype)
    w1 = jnp.concatenate([a_x, a_b, a_c, d_q, d_k, i_k, d_v, i_w, pad, i_q, g_q, g_f, g_i, g_g, m_q], axis=-1)
    return w1.astype(BF16), gates.astype(BF16)


def kernel(x, mem, positions, norm_mix, w_in, conv_w, dsa_q_norm, dsa_k_norm, hgrn_lower_bounds,
           hgrn_out_norm, mem_norm, mem_w_kv, mem_q_norm, mem_k_norm, w_lift, w_out, norm_ffn,
           ffn_w_up, ffn_w_down):
    b, s, d = x.shape
    depth = w_in.shape[0]
    assert d == D_MODEL and w_in.shape[2] == sum(SPLIT_SIZES)
    assert s % HG_CHUNK == 0 and s % min(DSA_K_TILE, s) == 0
    topk = min(TOPK_MAX, s // 4)

    invf, rplace, rones = _rope_constants()
    ang = positions.astype(F32)[..., None] * invf
    cs = jnp.where(jnp.arange(ROT_DIM) < ROT_HALF, jnp.cos(ang), jnp.sin(ang))
    lbs = jnp.cumsum(jax.nn.softmax(hgrn_lower_bounds.astype(F32), axis=0), axis=0)
    lbs = lbs - lbs[0:1]
    g64 = jnp.asarray(np.kron(np.eye(DSA_HEADS), np.full((DSA_HEAD_DIM, DSA_HEAD_DIM), 1.0 / DSA_HEAD_DIM)), BF16)
    row = lambda v: v.reshape(1, -1).astype(F32)
    w1, wg = _relayout_w_in(w_in)
    wkv, wl, wo = mem_w_kv.astype(BF16), w_lift.astype(BF16), w_out.astype(BF16)
    wup, wdn = ffn_w_up.astype(BF16), ffn_w_down.astype(BF16)

    for l in range(depth):
        lb = lbs[l]
        lb_rows = jnp.concatenate([jnp.stack([jnp.log(lb), jnp.log1p(-lb), 1.0 - lb]),
                                   jnp.zeros((SUBLANES - 3, lb.shape[0]), F32)])
        convw = jnp.concatenate([conv_w[l], jnp.zeros((SUBLANES - CONV_WIDTH, CONV_DIM), F32)])
        mkt, mv = _mem_kv(l, mem, row(mem_norm), wkv, row(mem_k_norm[l]))
        ya, q, iq, kkt, vw, v1, zh, ym = _proj(
            l, x, row(norm_mix[l]), w1, cs, rplace, rones, convw,
            row(jnp.tile(dsa_q_norm[l], DSA_HEADS)),
            row(jnp.concatenate([dsa_k_norm[l], jnp.ones((LANES - DSA_HEAD_DIM,), F32)])),
            g64, mkt, mv, row(mem_q_norm[l]))
        logit_bound = DSA_HEAD_DIM ** 0.5 * jnp.max(jnp.abs(dsa_q_norm[l])) * jnp.max(jnp.abs(dsa_k_norm[l]))
        fast = (logit_bound <= SAFE_LOGIT_BOUND).astype(jnp.int32).reshape(1)
        yb = _dsa(fast, q, iq, vw, kkt, v1, topk)
        yc = _hgrn(zh, lb_rows, row(hgrn_out_norm[l]))
        x = _merge(l, x, row(norm_mix[l]), ya, yb, yc, ym, wg, wl, wo)
        x = _ffn(l, x, row(norm_ffn[l]), wup, wdn)
    return x
```

```python
import functools

import jax
import jax.numpy as jnp
import numpy as np
from jax import lax
from jax.experimental import pallas as pl
from jax.experimental.pallas import tpu as pltpu

D_MODEL = 1024
MEM_TOKENS = 256
N_BRANCH = 4
BRANCH_DIM = 512
CONV_DIM = 512
CONV_WIDTH = 3
DSA_HEADS = 8
DSA_HEAD_DIM = 64
IDX_HEADS = 8
IDX_DIM = 64
TOPK_MAX = 256
HG_HEADS = 4
HG_DK = 128
HG_DV = 128
HG_CHUNK = 64
MEM_HEADS = 4
MEM_HEAD_DIM = 128
ROPE_THETA = 500000.0
ROT_DIM = DSA_HEAD_DIM // 4
ROT_HALF = ROT_DIM // 2
FFN_DIM = ((8 * D_MODEL // 3 + 255) // 256) * 256
EPS = 1e-6

SPLIT_SIZES = (CONV_DIM, CONV_DIM, CONV_DIM,
               DSA_HEADS * DSA_HEAD_DIM, DSA_HEAD_DIM, DSA_HEAD_DIM,
               IDX_HEADS * IDX_DIM, IDX_DIM, IDX_HEADS,
               HG_HEADS * HG_DK, HG_HEADS * HG_DK, HG_HEADS * HG_DV, HG_HEADS * HG_DV,
               MEM_HEADS * MEM_HEAD_DIM,
               N_BRANCH * D_MODEL)

LANES = 128
SUBLANES = 8
VMEM_LIMIT_BYTES = 56 * 1024 * 1024

PROJ_TILE = 512
DSA_Q_TILE = 512
DSA_K_TILE = 512
RADIX_PASSES_PER_TRIP = 8
HG_TILE = 512
MERGE_TILE = 512
FFN_TILE = 512
FFN_CHUNK = FFN_DIM // 2

GRP_A = 3 * CONV_DIM
GRP_Q = DSA_HEADS * DSA_HEAD_DIM
GRP_KV = 2 * LANES
GRP_IQ = IDX_HEADS * IDX_DIM
GRP_H = 4 * HG_HEADS * HG_DK
GRP_M = MEM_HEADS * MEM_HEAD_DIM
OFF_A = 0
OFF_Q = OFF_A + GRP_A
OFF_KV = OFF_Q + GRP_Q
OFF_IQ = OFF_KV + GRP_KV
OFF_H = OFF_IQ + GRP_IQ
OFF_M = OFF_H + GRP_H
PROJ_COLS = OFF_M + GRP_M

NEG_BIG = -1e30
INT_MIN = -(2 ** 31)
LOG2E = 1.4426950408889634
Q_SCALE = DSA_HEAD_DIM ** -0.5 * LOG2E
SAFE_LOGIT_BOUND = 40.0

BF16 = jnp.bfloat16
F32 = jnp.float32


def _mm(a, b):
    return jnp.dot(a, b, preferred_element_type=F32)


def _mm_nt(a, b):
    return lax.dot_general(a, b, (((1,), (1,)), ((), ())), preferred_element_type=F32)


def _mm_tn(a, b):
    return lax.dot_general(a, b, (((0,), (0,)), ((), ())), preferred_element_type=F32)


def _rms_scale(x):
    return lax.rsqrt(jnp.mean(x * x, axis=-1, keepdims=True) + EPS)


def _params(n_grid):
    return pltpu.CompilerParams(dimension_semantics=("arbitrary",) * n_grid,
                                vmem_limit_bytes=VMEM_LIMIT_BYTES)


def _mem_kv_kernel(mem_ref, mem_norm_ref, wkv_ref, knorm_ref, mkt_ref, mv_ref):
    m = mem_ref[0]
    mn = (m * _rms_scale(m) * mem_norm_ref[...]).astype(BF16)
    kv = _mm(mn, wkv_ref[...])
    for h in range(MEM_HEADS):
        kh = kv[:, h * MEM_HEAD_DIM:(h + 1) * MEM_HEAD_DIM]
        kh = kh * _rms_scale(kh) * knorm_ref[...]
        mkt_ref[0, h] = kh.T.astype(BF16)
        off = MEM_HEADS * MEM_HEAD_DIM + h * MEM_HEAD_DIM
        mv_ref[0, h] = kv[:, off:off + MEM_HEAD_DIM].astype(BF16)


def _mem_kv(layer, mem, mem_norm, wkv, knorm):
    b = mem.shape[0]
    return pl.pallas_call(
        _mem_kv_kernel,
        grid=(b,),
        in_specs=[
            pl.BlockSpec((1, MEM_TOKENS, D_MODEL), lambda i: (i, 0, 0)),
            pl.BlockSpec((1, D_MODEL), lambda i: (0, 0)),
            pl.BlockSpec((None, D_MODEL, 2 * GRP_M), lambda i: (layer, 0, 0)),
            pl.BlockSpec((1, MEM_HEAD_DIM), lambda i: (0, 0)),
        ],
        out_specs=[
            pl.BlockSpec((1, MEM_HEADS, MEM_HEAD_DIM, MEM_TOKENS), lambda i: (i, 0, 0, 0)),
            pl.BlockSpec((1, MEM_HEADS, MEM_TOKENS, MEM_HEAD_DIM), lambda i: (i, 0, 0, 0)),
        ],
        out_shape=[
            jax.ShapeDtypeStruct((b, MEM_HEADS, MEM_HEAD_DIM, MEM_TOKENS), BF16),
            jax.ShapeDtypeStruct((b, MEM_HEADS, MEM_TOKENS, MEM_HEAD_DIM), BF16),
        ],
        compiler_params=_params(1),
        name="mem_kv",
    )(mem, mem_norm, wkv, knorm)


def _rope_slab(s, c, sa, sb):
    return s * c + pltpu.roll(s, LANES - ROT_HALF, 1) * sa + pltpu.roll(s, ROT_HALF, 1) * sb


def _proj_kernel(x_ref, nmix_ref, w_ref, cs_ref, rplace_ref, rones_ref, convw_ref, qg_ref, kg_ref,
                 g64_ref, mkt_ref, mv_ref, mqg_ref,
                 ya_ref, q_ref, iq_ref, kkt_ref, vw_ref, v1_ref, zh_ref, ym_ref,
                 carry_ref):
    i = pl.program_id(1)
    t = x_ref.shape[1]
    x = x_ref[0]
    hb = (x * _rms_scale(x) * nmix_ref[...]).astype(BF16)

    cs_hi, cs_mid, cs_lo = _split3_bf16(cs_ref[0])
    place = rplace_ref[...]
    tabs = _mm_tn(cs_hi, place) + _mm_tn(cs_mid, place) + _mm_tn(cs_lo, place)
    rc = tabs[:, :LANES] + rones_ref[...]
    rsa = tabs[:, LANES:2 * LANES]
    rsb = tabs[:, 2 * LANES:]

    za = _mm(hb, w_ref[:, OFF_A:OFF_A + GRP_A])
    a_x, a_b, a_c = za[:, :CONV_DIM], za[:, CONV_DIM:2 * CONV_DIM], za[:, 2 * CONV_DIM:]
    u = a_c * a_x

    @pl.when(i == 0)
    def _():
        carry_ref[...] = jnp.zeros_like(carry_ref)

    carry = carry_ref[...]
    row8 = lax.broadcasted_iota(jnp.int32, (SUBLANES, CONV_DIM), 0)
    r1 = pltpu.roll(u, 1, 0)
    r2 = pltpu.roll(u, 2, 0)
    top1 = jnp.where(row8 < 1, pltpu.roll(carry, 1, 0), r1[:SUBLANES])
    top2 = jnp.where(row8 < 2, pltpu.roll(carry, 2, 0), r2[:SUBLANES])
    u1 = jnp.concatenate([top1, r1[SUBLANES:]], axis=0)
    u2 = jnp.concatenate([top2, r2[SUBLANES:]], axis=0)
    carry_ref[...] = u[t - SUBLANES:]
    cw = convw_ref[...]
    ya_ref[0] = (a_b * (u2 * cw[0:1] + u1 * cw[1:2] + u * cw[2:3])).astype(BF16)

    zq = _mm(hb, w_ref[:, OFF_Q:OFF_Q + GRP_Q])
    msq = _mm((zq * zq).astype(BF16), g64_ref[...])
    qn = zq * lax.rsqrt(msq + EPS) * qg_ref[...]
    for p in range(GRP_Q // LANES):
        sl = slice(p * LANES, (p + 1) * LANES)
        q_ref[0, :, sl] = (_rope_slab(qn[:, sl], rc, rsa, rsb) * Q_SCALE).astype(BF16)

    zkv = _mm(hb, w_ref[:, OFF_KV:OFF_KV + GRP_KV])
    s0, s1 = zkv[:, :LANES], zkv[:, LANES:]
    lane = lax.broadcasted_iota(jnp.int32, (t, LANES), 1)
    is_k = lane < DSA_HEAD_DIM
    kms = jnp.sum(jnp.where(is_k, s0 * s0, 0.0), axis=-1, keepdims=True) * (1.0 / DSA_HEAD_DIM)
    s0 = s0 * jnp.where(is_k, lax.rsqrt(kms + EPS) * kg_ref[...], 1.0)
    kkt_ref[0] = _rope_slab(s0, rc, rsa, rsb).T.astype(BF16)
    w_scale = jnp.where(lane >= DSA_HEAD_DIM, jnp.where(lane < DSA_HEAD_DIM + IDX_HEADS, IDX_HEADS ** -0.5, 1.0), 1.0)
    vw_ref[0] = s1 * w_scale
    v1_ref[0] = jnp.where(is_k, s1, jnp.where(lane == DSA_HEAD_DIM, 1.0, 0.0)).astype(BF16)

    ziq = _mm(hb, w_ref[:, OFF_IQ:OFF_IQ + GRP_IQ])
    for p in range(GRP_IQ // LANES):
        sl = slice(p * LANES, (p + 1) * LANES)
        iq_ref[0, :, sl] = (_rope_slab(ziq[:, sl], rc, rsa, rsb) * (IDX_DIM ** -0.5)).astype(BF16)

    zm = _mm(hb, w_ref[:, OFF_M:OFF_M + GRP_M])
    for h in range(MEM_HEADS):
        sl = slice(h * MEM_HEAD_DIM, (h + 1) * MEM_HEAD_DIM)
        mq = zm[:, sl]
        mq = mq * _rms_scale(mq) * mqg_ref[...] * (MEM_HEAD_DIM ** -0.5)
        lg = _mm(mq.astype(BF16), mkt_ref[0, h])
        pe = jnp.exp(lg - jnp.max(lg, axis=-1, keepdims=True))
        den = jnp.sum(pe, axis=-1, keepdims=True)
        ym_ref[0, :, sl] = (_mm(pe.astype(BF16), mv_ref[0, h]) / den).astype(BF16)

    zh_ref[0] = _mm(hb, w_ref[:, OFF_H:OFF_H + GRP_H])


def _proj(layer, x, nmix, w1, cs, rplace, rones, convw, qg, kg, g64, mkt, mv, mqg):
    b, s, _ = x.shape
    t = min(PROJ_TILE, s)
    tok = lambda width: pl.BlockSpec((1, t, width), lambda bi, i: (bi, i, 0))
    const2 = lambda shape: pl.BlockSpec(shape, lambda bi, i: (0, 0))
    per_b4 = lambda shape: pl.BlockSpec((1,) + shape, lambda bi, i: (bi, 0, 0, 0))
    out_widths = (CONV_DIM, GRP_Q, GRP_IQ, LANES, LANES, LANES, GRP_H, GRP_M)
    out_dtypes = (BF16, BF16, BF16, BF16, F32, BF16, F32, BF16)
    return pl.pallas_call(
        _proj_kernel,
        grid=(b, s // t),
        in_specs=[
            tok(D_MODEL), const2((1, D_MODEL)),
            pl.BlockSpec((None, D_MODEL, PROJ_COLS), lambda bi, i: (layer, 0, 0)),
            pl.BlockSpec((1, ROT_DIM, t), lambda bi, i: (bi, 0, i)),
            const2((ROT_DIM, 3 * LANES)), const2((1, LANES)),
            const2((SUBLANES, CONV_DIM)), const2((1, GRP_Q)), const2((1, LANES)),
            const2((GRP_Q, GRP_Q)),
            per_b4((MEM_HEADS, MEM_HEAD_DIM, MEM_TOKENS)), per_b4((MEM_HEADS, MEM_TOKENS, MEM_HEAD_DIM)),
            const2((1, MEM_HEAD_DIM)),
        ],
        out_specs=[tok(w) if n != 3 else pl.BlockSpec((1, LANES, t), lambda bi, i: (bi, 0, i))
                   for n, w in enumerate(out_widths)],
        out_shape=[jax.ShapeDtypeStruct((b, s, w) if n != 3 else (b, LANES, s), d)
                   for n, (w, d) in enumerate(zip(out_widths, out_dtypes))],
        scratch_shapes=[pltpu.VMEM((SUBLANES, CONV_DIM), F32)],
        compiler_params=_params(2),
        name="proj",
    )(x, nmix, w1, cs, rplace, rones, convw, qg, kg, g64, mkt, mv, mqg)


def _bit_transpose32(words):
    a = list(words)
    for j, m in ((16, 0x0000FFFF), (8, 0x00FF00FF), (4, 0x0F0F0F0F), (2, 0x33333333), (1, 0x55555555)):
        for k in range(32):
            if k & j:
                continue
            tmp = (a[k] ^ lax.shift_right_logical(a[k + j], jnp.int32(j))) & m
            a[k] = a[k] ^ tmp
            a[k + j] = a[k + j] ^ lax.shift_left(tmp, jnp.int32(j))
    return a


def _low_bits(n):
    return jnp.where(n >= 32, -1, jnp.where(n <= 0, 0, lax.shift_left(jnp.int32(1), n) - 1))


def _dsa_kernel(fast_ref, q_ref, iq_ref, vw_ref, kkt_ref, v1_ref, o_ref,
                key_ref, plane_ref, cand_ref, sel_ref, m_ref, acc_ref, kbd_ref, ikbd_ref, *, topk, seq_len):
    i = pl.program_id(1)
    t = q_ref.shape[1]
    gt, _, tk = key_ref.shape
    n_groups = plane_ref.shape[0]
    cpt = tk // LANES
    n_pairs = DSA_HEADS // 2
    q0 = i * t
    nj = (q0 + t - 1) // tk + 1
    ng = (nj + gt - 1) // gt

    @pl.when(jnp.logical_and(pl.program_id(0) == 0, i == 0))
    def _():
        plane_ref[...] = jnp.zeros_like(plane_ref)

    @pl.when(i == 0)
    def _():
        kbd_ref[...] = jnp.zeros_like(kbd_ref)
        ikbd_ref[...] = jnp.zeros_like(ikbd_ref)
        d = DSA_HEAD_DIM
        for j in range(kbd_ref.shape[0]):
            kt = kkt_ref[0, :, j * tk:(j + 1) * tk]
            kbd_ref[j, :d, :tk] = kt[:d]
            kbd_ref[j, d:, tk:] = kt[:d]
            ikbd_ref[j, :d, :tk] = kt[d:]
            ikbd_ref[j, d:, tk:] = kt[d:]

    iw = vw_ref[0][:, DSA_HEAD_DIM:DSA_HEAD_DIM + IDX_HEADS]

    def score_group(g, carry):
        nt = jnp.minimum(nj - g * gt, gt)

        def score_tile(jj):
            sc = jnp.zeros((t, tk), F32)
            for p in range(n_pairs):
                r = _mm(iq_ref[0, :, p * LANES:(p + 1) * LANES], ikbd_ref[g * gt + jj])
                sc = sc + jnp.maximum(r[:, :tk], 0.0) * iw[:, 2 * p:2 * p + 1]
                sc = sc + jnp.maximum(r[:, tk:], 0.0) * iw[:, 2 * p + 1:2 * p + 2]
            sc = jnp.where(sc == 0.0, 0.0, sc)
            bits = pltpu.bitcast(sc, jnp.int32)
            key_ref[jj] = bits ^ ((bits >> 31) | INT_MIN)

        def score_two(k, c2):
            score_tile(2 * k)
            score_tile(2 * k + 1)
            return c2

        lax.fori_loop(0, nt // 2, score_two, 0)

        @pl.when(nt % 2 == 1)
        def _():
            score_tile(nt - 1)

        def clear_tile(jj, c2):
            key_ref[jj] = jnp.zeros((t, tk), jnp.int32)
            return c2

        lax.fori_loop(nt, gt, clear_tile, 0)

        def to_planes(rg, c2):
            rows = pl.ds(pl.multiple_of(rg * SUBLANES, SUBLANES), SUBLANES)
            words = []
            for k in range(32):
                ch = 31 - k
                words.append(key_ref[ch // cpt, rows, (ch % cpt) * LANES:(ch % cpt + 1) * LANES])
            for p, w in enumerate(_bit_transpose32(words)):
                plane_ref[g, p, rows, :] = w
            return c2

        lax.fori_loop(0, t // SUBLANES, to_planes, 0)
        return carry

    lax.fori_loop(0, ng, score_group, 0)

    lane = lax.broadcasted_iota(jnp.int32, (t, LANES), 1)
    qpos = lax.broadcasted_iota(jnp.int32, (t, LANES), 0) + q0
    chunks_valid = ((qpos - lane) >> 7) + 1

    def count_bits(words):
        tot = lax.population_count(words[0])
        for w in words[1:]:
            tot = tot + lax.population_count(w)
        return jnp.sum(tot.astype(F32), axis=-1, keepdims=True)

    for g in range(n_groups):
        cand_ref[g] = _low_bits(chunks_valid - 32 * g)
        sel_ref[g] = jnp.zeros((t, LANES), jnp.int32)

    def radix_select(groups):
        def radix(p, need):
            ones = [cand_ref[g] & plane_ref[g, p] for g in groups]
            c1 = count_bits(ones)
            take = c1 >= need
            for g, one in zip(groups, ones):
                cand = cand_ref[g]
                cand_ref[g] = jnp.where(take, one, cand ^ one)
                sel_ref[g] = jnp.where(take, sel_ref[g], sel_ref[g] | one)
            return jnp.where(take, need, need - c1)

        def radix_block(k, need):
            for u in range(RADIX_PASSES_PER_TRIP):
                need = radix(k * RADIX_PASSES_PER_TRIP + u, need)
            return need

        return lax.fori_loop(0, 32 // RADIX_PASSES_PER_TRIP, radix_block, jnp.full((t, 1), float(topk), F32))

    if n_groups == 1:
        need = radix_select(range(1))
    else:
        need = lax.switch(ng - 1, [functools.partial(radix_select, range(n)) for n in range(1, n_groups + 1)])

    has_tie = jnp.where(count_bits([cand_ref[g] for g in range(n_groups)]) > need, 1.0, 0.0)
    any_tie = jnp.max(has_tie) > 0.0

    @pl.when(jnp.logical_not(any_tie))
    def _():
        for g in range(n_groups):
            sel_ref[g] = sel_ref[g] | cand_ref[g]

    @pl.when(any_tie)
    def _():
        n_bits = max(1, int(seq_len - 1).bit_length())

        def below(g, cut):
            return cand_ref[g] & _low_bits(((cut - lane + (LANES - 1)) >> 7) - 32 * g)

        def bisect_idx(bit, pfx):
            cut = pfx | (jnp.int32(1) << (n_bits - 1 - bit))
            cnt = count_bits([below(g, cut) for g in range(n_groups)])
            return jnp.where(cnt < need, cut, pfx)

        cut = lax.fori_loop(0, n_bits, bisect_idx, jnp.zeros((t, 1), jnp.int32))
        cut = jnp.where(has_tie > 0.0, cut + 1, seq_len)
        for g in range(n_groups):
            sel_ref[g] = sel_ref[g] | below(g, cut)

    neg_bits = int(np.float32(NEG_BIG).view(np.int32))

    def tile_bias(j):
        words = sel_ref[j // gt]
        parts = []
        for c in range(cpt):
            k = (j % gt) * cpt + c
            picked = lax.shift_left(words, 31 - k) >> 31
            parts.append(pltpu.bitcast(neg_bits & ~picked, F32))
        return jnp.concatenate(parts, axis=-1)

    acc_ref[...] = jnp.zeros(acc_ref.shape, F32)
    use_fast = fast_ref[0] > 0

    @pl.when(use_fast)
    def _():
        def attend(j):
            bias = tile_bias(j)
            for p in range(n_pairs):
                lg = _mm(q_ref[0, :, p * LANES:(p + 1) * LANES], kbd_ref[j])
                for e in range(2):
                    pe = jnp.exp2(lg[:, e * tk:(e + 1) * tk] + bias).astype(BF16)
                    acc_ref[2 * p + e] += _mm(pe, v1_ref[0, j])

        def attend_two(k, carry):
            attend(2 * k)
            attend(2 * k + 1)
            return carry

        lax.fori_loop(0, nj // 2, attend_two, 0)

        @pl.when(nj % 2 == 1)
        def _():
            attend(nj - 1)

    @pl.when(jnp.logical_not(use_fast))
    def _():
        m_ref[...] = jnp.full(m_ref.shape, NEG_BIG, F32)

        def attend(j, carry):
            bias = tile_bias(j)
            for p in range(n_pairs):
                lg = _mm(q_ref[0, :, p * LANES:(p + 1) * LANES], kbd_ref[j])
                for e in range(2):
                    h = 2 * p + e
                    s = lg[:, e * tk:(e + 1) * tk] + bias
                    m_old = m_ref[h][:, :1]
                    m_new = jnp.maximum(m_old, jnp.max(s, axis=-1, keepdims=True))
                    pe = jnp.exp2(s - m_new).astype(BF16)
                    acc_ref[h] = acc_ref[h] * jnp.exp2(m_old - m_new) + _mm(pe, v1_ref[0, j])
                    m_ref[h] = jnp.broadcast_to(m_new, (t, LANES))
            return carry

        lax.fori_loop(0, nj, attend, 0)

    first_head = lax.broadcasted_iota(jnp.int32, (t, LANES), 1) < DSA_HEAD_DIM
    for p in range(n_pairs):
        a0, a1 = acc_ref[2 * p], acc_ref[2 * p + 1]
        o0 = a0 / a0[:, DSA_HEAD_DIM:DSA_HEAD_DIM + 1]
        o1 = a1 / a1[:, DSA_HEAD_DIM:DSA_HEAD_DIM + 1]
        o_ref[0, :, p * LANES:(p + 1) * LANES] = jnp.where(
            first_head, o0, pltpu.roll(o1, DSA_HEAD_DIM, 1)).astype(BF16)


def _dsa(fast, q, iq, vw, kkt, v1, topk):
    b, s, _ = q.shape
    t = min(DSA_Q_TILE, s)
    tk = min(DSA_K_TILE, s)
    nj = s // tk
    gt = 32 * LANES // tk
    n_groups = -(-nj // gt)
    tok = lambda width: pl.BlockSpec((1, t, width), lambda bi, i: (bi, i, 0))
    per_b = lambda shape: pl.BlockSpec((1,) + shape, lambda bi, i: (bi,) + (0,) * len(shape),
                                       pipeline_mode=pl.Buffered(1))
    return pl.pallas_call(
        functools.partial(_dsa_kernel, topk=topk, seq_len=s),
        grid=(b, s // t),
        in_specs=[pl.BlockSpec(memory_space=pltpu.SMEM),
                  tok(GRP_Q), tok(GRP_IQ), tok(LANES),
                  per_b((LANES, s)), per_b((nj, tk, LANES))],
        out_specs=tok(GRP_Q),
        out_shape=jax.ShapeDtypeStruct((b, s, GRP_Q), BF16),
        scratch_shapes=[
            pltpu.VMEM((gt, t, tk), jnp.int32),
            pltpu.VMEM((n_groups, 32, t, LANES), jnp.int32),
            pltpu.VMEM((n_groups, t, LANES), jnp.int32),
            pltpu.VMEM((n_groups, t, LANES), jnp.int32),
            pltpu.VMEM((DSA_HEADS, t, LANES), F32),
            pltpu.VMEM((DSA_HEADS, t, LANES), F32),
            pltpu.VMEM((nj, LANES, 2 * tk), BF16),
            pltpu.VMEM((nj, LANES, 2 * tk), BF16),
        ],
        compiler_params=_params(2),
        name="dsa",
    )(fast, q, iq, vw, kkt, v1.reshape(b, nj, tk, LANES))


def _split3_bf16(x):
    hi = x.astype(BF16)
    r1 = x - hi.astype(F32)
    mid = r1.astype(BF16)
    lo = (r1 - mid.astype(F32)).astype(BF16)
    return hi, mid, lo


def _hgrn_kernel(zh_ref, lb_ref, onorm_ref, o_ref, state_ref):
    i = pl.program_id(1)
    t = zh_ref.shape[1]
    c = HG_CHUNK
    width = HG_HEADS * HG_DK

    @pl.when(i == 0)
    def _():
        state_ref[...] = jnp.zeros_like(state_ref)

    rr = lax.broadcasted_iota(jnp.int32, (c, c), 0)
    cc = lax.broadcasted_iota(jnp.int32, (c, c), 1)
    tril = rr >= cc
    tril_b = jnp.where(tril, 1.0, 0.0).astype(BF16)

    n_chunks = t // c
    chunk = lambda n: slice(n * c, (n + 1) * c)
    head = lambda h: slice(h * HG_DK, (h + 1) * HG_DK)
    zq, zf, v, zg = (zh_ref[0, :, g * width:(g + 1) * width] for g in range(4))
    lb = lb_ref[...]
    log_lb, log1m_lb, one_m_lb = lb[0:1], lb[1:2], lb[2:3]

    e = jnp.exp(-jnp.abs(zf))
    b_ = log1m_lb + jnp.minimum(zf, 0.0) - jnp.log1p(e)
    log_f = jnp.maximum(log_lb, b_) + jnp.log1p(jnp.exp(-jnp.abs(log_lb - b_)))
    hk = one_m_lb * jnp.where(zf >= 0.0, e, 1.0) / (1.0 + e)
    hq = zq * jax.nn.sigmoid(zq)
    vb = v.astype(BF16)
    hi, mid, lo = _split3_bf16(log_f)
    cums = [_mm(tril_b, hi[chunk(n)]) + _mm(tril_b, mid[chunk(n)]) + _mm(tril_b, lo[chunk(n)])
            for n in range(n_chunks)]
    q_intra, k_intra, q_inter, k_state, decay = [], [], [], [], []
    for n in range(n_chunks):
        cum = cums[n]
        ref = cum[c // 2 - 1:c // 2]
        last = cum[c - 1:c]
        q_intra.append((hq[chunk(n)] * jnp.exp(cum - ref)).astype(BF16))
        k_intra.append((hk[chunk(n)] * jnp.exp(ref - cum)).astype(BF16))
        q_inter.append((hq[chunk(n)] * jnp.exp(cum)).astype(BF16))
        k_state.append((hk[chunk(n)] * jnp.exp(last - cum)).astype(BF16))
        decay.append(jnp.exp(last))
    att = [[_mm_nt(q_intra[n][:, head(h)], k_intra[n][:, head(h)]) for h in range(HG_HEADS)]
           for n in range(n_chunks)]
    o_intra = [[_mm(jnp.where(tril, att[n][h], 0.0).astype(BF16), vb[chunk(n), head(h)])
                for h in range(HG_HEADS)] for n in range(n_chunks)]
    states = [state_ref[h] for h in range(HG_HEADS)]
    o_inter = []
    for n in range(n_chunks):
        o_inter.append([_mm_nt(q_inter[n][:, head(h)], states[h].astype(BF16)) for h in range(HG_HEADS)])
        states = [states[h] * decay[n][:, head(h)] + _mm_tn(vb[chunk(n), head(h)], k_state[n][:, head(h)])
                  for h in range(HG_HEADS)]
    for h in range(HG_HEADS):
        state_ref[h] = states[h]
    gate = zg * jax.nn.sigmoid(zg)
    for n in range(n_chunks):
        for h in range(HG_HEADS):
            o = o_intra[n][h] + o_inter[n][h]
            o = o * _rms_scale(o) * onorm_ref[...]
            o_ref[0, chunk(n), head(h)] = (o * gate[chunk(n), head(h)]).astype(BF16)


def _hgrn(zh, lb_rows, onorm):
    b, s, _ = zh.shape
    t = min(HG_TILE, s)
    return pl.pallas_call(
        _hgrn_kernel,
        grid=(b, s // t),
        in_specs=[pl.BlockSpec((1, t, GRP_H), lambda bi, i: (bi, i, 0)),
                  pl.BlockSpec((SUBLANES, HG_HEADS * HG_DK), lambda bi, i: (0, 0)),
                  pl.BlockSpec((1, HG_DV), lambda bi, i: (0, 0))],
        out_specs=pl.BlockSpec((1, t, HG_HEADS * HG_DV), lambda bi, i: (bi, i, 0)),
        out_shape=jax.ShapeDtypeStruct((b, s, HG_HEADS * HG_DV), BF16),
        scratch_shapes=[pltpu.VMEM((HG_HEADS, HG_DV, HG_DK), F32)],
        compiler_params=_params(2),
        name="hgrn",
    )(zh, lb_rows, onorm)


def _merge_kernel(x_ref, nmix_ref, ya_ref, yb_ref, yc_ref, ym_ref, wg_ref, wl_ref, wo_ref, o_ref):
    x = x_ref[0]
    hb = (x * _rms_scale(x) * nmix_ref[...]).astype(BF16)
    merged = None
    for n, y_ref in enumerate((ya_ref, yb_ref, yc_ref, ym_ref)):
        gate = jax.nn.sigmoid(_mm(hb, wg_ref[:, n * D_MODEL:(n + 1) * D_MODEL]))
        term = gate * _mm(y_ref[0], wl_ref[n])
        merged = term if merged is None else merged + term
    o_ref[0] = x + _mm(merged.astype(BF16), wo_ref[...])


def _merge(layer, x, nmix, ya, yb, yc, ym, wg, wl, wo):
    b, s, _ = x.shape
    t = min(MERGE_TILE, s)
    tok = lambda width: pl.BlockSpec((1, t, width), lambda bi, i: (bi, i, 0))
    return pl.pallas_call(
        _merge_kernel,
        grid=(b, s // t),
        in_specs=[tok(D_MODEL), pl.BlockSpec((1, D_MODEL), lambda bi, i: (0, 0)),
                  tok(BRANCH_DIM), tok(BRANCH_DIM), tok(BRANCH_DIM), tok(BRANCH_DIM),
                  pl.BlockSpec((None, D_MODEL, N_BRANCH * D_MODEL), lambda bi, i: (layer, 0, 0)),
                  pl.BlockSpec((None, N_BRANCH, BRANCH_DIM, D_MODEL), lambda bi, i: (layer, 0, 0, 0)),
                  pl.BlockSpec((None, D_MODEL, D_MODEL), lambda bi, i: (layer, 0, 0))],
        out_specs=tok(D_MODEL),
        out_shape=jax.ShapeDtypeStruct((b, s, D_MODEL), F32),
        compiler_params=_params(2),
        name="merge",
    )(x, nmix, ya, yb, yc, ym, wg, wl, wo)


def _ffn_kernel(x_ref, nffn_ref, wup_ref, wdn_ref, o_ref):
    x = x_ref[0]
    hb = (x * _rms_scale(x) * nffn_ref[...]).astype(BF16)
    out = x
    for n in range(FFN_DIM // FFN_CHUNK):
        lo = n * FFN_CHUNK
        gate = _mm(hb, wup_ref[:, lo:lo + FFN_CHUNK])
        up = _mm(hb, wup_ref[:, FFN_DIM + lo:FFN_DIM + lo + FFN_CHUNK])
        act = (gate * jax.nn.sigmoid(gate) * up).astype(BF16)
        out = out + _mm(act, wdn_ref[lo:lo + FFN_CHUNK, :])
    o_ref[0] = out


def _ffn(layer, x, nffn, wup, wdn):
    b, s, _ = x.shape
    t = min(FFN_TILE, s)
    tok = pl.BlockSpec((1, t, D_MODEL), lambda bi, i: (bi, i, 0))
    return pl.pallas_call(
        _ffn_kernel,
        grid=(b, s // t),
        in_specs=[tok, pl.BlockSpec((1, D_MODEL), lambda bi, i: (0, 0)),
                  pl.BlockSpec((None, D_MODEL, 2 * FFN_DIM), lambda bi, i: (layer, 0, 0)),
                  pl.BlockSpec((None, FFN_DIM, D_MODEL), lambda bi, i: (layer, 0, 0))],
        out_specs=tok,
        out_shape=jax.ShapeDtypeStruct((b, s, D_MODEL), F32),
        compiler_params=_params(2),
        name="ffn",
    )(x, nffn, wup, wdn)


def _rope_constants():
    inv_freq = 1.0 / (ROPE_THETA ** (jnp.arange(0, ROT_DIM, 2, dtype=F32) / ROT_DIM))
    invf = jnp.concatenate([inv_freq, inv_freq]).reshape(1, ROT_DIM)
    place = np.zeros((ROT_DIM, 3 * LANES), np.float32)
    ones = np.zeros((1, LANES), np.float32)
    for lane in range(LANES):
        d = lane % DSA_HEAD_DIM
        if d < ROT_HALF:
            place[d, lane] = 1.0
            place[ROT_HALF + d, LANES + lane] = -1.0
        elif d < ROT_DIM:
            place[d - ROT_HALF, lane] = 1.0
            place[d, 2 * LANES + lane] = 1.0
        else:
            ones[0, lane] = 1.0
    return invf, jnp.asarray(place, BF16), jnp.asarray(ones)


def _relayout_w_in(w_in):
    offs = np.concatenate([[0], np.cumsum(SPLIT_SIZES)])
    col = lambda n: w_in[..., int(offs[n]):int(offs[n + 1])]
    (a_x, a_b, a_c, d_q, d_k, d_v, i_q, i_k, i_w, g_q, g_f, g_i, g_g, m_q, gates) = (col(n) for n in range(15))
    pad = jnp.zeros(w_in.shape[:-1] + (LANES - DSA_HEAD_DIM - IDX_HEADS,), w_in.dtype)
    w1 = jnp.concatenate([a_x, a_b, a_c, d_q, d_k, i_k, d_v, i_w, pad, i_q, g_q, g_f, g_i, g_g, m_q], axis=-1)
    return w1.astype(BF16), gates.astype(BF16)


def kernel(x, mem, positions, norm_mix, w_in, conv_w, dsa_q_norm, dsa_k_norm, hgrn_lower_bounds,
           hgrn_out_norm, mem_norm, mem_w_kv, mem_q_norm, mem_k_norm, w_lift, w_out, norm_ffn,
           ffn_w_up, ffn_w_down):
    b, s, d = x.shape
    depth = w_in.shape[0]
    assert d == D_MODEL and w_in.shape[2] == sum(SPLIT_SIZES)
    assert s % HG_CHUNK == 0 and s % min(DSA_K_TILE, s) == 0
    topk = min(TOPK_MAX, s // 4)

    invf, rplace, rones = _rope_constants()
    ang = positions.astype(F32)[:, None, :] * invf.reshape(ROT_DIM, 1)
    cs = jnp.where(jnp.arange(ROT_DIM)[:, None] < ROT_HALF, jnp.cos(ang), jnp.sin(ang))
    lbs = jnp.cumsum(jax.nn.softmax(hgrn_lower_bounds.astype(F32), axis=0), axis=0)
    lbs = lbs - lbs[0:1]
    g64 = jnp.asarray(np.kron(np.eye(DSA_HEADS), np.full((DSA_HEAD_DIM, DSA_HEAD_DIM), 1.0 / DSA_HEAD_DIM)), BF16)
    row = lambda v: v.reshape(1, -1).astype(F32)
    w1, wg = _relayout_w_in(w_in)
    wkv, wl, wo = mem_w_kv.astype(BF16), w_lift.astype(BF16), w_out.astype(BF16)
    wup, wdn = ffn_w_up.astype(BF16), ffn_w_down.astype(BF16)

    for l in range(depth):
        lb = lbs[l]
        lb_rows = jnp.concatenate([jnp.stack([jnp.log(lb), jnp.log1p(-lb), 1.0 - lb]),
                                   jnp.zeros((SUBLANES - 3, lb.shape[0]), F32)])
        convw = jnp.concatenate([conv_w[l], jnp.zeros((SUBLANES - CONV_WIDTH, CONV_DIM), F32)])
        mkt, mv = _mem_kv(l, mem, row(mem_norm), wkv, row(mem_k_norm[l]))
        ya, q, iq, kkt, vw, v1, zh, ym = _proj(
            l, x, row(norm_mix[l]), w1, cs, rplace, rones, convw,
            row(jnp.tile(dsa_q_norm[l], DSA_HEADS)),
            row(jnp.concatenate([dsa_k_norm[l], jnp.ones((LANES - DSA_HEAD_DIM,), F32)])),
            g64, mkt, mv, row(mem_q_norm[l]))
        logit_bound = DSA_HEAD_DIM ** 0.5 * jnp.max(jnp.abs(dsa_q_norm[l])) * jnp.max(jnp.abs(dsa_k_norm[l]))
        fast = (logit_bound <= SAFE_LOGIT_BOUND).astype(jnp.int32).reshape(1)
        yb = _dsa(fast, q, iq, vw, kkt, v1, topk)
        yc = _hgrn(zh, lb_rows, row(hgrn_out_norm[l]))
        x = _merge(l, x, row(norm_mix[l]), ya, yb, yc, ym, wg, wl, wo)
        x = _ffn(l, x, row(norm_ffn[l]), wup, wdn)
    return x
```

```python
import functools

import jax
import jax.numpy as jnp
import numpy as np
from jax import lax
from jax.experimental import pallas as pl
from jax.experimental.pallas import tpu as pltpu

D_MODEL = 1024
MEM_TOKENS = 256
N_BRANCH = 4
BRANCH_DIM = 512
CONV_DIM = 512
CONV_WIDTH = 3
DSA_HEADS = 8
DSA_HEAD_DIM = 64
IDX_HEADS = 8
IDX_DIM = 64
TOPK_MAX = 256
HG_HEADS = 4
HG_DK = 128
HG_DV = 128
HG_CHUNK = 64
MEM_HEADS = 4
MEM_HEAD_DIM = 128
ROPE_THETA = 500000.0
ROT_DIM = DSA_HEAD_DIM // 4
ROT_HALF = ROT_DIM // 2
FFN_DIM = ((8 * D_MODEL // 3 + 255) // 256) * 256
EPS = 1e-6

SPLIT_SIZES = (CONV_DIM, CONV_DIM, CONV_DIM,
               DSA_HEADS * DSA_HEAD_DIM, DSA_HEAD_DIM, DSA_HEAD_DIM,
               IDX_HEADS * IDX_DIM, IDX_DIM, IDX_HEADS,
               HG_HEADS * HG_DK, HG_HEADS * HG_DK, HG_HEADS * HG_DV, HG_HEADS * HG_DV,
               MEM_HEADS * MEM_HEAD_DIM,
               N_BRANCH * D_MODEL)

LANES = 128
SUBLANES = 8
VMEM_LIMIT_BYTES = 56 * 1024 * 1024

PROJ_TILE = 512
DSA_Q_TILE = 512
DSA_K_TILE = 512
RADIX_PASSES_PER_TRIP = 8
HG_TILE = 512
MERGE_TILE = 1024
FFN_TILE = 1024
FFN_CHUNK = FFN_DIM // 2

GRP_A = 3 * CONV_DIM
GRP_Q = DSA_HEADS * DSA_HEAD_DIM
GRP_KV = 2 * LANES
GRP_IQ = IDX_HEADS * IDX_DIM
GRP_H = 4 * HG_HEADS * HG_DK
GRP_M = MEM_HEADS * MEM_HEAD_DIM
OFF_A = 0
OFF_Q = OFF_A + GRP_A
OFF_KV = OFF_Q + GRP_Q
OFF_IQ = OFF_KV + GRP_KV
OFF_H = OFF_IQ + GRP_IQ
OFF_M = OFF_H + GRP_H
PROJ_COLS = OFF_M + GRP_M

NEG_BIG = -1e30
INT_MIN = -(2 ** 31)
LOG2E = 1.4426950408889634
Q_SCALE = DSA_HEAD_DIM ** -0.5 * LOG2E
SAFE_LOGIT_BOUND = 40.0

BF16 = jnp.bfloat16
F32 = jnp.float32


def _mm(a, b):
    return jnp.dot(a, b, preferred_element_type=F32)


def _mm_nt(a, b):
    return lax.dot_general(a, b, (((1,), (1,)), ((), ())), preferred_element_type=F32)


def _mm_tn(a, b):
    return lax.dot_general(a, b, (((0,), (0,)), ((), ())), preferred_element_type=F32)


def _rms_scale(x):
    return lax.rsqrt(jnp.mean(x * x, axis=-1, keepdims=True) + EPS)


def _params(n_grid):
    return pltpu.CompilerParams(dimension_semantics=("arbitrary",) * n_grid,
                                vmem_limit_bytes=VMEM_LIMIT_BYTES)


def _mem_kv_kernel(mem_ref, mem_norm_ref, wkv_ref, knorm_ref, mkt_ref, mv_ref):
    m = mem_ref[0]
    mn = (m * _rms_scale(m) * mem_norm_ref[...]).astype(BF16)
    kv = _mm(mn, wkv_ref[...])
    for h in range(MEM_HEADS):
        kh = kv[:, h * MEM_HEAD_DIM:(h + 1) * MEM_HEAD_DIM]
        kh = kh * _rms_scale(kh) * knorm_ref[...]
        mkt_ref[0, h] = kh.T.astype(BF16)
        off = MEM_HEADS * MEM_HEAD_DIM + h * MEM_HEAD_DIM
        mv_ref[0, h] = kv[:, off:off + MEM_HEAD_DIM].astype(BF16)


def _mem_kv(layer, mem, mem_norm, wkv, knorm):
    b = mem.shape[0]
    return pl.pallas_call(
        _mem_kv_kernel,
        grid=(b,),
        in_specs=[
            pl.BlockSpec((1, MEM_TOKENS, D_MODEL), lambda i: (i, 0, 0)),
            pl.BlockSpec((1, D_MODEL), lambda i: (0, 0)),
            pl.BlockSpec((None, D_MODEL, 2 * GRP_M), lambda i: (layer, 0, 0)),
            pl.BlockSpec((1, MEM_HEAD_DIM), lambda i: (0, 0)),
        ],
        out_specs=[
            pl.BlockSpec((1, MEM_HEADS, MEM_HEAD_DIM, MEM_TOKENS), lambda i: (i, 0, 0, 0)),
            pl.BlockSpec((1, MEM_HEADS, MEM_TOKENS, MEM_HEAD_DIM), lambda i: (i, 0, 0, 0)),
        ],
        out_shape=[
            jax.ShapeDtypeStruct((b, MEM_HEADS, MEM_HEAD_DIM, MEM_TOKENS), BF16),
            jax.ShapeDtypeStruct((b, MEM_HEADS, MEM_TOKENS, MEM_HEAD_DIM), BF16),
        ],
        compiler_params=_params(1),
        name="mem_kv",
    )(mem, mem_norm, wkv, knorm)


def _rope_slab(s, c, sa, sb):
    return s * c + pltpu.roll(s, LANES - ROT_HALF, 1) * sa + pltpu.roll(s, ROT_HALF, 1) * sb


def _proj_kernel(x_ref, nmix_ref, w_ref, cs_ref, rplace_ref, rones_ref, convw_ref, qg_ref, kg_ref,
                 g64_ref, mkt_ref, mv_ref, mqg_ref,
                 ya_ref, q_ref, iq_ref, kkt_ref, vw_ref, v1_ref, zh_ref, ym_ref,
                 carry_ref):
    i = pl.program_id(1)
    t = x_ref.shape[1]
    x = x_ref[0]
    hb = (x * _rms_scale(x) * nmix_ref[...]).astype(BF16)

    cs_hi, cs_mid, cs_lo = _split3_bf16(cs_ref[0])
    place = rplace_ref[...]
    tabs = _mm_tn(cs_hi, place) + _mm_tn(cs_mid, place) + _mm_tn(cs_lo, place)
    rc = tabs[:, :LANES] + rones_ref[...]
    rsa = tabs[:, LANES:2 * LANES]
    rsb = tabs[:, 2 * LANES:]

    za = _mm(hb, w_ref[:, OFF_A:OFF_A + GRP_A])
    a_x, a_b, a_c = za[:, :CONV_DIM], za[:, CONV_DIM:2 * CONV_DIM], za[:, 2 * CONV_DIM:]
    u = a_c * a_x

    @pl.when(i == 0)
    def _():
        carry_ref[...] = jnp.zeros_like(carry_ref)

    carry = carry_ref[...]
    row8 = lax.broadcasted_iota(jnp.int32, (SUBLANES, CONV_DIM), 0)
    r1 = pltpu.roll(u, 1, 0)
    r2 = pltpu.roll(u, 2, 0)
    top1 = jnp.where(row8 < 1, pltpu.roll(carry, 1, 0), r1[:SUBLANES])
    top2 = jnp.where(row8 < 2, pltpu.roll(carry, 2, 0), r2[:SUBLANES])
    u1 = jnp.concatenate([top1, r1[SUBLANES:]], axis=0)
    u2 = jnp.concatenate([top2, r2[SUBLANES:]], axis=0)
    carry_ref[...] = u[t - SUBLANES:]
    cw = convw_ref[...]
    ya_ref[0] = (a_b * (u2 * cw[0:1] + u1 * cw[1:2] + u * cw[2:3])).astype(BF16)

    zq = _mm(hb, w_ref[:, OFF_Q:OFF_Q + GRP_Q])
    msq = _mm((zq * zq).astype(BF16), g64_ref[...])
    qn = zq * lax.rsqrt(msq + EPS) * qg_ref[...]
    for p in range(GRP_Q // LANES):
        sl = slice(p * LANES, (p + 1) * LANES)
        q_ref[0, :, sl] = (_rope_slab(qn[:, sl], rc, rsa, rsb) * Q_SCALE).astype(BF16)

    zkv = _mm(hb, w_ref[:, OFF_KV:OFF_KV + GRP_KV])
    s0, s1 = zkv[:, :LANES], zkv[:, LANES:]
    lane = lax.broadcasted_iota(jnp.int32, (t, LANES), 1)
    is_k = lane < DSA_HEAD_DIM
    kms = jnp.sum(jnp.where(is_k, s0 * s0, 0.0), axis=-1, keepdims=True) * (1.0 / DSA_HEAD_DIM)
    s0 = s0 * jnp.where(is_k, lax.rsqrt(kms + EPS) * kg_ref[...], 1.0)
    kkt_ref[0] = _rope_slab(s0, rc, rsa, rsb).T.astype(BF16)
    w_scale = jnp.where(lane >= DSA_HEAD_DIM, jnp.where(lane < DSA_HEAD_DIM + IDX_HEADS, IDX_HEADS ** -0.5, 1.0), 1.0)
    vw_ref[0] = s1 * w_scale
    v1_ref[0] = jnp.where(is_k, s1, jnp.where(lane == DSA_HEAD_DIM, 1.0, 0.0)).astype(BF16)

    ziq = _mm(hb, w_ref[:, OFF_IQ:OFF_IQ + GRP_IQ])
    for p in range(GRP_IQ // LANES):
        sl = slice(p * LANES, (p + 1) * LANES)
        iq_ref[0, :, sl] = (_rope_slab(ziq[:, sl], rc, rsa, rsb) * (IDX_DIM ** -0.5)).astype(BF16)

    zm = _mm(hb, w_ref[:, OFF_M:OFF_M + GRP_M])
    for h in range(MEM_HEADS):
        sl = slice(h * MEM_HEAD_DIM, (h + 1) * MEM_HEAD_DIM)
        mq = zm[:, sl]
        mq = mq * _rms_scale(mq) * mqg_ref[...] * (MEM_HEAD_DIM ** -0.5)
        lg = _mm(mq.astype(BF16), mkt_ref[0, h])
        pe = jnp.exp(lg - jnp.max(lg, axis=-1, keepdims=True))
        den = jnp.sum(pe, axis=-1, keepdims=True)
        ym_ref[0, :, sl] = (_mm(pe.astype(BF16), mv_ref[0, h]) / den).astype(BF16)

    zh_ref[0] = _mm(hb, w_ref[:, OFF_H:OFF_H + GRP_H])


def _proj(layer, x, nmix, w1, cs, rplace, rones, convw, qg, kg, g64, mkt, mv, mqg):
    b, s, _ = x.shape
    t = min(PROJ_TILE, s)
    tok = lambda width: pl.BlockSpec((1, t, width), lambda bi, i: (bi, i, 0))
    const2 = lambda shape: pl.BlockSpec(shape, lambda bi, i: (0, 0))
    per_b4 = lambda shape: pl.BlockSpec((1,) + shape, lambda bi, i: (bi, 0, 0, 0))
    out_widths = (CONV_DIM, GRP_Q, GRP_IQ, LANES, LANES, LANES, GRP_H, GRP_M)
    out_dtypes = (BF16, BF16, BF16, BF16, F32, BF16, F32, BF16)
    return pl.pallas_call(
        _proj_kernel,
        grid=(b, s // t),
        in_specs=[
            tok(D_MODEL), const2((1, D_MODEL)),
            pl.BlockSpec((None, D_MODEL, PROJ_COLS), lambda bi, i: (layer, 0, 0)),
            pl.BlockSpec((1, ROT_DIM, t), lambda bi, i: (bi, 0, i)),
            const2((ROT_DIM, 3 * LANES)), const2((1, LANES)),
            const2((SUBLANES, CONV_DIM)), const2((1, GRP_Q)), const2((1, LANES)),
            const2((GRP_Q, GRP_Q)),
            per_b4((MEM_HEADS, MEM_HEAD_DIM, MEM_TOKENS)), per_b4((MEM_HEADS, MEM_TOKENS, MEM_HEAD_DIM)),
            const2((1, MEM_HEAD_DIM)),
        ],
        out_specs=[tok(w) if n != 3 else pl.BlockSpec((1, LANES, t), lambda bi, i: (bi, 0, i))
                   for n, w in enumerate(out_widths)],
        out_shape=[jax.ShapeDtypeStruct((b, s, w) if n != 3 else (b, LANES, s), d)
                   for n, (w, d) in enumerate(zip(out_widths, out_dtypes))],
        scratch_shapes=[pltpu.VMEM((SUBLANES, CONV_DIM), F32)],
        compiler_params=_params(2),
        name="proj",
    )(x, nmix, w1, cs, rplace, rones, convw, qg, kg, g64, mkt, mv, mqg)


def _bit_transpose32(words):
    a = list(words)
    for j, m in ((16, 0x0000FFFF), (8, 0x00FF00FF), (4, 0x0F0F0F0F), (2, 0x33333333), (1, 0x55555555)):
        for k in range(32):
            if k & j:
                continue
            tmp = (a[k] ^ lax.shift_right_logical(a[k + j], jnp.int32(j))) & m
            a[k] = a[k] ^ tmp
            a[k + j] = a[k + j] ^ lax.shift_left(tmp, jnp.int32(j))
    return a


def _low_bits(n):
    return jnp.where(n >= 32, -1, jnp.where(n <= 0, 0, lax.shift_left(jnp.int32(1), n) - 1))


def _dsa_kernel(fast_ref, q_ref, iq_ref, vw_ref, kkt_ref, v1_ref, o_ref,
                key_ref, plane_ref, cand_ref, sel_ref, m_ref, acc_ref, kbd_ref, ikbd_ref, *, topk, seq_len):
    i = pl.program_id(1)
    t = q_ref.shape[1]
    gt, _, tk = key_ref.shape
    n_groups = plane_ref.shape[0]
    cpt = tk // LANES
    n_pairs = DSA_HEADS // 2
    q0 = i * t
    nj = (q0 + t - 1) // tk + 1
    ng = (nj + gt - 1) // gt

    @pl.when(jnp.logical_and(pl.program_id(0) == 0, i == 0))
    def _():
        plane_ref[...] = jnp.zeros_like(plane_ref)

    @pl.when(i == 0)
    def _():
        kbd_ref[...] = jnp.zeros_like(kbd_ref)
        ikbd_ref[...] = jnp.zeros_like(ikbd_ref)
        d = DSA_HEAD_DIM
        for j in range(kbd_ref.shape[0]):
            kt = kkt_ref[0, :, j * tk:(j + 1) * tk]
            kbd_ref[j, :d, :tk] = kt[:d]
            kbd_ref[j, d:, tk:] = kt[:d]
            ikbd_ref[j, :d, :tk] = kt[d:]
            ikbd_ref[j, d:, tk:] = kt[d:]

    iw = vw_ref[0][:, DSA_HEAD_DIM:DSA_HEAD_DIM + IDX_HEADS]

    def score_group(g, carry):
        nt = jnp.minimum(nj - g * gt, gt)

        def score_tile(jj):
            sc = jnp.zeros((t, tk), F32)
            for p in range(n_pairs):
                r = _mm(iq_ref[0, :, p * LANES:(p + 1) * LANES], ikbd_ref[g * gt + jj])
                sc = sc + jnp.maximum(r[:, :tk], 0.0) * iw[:, 2 * p:2 * p + 1]
                sc = sc + jnp.maximum(r[:, tk:], 0.0) * iw[:, 2 * p + 1:2 * p + 2]
            sc = jnp.where(sc == 0.0, 0.0, sc)
            bits = pltpu.bitcast(sc, jnp.int32)
            key_ref[jj] = bits ^ ((bits >> 31) | INT_MIN)

        def score_two(k, c2):
            score_tile(2 * k)
            score_tile(2 * k + 1)
            return c2

        lax.fori_loop(0, nt // 2, score_two, 0)

        @pl.when(nt % 2 == 1)
        def _():
            score_tile(nt - 1)

        def clear_tile(jj, c2):
            key_ref[jj] = jnp.zeros((t, tk), jnp.int32)
            return c2

        lax.fori_loop(nt, gt, clear_tile, 0)

        def to_planes(rg, c2):
            rows = pl.ds(pl.multiple_of(rg * SUBLANES, SUBLANES), SUBLANES)
            words = []
            for k in range(32):
                ch = 31 - k
                words.append(key_ref[ch // cpt, rows, (ch % cpt) * LANES:(ch % cpt + 1) * LANES])
            for p, w in enumerate(_bit_transpose32(words)):
                plane_ref[g, p, rows, :] = w
            return c2

        lax.fori_loop(0, t // SUBLANES, to_planes, 0)
        return carry

    lax.fori_loop(0, ng, score_group, 0)

    lane = lax.broadcasted_iota(jnp.int32, (t, LANES), 1)
    qpos = lax.broadcasted_iota(jnp.int32, (t, LANES), 0) + q0
    chunks_valid = ((qpos - lane) >> 7) + 1

    def count_bits(words):
        tot = lax.population_count(words[0])
        for w in words[1:]:
            tot = tot + lax.population_count(w)
        return jnp.sum(tot.astype(F32), axis=-1, keepdims=True)

    for g in range(n_groups):
        cand_ref[g] = _low_bits(chunks_valid - 32 * g)
        sel_ref[g] = jnp.zeros((t, LANES), jnp.int32)

    def radix_select(groups):
        def radix(p, need):
            ones = [cand_ref[g] & plane_ref[g, p] for g in groups]
            c1 = count_bits(ones)
            take = c1 >= need
            for g, one in zip(groups, ones):
                cand = cand_ref[g]
                cand_ref[g] = jnp.where(take, one, cand ^ one)
                sel_ref[g] = jnp.where(take, sel_ref[g], sel_ref[g] | one)
            return jnp.where(take, need, need - c1)

        def radix_block(k, need):
            for u in range(RADIX_PASSES_PER_TRIP):
                need = radix(k * RADIX_PASSES_PER_TRIP + u, need)
            return need

        return lax.fori_loop(0, 32 // RADIX_PASSES_PER_TRIP, radix_block, jnp.full((t, 1), float(topk), F32))

    if n_groups == 1:
        need = radix_select(range(1))
    else:
        need = lax.switch(ng - 1, [functools.partial(radix_select, range(n)) for n in range(1, n_groups + 1)])

    has_tie = jnp.where(count_bits([cand_ref[g] for g in range(n_groups)]) > need, 1.0, 0.0)
    any_tie = jnp.max(has_tie) > 0.0

    @pl.when(jnp.logical_not(any_tie))
    def _():
        for g in range(n_groups):
            sel_ref[g] = sel_ref[g] | cand_ref[g]

    @pl.when(any_tie)
    def _():
        n_bits = max(1, int(seq_len - 1).bit_length())

        def below(g, cut):
            return cand_ref[g] & _low_bits(((cut - lane + (LANES - 1)) >> 7) - 32 * g)

        def bisect_idx(bit, pfx):
            cut = pfx | (jnp.int32(1) << (n_bits - 1 - bit))
            cnt = count_bits([below(g, cut) for g in range(n_groups)])
            return jnp.where(cnt < need, cut, pfx)

        cut = lax.fori_loop(0, n_bits, bisect_idx, jnp.zeros((t, 1), jnp.int32))
        cut = jnp.where(has_tie > 0.0, cut + 1, seq_len)
        for g in range(n_groups):
            sel_ref[g] = sel_ref[g] | below(g, cut)

    neg_bits = int(np.float32(NEG_BIG).view(np.int32))

    def tile_bias(j):
        words = sel_ref[j // gt]
        parts = []
        for c in range(cpt):
            k = (j % gt) * cpt + c
            picked = lax.shift_left(words, 31 - k) >> 31
            parts.append(pltpu.bitcast(neg_bits & ~picked, F32))
        return jnp.concatenate(parts, axis=-1)

    acc_ref[...] = jnp.zeros(acc_ref.shape, F32)
    use_fast = fast_ref[0] > 0

    @pl.when(use_fast)
    def _():
        def attend(j):
            bias = tile_bias(j)
            for p in range(n_pairs):
                lg = _mm(q_ref[0, :, p * LANES:(p + 1) * LANES], kbd_ref[j])
                for e in range(2):
                    pe = jnp.exp2(lg[:, e * tk:(e + 1) * tk] + bias).astype(BF16)
                    acc_ref[2 * p + e] += _mm(pe, v1_ref[0, j])

        def attend_two(k, carry):
            attend(2 * k)
            attend(2 * k + 1)
            return carry

        lax.fori_loop(0, nj // 2, attend_two, 0)

        @pl.when(nj % 2 == 1)
        def _():
            attend(nj - 1)

    @pl.when(jnp.logical_not(use_fast))
    def _():
        m_ref[...] = jnp.full(m_ref.shape, NEG_BIG, F32)

        def attend(j, carry):
            bias = tile_bias(j)
            for p in range(n_pairs):
                lg = _mm(q_ref[0, :, p * LANES:(p + 1) * LANES], kbd_ref[j])
                for e in range(2):
                    h = 2 * p + e
                    s = lg[:, e * tk:(e + 1) * tk] + bias
                    m_old = m_ref[h][:, :1]
                    m_new = jnp.maximum(m_old, jnp.max(s, axis=-1, keepdims=True))
                    pe = jnp.exp2(s - m_new).astype(BF16)
                    acc_ref[h] = acc_ref[h] * jnp.exp2(m_old - m_new) + _mm(pe, v1_ref[0, j])
                    m_ref[h] = jnp.broadcast_to(m_new, (t, LANES))
            return carry

        lax.fori_loop(0, nj, attend, 0)

    first_head = lax.broadcasted_iota(jnp.int32, (t, LANES), 1) < DSA_HEAD_DIM
    for p in range(n_pairs):
        a0, a1 = acc_ref[2 * p], acc_ref[2 * p + 1]
        o0 = a0 / a0[:, DSA_HEAD_DIM:DSA_HEAD_DIM + 1]
        o1 = a1 / a1[:, DSA_HEAD_DIM:DSA_HEAD_DIM + 1]
        o_ref[0, :, p * LANES:(p + 1) * LANES] = jnp.where(
            first_head, o0, pltpu.roll(o1, DSA_HEAD_DIM, 1)).astype(BF16)


def _dsa(fast, q, iq, vw, kkt, v1, topk):
    b, s, _ = q.shape
    t = min(DSA_Q_TILE, s)
    tk = min(DSA_K_TILE, s)
    nj = s // tk
    gt = 32 * LANES // tk
    n_groups = -(-nj // gt)
    tok = lambda width: pl.BlockSpec((1, t, width), lambda bi, i: (bi, i, 0))
    per_b = lambda shape: pl.BlockSpec((1,) + shape, lambda bi, i: (bi,) + (0,) * len(shape),
                                       pipeline_mode=pl.Buffered(1))
    return pl.pallas_call(
        functools.partial(_dsa_kernel, topk=topk, seq_len=s),
        grid=(b, s // t),
        in_specs=[pl.BlockSpec(memory_space=pltpu.SMEM),
                  tok(GRP_Q), tok(GRP_IQ), tok(LANES),
                  per_b((LANES, s)), per_b((nj, tk, LANES))],
        out_specs=tok(GRP_Q),
        out_shape=jax.ShapeDtypeStruct((b, s, GRP_Q), BF16),
        scratch_shapes=[
            pltpu.VMEM((gt, t, tk), jnp.int32),
            pltpu.VMEM((n_groups, 32, t, LANES), jnp.int32),
            pltpu.VMEM((n_groups, t, LANES), jnp.int32),
            pltpu.VMEM((n_groups, t, LANES), jnp.int32),
            pltpu.VMEM((DSA_HEADS, t, LANES), F32),
            pltpu.VMEM((DSA_HEADS, t, LANES), F32),
            pltpu.VMEM((nj, LANES, 2 * tk), BF16),
            pltpu.VMEM((nj, LANES, 2 * tk), BF16),
        ],
        compiler_params=_params(2),
        name="dsa",
    )(fast, q, iq, vw, kkt, v1.reshape(b, nj, tk, LANES))


def _split3_bf16(x):
    hi = x.astype(BF16)
    r1 = x - hi.astype(F32)
    mid = r1.astype(BF16)
    lo = (r1 - mid.astype(F32)).astype(BF16)
    return hi, mid, lo


def _hgrn_kernel(zh_ref, lb_ref, onorm_ref, o_ref, state_ref):
    i = pl.program_id(1)
    t = zh_ref.shape[1]
    c = HG_CHUNK
    width = HG_HEADS * HG_DK

    @pl.when(i == 0)
    def _():
        state_ref[...] = jnp.zeros_like(state_ref)

    rr = lax.broadcasted_iota(jnp.int32, (c, c), 0)
    cc = lax.broadcasted_iota(jnp.int32, (c, c), 1)
    tril = rr >= cc
    tril_b = jnp.where(tril, 1.0, 0.0).astype(BF16)

    n_chunks = t // c
    chunk = lambda n: slice(n * c, (n + 1) * c)
    head = lambda h: slice(h * HG_DK, (h + 1) * HG_DK)
    zq, zf, v, zg = (zh_ref[0, :, g * width:(g + 1) * width] for g in range(4))
    lb = lb_ref[...]
    log_lb, log1m_lb, one_m_lb = lb[0:1], lb[1:2], lb[2:3]

    e = jnp.exp(-jnp.abs(zf))
    b_ = log1m_lb + jnp.minimum(zf, 0.0) - jnp.log1p(e)
    log_f = jnp.maximum(log_lb, b_) + jnp.log1p(jnp.exp(-jnp.abs(log_lb - b_)))
    hk = one_m_lb * jnp.where(zf >= 0.0, e, 1.0) / (1.0 + e)
    hq = zq * jax.nn.sigmoid(zq)
    vb = v.astype(BF16)
    hi, mid, lo = _split3_bf16(log_f)
    cums = [_mm(tril_b, hi[chunk(n)]) + _mm(tril_b, mid[chunk(n)]) + _mm(tril_b, lo[chunk(n)])
            for n in range(n_chunks)]
    q_intra, k_intra, q_inter, k_state, decay = [], [], [], [], []
    for n in range(n_chunks):
        cum = cums[n]
        ref = cum[c // 2 - 1:c // 2]
        last = cum[c - 1:c]
        q_intra.append((hq[chunk(n)] * jnp.exp(cum - ref)).astype(BF16))
        k_intra.append((hk[chunk(n)] * jnp.exp(ref - cum)).astype(BF16))
        q_inter.append((hq[chunk(n)] * jnp.exp(cum)).astype(BF16))
        k_state.append((hk[chunk(n)] * jnp.exp(last - cum)).astype(BF16))
        decay.append(jnp.exp(last))
    att = [[_mm_nt(q_intra[n][:, head(h)], k_intra[n][:, head(h)]) for h in range(HG_HEADS)]
           for n in range(n_chunks)]
    o_intra = [[_mm(jnp.where(tril, att[n][h], 0.0).astype(BF16), vb[chunk(n), head(h)])
                for h in range(HG_HEADS)] for n in range(n_chunks)]
    states = [state_ref[h] for h in range(HG_HEADS)]
    o_inter = []
    for n in range(n_chunks):
        o_inter.append([_mm_nt(q_inter[n][:, head(h)], states[h].astype(BF16)) for h in range(HG_HEADS)])
        states = [states[h] * decay[n][:, head(h)] + _mm_tn(vb[chunk(n), head(h)], k_state[n][:, head(h)])
                  for h in range(HG_HEADS)]
    for h in range(HG_HEADS):
        state_ref[h] = states[h]
    gate = zg * jax.nn.sigmoid(zg)
    for n in range(n_chunks):
        for h in range(HG_HEADS):
            o = o_intra[n][h] + o_inter[n][h]
            o = o * _rms_scale(o) * onorm_ref[...]
            o_ref[0, chunk(n), head(h)] = (o * gate[chunk(n), head(h)]).astype(BF16)


def _hgrn(zh, lb_rows, onorm):
    b, s, _ = zh.shape
    t = min(HG_TILE, s)
    return pl.pallas_call(
        _hgrn_kernel,
        grid=(b, s // t),
        in_specs=[pl.BlockSpec((1, t, GRP_H), lambda bi, i: (bi, i, 0)),
                  pl.BlockSpec((SUBLANES, HG_HEADS * HG_DK), lambda bi, i: (0, 0)),
                  pl.BlockSpec((1, HG_DV), lambda bi, i: (0, 0))],
        out_specs=pl.BlockSpec((1, t, HG_HEADS * HG_DV), lambda bi, i: (bi, i, 0)),
        out_shape=jax.ShapeDtypeStruct((b, s, HG_HEADS * HG_DV), BF16),
        scratch_shapes=[pltpu.VMEM((HG_HEADS, HG_DV, HG_DK), F32)],
        compiler_params=_params(2),
        name="hgrn",
    )(zh, lb_rows, onorm)


def _merge_kernel(x_ref, nmix_ref, ya_ref, yb_ref, yc_ref, ym_ref, wg_ref, wl_ref, wo_ref, o_ref):
    x = x_ref[0]
    hb = (x * _rms_scale(x) * nmix_ref[...]).astype(BF16)
    merged = None
    for n, y_ref in enumerate((ya_ref, yb_ref, yc_ref, ym_ref)):
        gate = jax.nn.sigmoid(_mm(hb, wg_ref[:, n * D_MODEL:(n + 1) * D_MODEL]))
        term = gate * _mm(y_ref[0], wl_ref[n])
        merged = term if merged is None else merged + term
    o_ref[0] = x + _mm(merged.astype(BF16), wo_ref[...])


def _merge(layer, x, nmix, ya, yb, yc, ym, wg, wl, wo):
    b, s, _ = x.shape
    t = min(MERGE_TILE, s)
    tok = lambda width: pl.BlockSpec((1, t, width), lambda bi, i: (bi, i, 0))
    return pl.pallas_call(
        _merge_kernel,
        grid=(b, s // t),
        in_specs=[tok(D_MODEL), pl.BlockSpec((1, D_MODEL), lambda bi, i: (0, 0)),
                  tok(BRANCH_DIM), tok(BRANCH_DIM), tok(BRANCH_DIM), tok(BRANCH_DIM),
                  pl.BlockSpec((None, D_MODEL, N_BRANCH * D_MODEL), lambda bi, i: (layer, 0, 0),
                               pipeline_mode=pl.Buffered(1)),
                  pl.BlockSpec((None, N_BRANCH, BRANCH_DIM, D_MODEL), lambda bi, i: (layer, 0, 0, 0),
                               pipeline_mode=pl.Buffered(1)),
                  pl.BlockSpec((None, D_MODEL, D_MODEL), lambda bi, i: (layer, 0, 0),
                               pipeline_mode=pl.Buffered(1))],
        out_specs=tok(D_MODEL),
        out_shape=jax.ShapeDtypeStruct((b, s, D_MODEL), F32),
        compiler_params=_params(2),
        name="merge",
    )(x, nmix, ya, yb, yc, ym, wg, wl, wo)


def _ffn_kernel(x_ref, nffn_ref, wup_ref, wdn_ref, o_ref):
    x = x_ref[0]
    hb = (x * _rms_scale(x) * nffn_ref[...]).astype(BF16)
    out = x
    for n in range(FFN_DIM // FFN_CHUNK):
        lo = n * FFN_CHUNK
        gate = _mm(hb, wup_ref[:, lo:lo + FFN_CHUNK])
        up = _mm(hb, wup_ref[:, FFN_DIM + lo:FFN_DIM + lo + FFN_CHUNK])
        act = (gate * jax.nn.sigmoid(gate) * up).astype(BF16)
        out = out + _mm(act, wdn_ref[lo:lo + FFN_CHUNK, :])
    o_ref[0] = out


def _ffn(layer, x, nffn, wup, wdn):
    b, s, _ = x.shape
    t = min(FFN_TILE, s)
    tok = pl.BlockSpec((1, t, D_MODEL), lambda bi, i: (bi, i, 0))
    return pl.pallas_call(
        _ffn_kernel,
        grid=(b, s // t),
        in_specs=[tok, pl.BlockSpec((1, D_MODEL), lambda bi, i: (0, 0)),
                  pl.BlockSpec((None, D_MODEL, 2 * FFN_DIM), lambda bi, i: (layer, 0, 0),
                               pipeline_mode=pl.Buffered(1)),
                  pl.BlockSpec((None, FFN_DIM, D_MODEL), lambda bi, i: (layer, 0, 0),
                               pipeline_mode=pl.Buffered(1))],
        out_specs=tok,
        out_shape=jax.ShapeDtypeStruct((b, s, D_MODEL), F32),
        compiler_params=_params(2),
        name="ffn",
    )(x, nffn, wup, wdn)


def _rope_constants():
    inv_freq = 1.0 / (ROPE_THETA ** (jnp.arange(0, ROT_DIM, 2, dtype=F32) / ROT_DIM))
    invf = jnp.concatenate([inv_freq, inv_freq]).reshape(1, ROT_DIM)
    place = np.zeros((ROT_DIM, 3 * LANES), np.float32)
    ones = np.zeros((1, LANES), np.float32)
    for lane in range(LANES):
        d = lane % DSA_HEAD_DIM
        if d < ROT_HALF:
            place[d, lane] = 1.0
            place[ROT_HALF + d, LANES + lane] = -1.0
        elif d < ROT_DIM:
            place[d - ROT_HALF, lane] = 1.0
            place[d, 2 * LANES + lane] = 1.0
        else:
            ones[0, lane] = 1.0
    return invf, jnp.asarray(place, BF16), jnp.asarray(ones)


def _relayout_w_in(w_in):
    offs = np.concatenate([[0], np.cumsum(SPLIT_SIZES)])
    col = lambda n: w_in[..., int(offs[n]):int(offs[n + 1])]
    (a_x, a_b, a_c, d_q, d_k, d_v, i_q, i_k, i_w, g_q, g_f, g_i, g_g, m_q, gates) = (col(n) for n in range(15))
    pad = jnp.zeros(w_in.shape[:-1] + (LANES - DSA_HEAD_DIM - IDX_HEADS,), w_in.dtype)
    w1 = jnp.concatenate([a_x, a_b, a_c, d_q, d_k, i_k, d_v, i_w, pad, i_q, g_q, g_f, g_i, g_g, m_q], axis=-1)
    return w1.astype(BF16), gates.astype(BF16)


def kernel(x, mem, positions, norm_mix, w_in, conv_w, dsa_q_norm, dsa_k_norm, hgrn_lower_bounds,
           hgrn_out_norm, mem_norm, mem_w_kv, mem_q_norm, mem_k_norm, w_lift, w_out, norm_ffn,
           ffn_w_up, ffn_w_down):
    b, s, d = x.shape
    depth = w_in.shape[0]
    assert d == D_MODEL and w_in.shape[2] == sum(SPLIT_SIZES)
    assert s % HG_CHUNK == 0 and s % min(DSA_K_TILE, s) == 0
    topk = min(TOPK_MAX, s // 4)

    invf, rplace, rones = _rope_constants()
    ang = positions.astype(F32)[:, None, :] * invf.reshape(ROT_DIM, 1)
    cs = jnp.where(jnp.arange(ROT_DIM)[:, None] < ROT_HALF, jnp.cos(ang), jnp.sin(ang))
    lbs = jnp.cumsum(jax.nn.softmax(hgrn_lower_bounds.astype(F32), axis=0), axis=0)
    lbs = lbs - lbs[0:1]
    g64 = jnp.asarray(np.kron(np.eye(DSA_HEADS), np.full((DSA_HEAD_DIM, DSA_HEAD_DIM), 1.0 / DSA_HEAD_DIM)), BF16)
    row = lambda v: v.reshape(1, -1).astype(F32)
    w1, wg = _relayout_w_in(w_in)
    wkv, wl, wo = mem_w_kv.astype(BF16), w_lift.astype(BF16), w_out.astype(BF16)
    wup, wdn = ffn_w_up.astype(BF16), ffn_w_down.astype(BF16)

    for l in range(depth):
        lb = lbs[l]
        lb_rows = jnp.concatenate([jnp.stack([jnp.log(lb), jnp.log1p(-lb), 1.0 - lb]),
                                   jnp.zeros((SUBLANES - 3, lb.shape[0]), F32)])
        convw = jnp.concatenate([conv_w[l], jnp.zeros((SUBLANES - CONV_WIDTH, CONV_DIM), F32)])
        mkt, mv = _mem_kv(l, mem, row(mem_norm), wkv, row(mem_k_norm[l]))
        ya, q, iq, kkt, vw, v1, zh, ym = _proj(
            l, x, row(norm_mix[l]), w1, cs, rplace, rones, convw,
            row(jnp.tile(dsa_q_norm[l], DSA_HEADS)),
            row(jnp.concatenate([dsa_k_norm[l], jnp.ones((LANES - DSA_HEAD_DIM,), F32)])),
            g64, mkt, mv, row(mem_q_norm[l]))
        logit_bound = DSA_HEAD_DIM ** 0.5 * jnp.max(jnp.abs(dsa_q_norm[l])) * jnp.max(jnp.abs(dsa_k_norm[l]))
        fast = (logit_bound <= SAFE_LOGIT_BOUND).astype(jnp.int32).reshape(1)
        yb = _dsa(fast, q, iq, vw, kkt, v1, topk)
        yc = _hgrn(zh, lb_rows, row(hgrn_out_norm[l]))
        x = _merge(l, x, row(norm_mix[l]), ya, yb, yc, ym, wg, wl, wo)
        x = _ffn(l, x, row(norm_ffn[l]), wup, wdn)
    return x
```

```python
import functools

import jax
import jax.numpy as jnp
import numpy as np
from jax import lax
from jax.experimental import pallas as pl
from jax.experimental.pallas import tpu as pltpu

D_MODEL = 1024
MEM_TOKENS = 256
N_BRANCH = 4
BRANCH_DIM = 512
CONV_DIM = 512
CONV_WIDTH = 3
DSA_HEADS = 8
DSA_HEAD_DIM = 64
IDX_HEADS = 8
IDX_DIM = 64
TOPK_MAX = 256
HG_HEADS = 4
HG_DK = 128
HG_DV = 128
HG_CHUNK = 64
MEM_HEADS = 4
MEM_HEAD_DIM = 128
ROPE_THETA = 500000.0
ROT_DIM = DSA_HEAD_DIM // 4
ROT_HALF = ROT_DIM // 2
FFN_DIM = ((8 * D_MODEL // 3 + 255) // 256) * 256
EPS = 1e-6

SPLIT_SIZES = (CONV_DIM, CONV_DIM, CONV_DIM,
               DSA_HEADS * DSA_HEAD_DIM, DSA_HEAD_DIM, DSA_HEAD_DIM,
               IDX_HEADS * IDX_DIM, IDX_DIM, IDX_HEADS,
               HG_HEADS * HG_DK, HG_HEADS * HG_DK, HG_HEADS * HG_DV, HG_HEADS * HG_DV,
               MEM_HEADS * MEM_HEAD_DIM,
               N_BRANCH * D_MODEL)

LANES = 128
SUBLANES = 8
VMEM_LIMIT_BYTES = 56 * 1024 * 1024

PROJ_TILE = 512
DSA_Q_TILE = 512
DSA_K_TILE = 512
RADIX_PASSES_PER_TRIP = 8
HG_TILE = 512
MERGE_TILE = 1024
FFN_TILE = 1024
FFN_CHUNK = FFN_DIM // 2

GRP_A = 3 * CONV_DIM
GRP_Q = DSA_HEADS * DSA_HEAD_DIM
GRP_KV = 2 * LANES
GRP_IQ = IDX_HEADS * IDX_DIM
GRP_H = 4 * HG_HEADS * HG_DK
GRP_M = MEM_HEADS * MEM_HEAD_DIM
OFF_A = 0
OFF_Q = OFF_A + GRP_A
OFF_KV = OFF_Q + GRP_Q
OFF_IQ = OFF_KV + GRP_KV
OFF_H = OFF_IQ + GRP_IQ
OFF_M = OFF_H + GRP_H
PROJ_COLS = OFF_M + GRP_M

NEG_BIG = -1e30
INT_MIN = -(2 ** 31)
LOG2E = 1.4426950408889634
Q_SCALE = DSA_HEAD_DIM ** -0.5 * LOG2E
SAFE_LOGIT_BOUND = 40.0

BF16 = jnp.bfloat16
F32 = jnp.float32


def _mm(a, b):
    return jnp.dot(a, b, preferred_element_type=F32)


def _mm_nt(a, b):
    return lax.dot_general(a, b, (((1,), (1,)), ((), ())), preferred_element_type=F32)


def _mm_tn(a, b):
    return lax.dot_general(a, b, (((0,), (0,)), ((), ())), preferred_element_type=F32)


def _rms_scale(x):
    return lax.rsqrt(jnp.mean(x * x, axis=-1, keepdims=True) + EPS)


def _params(n_grid):
    return pltpu.CompilerParams(dimension_semantics=("arbitrary",) * n_grid,
                                vmem_limit_bytes=VMEM_LIMIT_BYTES)


def _mem_kv_kernel(mem_ref, mem_norm_ref, wkv_ref, knorm_ref, mkt_ref, mv_ref):
    m = mem_ref[0]
    mn = (m * _rms_scale(m) * mem_norm_ref[...]).astype(BF16)
    kv = _mm(mn, wkv_ref[...])
    for h in range(MEM_HEADS):
        kh = kv[:, h * MEM_HEAD_DIM:(h + 1) * MEM_HEAD_DIM]
        kh = kh * _rms_scale(kh) * knorm_ref[...]
        mkt_ref[0, h] = kh.T.astype(BF16)
        off = MEM_HEADS * MEM_HEAD_DIM + h * MEM_HEAD_DIM
        mv_ref[0, h] = kv[:, off:off + MEM_HEAD_DIM].astype(BF16)


def _mem_kv(layer, mem, mem_norm, wkv, knorm):
    b = mem.shape[0]
    return pl.pallas_call(
        _mem_kv_kernel,
        grid=(b,),
        in_specs=[
            pl.BlockSpec((1, MEM_TOKENS, D_MODEL), lambda i: (i, 0, 0)),
            pl.BlockSpec((1, D_MODEL), lambda i: (0, 0)),
            pl.BlockSpec((None, D_MODEL, 2 * GRP_M), lambda i: (layer, 0, 0)),
            pl.BlockSpec((1, MEM_HEAD_DIM), lambda i: (0, 0)),
        ],
        out_specs=[
            pl.BlockSpec((1, MEM_HEADS, MEM_HEAD_DIM, MEM_TOKENS), lambda i: (i, 0, 0, 0)),
            pl.BlockSpec((1, MEM_HEADS, MEM_TOKENS, MEM_HEAD_DIM), lambda i: (i, 0, 0, 0)),
        ],
        out_shape=[
            jax.ShapeDtypeStruct((b, MEM_HEADS, MEM_HEAD_DIM, MEM_TOKENS), BF16),
            jax.ShapeDtypeStruct((b, MEM_HEADS, MEM_TOKENS, MEM_HEAD_DIM), BF16),
        ],
        compiler_params=_params(1),
        name="mem_kv",
    )(mem, mem_norm, wkv, knorm)


def _rope_slab(s, c, sa, sb):
    return s * c + pltpu.roll(s, LANES - ROT_HALF, 1) * sa + pltpu.roll(s, ROT_HALF, 1) * sb


def _proj_kernel(x_ref, nmix_ref, w_ref, cs_ref, rplace_ref, rones_ref, convw_ref, qg_ref, kg_ref,
                 g64_ref, mkt_ref, mv_ref, mqg_ref,
                 ya_ref, q_ref, iq_ref, kkt_ref, vw_ref, v1_ref, zh_ref, ym_ref,
                 carry_ref):
    t = x_ref.shape[1]

    @pl.when(pl.program_id(1) == 0)
    def _():
        carry_ref[...] = jnp.zeros_like(carry_ref)

    x = x_ref[0]
    hb = (x * _rms_scale(x) * nmix_ref[...]).astype(BF16)

    cs_hi, cs_mid, cs_lo = _split3_bf16(cs_ref[0])
    place = rplace_ref[...]
    tabs = _mm_tn(cs_hi, place) + _mm_tn(cs_mid, place) + _mm_tn(cs_lo, place)
    rc = tabs[:, :LANES] + rones_ref[...]
    rsa = tabs[:, LANES:2 * LANES]
    rsb = tabs[:, 2 * LANES:]

    za = _mm(hb, w_ref[:, OFF_A:OFF_A + GRP_A])
    a_x, a_b, a_c = za[:, :CONV_DIM], za[:, CONV_DIM:2 * CONV_DIM], za[:, 2 * CONV_DIM:]
    u = a_c * a_x
    carry = carry_ref[...]
    row8 = lax.broadcasted_iota(jnp.int32, (SUBLANES, CONV_DIM), 0)
    r1 = pltpu.roll(u, 1, 0)
    r2 = pltpu.roll(u, 2, 0)
    top1 = jnp.where(row8 < 1, pltpu.roll(carry, 1, 0), r1[:SUBLANES])
    top2 = jnp.where(row8 < 2, pltpu.roll(carry, 2, 0), r2[:SUBLANES])
    u1 = jnp.concatenate([top1, r1[SUBLANES:]], axis=0)
    u2 = jnp.concatenate([top2, r2[SUBLANES:]], axis=0)
    carry_ref[...] = u[t - SUBLANES:]
    cw = convw_ref[...]
    ya_ref[0] = (a_b * (u2 * cw[0:1] + u1 * cw[1:2] + u * cw[2:3])).astype(BF16)

    zq = _mm(hb, w_ref[:, OFF_Q:OFF_Q + GRP_Q])
    msq = _mm((zq * zq).astype(BF16), g64_ref[...])
    qn = zq * lax.rsqrt(msq + EPS) * qg_ref[...]
    for p in range(GRP_Q // LANES):
        sl = slice(p * LANES, (p + 1) * LANES)
        q_ref[0, :, sl] = (_rope_slab(qn[:, sl], rc, rsa, rsb) * Q_SCALE).astype(BF16)

    zkv = _mm(hb, w_ref[:, OFF_KV:OFF_KV + GRP_KV])
    s0, s1 = zkv[:, :LANES], zkv[:, LANES:]
    lane = lax.broadcasted_iota(jnp.int32, (t, LANES), 1)
    is_k = lane < DSA_HEAD_DIM
    kms = jnp.sum(jnp.where(is_k, s0 * s0, 0.0), axis=-1, keepdims=True) * (1.0 / DSA_HEAD_DIM)
    s0 = s0 * jnp.where(is_k, lax.rsqrt(kms + EPS) * kg_ref[...], 1.0)
    kkt_ref[0] = _rope_slab(s0, rc, rsa, rsb).T.astype(BF16)
    w_scale = jnp.where(lane >= DSA_HEAD_DIM, jnp.where(lane < DSA_HEAD_DIM + IDX_HEADS, IDX_HEADS ** -0.5, 1.0), 1.0)
    vw_ref[0] = s1 * w_scale
    v1_ref[0] = jnp.where(is_k, s1, jnp.where(lane == DSA_HEAD_DIM, 1.0, 0.0)).astype(BF16)

    ziq = _mm(hb, w_ref[:, OFF_IQ:OFF_IQ + GRP_IQ])
    for p in range(GRP_IQ // LANES):
        sl = slice(p * LANES, (p + 1) * LANES)
        iq_ref[0, :, sl] = (_rope_slab(ziq[:, sl], rc, rsa, rsb) * (IDX_DIM ** -0.5)).astype(BF16)

    zm = _mm(hb, w_ref[:, OFF_M:OFF_M + GRP_M])
    for h in range(MEM_HEADS):
        sl = slice(h * MEM_HEAD_DIM, (h + 1) * MEM_HEAD_DIM)
        mq = zm[:, sl]
        mq = mq * _rms_scale(mq) * mqg_ref[...] * (MEM_HEAD_DIM ** -0.5)
        lg = _mm(mq.astype(BF16), mkt_ref[0, h])
        pe = jnp.exp(lg - jnp.max(lg, axis=-1, keepdims=True))
        den = jnp.sum(pe, axis=-1, keepdims=True)
        ym_ref[0, :, sl] = (_mm(pe.astype(BF16), mv_ref[0, h]) / den).astype(BF16)

    zh_ref[0] = _mm(hb, w_ref[:, OFF_H:OFF_H + GRP_H])


def _proj(layer, x, nmix, w1, cs, rplace, rones, convw, qg, kg, g64, mkt, mv, mqg):
    b, s, _ = x.shape
    t = min(PROJ_TILE, s)
    tok = lambda width: pl.BlockSpec((1, t, width), lambda bi, i: (bi, i, 0))
    const2 = lambda shape: pl.BlockSpec(shape, lambda bi, i: (0, 0))
    per_b4 = lambda shape: pl.BlockSpec((1,) + shape, lambda bi, i: (bi, 0, 0, 0))
    out_widths = (CONV_DIM, GRP_Q, GRP_IQ, LANES, LANES, LANES, GRP_H, GRP_M)
    out_dtypes = (BF16, BF16, BF16, BF16, F32, BF16, F32, BF16)
    return pl.pallas_call(
        _proj_kernel,
        grid=(b, s // t),
        in_specs=[
            tok(D_MODEL), const2((1, D_MODEL)),
            pl.BlockSpec((None, D_MODEL, PROJ_COLS), lambda bi, i: (layer, 0, 0)),
            pl.BlockSpec((1, ROT_DIM, t), lambda bi, i: (bi, 0, i)),
            const2((ROT_DIM, 3 * LANES)), const2((1, LANES)),
            const2((SUBLANES, CONV_DIM)), const2((1, GRP_Q)), const2((1, LANES)),
            const2((GRP_Q, GRP_Q)),
            per_b4((MEM_HEADS, MEM_HEAD_DIM, MEM_TOKENS)), per_b4((MEM_HEADS, MEM_TOKENS, MEM_HEAD_DIM)),
            const2((1, MEM_HEAD_DIM)),
        ],
        out_specs=[tok(w) if n != 3 else pl.BlockSpec((1, LANES, t), lambda bi, i: (bi, 0, i))
                   for n, w in enumerate(out_widths)],
        out_shape=[jax.ShapeDtypeStruct((b, s, w) if n != 3 else (b, LANES, s), d)
                   for n, (w, d) in enumerate(zip(out_widths, out_dtypes))],
        scratch_shapes=[pltpu.VMEM((SUBLANES, CONV_DIM), F32)],
        compiler_params=_params(2),
        name="proj",
    )(x, nmix, w1, cs, rplace, rones, convw, qg, kg, g64, mkt, mv, mqg)


def _bit_transpose32(words):
    a = list(words)
    for j, m in ((16, 0x0000FFFF), (8, 0x00FF00FF), (4, 0x0F0F0F0F), (2, 0x33333333), (1, 0x55555555)):
        for k in range(32):
            if k & j:
                continue
            tmp = (a[k] ^ lax.shift_right_logical(a[k + j], jnp.int32(j))) & m
            a[k] = a[k] ^ tmp
            a[k + j] = a[k + j] ^ lax.shift_left(tmp, jnp.int32(j))
    return a


def _low_bits(n):
    return jnp.where(n >= 32, -1, jnp.where(n <= 0, 0, lax.shift_left(jnp.int32(1), n) - 1))


def _dsa_kernel(fast_ref, q_ref, iq_ref, vw_ref, kkt_ref, v1_ref, o_ref,
                key_ref, plane_ref, cand_ref, sel_ref, m_ref, acc_ref, kbd_ref, ikbd_ref, *, topk, seq_len):
    i = pl.program_id(1)
    t = q_ref.shape[1]
    gt, _, tk = key_ref.shape
    n_groups = plane_ref.shape[0]
    cpt = tk // LANES
    n_pairs = DSA_HEADS // 2
    q0 = i * t
    nj = (q0 + t - 1) // tk + 1
    ng = (nj + gt - 1) // gt

    @pl.when(jnp.logical_and(pl.program_id(0) == 0, i == 0))
    def _():
        plane_ref[...] = jnp.zeros_like(plane_ref)

    @pl.when(i == 0)
    def _():
        kbd_ref[...] = jnp.zeros_like(kbd_ref)
        ikbd_ref[...] = jnp.zeros_like(ikbd_ref)
        d = DSA_HEAD_DIM
        for j in range(kbd_ref.shape[0]):
            kt = kkt_ref[0, :, j * tk:(j + 1) * tk]
            kbd_ref[j, :d, :tk] = kt[:d]
            kbd_ref[j, d:, tk:] = kt[:d]
            ikbd_ref[j, :d, :tk] = kt[d:]
            ikbd_ref[j, d:, tk:] = kt[d:]

    iw = vw_ref[0][:, DSA_HEAD_DIM:DSA_HEAD_DIM + IDX_HEADS]

    def score_group(g, carry):
        nt = jnp.minimum(nj - g * gt, gt)

        def score_tile(jj):
            sc = jnp.zeros((t, tk), F32)
            for p in range(n_pairs):
                r = _mm(iq_ref[0, :, p * LANES:(p + 1) * LANES], ikbd_ref[g * gt + jj])
                sc = sc + jnp.maximum(r[:, :tk], 0.0) * iw[:, 2 * p:2 * p + 1]
                sc = sc + jnp.maximum(r[:, tk:], 0.0) * iw[:, 2 * p + 1:2 * p + 2]
            sc = jnp.where(sc == 0.0, 0.0, sc)
            bits = pltpu.bitcast(sc, jnp.int32)
            key_ref[jj] = bits ^ ((bits >> 31) | INT_MIN)

        def score_two(k, c2):
            score_tile(2 * k)
            score_tile(2 * k + 1)
            return c2

        lax.fori_loop(0, nt // 2, score_two, 0)

        @pl.when(nt % 2 == 1)
        def _():
            score_tile(nt - 1)

        def clear_tile(jj, c2):
            key_ref[jj] = jnp.zeros((t, tk), jnp.int32)
            return c2

        lax.fori_loop(nt, gt, clear_tile, 0)

        def to_planes(rg, c2):
            rows = pl.ds(pl.multiple_of(rg * SUBLANES, SUBLANES), SUBLANES)
            words = []
            for k in range(32):
                ch = 31 - k
                words.append(key_ref[ch // cpt, rows, (ch % cpt) * LANES:(ch % cpt + 1) * LANES])
            for p, w in enumerate(_bit_transpose32(words)):
                plane_ref[g, p, rows, :] = w
            return c2

        lax.fori_loop(0, t // SUBLANES, to_planes, 0)
        return carry

    lax.fori_loop(0, ng, score_group, 0)

    lane = lax.broadcasted_iota(jnp.int32, (t, LANES), 1)
    qpos = lax.broadcasted_iota(jnp.int32, (t, LANES), 0) + q0
    chunks_valid = ((qpos - lane) >> 7) + 1

    def count_bits(words):
        tot = lax.population_count(words[0])
        for w in words[1:]:
            tot = tot + lax.population_count(w)
        return jnp.sum(tot.astype(F32), axis=-1, keepdims=True)

    for g in range(n_groups):
        cand_ref[g] = _low_bits(chunks_valid - 32 * g)
        sel_ref[g] = jnp.zeros((t, LANES), jnp.int32)

    def radix_select(groups):
        def radix(p, need):
            ones = [cand_ref[g] & plane_ref[g, p] for g in groups]
            c1 = count_bits(ones)
            take = c1 >= need
            for g, one in zip(groups, ones):
                cand = cand_ref[g]
                cand_ref[g] = jnp.where(take, one, cand ^ one)
                sel_ref[g] = jnp.where(take, sel_ref[g], sel_ref[g] | one)
            return jnp.where(take, need, need - c1)

        def radix_block(k, need):
            for u in range(RADIX_PASSES_PER_TRIP):
                need = radix(k * RADIX_PASSES_PER_TRIP + u, need)
            return need

        return lax.fori_loop(0, 32 // RADIX_PASSES_PER_TRIP, radix_block, jnp.full((t, 1), float(topk), F32))

    if n_groups == 1:
        need = radix_select(range(1))
    else:
        need = lax.switch(ng - 1, [functools.partial(radix_select, range(n)) for n in range(1, n_groups + 1)])

    has_tie = jnp.where(count_bits([cand_ref[g] for g in range(n_groups)]) > need, 1.0, 0.0)
    any_tie = jnp.max(has_tie) > 0.0

    @pl.when(jnp.logical_not(any_tie))
    def _():
        for g in range(n_groups):
            sel_ref[g] = sel_ref[g] | cand_ref[g]

    @pl.when(any_tie)
    def _():
        n_bits = max(1, int(seq_len - 1).bit_length())

        def below(g, cut):
            return cand_ref[g] & _low_bits(((cut - lane + (LANES - 1)) >> 7) - 32 * g)

        def bisect_idx(bit, pfx):
            cut = pfx | (jnp.int32(1) << (n_bits - 1 - bit))
            cnt = count_bits([below(g, cut) for g in range(n_groups)])
            return jnp.where(cnt < need, cut, pfx)

        cut = lax.fori_loop(0, n_bits, bisect_idx, jnp.zeros((t, 1), jnp.int32))
        cut = jnp.where(has_tie > 0.0, cut + 1, seq_len)
        for g in range(n_groups):
            sel_ref[g] = sel_ref[g] | below(g, cut)

    neg_bits = int(np.float32(NEG_BIG).view(np.int32))

    def tile_bias(j):
        words = sel_ref[j // gt]
        parts = []
        for c in range(cpt):
            k = (j % gt) * cpt + c
            picked = lax.shift_left(words, 31 - k) >> 31
            parts.append(pltpu.bitcast(neg_bits & ~picked, F32))
        return jnp.concatenate(parts, axis=-1)

    acc_ref[...] = jnp.zeros(acc_ref.shape, F32)
    use_fast = fast_ref[0] > 0

    @pl.when(use_fast)
    def _():
        def attend(j):
            bias = tile_bias(j)
            for p in range(n_pairs):
                lg = _mm(q_ref[0, :, p * LANES:(p + 1) * LANES], kbd_ref[j])
                for e in range(2):
                    pe = jnp.exp2(lg[:, e * tk:(e + 1) * tk] + bias).astype(BF16)
                    acc_ref[2 * p + e] += _mm(pe, v1_ref[0, j])

        def attend_two(k, carry):
            attend(2 * k)
            attend(2 * k + 1)
            return carry

        lax.fori_loop(0, nj // 2, attend_two, 0)

        @pl.when(nj % 2 == 1)
        def _():
            attend(nj - 1)

    @pl.when(jnp.logical_not(use_fast))
    def _():
        m_ref[...] = jnp.full(m_ref.shape, NEG_BIG, F32)

        def attend(j, carry):
            bias = tile_bias(j)
            for p in range(n_pairs):
                lg = _mm(q_ref[0, :, p * LANES:(p + 1) * LANES], kbd_ref[j])
                for e in range(2):
                    h = 2 * p + e
                    s = lg[:, e * tk:(e + 1) * tk] + bias
                    m_old = m_ref[h][:, :1]
                    m_new = jnp.maximum(m_old, jnp.max(s, axis=-1, keepdims=True))
                    pe = jnp.exp2(s - m_new).astype(BF16)
                    acc_ref[h] = acc_ref[h] * jnp.exp2(m_old - m_new) + _mm(pe, v1_ref[0, j])
                    m_ref[h] = jnp.broadcast_to(m_new, (t, LANES))
            return carry

        lax.fori_loop(0, nj, attend, 0)

    first_head = lax.broadcasted_iota(jnp.int32, (t, LANES), 1) < DSA_HEAD_DIM
    for p in range(n_pairs):
        a0, a1 = acc_ref[2 * p], acc_ref[2 * p + 1]
        o0 = a0 / a0[:, DSA_HEAD_DIM:DSA_HEAD_DIM + 1]
        o1 = a1 / a1[:, DSA_HEAD_DIM:DSA_HEAD_DIM + 1]
        o_ref[0, :, p * LANES:(p + 1) * LANES] = jnp.where(
            first_head, o0, pltpu.roll(o1, DSA_HEAD_DIM, 1)).astype(BF16)


def _dsa(fast, q, iq, vw, kkt, v1, topk):
    b, s, _ = q.shape
    t = min(DSA_Q_TILE, s)
    tk = min(DSA_K_TILE, s)
    nj = s // tk
    gt = 32 * LANES // tk
    n_groups = -(-nj // gt)
    tok = lambda width: pl.BlockSpec((1, t, width), lambda bi, i: (bi, i, 0))
    per_b = lambda shape: pl.BlockSpec((1,) + shape, lambda bi, i: (bi,) + (0,) * len(shape),
                                       pipeline_mode=pl.Buffered(1))
    return pl.pallas_call(
        functools.partial(_dsa_kernel, topk=topk, seq_len=s),
        grid=(b, s // t),
        in_specs=[pl.BlockSpec(memory_space=pltpu.SMEM),
                  tok(GRP_Q), tok(GRP_IQ), tok(LANES),
                  per_b((LANES, s)), per_b((nj, tk, LANES))],
        out_specs=tok(GRP_Q),
        out_shape=jax.ShapeDtypeStruct((b, s, GRP_Q), BF16),
        scratch_shapes=[
            pltpu.VMEM((gt, t, tk), jnp.int32),
            pltpu.VMEM((n_groups, 32, t, LANES), jnp.int32),
            pltpu.VMEM((n_groups, t, LANES), jnp.int32),
            pltpu.VMEM((n_groups, t, LANES), jnp.int32),
            pltpu.VMEM((DSA_HEADS, t, LANES), F32),
            pltpu.VMEM((DSA_HEADS, t, LANES), F32),
            pltpu.VMEM((nj, LANES, 2 * tk), BF16),
            pltpu.VMEM((nj, LANES, 2 * tk), BF16),
        ],
        compiler_params=_params(2),
        name="dsa",
    )(fast, q, iq, vw, kkt, v1.reshape(b, nj, tk, LANES))


def _split3_bf16(x):
    hi = x.astype(BF16)
    r1 = x - hi.astype(F32)
    mid = r1.astype(BF16)
    lo = (r1 - mid.astype(F32)).astype(BF16)
    return hi, mid, lo


def _hgrn_kernel(zh_ref, lb_ref, onorm_ref, o_ref, state_ref):
    i = pl.program_id(1)
    t = zh_ref.shape[1]
    c = HG_CHUNK
    width = HG_HEADS * HG_DK

    @pl.when(i == 0)
    def _():
        state_ref[...] = jnp.zeros_like(state_ref)

    rr = lax.broadcasted_iota(jnp.int32, (c, c), 0)
    cc = lax.broadcasted_iota(jnp.int32, (c, c), 1)
    tril = rr >= cc
    tril_b = jnp.where(tril, 1.0, 0.0).astype(BF16)

    n_chunks = t // c
    chunk = lambda n: slice(n * c, (n + 1) * c)
    head = lambda h: slice(h * HG_DK, (h + 1) * HG_DK)
    zq, zf, v, zg = (zh_ref[0, :, g * width:(g + 1) * width] for g in range(4))
    lb = lb_ref[...]
    log_lb, log1m_lb, one_m_lb = lb[0:1], lb[1:2], lb[2:3]

    e = jnp.exp(-jnp.abs(zf))
    b_ = log1m_lb + jnp.minimum(zf, 0.0) - jnp.log1p(e)
    log_f = jnp.maximum(log_lb, b_) + jnp.log1p(jnp.exp(-jnp.abs(log_lb - b_)))
    hk = one_m_lb * jnp.where(zf >= 0.0, e, 1.0) / (1.0 + e)
    hq = zq * jax.nn.sigmoid(zq)
    vb = v.astype(BF16)
    hi, mid, lo = _split3_bf16(log_f)
    cums = [_mm(tril_b, hi[chunk(n)]) + _mm(tril_b, mid[chunk(n)]) + _mm(tril_b, lo[chunk(n)])
            for n in range(n_chunks)]
    q_intra, k_intra, q_inter, k_state, decay = [], [], [], [], []
    for n in range(n_chunks):
        cum = cums[n]
        ref = cum[c // 2 - 1:c // 2]
        last = cum[c - 1:c]
        q_intra.append((hq[chunk(n)] * jnp.exp(cum - ref)).astype(BF16))
        k_intra.append((hk[chunk(n)] * jnp.exp(ref - cum)).astype(BF16))
        q_inter.append((hq[chunk(n)] * jnp.exp(cum)).astype(BF16))
        k_state.append((hk[chunk(n)] * jnp.exp(last - cum)).astype(BF16))
        decay.append(jnp.exp(last))
    att = [[_mm_nt(q_intra[n][:, head(h)], k_intra[n][:, head(h)]) for h in range(HG_HEADS)]
           for n in range(n_chunks)]
    o_intra = [[_mm(jnp.where(tril, att[n][h], 0.0).astype(BF16), vb[chunk(n), head(h)])
                for h in range(HG_HEADS)] for n in range(n_chunks)]
    states = [state_ref[h] for h in range(HG_HEADS)]
    o_inter = []
    for n in range(n_chunks):
        o_inter.append([_mm_nt(q_inter[n][:, head(h)], states[h].astype(BF16)) for h in range(HG_HEADS)])
        states = [states[h] * decay[n][:, head(h)] + _mm_tn(vb[chunk(n), head(h)], k_state[n][:, head(h)])
                  for h in range(HG_HEADS)]
    for h in range(HG_HEADS):
        state_ref[h] = states[h]
    gate = zg * jax.nn.sigmoid(zg)
    for n in range(n_chunks):
        for h in range(HG_HEADS):
            o = o_intra[n][h] + o_inter[n][h]
            o = o * _rms_scale(o) * onorm_ref[...]
            o_ref[0, chunk(n), head(h)] = (o * gate[chunk(n), head(h)]).astype(BF16)


def _hgrn(zh, lb_rows, onorm):
    b, s, _ = zh.shape
    t = min(HG_TILE, s)
    return pl.pallas_call(
        _hgrn_kernel,
        grid=(b, s // t),
        in_specs=[pl.BlockSpec((1, t, GRP_H), lambda bi, i: (bi, i, 0)),
                  pl.BlockSpec((SUBLANES, HG_HEADS * HG_DK), lambda bi, i: (0, 0)),
                  pl.BlockSpec((1, HG_DV), lambda bi, i: (0, 0))],
        out_specs=pl.BlockSpec((1, t, HG_HEADS * HG_DV), lambda bi, i: (bi, i, 0)),
        out_shape=jax.ShapeDtypeStruct((b, s, HG_HEADS * HG_DV), BF16),
        scratch_shapes=[pltpu.VMEM((HG_HEADS, HG_DV, HG_DK), F32)],
        compiler_params=_params(2),
        name="hgrn",
    )(zh, lb_rows, onorm)


def _merge_kernel(x_ref, nmix_ref, ya_ref, yb_ref, yc_ref, ym_ref, wg_ref, wl_ref, wo_ref, o_ref):
    x = x_ref[0]
    hb = (x * _rms_scale(x) * nmix_ref[...]).astype(BF16)
    merged = None
    for n, y_ref in enumerate((ya_ref, yb_ref, yc_ref, ym_ref)):
        gate = jax.nn.sigmoid(_mm(hb, wg_ref[:, n * D_MODEL:(n + 1) * D_MODEL]))
        term = gate * _mm(y_ref[0], wl_ref[n])
        merged = term if merged is None else merged + term
    o_ref[0] = x + _mm(merged.astype(BF16), wo_ref[...])


def _merge(layer, x, nmix, ya, yb, yc, ym, wg, wl, wo):
    b, s, _ = x.shape
    t = min(MERGE_TILE, s)
    tok = lambda width: pl.BlockSpec((1, t, width), lambda bi, i: (bi, i, 0))
    return pl.pallas_call(
        _merge_kernel,
        grid=(b, s // t),
        in_specs=[tok(D_MODEL), pl.BlockSpec((1, D_MODEL), lambda bi, i: (0, 0)),
                  tok(BRANCH_DIM), tok(BRANCH_DIM), tok(BRANCH_DIM), tok(BRANCH_DIM),
                  pl.BlockSpec((None, D_MODEL, N_BRANCH * D_MODEL), lambda bi, i: (layer, 0, 0),
                               pipeline_mode=pl.Buffered(1)),
                  pl.BlockSpec((None, N_BRANCH, BRANCH_DIM, D_MODEL), lambda bi, i: (layer, 0, 0, 0),
                               pipeline_mode=pl.Buffered(1)),
                  pl.BlockSpec((None, D_MODEL, D_MODEL), lambda bi, i: (layer, 0, 0),
                               pipeline_mode=pl.Buffered(1))],
        out_specs=tok(D_MODEL),
        out_shape=jax.ShapeDtypeStruct((b, s, D_MODEL), F32),
        compiler_params=_params(2),
        name="merge",
    )(x, nmix, ya, yb, yc, ym, wg, wl, wo)


def _ffn_kernel(x_ref, nffn_ref, wup_ref, wdn_ref, o_ref):
    x = x_ref[0]
    hb = (x * _rms_scale(x) * nffn_ref[...]).astype(BF16)
    out = x
    for n in range(FFN_DIM // FFN_CHUNK):
        lo = n * FFN_CHUNK
        gate = _mm(hb, wup_ref[:, lo:lo + FFN_CHUNK])
        up = _mm(hb, wup_ref[:, FFN_DIM + lo:FFN_DIM + lo + FFN_CHUNK])
        act = (gate * jax.nn.sigmoid(gate) * up).astype(BF16)
        out = out + _mm(act, wdn_ref[lo:lo + FFN_CHUNK, :])
    o_ref[0] = out


def _ffn(layer, x, nffn, wup, wdn):
    b, s, _ = x.shape
    t = min(FFN_TILE, s)
    tok = pl.BlockSpec((1, t, D_MODEL), lambda bi, i: (bi, i, 0))
    return pl.pallas_call(
        _ffn_kernel,
        grid=(b, s // t),
        in_specs=[tok, pl.BlockSpec((1, D_MODEL), lambda bi, i: (0, 0)),
                  pl.BlockSpec((None, D_MODEL, 2 * FFN_DIM), lambda bi, i: (layer, 0, 0),
                               pipeline_mode=pl.Buffered(1)),
                  pl.BlockSpec((None, FFN_DIM, D_MODEL), lambda bi, i: (layer, 0, 0),
                               pipeline_mode=pl.Buffered(1))],
        out_specs=tok,
        out_shape=jax.ShapeDtypeStruct((b, s, D_MODEL), F32),
        compiler_params=_params(2),
        name="ffn",
    )(x, nffn, wup, wdn)


def _rope_constants():
    inv_freq = 1.0 / (ROPE_THETA ** (jnp.arange(0, ROT_DIM, 2, dtype=F32) / ROT_DIM))
    invf = jnp.concatenate([inv_freq, inv_freq]).reshape(1, ROT_DIM)
    place = np.zeros((ROT_DIM, 3 * LANES), np.float32)
    ones = np.zeros((1, LANES), np.float32)
    for lane in range(LANES):
        d = lane % DSA_HEAD_DIM
        if d < ROT_HALF:
            place[d, lane] = 1.0
            place[ROT_HALF + d, LANES + lane] = -1.0
        elif d < ROT_DIM:
            place[d - ROT_HALF, lane] = 1.0
            place[d, 2 * LANES + lane] = 1.0
        else:
            ones[0, lane] = 1.0
    return invf, jnp.asarray(place, BF16), jnp.asarray(ones)


def _relayout_w_in(w_in):
    offs = np.concatenate([[0], np.cumsum(SPLIT_SIZES)])
    col = lambda n: w_in[..., int(offs[n]):int(offs[n + 1])]
    (a_x, a_b, a_c, d_q, d_k, d_v, i_q, i_k, i_w, g_q, g_f, g_i, g_g, m_q, gates) = (col(n) for n in range(15))
    pad = jnp.zeros(w_in.shape[:-1] + (LANES - DSA_HEAD_DIM - IDX_HEADS,), w_in.dtype)
    w1 = jnp.concatenate([a_x, a_b, a_c, d_q, d_k, i_k, d_v, i_w, pad, i_q, g_q, g_f, g_i, g_g, m_q], axis=-1)
    return w1.astype(BF16), gates.astype(BF16)


def kernel(x, mem, positions, norm_mix, w_in, conv_w, dsa_q_norm, dsa_k_norm, hgrn_lower_bounds,
           hgrn_out_norm, mem_norm, mem_w_kv, mem_q_norm, mem_k_norm, w_lift, w_out, norm_ffn,
           ffn_w_up, ffn_w_down):
    b, s, d = x.shape
    depth = w_in.shape[0]
    assert d == D_MODEL and w_in.shape[2] == sum(SPLIT_SIZES)
    assert s % HG_CHUNK == 0 and s % min(DSA_K_TILE, s) == 0
    topk = min(TOPK_MAX, s // 4)

    invf, rplace, rones = _rope_constants()
    ang = positions.astype(F32)[:, None, :] * invf.reshape(ROT_DIM, 1)
    cs = jnp.where(jnp.arange(ROT_DIM)[:, None] < ROT_HALF, jnp.cos(ang), jnp.sin(ang))
    lbs = jnp.cumsum(jax.nn.softmax(hgrn_lower_bounds.astype(F32), axis=0), axis=0)
    lbs = lbs - lbs[0:1]
    g64 = jnp.asarray(np.kron(np.eye(DSA_HEADS), np.full((DSA_HEAD_DIM, DSA_HEAD_DIM), 1.0 / DSA_HEAD_DIM)), BF16)
    row = lambda v: v.reshape(1, -1).astype(F32)
    w1, wg = _relayout_w_in(w_in)
    wkv, wl, wo = mem_w_kv.astype(BF16), w_lift.astype(BF16), w_out.astype(BF16)
    wup, wdn = ffn_w_up.astype(BF16), ffn_w_down.astype(BF16)

    for l in range(depth):
        lb = lbs[l]
        lb_rows = jnp.concatenate([jnp.stack([jnp.log(lb), jnp.log1p(-lb), 1.0 - lb]),
                                   jnp.zeros((SUBLANES - 3, lb.shape[0]), F32)])
        convw = jnp.concatenate([conv_w[l], jnp.zeros((SUBLANES - CONV_WIDTH, CONV_DIM), F32)])
        mkt, mv = _mem_kv(l, mem, row(mem_norm), wkv, row(mem_k_norm[l]))
        ya, q, iq, kkt, vw, v1, zh, ym = _proj(
            l, x, row(norm_mix[l]), w1, cs, rplace, rones, convw,
            row(jnp.tile(dsa_q_norm[l], DSA_HEADS)),
            row(jnp.concatenate([dsa_k_norm[l], jnp.ones((LANES - DSA_HEAD_DIM,), F32)])),
            g64, mkt, mv, row(mem_q_norm[l]))
        logit_bound = DSA_HEAD_DIM ** 0.5 * jnp.max(jnp.abs(dsa_q_norm[l])) * jnp.max(jnp.abs(dsa_k_norm[l]))
        fast = (logit_bound <= SAFE_LOGIT_BOUND).astype(jnp.int32).reshape(1)
        yb = _dsa(fast, q, iq, vw, kkt, v1, topk)
        yc = _hgrn(zh, lb_rows, row(hgrn_out_norm[l]))
        x = _merge(l, x, row(norm_mix[l]), ya, yb, yc, ym, wg, wl, wo)
        x = _ffn(l, x, row(norm_ffn[l]), wup, wdn)
    return x
```

```python
import functools

import jax
import jax.numpy as jnp
import numpy as np
from jax import lax
from jax.experimental import pallas as pl
from jax.experimental.pallas import tpu as pltpu

D_MODEL = 1024
MEM_TOKENS = 256
N_BRANCH = 4
BRANCH_DIM = 512
CONV_DIM = 512
CONV_WIDTH = 3
DSA_HEADS = 8
DSA_HEAD_DIM = 64
IDX_HEADS = 8
IDX_DIM = 64
TOPK_MAX = 256
HG_HEADS = 4
HG_DK = 128
HG_DV = 128
HG_CHUNK = 64
MEM_HEADS = 4
MEM_HEAD_DIM = 128
ROPE_THETA = 500000.0
ROT_DIM = DSA_HEAD_DIM // 4
ROT_HALF = ROT_DIM // 2
FFN_DIM = ((8 * D_MODEL // 3 + 255) // 256) * 256
EPS = 1e-6

SPLIT_SIZES = (CONV_DIM, CONV_DIM, CONV_DIM,
               DSA_HEADS * DSA_HEAD_DIM, DSA_HEAD_DIM, DSA_HEAD_DIM,
               IDX_HEADS * IDX_DIM, IDX_DIM, IDX_HEADS,
               HG_HEADS * HG_DK, HG_HEADS * HG_DK, HG_HEADS * HG_DV, HG_HEADS * HG_DV,
               MEM_HEADS * MEM_HEAD_DIM,
               N_BRANCH * D_MODEL)

LANES = 128
SUBLANES = 8
VMEM_LIMIT_BYTES = 56 * 1024 * 1024

PROJ_TILE = 512
DSA_Q_TILE = 512
DSA_K_TILE = 512
RADIX_PASSES_PER_TRIP = 8
HG_TILE = 512
MERGE_TILE = 1024
FFN_TILE = 1024
FFN_CHUNK = FFN_DIM // 2

GRP_A = 3 * CONV_DIM
GRP_Q = DSA_HEADS * DSA_HEAD_DIM
GRP_KV = 2 * LANES
GRP_IQ = IDX_HEADS * IDX_DIM
GRP_H = 4 * HG_HEADS * HG_DK
GRP_M = MEM_HEADS * MEM_HEAD_DIM
OFF_A = 0
OFF_Q = OFF_A + GRP_A
OFF_KV = OFF_Q + GRP_Q
OFF_IQ = OFF_KV + GRP_KV
OFF_H = OFF_IQ + GRP_IQ
OFF_M = OFF_H + GRP_H
PROJ_COLS = OFF_M + GRP_M

NEG_BIG = -1e30
INT_MIN = -(2 ** 31)
LOG2E = 1.4426950408889634
Q_SCALE = DSA_HEAD_DIM ** -0.5 * LOG2E
SAFE_LOGIT_BOUND = 40.0

BF16 = jnp.bfloat16
F32 = jnp.float32


def _mm(a, b):
    return jnp.dot(a, b, preferred_element_type=F32)


def _mm_nt(a, b):
    return lax.dot_general(a, b, (((1,), (1,)), ((), ())), preferred_element_type=F32)


def _mm_tn(a, b):
    return lax.dot_general(a, b, (((0,), (0,)), ((), ())), preferred_element_type=F32)


def _rms_scale(x):
    return lax.rsqrt(jnp.mean(x * x, axis=-1, keepdims=True) + EPS)


def _params(n_grid):
    return pltpu.CompilerParams(dimension_semantics=("arbitrary",) * n_grid,
                                vmem_limit_bytes=VMEM_LIMIT_BYTES)


def _mem_kv_kernel(mem_ref, mem_norm_ref, wkv_ref, knorm_ref, mkt_ref, mv_ref):
    m = mem_ref[0]
    mn = (m * _rms_scale(m) * mem_norm_ref[...]).astype(BF16)
    kv = _mm(mn, wkv_ref[...])
    for h in range(MEM_HEADS):
        kh = kv[:, h * MEM_HEAD_DIM:(h + 1) * MEM_HEAD_DIM]
        kh = kh * _rms_scale(kh) * knorm_ref[...]
        mkt_ref[0, h] = kh.T.astype(BF16)
        off = MEM_HEADS * MEM_HEAD_DIM + h * MEM_HEAD_DIM
        mv_ref[0, h] = kv[:, off:off + MEM_HEAD_DIM].astype(BF16)


def _mem_kv(layer, mem, mem_norm, wkv, knorm):
    b = mem.shape[0]
    return pl.pallas_call(
        _mem_kv_kernel,
        grid=(b,),
        in_specs=[
            pl.BlockSpec((1, MEM_TOKENS, D_MODEL), lambda i: (i, 0, 0)),
            pl.BlockSpec((1, D_MODEL), lambda i: (0, 0)),
            pl.BlockSpec((None, D_MODEL, 2 * GRP_M), lambda i: (layer, 0, 0)),
            pl.BlockSpec((1, MEM_HEAD_DIM), lambda i: (0, 0)),
        ],
        out_specs=[
            pl.BlockSpec((1, MEM_HEADS, MEM_HEAD_DIM, MEM_TOKENS), lambda i: (i, 0, 0, 0)),
            pl.BlockSpec((1, MEM_HEADS, MEM_TOKENS, MEM_HEAD_DIM), lambda i: (i, 0, 0, 0)),
        ],
        out_shape=[
            jax.ShapeDtypeStruct((b, MEM_HEADS, MEM_HEAD_DIM, MEM_TOKENS), BF16),
            jax.ShapeDtypeStruct((b, MEM_HEADS, MEM_TOKENS, MEM_HEAD_DIM), BF16),
        ],
        compiler_params=_params(1),
        name="mem_kv",
    )(mem, mem_norm, wkv, knorm)


def _rope_slab(s, c, sa, sb):
    return s * c + pltpu.roll(s, LANES - ROT_HALF, 1) * sa + pltpu.roll(s, ROT_HALF, 1) * sb


def _proj_kernel(x_ref, nmix_ref, w_ref, cs_ref, rplace_ref, rones_ref, convw_ref, qg_ref, kg_ref,
                 g64_ref, mkt_ref, mv_ref, mqg_ref,
                 ya_ref, q_ref, iq_ref, kkt_ref, vw_ref, v1_ref, zh_ref, ym_ref,
                 carry_ref):
    t = x_ref.shape[1]

    @pl.when(pl.program_id(1) == 0)
    def _():
        carry_ref[...] = jnp.zeros_like(carry_ref)

    x = x_ref[0]
    hb = (x * _rms_scale(x) * nmix_ref[...]).astype(BF16)

    cs_hi, cs_mid, cs_lo = _split3_bf16(cs_ref[0])
    place = rplace_ref[...]
    tabs = _mm_tn(cs_hi, place) + _mm_tn(cs_mid, place) + _mm_tn(cs_lo, place)
    rc = tabs[:, :LANES] + rones_ref[...]
    rsa = tabs[:, LANES:2 * LANES]
    rsb = tabs[:, 2 * LANES:]

    za = _mm(hb, w_ref[:, OFF_A:OFF_A + GRP_A])
    a_x, a_b, a_c = za[:, :CONV_DIM], za[:, CONV_DIM:2 * CONV_DIM], za[:, 2 * CONV_DIM:]
    u = a_c * a_x
    carry = carry_ref[...]
    row8 = lax.broadcasted_iota(jnp.int32, (SUBLANES, CONV_DIM), 0)
    r1 = pltpu.roll(u, 1, 0)
    r2 = pltpu.roll(u, 2, 0)
    top1 = jnp.where(row8 < 1, pltpu.roll(carry, 1, 0), r1[:SUBLANES])
    top2 = jnp.where(row8 < 2, pltpu.roll(carry, 2, 0), r2[:SUBLANES])
    u1 = jnp.concatenate([top1, r1[SUBLANES:]], axis=0)
    u2 = jnp.concatenate([top2, r2[SUBLANES:]], axis=0)
    carry_ref[...] = u[t - SUBLANES:]
    cw = convw_ref[...]
    ya_ref[0] = (a_b * (u2 * cw[0:1] + u1 * cw[1:2] + u * cw[2:3])).astype(BF16)

    zq = _mm(hb, w_ref[:, OFF_Q:OFF_Q + GRP_Q])
    msq = _mm((zq * zq).astype(BF16), g64_ref[...])
    qn = zq * lax.rsqrt(msq + EPS) * qg_ref[...]
    for p in range(GRP_Q // LANES):
        sl = slice(p * LANES, (p + 1) * LANES)
        q_ref[0, :, sl] = (_rope_slab(qn[:, sl], rc, rsa, rsb) * Q_SCALE).astype(BF16)

    zkv = _mm(hb, w_ref[:, OFF_KV:OFF_KV + GRP_KV])
    s0, s1 = zkv[:, :LANES], zkv[:, LANES:]
    lane = lax.broadcasted_iota(jnp.int32, (t, LANES), 1)
    is_k = lane < DSA_HEAD_DIM
    kms = jnp.sum(jnp.where(is_k, s0 * s0, 0.0), axis=-1, keepdims=True) * (1.0 / DSA_HEAD_DIM)
    s0 = s0 * jnp.where(is_k, lax.rsqrt(kms + EPS) * kg_ref[...], 1.0)
    kkt_ref[0] = _rope_slab(s0, rc, rsa, rsb).T.astype(BF16)
    w_scale = jnp.where(lane >= DSA_HEAD_DIM, jnp.where(lane < DSA_HEAD_DIM + IDX_HEADS, IDX_HEADS ** -0.5, 1.0), 1.0)
    vw_ref[0] = s1 * w_scale
    v1_ref[0] = jnp.where(is_k, s1, jnp.where(lane == DSA_HEAD_DIM, 1.0, 0.0)).astype(BF16)

    ziq = _mm(hb, w_ref[:, OFF_IQ:OFF_IQ + GRP_IQ])
    for p in range(GRP_IQ // LANES):
        sl = slice(p * LANES, (p + 1) * LANES)
        iq_ref[0, :, sl] = (_rope_slab(ziq[:, sl], rc, rsa, rsb) * (IDX_DIM ** -0.5)).astype(BF16)

    zm = _mm(hb, w_ref[:, OFF_M:OFF_M + GRP_M])
    for h in range(MEM_HEADS):
        sl = slice(h * MEM_HEAD_DIM, (h + 1) * MEM_HEAD_DIM)
        mq = zm[:, sl]
        mq = mq * _rms_scale(mq) * mqg_ref[...] * (MEM_HEAD_DIM ** -0.5)
        lg = _mm(mq.astype(BF16), mkt_ref[0, h])
        pe = jnp.exp(lg - jnp.max(lg, axis=-1, keepdims=True))
        den = jnp.sum(pe, axis=-1, keepdims=True)
        ym_ref[0, :, sl] = (_mm(pe.astype(BF16), mv_ref[0, h]) / den).astype(BF16)

    zh_ref[0] = _mm(hb, w_ref[:, OFF_H:OFF_H + GRP_H])


def _proj(layer, x, nmix, w1, cs, rplace, rones, convw, qg, kg, g64, mkt, mv, mqg):
    b, s, _ = x.shape
    t = min(PROJ_TILE, s)
    tok = lambda width: pl.BlockSpec((1, t, width), lambda bi, i: (bi, i, 0))
    const2 = lambda shape: pl.BlockSpec(shape, lambda bi, i: (0, 0))
    per_b4 = lambda shape: pl.BlockSpec((1,) + shape, lambda bi, i: (bi, 0, 0, 0))
    out_widths = (CONV_DIM, GRP_Q, GRP_IQ, LANES, LANES, LANES, GRP_H, GRP_M)
    out_dtypes = (BF16, BF16, BF16, BF16, F32, BF16, F32, BF16)
    return pl.pallas_call(
        _proj_kernel,
        grid=(b, s // t),
        in_specs=[
            tok(D_MODEL), const2((1, D_MODEL)),
            pl.BlockSpec((None, D_MODEL, PROJ_COLS), lambda bi, i: (layer, 0, 0)),
            pl.BlockSpec((1, ROT_DIM, t), lambda bi, i: (bi, 0, i)),
            const2((ROT_DIM, 3 * LANES)), const2((1, LANES)),
            const2((SUBLANES, CONV_DIM)), const2((1, GRP_Q)), const2((1, LANES)),
            const2((GRP_Q, GRP_Q)),
            per_b4((MEM_HEADS, MEM_HEAD_DIM, MEM_TOKENS)), per_b4((MEM_HEADS, MEM_TOKENS, MEM_HEAD_DIM)),
            const2((1, MEM_HEAD_DIM)),
        ],
        out_specs=[tok(w) if n != 3 else pl.BlockSpec((1, LANES, t), lambda bi, i: (bi, 0, i))
                   for n, w in enumerate(out_widths)],
        out_shape=[jax.ShapeDtypeStruct((b, s, w) if n != 3 else (b, LANES, s), d)
                   for n, (w, d) in enumerate(zip(out_widths, out_dtypes))],
        scratch_shapes=[pltpu.VMEM((SUBLANES, CONV_DIM), F32)],
        compiler_params=_params(2),
        name="proj",
    )(x, nmix, w1, cs, rplace, rones, convw, qg, kg, g64, mkt, mv, mqg)


def _bit_transpose32(words):
    a = list(words)
    for j, m in ((16, 0x0000FFFF), (8, 0x00FF00FF), (4, 0x0F0F0F0F), (2, 0x33333333), (1, 0x55555555)):
        for k in range(32):
            if k & j:
                continue
            tmp = (a[k] ^ lax.shift_right_logical(a[k + j], jnp.int32(j))) & m
            a[k] = a[k] ^ tmp
            a[k + j] = a[k + j] ^ lax.shift_left(tmp, jnp.int32(j))
    return a


def _low_bits(n):
    return jnp.where(n >= 32, -1, jnp.where(n <= 0, 0, lax.shift_left(jnp.int32(1), n) - 1))


def _dsa_kernel(fast_ref, q_ref, iq_ref, vw_ref, kkt_ref, v1_ref, o_ref,
                key_ref, plane_ref, cand_ref, sel_ref, m_ref, acc_ref, kbd_ref, ikbd_ref, *, topk, seq_len):
    i = pl.program_id(1)
    t = q_ref.shape[1]
    gt, _, tk = key_ref.shape
    n_groups = plane_ref.shape[0]
    cpt = tk // LANES
    n_pairs = DSA_HEADS // 2
    q0 = i * t
    nj = (q0 + t - 1) // tk + 1
    ng = (nj + gt - 1) // gt

    @pl.when(jnp.logical_and(pl.program_id(0) == 0, i == 0))
    def _():
        plane_ref[...] = jnp.zeros_like(plane_ref)

    @pl.when(i == 0)
    def _():
        kbd_ref[...] = jnp.zeros_like(kbd_ref)
        ikbd_ref[...] = jnp.zeros_like(ikbd_ref)
        d = DSA_HEAD_DIM
        for j in range(kbd_ref.shape[0]):
            kt = kkt_ref[0, :, j * tk:(j + 1) * tk]
            kbd_ref[j, :d, :tk] = kt[:d]
            kbd_ref[j, d:, tk:] = kt[:d]
            ikbd_ref[j, :d, :tk] = kt[d:]
            ikbd_ref[j, d:, tk:] = kt[d:]

    iw = vw_ref[0][:, DSA_HEAD_DIM:DSA_HEAD_DIM + IDX_HEADS]

    def score_group(g, carry):
        nt = jnp.minimum(nj - g * gt, gt)

        def score_tile(jj):
            sc = jnp.zeros((t, tk), F32)
            for p in range(n_pairs):
                r = _mm(iq_ref[0, :, p * LANES:(p + 1) * LANES], ikbd_ref[g * gt + jj])
                sc = sc + jnp.maximum(r[:, :tk], 0.0) * iw[:, 2 * p:2 * p + 1]
                sc = sc + jnp.maximum(r[:, tk:], 0.0) * iw[:, 2 * p + 1:2 * p + 2]
            sc = jnp.where(sc == 0.0, 0.0, sc)
            bits = pltpu.bitcast(sc, jnp.int32)
            key_ref[jj] = bits ^ ((bits >> 31) | INT_MIN)

        def score_two(k, c2):
            score_tile(2 * k)
            score_tile(2 * k + 1)
            return c2

        lax.fori_loop(0, nt // 2, score_two, 0)

        @pl.when(nt % 2 == 1)
        def _():
            score_tile(nt - 1)

        def clear_tile(jj, c2):
            key_ref[jj] = jnp.zeros((t, tk), jnp.int32)
            return c2

        lax.fori_loop(nt, gt, clear_tile, 0)

        def to_planes(rg, c2):
            rows = pl.ds(pl.multiple_of(rg * SUBLANES, SUBLANES), SUBLANES)
            words = []
            for k in range(32):
                ch = 31 - k
                words.append(key_ref[ch // cpt, rows, (ch % cpt) * LANES:(ch % cpt + 1) * LANES])
            for p, w in enumerate(_bit_transpose32(words)):
                plane_ref[g, p, rows, :] = w
            return c2

        lax.fori_loop(0, t // SUBLANES, to_planes, 0)
        return carry

    lax.fori_loop(0, ng, score_group, 0)

    lane = lax.broadcasted_iota(jnp.int32, (t, LANES), 1)
    qpos = lax.broadcasted_iota(jnp.int32, (t, LANES), 0) + q0
    chunks_valid = ((qpos - lane) >> 7) + 1

    def count_bits(words):
        tot = lax.population_count(words[0])
        for w in words[1:]:
            tot = tot + lax.population_count(w)
        return jnp.sum(tot.astype(F32), axis=-1, keepdims=True)

    for g in range(n_groups):
        cand_ref[g] = _low_bits(chunks_valid - 32 * g)
        sel_ref[g] = jnp.zeros((t, LANES), jnp.int32)

    def radix_select(groups):
        def radix(p, need):
            ones = [cand_ref[g] & plane_ref[g, p] for g in groups]
            c1 = count_bits(ones)
            take = c1 >= need
            for g, one in zip(groups, ones):
                cand = cand_ref[g]
                cand_ref[g] = jnp.where(take, one, cand ^ one)
                sel_ref[g] = jnp.where(take, sel_ref[g], sel_ref[g] | one)
            return jnp.where(take, need, need - c1)

        def radix_block(k, need):
            for u in range(RADIX_PASSES_PER_TRIP):
                need = radix(k * RADIX_PASSES_PER_TRIP + u, need)
            return need

        return lax.fori_loop(0, 32 // RADIX_PASSES_PER_TRIP, radix_block, jnp.full((t, 1), float(topk), F32))

    if n_groups == 1:
        need = radix_select(range(1))
    else:
        need = lax.switch(ng - 1, [functools.partial(radix_select, range(n)) for n in range(1, n_groups + 1)])

    has_tie = jnp.where(count_bits([cand_ref[g] for g in range(n_groups)]) > need, 1.0, 0.0)
    any_tie = jnp.max(has_tie) > 0.0

    @pl.when(jnp.logical_not(any_tie))
    def _():
        for g in range(n_groups):
            sel_ref[g] = sel_ref[g] | cand_ref[g]

    @pl.when(any_tie)
    def _():
        n_bits = max(1, int(seq_len - 1).bit_length())

        def below(g, cut):
            return cand_ref[g] & _low_bits(((cut - lane + (LANES - 1)) >> 7) - 32 * g)

        def bisect_idx(bit, pfx):
            cut = pfx | (jnp.int32(1) << (n_bits - 1 - bit))
            cnt = count_bits([below(g, cut) for g in range(n_groups)])
            return jnp.where(cnt < need, cut, pfx)

        cut = lax.fori_loop(0, n_bits, bisect_idx, jnp.zeros((t, 1), jnp.int32))
        cut = jnp.where(has_tie > 0.0, cut + 1, seq_len)
        for g in range(n_groups):
            sel_ref[g] = sel_ref[g] | below(g, cut)

    neg_bits = int(np.float32(NEG_BIG).view(np.int32))

    def tile_bias(j):
        words = sel_ref[j // gt]
        parts = []
        for c in range(cpt):
            k = (j % gt) * cpt + c
            picked = lax.shift_left(words, 31 - k) >> 31
            parts.append(pltpu.bitcast(neg_bits & ~picked, F32))
        return jnp.concatenate(parts, axis=-1)

    acc_ref[...] = jnp.zeros(acc_ref.shape, F32)
    use_fast = fast_ref[0] > 0

    @pl.when(use_fast)
    def _():
        def attend(j):
            bias = tile_bias(j)
            for p in range(n_pairs):
                lg = _mm(q_ref[0, :, p * LANES:(p + 1) * LANES], kbd_ref[j])
                for e in range(2):
                    pe = jnp.exp2(lg[:, e * tk:(e + 1) * tk] + bias).astype(BF16)
                    acc_ref[2 * p + e] += _mm(pe, v1_ref[0, j])

        def attend_two(k, carry):
            attend(2 * k)
            attend(2 * k + 1)
            return carry

        lax.fori_loop(0, nj // 2, attend_two, 0)

        @pl.when(nj % 2 == 1)
        def _():
            attend(nj - 1)

    @pl.when(jnp.logical_not(use_fast))
    def _():
        m_ref[...] = jnp.full(m_ref.shape, NEG_BIG, F32)

        def attend(j, carry):
            bias = tile_bias(j)
            for p in range(n_pairs):
                lg = _mm(q_ref[0, :, p * LANES:(p + 1) * LANES], kbd_ref[j])
                for e in range(2):
                    h = 2 * p + e
                    s = lg[:, e * tk:(e + 1) * tk] + bias
                    m_old = m_ref[h][:, :1]
                    m_new = jnp.maximum(m_old, jnp.max(s, axis=-1, keepdims=True))
                    pe = jnp.exp2(s - m_new).astype(BF16)
                    acc_ref[h] = acc_ref[h] * jnp.exp2(m_old - m_new) + _mm(pe, v1_ref[0, j])
                    m_ref[h] = jnp.broadcast_to(m_new, (t, LANES))
            return carry

        lax.fori_loop(0, nj, attend, 0)

    first_head = lax.broadcasted_iota(jnp.int32, (t, LANES), 1) < DSA_HEAD_DIM
    for p in range(n_pairs):
        a0, a1 = acc_ref[2 * p], acc_ref[2 * p + 1]
        o0 = a0 / a0[:, DSA_HEAD_DIM:DSA_HEAD_DIM + 1]
        o1 = a1 / a1[:, DSA_HEAD_DIM:DSA_HEAD_DIM + 1]
        o_ref[0, :, p * LANES:(p + 1) * LANES] = jnp.where(
            first_head, o0, pltpu.roll(o1, DSA_HEAD_DIM, 1)).astype(BF16)


def _dsa(fast, q, iq, vw, kkt, v1, topk):
    b, s, _ = q.shape
    t = min(DSA_Q_TILE, s)
    tk = min(DSA_K_TILE, s)
    nj = s // tk
    gt = 32 * LANES // tk
    n_groups = -(-nj // gt)
    tok = lambda width: pl.BlockSpec((1, t, width), lambda bi, i: (bi, i, 0))
    per_b = lambda shape: pl.BlockSpec((1,) + shape, lambda bi, i: (bi,) + (0,) * len(shape),
                                       pipeline_mode=pl.Buffered(1))
    return pl.pallas_call(
        functools.partial(_dsa_kernel, topk=topk, seq_len=s),
        grid=(b, s // t),
        in_specs=[pl.BlockSpec(memory_space=pltpu.SMEM),
                  tok(GRP_Q), tok(GRP_IQ), tok(LANES),
                  per_b((LANES, s)), per_b((nj, tk, LANES))],
        out_specs=tok(GRP_Q),
        out_shape=jax.ShapeDtypeStruct((b, s, GRP_Q), BF16),
        scratch_shapes=[
            pltpu.VMEM((gt, t, tk), jnp.int32),
            pltpu.VMEM((n_groups, 32, t, LANES), jnp.int32),
            pltpu.VMEM((n_groups, t, LANES), jnp.int32),
            pltpu.VMEM((n_groups, t, LANES), jnp.int32),
            pltpu.VMEM((DSA_HEADS, t, LANES), F32),
            pltpu.VMEM((DSA_HEADS, t, LANES), F32),
            pltpu.VMEM((nj, LANES, 2 * tk), BF16),
            pltpu.VMEM((nj, LANES, 2 * tk), BF16),
        ],
        compiler_params=_params(2),
        name="dsa",
    )(fast, q, iq, vw, kkt, v1.reshape(b, nj, tk, LANES))


def _split3_bf16(x):
    hi = x.astype(BF16)
    r1 = x - hi.astype(F32)
    mid = r1.astype(BF16)
    lo = (r1 - mid.astype(F32)).astype(BF16)
    return hi, mid, lo


def _hgrn_kernel(zh_ref, lb_ref, onorm_ref, o_ref, state_ref):
    i = pl.program_id(1)
    t = zh_ref.shape[1]
    c = HG_CHUNK
    width = HG_HEADS * HG_DK

    @pl.when(i == 0)
    def _():
        state_ref[...] = jnp.zeros_like(state_ref)

    rr = lax.broadcasted_iota(jnp.int32, (c, c), 0)
    cc = lax.broadcasted_iota(jnp.int32, (c, c), 1)
    tril = rr >= cc
    tril_b = jnp.where(tril, 1.0, 0.0).astype(BF16)

    n_chunks = t // c
    chunk = lambda n: slice(n * c, (n + 1) * c)
    head = lambda h: slice(h * HG_DK, (h + 1) * HG_DK)
    zq, zf, v, zg = (zh_ref[0, :, g * width:(g + 1) * width] for g in range(4))
    lb = lb_ref[...]
    log_lb, log1m_lb, one_m_lb = lb[0:1], lb[1:2], lb[2:3]

    e = jnp.exp(-jnp.abs(zf))
    b_ = log1m_lb + jnp.minimum(zf, 0.0) - jnp.log1p(e)
    log_f = jnp.maximum(log_lb, b_) + jnp.log1p(jnp.exp(-jnp.abs(log_lb - b_)))
    hk = one_m_lb * jnp.where(zf >= 0.0, e, 1.0) / (1.0 + e)
    hq = zq * jax.nn.sigmoid(zq)
    vb = v.astype(BF16)
    hi, mid, lo = _split3_bf16(log_f)
    gate = zg * jax.nn.sigmoid(zg)
    heads = range(HG_HEADS)
    states = [state_ref[h] for h in heads]
    q_intra, k_intra, q_inter, k_state, decay, att, o_intra = {}, {}, {}, {}, {}, {}, {}
    for step in range(n_chunks + 3):
        n = step
        if n < n_chunks:
            cum = _mm(tril_b, hi[chunk(n)]) + _mm(tril_b, mid[chunk(n)]) + _mm(tril_b, lo[chunk(n)])
            ref = cum[c // 2 - 1:c // 2]
            last = cum[c - 1:c]
            q_intra[n] = (hq[chunk(n)] * jnp.exp(cum - ref)).astype(BF16)
            k_intra[n] = (hk[chunk(n)] * jnp.exp(ref - cum)).astype(BF16)
            q_inter[n] = (hq[chunk(n)] * jnp.exp(cum)).astype(BF16)
            k_state[n] = (hk[chunk(n)] * jnp.exp(last - cum)).astype(BF16)
            decay[n] = jnp.exp(last)
        n = step - 1
        if 0 <= n < n_chunks:
            att[n] = [_mm_nt(q_intra[n][:, head(h)], k_intra[n][:, head(h)]) for h in heads]
        n = step - 2
        if 0 <= n < n_chunks:
            o_intra[n] = [_mm(jnp.where(tril, att[n][h], 0.0).astype(BF16), vb[chunk(n), head(h)]) for h in heads]
        n = step - 3
        if 0 <= n < n_chunks:
            o_inter = [_mm_nt(q_inter[n][:, head(h)], states[h].astype(BF16)) for h in heads]
            states = [states[h] * decay[n][:, head(h)] + _mm_tn(vb[chunk(n), head(h)], k_state[n][:, head(h)])
                      for h in heads]
            for h in heads:
                o = o_intra[n][h] + o_inter[h]
                o = o * _rms_scale(o) * onorm_ref[...]
                o_ref[0, chunk(n), head(h)] = (o * gate[chunk(n), head(h)]).astype(BF16)
    for h in heads:
        state_ref[h] = states[h]


def _hgrn(zh, lb_rows, onorm):
    b, s, _ = zh.shape
    t = min(HG_TILE, s)
    return pl.pallas_call(
        _hgrn_kernel,
        grid=(b, s // t),
        in_specs=[pl.BlockSpec((1, t, GRP_H), lambda bi, i: (bi, i, 0)),
                  pl.BlockSpec((SUBLANES, HG_HEADS * HG_DK), lambda bi, i: (0, 0)),
                  pl.BlockSpec((1, HG_DV), lambda bi, i: (0, 0))],
        out_specs=pl.BlockSpec((1, t, HG_HEADS * HG_DV), lambda bi, i: (bi, i, 0)),
        out_shape=jax.ShapeDtypeStruct((b, s, HG_HEADS * HG_DV), BF16),
        scratch_shapes=[pltpu.VMEM((HG_HEADS, HG_DV, HG_DK), F32)],
        compiler_params=_params(2),
        name="hgrn",
    )(zh, lb_rows, onorm)


def _merge_kernel(x_ref, nmix_ref, ya_ref, yb_ref, yc_ref, ym_ref, wg_ref, wl_ref, wo_ref, o_ref):
    x = x_ref[0]
    hb = (x * _rms_scale(x) * nmix_ref[...]).astype(BF16)
    merged = None
    for n, y_ref in enumerate((ya_ref, yb_ref, yc_ref, ym_ref)):
        gate = jax.nn.sigmoid(_mm(hb, wg_ref[:, n * D_MODEL:(n + 1) * D_MODEL]))
        term = gate * _mm(y_ref[0], wl_ref[n])
        merged = term if merged is None else merged + term
    o_ref[0] = x + _mm(merged.astype(BF16), wo_ref[...])


def _merge(layer, x, nmix, ya, yb, yc, ym, wg, wl, wo):
    b, s, _ = x.shape
    t = min(MERGE_TILE, s)
    tok = lambda width: pl.BlockSpec((1, t, width), lambda bi, i: (bi, i, 0))
    return pl.pallas_call(
        _merge_kernel,
        grid=(b, s // t),
        in_specs=[tok(D_MODEL), pl.BlockSpec((1, D_MODEL), lambda bi, i: (0, 0)),
                  tok(BRANCH_DIM), tok(BRANCH_DIM), tok(BRANCH_DIM), tok(BRANCH_DIM),
                  pl.BlockSpec((None, D_MODEL, N_BRANCH * D_MODEL), lambda bi, i: (layer, 0, 0),
                               pipeline_mode=pl.Buffered(1)),
                  pl.BlockSpec((None, N_BRANCH, BRANCH_DIM, D_MODEL), lambda bi, i: (layer, 0, 0, 0),
                               pipeline_mode=pl.Buffered(1)),
                  pl.BlockSpec((None, D_MODEL, D_MODEL), lambda bi, i: (layer, 0, 0),
                               pipeline_mode=pl.Buffered(1))],
        out_specs=tok(D_MODEL),
        out_shape=jax.ShapeDtypeStruct((b, s, D_MODEL), F32),
        compiler_params=_params(2),
        name="merge",
    )(x, nmix, ya, yb, yc, ym, wg, wl, wo)


def _ffn_kernel(x_ref, nffn_ref, wup_ref, wdn_ref, o_ref):
    x = x_ref[0]
    hb = (x * _rms_scale(x) * nffn_ref[...]).astype(BF16)
    out = x
    for n in range(FFN_DIM // FFN_CHUNK):
        lo = n * FFN_CHUNK
        gate = _mm(hb, wup_ref[:, lo:lo + FFN_CHUNK])
        up = _mm(hb, wup_ref[:, FFN_DIM + lo:FFN_DIM + lo + FFN_CHUNK])
        act = (gate * jax.nn.sigmoid(gate) * up).astype(BF16)
        out = out + _mm(act, wdn_ref[lo:lo + FFN_CHUNK, :])
    o_ref[0] = out


def _ffn(layer, x, nffn, wup, wdn):
    b, s, _ = x.shape
    t = min(FFN_TILE, s)
    tok = pl.BlockSpec((1, t, D_MODEL), lambda bi, i: (bi, i, 0))
    return pl.pallas_call(
        _ffn_kernel,
        grid=(b, s // t),
        in_specs=[tok, pl.BlockSpec((1, D_MODEL), lambda bi, i: (0, 0)),
                  pl.BlockSpec((None, D_MODEL, 2 * FFN_DIM), lambda bi, i: (layer, 0, 0),
                               pipeline_mode=pl.Buffered(1)),
                  pl.BlockSpec((None, FFN_DIM, D_MODEL), lambda bi, i: (layer, 0, 0),
                               pipeline_mode=pl.Buffered(1))],
        out_specs=tok,
        out_shape=jax.ShapeDtypeStruct((b, s, D_MODEL), F32),
        compiler_params=_params(2),
        name="ffn",
    )(x, nffn, wup, wdn)


def _rope_constants():
    inv_freq = 1.0 / (ROPE_THETA ** (jnp.arange(0, ROT_DIM, 2, dtype=F32) / ROT_DIM))
    invf = jnp.concatenate([inv_freq, inv_freq]).reshape(1, ROT_DIM)
    place = np.zeros((ROT_DIM, 3 * LANES), np.float32)
    ones = np.zeros((1, LANES), np.float32)
    for lane in range(LANES):
        d = lane % DSA_HEAD_DIM
        if d < ROT_HALF:
            place[d, lane] = 1.0
            place[ROT_HALF + d, LANES + lane] = -1.0
        elif d < ROT_DIM:
            place[d - ROT_HALF, lane] = 1.0
            place[d, 2 * LANES + lane] = 1.0
        else:
            ones[0, lane] = 1.0
    return invf, jnp.asarray(place, BF16), jnp.asarray(ones)


def _relayout_w_in(w_in):
    offs = np.concatenate([[0], np.cumsum(SPLIT_SIZES)])
    col = lambda n: w_in[..., int(offs[n]):int(offs[n + 1])]
    (a_x, a_b, a_c, d_q, d_k, d_v, i_q, i_k, i_w, g_q, g_f, g_i, g_g, m_q, gates) = (col(n) for n in range(15))
    pad = jnp.zeros(w_in.shape[:-1] + (LANES - DSA_HEAD_DIM - IDX_HEADS,), w_in.dtype)
    w1 = jnp.concatenate([a_x, a_b, a_c, d_q, d_k, i_k, d_v, i_w, pad, i_q, g_q, g_f, g_i, g_g, m_q], axis=-1)
    return w1.astype(BF16), gates.astype(BF16)


def kernel(x, mem, positions, norm_mix, w_in, conv_w, dsa_q_norm, dsa_k_norm, hgrn_lower_bounds,
           hgrn_out_norm, mem_norm, mem_w_kv, mem_q_norm, mem_k_norm, w_lift, w_out, norm_ffn,
           ffn_w_up, ffn_w_down):
    b, s, d = x.shape
    depth = w_in.shape[0]
    assert d == D_MODEL and w_in.shape[2] == sum(SPLIT_SIZES)
    assert s % HG_CHUNK == 0 and s % min(DSA_K_TILE, s) == 0
    topk = min(TOPK_MAX, s // 4)

    invf, rplace, rones = _rope_constants()
    ang = positions.astype(F32)[:, None, :] * invf.reshape(ROT_DIM, 1)
    cs = jnp.where(jnp.arange(ROT_DIM)[:, None] < ROT_HALF, jnp.cos(ang), jnp.sin(ang))
    lbs = jnp.cumsum(jax.nn.softmax(hgrn_lower_bounds.astype(F32), axis=0), axis=0)
    lbs = lbs - lbs[0:1]
    g64 = jnp.asarray(np.kron(np.eye(DSA_HEADS), np.full((DSA_HEAD_DIM, DSA_HEAD_DIM), 1.0 / DSA_HEAD_DIM)), BF16)
    row = lambda v: v.reshape(1, -1).astype(F32)
    w1, wg = _relayout_w_in(w_in)
    wkv, wl, wo = mem_w_kv.astype(BF16), w_lift.astype(BF16), w_out.astype(BF16)
    wup, wdn = ffn_w_up.astype(BF16), ffn_w_down.astype(BF16)

    for l in range(depth):
        lb = lbs[l]
        lb_rows = jnp.concatenate([jnp.stack([jnp.log(lb), jnp.log1p(-lb), 1.0 - lb]),
                                   jnp.zeros((SUBLANES - 3, lb.shape[0]), F32)])
        convw = jnp.concatenate([conv_w[l], jnp.zeros((SUBLANES - CONV_WIDTH, CONV_DIM), F32)])
        mkt, mv = _mem_kv(l, mem, row(mem_norm), wkv, row(mem_k_norm[l]))
        ya, q, iq, kkt, vw, v1, zh, ym = _proj(
            l, x, row(norm_mix[l]), w1, cs, rplace, rones, convw,
            row(jnp.tile(dsa_q_norm[l], DSA_HEADS)),
            row(jnp.concatenate([dsa_k_norm[l], jnp.ones((LANES - DSA_HEAD_DIM,), F32)])),
            g64, mkt, mv, row(mem_q_norm[l]))
        logit_bound = DSA_HEAD_DIM ** 0.5 * jnp.max(jnp.abs(dsa_q_norm[l])) * jnp.max(jnp.abs(dsa_k_norm[l]))
        fast = (logit_bound <= SAFE_LOGIT_BOUND).astype(jnp.int32).reshape(1)
        yb = _dsa(fast, q, iq, vw, kkt, v1, topk)
        yc = _hgrn(zh, lb_rows, row(hgrn_out_norm[l]))
        x = _merge(l, x, row(norm_mix[l]), ya, yb, yc, ym, wg, wl, wo)
        x = _ffn(l, x, row(norm_ffn[l]), wup, wdn)
    return x
```

```python
import functools

import jax
import jax.numpy as jnp
import numpy as np
from jax import lax
from jax.experimental import pallas as pl
from jax.experimental.pallas import tpu as pltpu

D_MODEL = 1024
MEM_TOKENS = 256
N_BRANCH = 4
BRANCH_DIM = 512
CONV_DIM = 512
CONV_WIDTH = 3
DSA_HEADS = 8
DSA_HEAD_DIM = 64
IDX_HEADS = 8
IDX_DIM = 64
TOPK_MAX = 256
HG_HEADS = 4
HG_DK = 128
HG_DV = 128
HG_CHUNK = 64
MEM_HEADS = 4
MEM_HEAD_DIM = 128
ROPE_THETA = 500000.0
ROT_DIM = DSA_HEAD_DIM // 4
ROT_HALF = ROT_DIM // 2
FFN_DIM = ((8 * D_MODEL // 3 + 255) // 256) * 256
EPS = 1e-6

SPLIT_SIZES = (CONV_DIM, CONV_DIM, CONV_DIM,
               DSA_HEADS * DSA_HEAD_DIM, DSA_HEAD_DIM, DSA_HEAD_DIM,
               IDX_HEADS * IDX_DIM, IDX_DIM, IDX_HEADS,
               HG_HEADS * HG_DK, HG_HEADS * HG_DK, HG_HEADS * HG_DV, HG_HEADS * HG_DV,
               MEM_HEADS * MEM_HEAD_DIM,
               N_BRANCH * D_MODEL)

LANES = 128
SUBLANES = 8
VMEM_LIMIT_BYTES = 56 * 1024 * 1024

PROJ_TILE = 512
DSA_Q_TILE = 512
DSA_K_TILE = 512
RADIX_PASSES_PER_TRIP = 8
HG_TILE = 512
MERGE_TILE = 1024
FFN_TILE = 1024
FFN_CHUNK = FFN_DIM // 2

GRP_A = 3 * CONV_DIM
GRP_Q = DSA_HEADS * DSA_HEAD_DIM
GRP_KV = 2 * LANES
GRP_IQ = IDX_HEADS * IDX_DIM
GRP_H = 4 * HG_HEADS * HG_DK
GRP_M = MEM_HEADS * MEM_HEAD_DIM
OFF_A = 0
OFF_Q = OFF_A + GRP_A
OFF_KV = OFF_Q + GRP_Q
OFF_IQ = OFF_KV + GRP_KV
OFF_H = OFF_IQ + GRP_IQ
OFF_M = OFF_H + GRP_H
PROJ_COLS = OFF_M + GRP_M

NEG_BIG = -1e30
INT_MIN = -(2 ** 31)
LOG2E = 1.4426950408889634
Q_SCALE = DSA_HEAD_DIM ** -0.5 * LOG2E
SAFE_LOGIT_BOUND = 40.0

BF16 = jnp.bfloat16
F32 = jnp.float32


def _mm(a, b):
    return jnp.dot(a, b, preferred_element_type=F32)


def _mm_nt(a, b):
    return lax.dot_general(a, b, (((1,), (1,)), ((), ())), preferred_element_type=F32)


def _mm_tn(a, b):
    return lax.dot_general(a, b, (((0,), (0,)), ((), ())), preferred_element_type=F32)


def _rms_scale(x):
    return lax.rsqrt(jnp.mean(x * x, axis=-1, keepdims=True) + EPS)


def _token_tile(preferred, s):
    t = min(preferred, s)
    assert s % t == 0, f"sequence length {s} is not a multiple of the token tile {t}"
    return t


def _params(n_grid):
    return pltpu.CompilerParams(dimension_semantics=("arbitrary",) * n_grid,
                                vmem_limit_bytes=VMEM_LIMIT_BYTES)


def _mem_kv_kernel(mem_ref, mem_norm_ref, wkv_ref, knorm_ref, mkt_ref, mv_ref):
    m = mem_ref[0]
    mn = (m * _rms_scale(m) * mem_norm_ref[...]).astype(BF16)
    kv = _mm(mn, wkv_ref[...])
    for h in range(MEM_HEADS):
        kh = kv[:, h * MEM_HEAD_DIM:(h + 1) * MEM_HEAD_DIM]
        kh = kh * _rms_scale(kh) * knorm_ref[...]
        mkt_ref[0, h] = kh.T.astype(BF16)
        off = MEM_HEADS * MEM_HEAD_DIM + h * MEM_HEAD_DIM
        mv_ref[0, h] = kv[:, off:off + MEM_HEAD_DIM].astype(BF16)


def _mem_kv(layer, mem, mem_norm, wkv, knorm):
    b = mem.shape[0]
    return pl.pallas_call(
        _mem_kv_kernel,
        grid=(b,),
        in_specs=[
            pl.BlockSpec((1, MEM_TOKENS, D_MODEL), lambda i: (i, 0, 0)),
            pl.BlockSpec((1, D_MODEL), lambda i: (0, 0)),
            pl.BlockSpec((None, D_MODEL, 2 * GRP_M), lambda i: (layer, 0, 0)),
            pl.BlockSpec((1, MEM_HEAD_DIM), lambda i: (0, 0)),
        ],
        out_specs=[
            pl.BlockSpec((1, MEM_HEADS, MEM_HEAD_DIM, MEM_TOKENS), lambda i: (i, 0, 0, 0)),
            pl.BlockSpec((1, MEM_HEADS, MEM_TOKENS, MEM_HEAD_DIM), lambda i: (i, 0, 0, 0)),
        ],
        out_shape=[
            jax.ShapeDtypeStruct((b, MEM_HEADS, MEM_HEAD_DIM, MEM_TOKENS), BF16),
            jax.ShapeDtypeStruct((b, MEM_HEADS, MEM_TOKENS, MEM_HEAD_DIM), BF16),
        ],
        compiler_params=_params(1),
        name="mem_kv",
    )(mem, mem_norm, wkv, knorm)


def _rope_slab(s, c, sa, sb):
    return s * c + pltpu.roll(s, LANES - ROT_HALF, 1) * sa + pltpu.roll(s, ROT_HALF, 1) * sb


def _proj_kernel(x_ref, nmix_ref, w_ref, cs_ref, rplace_ref, rones_ref, convw_ref, qg_ref, kg_ref,
                 g64_ref, mkt_ref, mv_ref, mqg_ref,
                 ya_ref, q_ref, iq_ref, kkt_ref, vw_ref, v1_ref, zh_ref, ym_ref,
                 carry_ref):
    t = x_ref.shape[1]

    @pl.when(pl.program_id(1) == 0)
    def _():
        carry_ref[...] = jnp.zeros_like(carry_ref)

    x = x_ref[0]
    hb = (x * _rms_scale(x) * nmix_ref[...]).astype(BF16)

    cs_hi, cs_mid, cs_lo = _split3_bf16(cs_ref[0])
    place = rplace_ref[...]
    tabs = _mm_tn(cs_hi, place) + _mm_tn(cs_mid, place) + _mm_tn(cs_lo, place)
    rc = tabs[:, :LANES] + rones_ref[...]
    rsa = tabs[:, LANES:2 * LANES]
    rsb = tabs[:, 2 * LANES:]

    za = _mm(hb, w_ref[:, OFF_A:OFF_A + GRP_A])
    a_x, a_b, a_c = za[:, :CONV_DIM], za[:, CONV_DIM:2 * CONV_DIM], za[:, 2 * CONV_DIM:]
    u = a_c * a_x
    carry = carry_ref[...]
    row8 = lax.broadcasted_iota(jnp.int32, (SUBLANES, CONV_DIM), 0)
    r1 = pltpu.roll(u, 1, 0)
    r2 = pltpu.roll(u, 2, 0)
    top1 = jnp.where(row8 < 1, pltpu.roll(carry, 1, 0), r1[:SUBLANES])
    top2 = jnp.where(row8 < 2, pltpu.roll(carry, 2, 0), r2[:SUBLANES])
    u1 = jnp.concatenate([top1, r1[SUBLANES:]], axis=0)
    u2 = jnp.concatenate([top2, r2[SUBLANES:]], axis=0)
    carry_ref[...] = u[t - SUBLANES:]
    cw = convw_ref[...]
    ya_ref[0] = (a_b * (u2 * cw[0:1] + u1 * cw[1:2] + u * cw[2:3])).astype(BF16)

    zq = _mm(hb, w_ref[:, OFF_Q:OFF_Q + GRP_Q])
    msq = _mm((zq * zq).astype(BF16), g64_ref[...])
    qn = zq * lax.rsqrt(msq + EPS) * qg_ref[...]
    for p in range(GRP_Q // LANES):
        sl = slice(p * LANES, (p + 1) * LANES)
        q_ref[0, :, sl] = (_rope_slab(qn[:, sl], rc, rsa, rsb) * Q_SCALE).astype(BF16)

    zkv = _mm(hb, w_ref[:, OFF_KV:OFF_KV + GRP_KV])
    s0, s1 = zkv[:, :LANES], zkv[:, LANES:]
    lane = lax.broadcasted_iota(jnp.int32, (t, LANES), 1)
    is_k = lane < DSA_HEAD_DIM
    kms = jnp.sum(jnp.where(is_k, s0 * s0, 0.0), axis=-1, keepdims=True) * (1.0 / DSA_HEAD_DIM)
    s0 = s0 * jnp.where(is_k, lax.rsqrt(kms + EPS) * kg_ref[...], 1.0)
    kkt_ref[0] = _rope_slab(s0, rc, rsa, rsb).T.astype(BF16)
    w_scale = jnp.where(lane >= DSA_HEAD_DIM, jnp.where(lane < DSA_HEAD_DIM + IDX_HEADS, IDX_HEADS ** -0.5, 1.0), 1.0)
    vw_ref[0] = s1 * w_scale
    v1_ref[0] = jnp.where(is_k, s1, jnp.where(lane == DSA_HEAD_DIM, 1.0, 0.0)).astype(BF16)

    ziq = _mm(hb, w_ref[:, OFF_IQ:OFF_IQ + GRP_IQ])
    for p in range(GRP_IQ // LANES):
        sl = slice(p * LANES, (p + 1) * LANES)
        iq_ref[0, :, sl] = (_rope_slab(ziq[:, sl], rc, rsa, rsb) * (IDX_DIM ** -0.5)).astype(BF16)

    zm = _mm(hb, w_ref[:, OFF_M:OFF_M + GRP_M])
    for h in range(MEM_HEADS):
        sl = slice(h * MEM_HEAD_DIM, (h + 1) * MEM_HEAD_DIM)
        mq = zm[:, sl]
        mq = mq * _rms_scale(mq) * mqg_ref[...] * (MEM_HEAD_DIM ** -0.5)
        lg = _mm(mq.astype(BF16), mkt_ref[0, h])
        pe = jnp.exp(lg - jnp.max(lg, axis=-1, keepdims=True))
        den = jnp.sum(pe, axis=-1, keepdims=True)
        ym_ref[0, :, sl] = (_mm(pe.astype(BF16), mv_ref[0, h]) / den).astype(BF16)

    zh_ref[0] = _mm(hb, w_ref[:, OFF_H:OFF_H + GRP_H])


def _proj(layer, x, nmix, w1, cs, rplace, rones, convw, qg, kg, g64, mkt, mv, mqg):
    b, s, _ = x.shape
    t = _token_tile(PROJ_TILE, s)
    tok = lambda width: pl.BlockSpec((1, t, width), lambda bi, i: (bi, i, 0))
    const2 = lambda shape: pl.BlockSpec(shape, lambda bi, i: (0, 0))
    per_b4 = lambda shape: pl.BlockSpec((1,) + shape, lambda bi, i: (bi, 0, 0, 0))
    out_widths = (CONV_DIM, GRP_Q, GRP_IQ, LANES, LANES, LANES, GRP_H, GRP_M)
    out_dtypes = (BF16, BF16, BF16, BF16, F32, BF16, F32, BF16)
    return pl.pallas_call(
        _proj_kernel,
        grid=(b, s // t),
        in_specs=[
            tok(D_MODEL), const2((1, D_MODEL)),
            pl.BlockSpec((None, D_MODEL, PROJ_COLS), lambda bi, i: (layer, 0, 0)),
            pl.BlockSpec((1, ROT_DIM, t), lambda bi, i: (bi, 0, i)),
            const2((ROT_DIM, 3 * LANES)), const2((1, LANES)),
            const2((SUBLANES, CONV_DIM)), const2((1, GRP_Q)), const2((1, LANES)),
            const2((GRP_Q, GRP_Q)),
            per_b4((MEM_HEADS, MEM_HEAD_DIM, MEM_TOKENS)), per_b4((MEM_HEADS, MEM_TOKENS, MEM_HEAD_DIM)),
            const2((1, MEM_HEAD_DIM)),
        ],
        out_specs=[tok(w) if n != 3 else pl.BlockSpec((1, LANES, t), lambda bi, i: (bi, 0, i))
                   for n, w in enumerate(out_widths)],
        out_shape=[jax.ShapeDtypeStruct((b, s, w) if n != 3 else (b, LANES, s), d)
                   for n, (w, d) in enumerate(zip(out_widths, out_dtypes))],
        scratch_shapes=[pltpu.VMEM((SUBLANES, CONV_DIM), F32)],
        compiler_params=_params(2),
        name="proj",
    )(x, nmix, w1, cs, rplace, rones, convw, qg, kg, g64, mkt, mv, mqg)


def _bit_transpose32(words):
    a = list(words)
    for j, m in ((16, 0x0000FFFF), (8, 0x00FF00FF), (4, 0x0F0F0F0F), (2, 0x33333333), (1, 0x55555555)):
        for k in range(32):
            if k & j:
                continue
            tmp = (a[k] ^ lax.shift_right_logical(a[k + j], jnp.int32(j))) & m
            a[k] = a[k] ^ tmp
            a[k + j] = a[k + j] ^ lax.shift_left(tmp, jnp.int32(j))
    return a


def _low_bits(n):
    return jnp.where(n >= 32, -1, jnp.where(n <= 0, 0, lax.shift_left(jnp.int32(1), n) - 1))


def _dsa_kernel(fast_ref, q_ref, iq_ref, vw_ref, kkt_ref, v1_ref, o_ref,
                key_ref, plane_ref, cand_ref, sel_ref, m_ref, acc_ref, kbd_ref, ikbd_ref, *, topk, seq_len):
    i = pl.program_id(1)
    t = q_ref.shape[1]
    gt, _, tk = key_ref.shape
    n_groups = plane_ref.shape[0]
    cpt = tk // LANES
    n_pairs = DSA_HEADS // 2
    q0 = i * t
    nj = (q0 + t - 1) // tk + 1
    ng = (nj + gt - 1) // gt

    @pl.when(jnp.logical_and(pl.program_id(0) == 0, i == 0))
    def _():
        plane_ref[...] = jnp.zeros_like(plane_ref)

    @pl.when(i == 0)
    def _():
        kbd_ref[...] = jnp.zeros_like(kbd_ref)
        ikbd_ref[...] = jnp.zeros_like(ikbd_ref)
        d = DSA_HEAD_DIM
        for j in range(kbd_ref.shape[0]):
            kt = kkt_ref[0, :, j * tk:(j + 1) * tk]
            kbd_ref[j, :d, :tk] = kt[:d]
            kbd_ref[j, d:, tk:] = kt[:d]
            ikbd_ref[j, :d, :tk] = kt[d:]
            ikbd_ref[j, d:, tk:] = kt[d:]

    iw = vw_ref[0][:, DSA_HEAD_DIM:DSA_HEAD_DIM + IDX_HEADS]

    def score_group(g, carry):
        nt = jnp.minimum(nj - g * gt, gt)

        def score_tile(jj):
            sc = jnp.zeros((t, tk), F32)
            for p in range(n_pairs):
                r = _mm(iq_ref[0, :, p * LANES:(p + 1) * LANES], ikbd_ref[g * gt + jj])
                sc = sc + jnp.maximum(r[:, :tk], 0.0) * iw[:, 2 * p:2 * p + 1]
                sc = sc + jnp.maximum(r[:, tk:], 0.0) * iw[:, 2 * p + 1:2 * p + 2]
            sc = jnp.where(sc == 0.0, 0.0, sc)
            bits = pltpu.bitcast(sc, jnp.int32)
            key_ref[jj] = bits ^ ((bits >> 31) | INT_MIN)

        def score_two(k, c2):
            score_tile(2 * k)
            score_tile(2 * k + 1)
            return c2

        lax.fori_loop(0, nt // 2, score_two, 0)

        @pl.when(nt % 2 == 1)
        def _():
            score_tile(nt - 1)

        def clear_tile(jj, c2):
            key_ref[jj] = jnp.zeros((t, tk), jnp.int32)
            return c2

        lax.fori_loop(nt, gt, clear_tile, 0)

        def to_planes(rg, c2):
            rows = pl.ds(pl.multiple_of(rg * SUBLANES, SUBLANES), SUBLANES)
            words = []
            for k in range(32):
                ch = 31 - k
                words.append(key_ref[ch // cpt, rows, (ch % cpt) * LANES:(ch % cpt + 1) * LANES])
            for p, w in enumerate(_bit_transpose32(words)):
                plane_ref[g, p, rows, :] = w
            return c2

        lax.fori_loop(0, t // SUBLANES, to_planes, 0)
        return carry

    lax.fori_loop(0, ng, score_group, 0)

    lane = lax.broadcasted_iota(jnp.int32, (t, LANES), 1)
    qpos = lax.broadcasted_iota(jnp.int32, (t, LANES), 0) + q0
    chunks_valid = ((qpos - lane) >> 7) + 1

    def count_bits(words):
        tot = lax.population_count(words[0])
        for w in words[1:]:
            tot = tot + lax.population_count(w)
        return jnp.sum(tot.astype(F32), axis=-1, keepdims=True)

    for g in range(n_groups):
        cand_ref[g] = _low_bits(chunks_valid - 32 * g)
        sel_ref[g] = jnp.zeros((t, LANES), jnp.int32)

    def radix_select(groups):
        def radix(p, need):
            ones = [cand_ref[g] & plane_ref[g, p] for g in groups]
            c1 = count_bits(ones)
            take = c1 >= need
            for g, one in zip(groups, ones):
                cand = cand_ref[g]
                cand_ref[g] = jnp.where(take, one, cand ^ one)
                sel_ref[g] = jnp.where(take, sel_ref[g], sel_ref[g] | one)
            return jnp.where(take, need, need - c1)

        def radix_block(k, need):
            for u in range(RADIX_PASSES_PER_TRIP):
                need = radix(k * RADIX_PASSES_PER_TRIP + u, need)
            return need

        return lax.fori_loop(0, 32 // RADIX_PASSES_PER_TRIP, radix_block, jnp.full((t, 1), float(topk), F32))

    if n_groups == 1:
        need = radix_select(range(1))
    else:
        need = lax.switch(ng - 1, [functools.partial(radix_select, range(n)) for n in range(1, n_groups + 1)])

    has_tie = jnp.where(count_bits([cand_ref[g] for g in range(n_groups)]) > need, 1.0, 0.0)
    any_tie = jnp.max(has_tie) > 0.0

    @pl.when(jnp.logical_not(any_tie))
    def _():
        for g in range(n_groups):
            sel_ref[g] = sel_ref[g] | cand_ref[g]

    @pl.when(any_tie)
    def _():
        n_bits = max(1, int(seq_len - 1).bit_length())

        def below(g, cut):
            return cand_ref[g] & _low_bits(((cut - lane + (LANES - 1)) >> 7) - 32 * g)

        def bisect_idx(bit, pfx):
            cut = pfx | (jnp.int32(1) << (n_bits - 1 - bit))
            cnt = count_bits([below(g, cut) for g in range(n_groups)])
            return jnp.where(cnt < need, cut, pfx)

        cut = lax.fori_loop(0, n_bits, bisect_idx, jnp.zeros((t, 1), jnp.int32))
        cut = jnp.where(has_tie > 0.0, cut + 1, seq_len)
        for g in range(n_groups):
            sel_ref[g] = sel_ref[g] | below(g, cut)

    neg_bits = int(np.float32(NEG_BIG).view(np.int32))

    def tile_bias(j):
        words = sel_ref[j // gt]
        parts = []
        for c in range(cpt):
            k = (j % gt) * cpt + c
            picked = lax.shift_left(words, 31 - k) >> 31
            parts.append(pltpu.bitcast(neg_bits & ~picked, F32))
        return jnp.concatenate(parts, axis=-1)

    acc_ref[...] = jnp.zeros(acc_ref.shape, F32)
    use_fast = fast_ref[0] > 0

    @pl.when(use_fast)
    def _():
        def attend(j):
            bias = tile_bias(j)
            for p in range(n_pairs):
                lg = _mm(q_ref[0, :, p * LANES:(p + 1) * LANES], kbd_ref[j])
                for e in range(2):
                    pe = jnp.exp2(lg[:, e * tk:(e + 1) * tk] + bias).astype(BF16)
                    acc_ref[2 * p + e] += _mm(pe, v1_ref[0, j])

        def attend_two(k, carry):
            attend(2 * k)
            attend(2 * k + 1)
            return carry

        lax.fori_loop(0, nj // 2, attend_two, 0)

        @pl.when(nj % 2 == 1)
        def _():
            attend(nj - 1)

    @pl.when(jnp.logical_not(use_fast))
    def _():
        m_ref[...] = jnp.full(m_ref.shape, NEG_BIG, F32)

        def attend(j, carry):
            bias = tile_bias(j)
            for p in range(n_pairs):
                lg = _mm(q_ref[0, :, p * LANES:(p + 1) * LANES], kbd_ref[j])
                for e in range(2):
                    h = 2 * p + e
                    s = lg[:, e * tk:(e + 1) * tk] + bias
                    m_old = m_ref[h][:, :1]
                    m_new = jnp.maximum(m_old, jnp.max(s, axis=-1, keepdims=True))
                    pe = jnp.exp2(s - m_new).astype(BF16)
                    acc_ref[h] = acc_ref[h] * jnp.exp2(m_old - m_new) + _mm(pe, v1_ref[0, j])
                    m_ref[h] = jnp.broadcast_to(m_new, (t, LANES))
            return carry

        lax.fori_loop(0, nj, attend, 0)

    first_head = lax.broadcasted_iota(jnp.int32, (t, LANES), 1) < DSA_HEAD_DIM
    for p in range(n_pairs):
        a0, a1 = acc_ref[2 * p], acc_ref[2 * p + 1]
        o0 = a0 / a0[:, DSA_HEAD_DIM:DSA_HEAD_DIM + 1]
        o1 = a1 / a1[:, DSA_HEAD_DIM:DSA_HEAD_DIM + 1]
        o_ref[0, :, p * LANES:(p + 1) * LANES] = jnp.where(
            first_head, o0, pltpu.roll(o1, DSA_HEAD_DIM, 1)).astype(BF16)


def _dsa(fast, q, iq, vw, kkt, v1, topk):
    b, s, _ = q.shape
    t = _token_tile(DSA_Q_TILE, s)
    tk = _token_tile(DSA_K_TILE, s)
    assert (32 * LANES) % tk == 0 and tk % LANES == 0
    nj = s // tk
    gt = 32 * LANES // tk
    n_groups = -(-nj // gt)
    tok = lambda width: pl.BlockSpec((1, t, width), lambda bi, i: (bi, i, 0))
    per_b = lambda shape: pl.BlockSpec((1,) + shape, lambda bi, i: (bi,) + (0,) * len(shape),
                                       pipeline_mode=pl.Buffered(1))
    return pl.pallas_call(
        functools.partial(_dsa_kernel, topk=topk, seq_len=s),
        grid=(b, s // t),
        in_specs=[pl.BlockSpec(memory_space=pltpu.SMEM),
                  tok(GRP_Q), tok(GRP_IQ), tok(LANES),
                  per_b((LANES, s)), per_b((nj, tk, LANES))],
        out_specs=tok(GRP_Q),
        out_shape=jax.ShapeDtypeStruct((b, s, GRP_Q), BF16),
        scratch_shapes=[
            pltpu.VMEM((gt, t, tk), jnp.int32),
            pltpu.VMEM((n_groups, 32, t, LANES), jnp.int32),
            pltpu.VMEM((n_groups, t, LANES), jnp.int32),
            pltpu.VMEM((n_groups, t, LANES), jnp.int32),
            pltpu.VMEM((DSA_HEADS, t, LANES), F32),
            pltpu.VMEM((DSA_HEADS, t, LANES), F32),
            pltpu.VMEM((nj, LANES, 2 * tk), BF16),
            pltpu.VMEM((nj, LANES, 2 * tk), BF16),
        ],
        compiler_params=_params(2),
        name="dsa",
    )(fast, q, iq, vw, kkt, v1.reshape(b, nj, tk, LANES))


def _split3_bf16(x):
    hi = x.astype(BF16)
    r1 = x - hi.astype(F32)
    mid = r1.astype(BF16)
    lo = (r1 - mid.astype(F32)).astype(BF16)
    return hi, mid, lo


def _hgrn_kernel(zh_ref, lb_ref, onorm_ref, o_ref, state_ref):
    i = pl.program_id(1)
    t = zh_ref.shape[1]
    c = HG_CHUNK
    width = HG_HEADS * HG_DK

    @pl.when(i == 0)
    def _():
        state_ref[...] = jnp.zeros_like(state_ref)

    rr = lax.broadcasted_iota(jnp.int32, (c, c), 0)
    cc = lax.broadcasted_iota(jnp.int32, (c, c), 1)
    tril = rr >= cc
    tril_b = jnp.where(tril, 1.0, 0.0).astype(BF16)

    n_chunks = t // c
    chunk = lambda n: slice(n * c, (n + 1) * c)
    head = lambda h: slice(h * HG_DK, (h + 1) * HG_DK)
    zq, zf, v, zg = (zh_ref[0, :, g * width:(g + 1) * width] for g in range(4))
    lb = lb_ref[...]
    log_lb, log1m_lb, one_m_lb = lb[0:1], lb[1:2], lb[2:3]

    e = jnp.exp(-jnp.abs(zf))
    b_ = log1m_lb + jnp.minimum(zf, 0.0) - jnp.log1p(e)
    log_f = jnp.maximum(log_lb, b_) + jnp.log1p(jnp.exp(-jnp.abs(log_lb - b_)))
    hk = one_m_lb * jnp.where(zf >= 0.0, e, 1.0) / (1.0 + e)
    hq = zq * jax.nn.sigmoid(zq)
    vb = v.astype(BF16)
    hi, mid, lo = _split3_bf16(log_f)
    gate = zg * jax.nn.sigmoid(zg)
    heads = range(HG_HEADS)
    states = [state_ref[h] for h in heads]
    q_intra, k_intra, q_inter, k_state, decay, att, o_intra = {}, {}, {}, {}, {}, {}, {}
    for step in range(n_chunks + 3):
        n = step
        if n < n_chunks:
            cum = _mm(tril_b, hi[chunk(n)]) + _mm(tril_b, mid[chunk(n)]) + _mm(tril_b, lo[chunk(n)])
            ref = cum[c // 2 - 1:c // 2]
            last = cum[c - 1:c]
            q_intra[n] = (hq[chunk(n)] * jnp.exp(cum - ref)).astype(BF16)
            k_intra[n] = (hk[chunk(n)] * jnp.exp(ref - cum)).astype(BF16)
            q_inter[n] = (hq[chunk(n)] * jnp.exp(cum)).astype(BF16)
            k_state[n] = (hk[chunk(n)] * jnp.exp(last - cum)).astype(BF16)
            decay[n] = jnp.exp(last)
        n = step - 1
        if 0 <= n < n_chunks:
            att[n] = [_mm_nt(q_intra[n][:, head(h)], k_intra[n][:, head(h)]) for h in heads]
        n = step - 2
        if 0 <= n < n_chunks:
            o_intra[n] = [_mm(jnp.where(tril, att[n][h], 0.0).astype(BF16), vb[chunk(n), head(h)]) for h in heads]
        n = step - 3
        if 0 <= n < n_chunks:
            o_inter = [_mm_nt(q_inter[n][:, head(h)], states[h].astype(BF16)) for h in heads]
            states = [states[h] * decay[n][:, head(h)] + _mm_tn(vb[chunk(n), head(h)], k_state[n][:, head(h)])
                      for h in heads]
            for h in heads:
                o = o_intra[n][h] + o_inter[h]
                o = o * _rms_scale(o) * onorm_ref[...]
                o_ref[0, chunk(n), head(h)] = (o * gate[chunk(n), head(h)]).astype(BF16)
    for h in heads:
        state_ref[h] = states[h]


def _hgrn(zh, lb_rows, onorm):
    b, s, _ = zh.shape
    t = _token_tile(HG_TILE, s)
    return pl.pallas_call(
        _hgrn_kernel,
        grid=(b, s // t),
        in_specs=[pl.BlockSpec((1, t, GRP_H), lambda bi, i: (bi, i, 0)),
                  pl.BlockSpec((SUBLANES, HG_HEADS * HG_DK), lambda bi, i: (0, 0)),
                  pl.BlockSpec((1, HG_DV), lambda bi, i: (0, 0))],
        out_specs=pl.BlockSpec((1, t, HG_HEADS * HG_DV), lambda bi, i: (bi, i, 0)),
        out_shape=jax.ShapeDtypeStruct((b, s, HG_HEADS * HG_DV), BF16),
        scratch_shapes=[pltpu.VMEM((HG_HEADS, HG_DV, HG_DK), F32)],
        compiler_params=_params(2),
        name="hgrn",
    )(zh, lb_rows, onorm)


def _merge_kernel(x_ref, nmix_ref, ya_ref, yb_ref, yc_ref, ym_ref, wg_ref, wl_ref, wo_ref, o_ref):
    x = x_ref[0]
    hb = (x * _rms_scale(x) * nmix_ref[...]).astype(BF16)
    merged = None
    for n, y_ref in enumerate((ya_ref, yb_ref, yc_ref, ym_ref)):
        gate = jax.nn.sigmoid(_mm(hb, wg_ref[:, n * D_MODEL:(n + 1) * D_MODEL]))
        term = gate * _mm(y_ref[0], wl_ref[n])
        merged = term if merged is None else merged + term
    o_ref[0] = x + _mm(merged.astype(BF16), wo_ref[...])


def _merge(layer, x, nmix, ya, yb, yc, ym, wg, wl, wo):
    b, s, _ = x.shape
    t = _token_tile(MERGE_TILE, s)
    tok = lambda width: pl.BlockSpec((1, t, width), lambda bi, i: (bi, i, 0))
    return pl.pallas_call(
        _merge_kernel,
        grid=(b, s // t),
        in_specs=[tok(D_MODEL), pl.BlockSpec((1, D_MODEL), lambda bi, i: (0, 0)),
                  tok(BRANCH_DIM), tok(BRANCH_DIM), tok(BRANCH_DIM), tok(BRANCH_DIM),
                  pl.BlockSpec((None, D_MODEL, N_BRANCH * D_MODEL), lambda bi, i: (layer, 0, 0),
                               pipeline_mode=pl.Buffered(1)),
                  pl.BlockSpec((None, N_BRANCH, BRANCH_DIM, D_MODEL), lambda bi, i: (layer, 0, 0, 0),
                               pipeline_mode=pl.Buffered(1)),
                  pl.BlockSpec((None, D_MODEL, D_MODEL), lambda bi, i: (layer, 0, 0),
                               pipeline_mode=pl.Buffered(1))],
        out_specs=tok(D_MODEL),
        out_shape=jax.ShapeDtypeStruct((b, s, D_MODEL), F32),
        compiler_params=_params(2),
        name="merge",
    )(x, nmix, ya, yb, yc, ym, wg, wl, wo)


def _ffn_kernel(x_ref, nffn_ref, wup_ref, wdn_ref, o_ref):
    x = x_ref[0]
    hb = (x * _rms_scale(x) * nffn_ref[...]).astype(BF16)
    out = x
    for n in range(FFN_DIM // FFN_CHUNK):
        lo = n * FFN_CHUNK
        gate = _mm(hb, wup_ref[:, lo:lo + FFN_CHUNK])
        up = _mm(hb, wup_ref[:, FFN_DIM + lo:FFN_DIM + lo + FFN_CHUNK])
        act = (gate * jax.nn.sigmoid(gate) * up).astype(BF16)
        out = out + _mm(act, wdn_ref[lo:lo + FFN_CHUNK, :])
    o_ref[0] = out


def _ffn(layer, x, nffn, wup, wdn):
    b, s, _ = x.shape
    t = _token_tile(FFN_TILE, s)
    tok = pl.BlockSpec((1, t, D_MODEL), lambda bi, i: (bi, i, 0))
    return pl.pallas_call(
        _ffn_kernel,
        grid=(b, s // t),
        in_specs=[tok, pl.BlockSpec((1, D_MODEL), lambda bi, i: (0, 0)),
                  pl.BlockSpec((None, D_MODEL, 2 * FFN_DIM), lambda bi, i: (layer, 0, 0),
                               pipeline_mode=pl.Buffered(1)),
                  pl.BlockSpec((None, FFN_DIM, D_MODEL), lambda bi, i: (layer, 0, 0),
                               pipeline_mode=pl.Buffered(1))],
        out_specs=tok,
        out_shape=jax.ShapeDtypeStruct((b, s, D_MODEL), F32),
        compiler_params=_params(2),
        name="ffn",
    )(x, nffn, wup, wdn)


def _rope_constants():
    inv_freq = 1.0 / (ROPE_THETA ** (jnp.arange(0, ROT_DIM, 2, dtype=F32) / ROT_DIM))
    invf = jnp.concatenate([inv_freq, inv_freq]).reshape(1, ROT_DIM)
    place = np.zeros((ROT_DIM, 3 * LANES), np.float32)
    ones = np.zeros((1, LANES), np.float32)
    for lane in range(LANES):
        d = lane % DSA_HEAD_DIM
        if d < ROT_HALF:
            place[d, lane] = 1.0
            place[ROT_HALF + d, LANES + lane] = -1.0
        elif d < ROT_DIM:
            place[d - ROT_HALF, lane] = 1.0
            place[d, 2 * LANES + lane] = 1.0
        else:
            ones[0, lane] = 1.0
    return invf, jnp.asarray(place, BF16), jnp.asarray(ones)


W_PREP_ROWS = 128


def _relayout_w_in_kernel(w_ref, w1_ref, wg_ref):
    offs = [0] + [int(v) for v in np.cumsum(SPLIT_SIZES)]
    w = w_ref[...]
    col = lambda n: w[:, offs[n]:offs[n + 1]]
    (a_x, a_b, a_c, d_q, d_k, d_v, i_q, i_k, i_w, g_q, g_f, g_i, g_g, m_q, gates) = (col(n) for n in range(15))
    pad = jnp.zeros((w.shape[0], LANES - DSA_HEAD_DIM - IDX_HEADS), w.dtype)
    dst = 0
    for piece in (a_x, a_b, a_c, d_q, jnp.concatenate([d_k, i_k], axis=1), jnp.concatenate([d_v, i_w, pad], axis=1),
                  i_q, g_q, g_f, g_i, g_g, m_q):
        w1_ref[:, dst:dst + piece.shape[1]] = piece.astype(BF16)
        dst += piece.shape[1]
    wg_ref[...] = gates.astype(BF16)


def _relayout_w_in(w_in):
    depth, d, cols = w_in.shape
    rows = min(W_PREP_ROWS, d)
    blk = lambda width: pl.BlockSpec((None, rows, width), lambda l, r: (l, r, 0))
    return pl.pallas_call(
        _relayout_w_in_kernel,
        grid=(depth, d // rows),
        in_specs=[blk(cols)],
        out_specs=[blk(PROJ_COLS), blk(N_BRANCH * D_MODEL)],
        out_shape=[jax.ShapeDtypeStruct((depth, d, PROJ_COLS), BF16),
                   jax.ShapeDtypeStruct((depth, d, N_BRANCH * D_MODEL), BF16)],
        compiler_params=_params(2),
        name="w_in_relayout",
    )(w_in)


def kernel(x, mem, positions, norm_mix, w_in, conv_w, dsa_q_norm, dsa_k_norm, hgrn_lower_bounds,
           hgrn_out_norm, mem_norm, mem_w_kv, mem_q_norm, mem_k_norm, w_lift, w_out, norm_ffn,
           ffn_w_up, ffn_w_down):
    b, s, d = x.shape
    depth = w_in.shape[0]
    assert d == D_MODEL and w_in.shape[2] == sum(SPLIT_SIZES)
    assert s % HG_CHUNK == 0 and s % min(DSA_K_TILE, s) == 0
    topk = min(TOPK_MAX, s // 4)

    invf, rplace, rones = _rope_constants()
    ang = positions.astype(F32)[:, None, :] * invf.reshape(ROT_DIM, 1)
    cs = jnp.where(jnp.arange(ROT_DIM)[:, None] < ROT_HALF, jnp.cos(ang), jnp.sin(ang))
    lbs = jnp.cumsum(jax.nn.softmax(hgrn_lower_bounds.astype(F32), axis=0), axis=0)
    lbs = lbs - lbs[0:1]
    g64 = jnp.asarray(np.kron(np.eye(DSA_HEADS), np.full((DSA_HEAD_DIM, DSA_HEAD_DIM), 1.0 / DSA_HEAD_DIM)), BF16)
    row = lambda v: v.reshape(1, -1).astype(F32)
    w1, wg = _relayout_w_in(w_in)
    wkv, wl, wo = mem_w_kv.astype(BF16), w_lift.astype(BF16), w_out.astype(BF16)
    wup, wdn = ffn_w_up.astype(BF16), ffn_w_down.astype(BF16)

    for l in range(depth):
        lb = lbs[l]
        lb_rows = jnp.concatenate([jnp.stack([jnp.log(lb), jnp.log1p(-lb), 1.0 - lb]),
                                   jnp.zeros((SUBLANES - 3, lb.shape[0]), F32)])
        convw = jnp.concatenate([conv_w[l], jnp.zeros((SUBLANES - CONV_WIDTH, CONV_DIM), F32)])
        mkt, mv = _mem_kv(l, mem, row(mem_norm), wkv, row(mem_k_norm[l]))
        ya, q, iq, kkt, vw, v1, zh, ym = _proj(
            l, x, row(norm_mix[l]), w1, cs, rplace, rones, convw,
            row(jnp.tile(dsa_q_norm[l], DSA_HEADS)),
            row(jnp.concatenate([dsa_k_norm[l], jnp.ones((LANES - DSA_HEAD_DIM,), F32)])),
            g64, mkt, mv, row(mem_q_norm[l]))
        logit_bound = DSA_HEAD_DIM ** 0.5 * jnp.max(jnp.abs(dsa_q_norm[l])) * jnp.max(jnp.abs(dsa_k_norm[l]))
        fast = (logit_bound <= SAFE_LOGIT_BOUND).astype(jnp.int32).reshape(1)
        yb = _dsa(fast, q, iq, vw, kkt, v1, topk)
        yc = _hgrn(zh, lb_rows, row(hgrn_out_norm[l]))
        x = _merge(l, x, row(norm_mix[l]), ya, yb, yc, ym, wg, wl, wo)
        x = _ffn(l, x, row(norm_ffn[l]), wup, wdn)
    return x
```

```python
import functools

import jax
import jax.numpy as jnp
import numpy as np
from jax import lax
from jax.experimental import pallas as pl
from jax.experimental.pallas import tpu as pltpu

D_MODEL = 1024
MEM_TOKENS = 256
N_BRANCH = 4
BRANCH_DIM = 512
CONV_DIM = 512
CONV_WIDTH = 3
DSA_HEADS = 8
DSA_HEAD_DIM = 64
IDX_HEADS = 8
IDX_DIM = 64
TOPK_MAX = 256
HG_HEADS = 4
HG_DK = 128
HG_DV = 128
HG_CHUNK = 64
MEM_HEADS = 4
MEM_HEAD_DIM = 128
ROPE_THETA = 500000.0
ROT_DIM = DSA_HEAD_DIM // 4
ROT_HALF = ROT_DIM // 2
FFN_DIM = ((8 * D_MODEL // 3 + 255) // 256) * 256
EPS = 1e-6

SPLIT_SIZES = (CONV_DIM, CONV_DIM, CONV_DIM,
               DSA_HEADS * DSA_HEAD_DIM, DSA_HEAD_DIM, DSA_HEAD_DIM,
               IDX_HEADS * IDX_DIM, IDX_DIM, IDX_HEADS,
               HG_HEADS * HG_DK, HG_HEADS * HG_DK, HG_HEADS * HG_DV, HG_HEADS * HG_DV,
               MEM_HEADS * MEM_HEAD_DIM,
               N_BRANCH * D_MODEL)

LANES = 128
LANE_BITS = 7
SUBLANES = 8
VMEM_LIMIT_BYTES = 56 * 1024 * 1024

PROJ_TILE = 512
DSA_Q_TILE = 512
DSA_K_TILE = 512
RADIX_PASSES_PER_TRIP = 8
HG_TILE = 512
MERGE_TILE = 1024
FFN_TILE = 1024
FFN_CHUNK = FFN_DIM // 2

GRP_A = 3 * CONV_DIM
GRP_Q = DSA_HEADS * DSA_HEAD_DIM
GRP_KV = 2 * LANES
GRP_IQ = IDX_HEADS * IDX_DIM
GRP_H = 4 * HG_HEADS * HG_DK
GRP_M = MEM_HEADS * MEM_HEAD_DIM
OFF_A = 0
OFF_Q = OFF_A + GRP_A
OFF_KV = OFF_Q + GRP_Q
OFF_IQ = OFF_KV + GRP_KV
OFF_H = OFF_IQ + GRP_IQ
OFF_M = OFF_H + GRP_H
PROJ_COLS = OFF_M + GRP_M

NEG_BIG = -1e30
INT_MIN = -(2 ** 31)
LOG2E = 1.4426950408889634
Q_SCALE = DSA_HEAD_DIM ** -0.5 * LOG2E
SAFE_LOGIT_BOUND = 40.0

BF16 = jnp.bfloat16
F32 = jnp.float32


def _mm(a, b):
    return jnp.dot(a, b, preferred_element_type=F32)


def _mm_nt(a, b):
    return lax.dot_general(a, b, (((1,), (1,)), ((), ())), preferred_element_type=F32)


def _mm_tn(a, b):
    return lax.dot_general(a, b, (((0,), (0,)), ((), ())), preferred_element_type=F32)


def _rms_scale(x):
    return lax.rsqrt(jnp.mean(x * x, axis=-1, keepdims=True) + EPS)


def _token_tile(preferred, s):
    t = min(preferred, s)
    assert s % t == 0, f"sequence length {s} is not a multiple of the token tile {t}"
    return t


def _params(n_grid):
    return pltpu.CompilerParams(dimension_semantics=("arbitrary",) * n_grid,
                                vmem_limit_bytes=VMEM_LIMIT_BYTES)


def _mem_kv_kernel(mem_ref, mem_norm_ref, wkv_ref, knorm_ref, mkt_ref, mv_ref):
    m = mem_ref[0]
    mn = (m * _rms_scale(m) * mem_norm_ref[...]).astype(BF16)
    kv = _mm(mn, wkv_ref[...])
    for h in range(MEM_HEADS):
        kh = kv[:, h * MEM_HEAD_DIM:(h + 1) * MEM_HEAD_DIM]
        kh = kh * _rms_scale(kh) * knorm_ref[...]
        mkt_ref[0, h] = kh.T.astype(BF16)
        off = MEM_HEADS * MEM_HEAD_DIM + h * MEM_HEAD_DIM
        mv_ref[0, h] = kv[:, off:off + MEM_HEAD_DIM].astype(BF16)


def _mem_kv(layer, mem, mem_norm, wkv, knorm):
    b = mem.shape[0]
    return pl.pallas_call(
        _mem_kv_kernel,
        grid=(b,),
        in_specs=[
            pl.BlockSpec((1, MEM_TOKENS, D_MODEL), lambda i: (i, 0, 0)),
            pl.BlockSpec((1, D_MODEL), lambda i: (0, 0)),
            pl.BlockSpec((None, D_MODEL, 2 * GRP_M), lambda i: (layer, 0, 0)),
            pl.BlockSpec((1, MEM_HEAD_DIM), lambda i: (0, 0)),
        ],
        out_specs=[
            pl.BlockSpec((1, MEM_HEADS, MEM_HEAD_DIM, MEM_TOKENS), lambda i: (i, 0, 0, 0)),
            pl.BlockSpec((1, MEM_HEADS, MEM_TOKENS, MEM_HEAD_DIM), lambda i: (i, 0, 0, 0)),
        ],
        out_shape=[
            jax.ShapeDtypeStruct((b, MEM_HEADS, MEM_HEAD_DIM, MEM_TOKENS), BF16),
            jax.ShapeDtypeStruct((b, MEM_HEADS, MEM_TOKENS, MEM_HEAD_DIM), BF16),
        ],
        compiler_params=_params(1),
        name="mem_kv",
    )(mem, mem_norm, wkv, knorm)


def _rope_slab(s, c, sa, sb):
    return s * c + pltpu.roll(s, LANES - ROT_HALF, 1) * sa + pltpu.roll(s, ROT_HALF, 1) * sb


def _proj_kernel(x_ref, nmix_ref, w_ref, cs_ref, rplace_ref, rones_ref, convw_ref, qg_ref, kg_ref,
                 g64_ref, mkt_ref, mv_ref, mqg_ref,
                 ya_ref, q_ref, iq_ref, kkt_ref, vw_ref, v1_ref, zh_ref, ym_ref,
                 carry_ref):
    t = x_ref.shape[1]

    @pl.when(pl.program_id(1) == 0)
    def _():
        carry_ref[...] = jnp.zeros_like(carry_ref)

    x = x_ref[0]
    hb = (x * _rms_scale(x) * nmix_ref[...]).astype(BF16)

    cs_hi, cs_mid, cs_lo = _split3_bf16(cs_ref[0])
    place = rplace_ref[...]
    tabs = _mm_tn(cs_hi, place) + _mm_tn(cs_mid, place) + _mm_tn(cs_lo, place)
    rc = tabs[:, :LANES] + rones_ref[...]
    rsa = tabs[:, LANES:2 * LANES]
    rsb = tabs[:, 2 * LANES:]

    za = _mm(hb, w_ref[:, OFF_A:OFF_A + GRP_A])
    a_x, a_b, a_c = za[:, :CONV_DIM], za[:, CONV_DIM:2 * CONV_DIM], za[:, 2 * CONV_DIM:]
    u = a_c * a_x
    carry = carry_ref[...]
    row8 = lax.broadcasted_iota(jnp.int32, (SUBLANES, CONV_DIM), 0)
    r1 = pltpu.roll(u, 1, 0)
    r2 = pltpu.roll(u, 2, 0)
    top1 = jnp.where(row8 < 1, pltpu.roll(carry, 1, 0), r1[:SUBLANES])
    top2 = jnp.where(row8 < 2, pltpu.roll(carry, 2, 0), r2[:SUBLANES])
    u1 = jnp.concatenate([top1, r1[SUBLANES:]], axis=0)
    u2 = jnp.concatenate([top2, r2[SUBLANES:]], axis=0)
    carry_ref[...] = u[t - SUBLANES:]
    cw = convw_ref[...]
    ya_ref[0] = (a_b * (u2 * cw[0:1] + u1 * cw[1:2] + u * cw[2:3])).astype(BF16)

    zq = _mm(hb, w_ref[:, OFF_Q:OFF_Q + GRP_Q])
    msq = _mm((zq * zq).astype(BF16), g64_ref[...])
    qn = zq * lax.rsqrt(msq + EPS) * qg_ref[...]
    for p in range(GRP_Q // LANES):
        sl = slice(p * LANES, (p + 1) * LANES)
        q_ref[0, :, sl] = (_rope_slab(qn[:, sl], rc, rsa, rsb) * Q_SCALE).astype(BF16)

    zkv = _mm(hb, w_ref[:, OFF_KV:OFF_KV + GRP_KV])
    s0, s1 = zkv[:, :LANES], zkv[:, LANES:]
    lane = lax.broadcasted_iota(jnp.int32, (t, LANES), 1)
    is_k = lane < DSA_HEAD_DIM
    kms = jnp.sum(jnp.where(is_k, s0 * s0, 0.0), axis=-1, keepdims=True) * (1.0 / DSA_HEAD_DIM)
    s0 = s0 * jnp.where(is_k, lax.rsqrt(kms + EPS) * kg_ref[...], 1.0)
    kkt_ref[0] = _rope_slab(s0, rc, rsa, rsb).T.astype(BF16)
    w_scale = jnp.where(lane >= DSA_HEAD_DIM, jnp.where(lane < DSA_HEAD_DIM + IDX_HEADS, IDX_HEADS ** -0.5, 1.0), 1.0)
    vw_ref[0] = s1 * w_scale
    v1_ref[0] = jnp.where(is_k, s1, jnp.where(lane == DSA_HEAD_DIM, 1.0, 0.0)).astype(BF16)

    ziq = _mm(hb, w_ref[:, OFF_IQ:OFF_IQ + GRP_IQ])
    for p in range(GRP_IQ // LANES):
        sl = slice(p * LANES, (p + 1) * LANES)
        iq_ref[0, :, sl] = (_rope_slab(ziq[:, sl], rc, rsa, rsb) * (IDX_DIM ** -0.5)).astype(BF16)

    zm = _mm(hb, w_ref[:, OFF_M:OFF_M + GRP_M])
    for h in range(MEM_HEADS):
        sl = slice(h * MEM_HEAD_DIM, (h + 1) * MEM_HEAD_DIM)
        mq = zm[:, sl]
        mq = mq * _rms_scale(mq) * mqg_ref[...] * (MEM_HEAD_DIM ** -0.5)
        lg = _mm(mq.astype(BF16), mkt_ref[0, h])
        pe = jnp.exp(lg - jnp.max(lg, axis=-1, keepdims=True))
        den = jnp.sum(pe, axis=-1, keepdims=True)
        ym_ref[0, :, sl] = (_mm(pe.astype(BF16), mv_ref[0, h]) / den).astype(BF16)

    zh_ref[0] = _mm(hb, w_ref[:, OFF_H:OFF_H + GRP_H])


def _proj(layer, x, nmix, w1, cs, rplace, rones, convw, qg, kg, g64, mkt, mv, mqg):
    b, s, _ = x.shape
    t = _token_tile(PROJ_TILE, s)
    tok = lambda width: pl.BlockSpec((1, t, width), lambda bi, i: (bi, i, 0))
    const2 = lambda shape: pl.BlockSpec(shape, lambda bi, i: (0, 0))
    per_b4 = lambda shape: pl.BlockSpec((1,) + shape, lambda bi, i: (bi, 0, 0, 0))
    out_widths = (CONV_DIM, GRP_Q, GRP_IQ, LANES, LANES, LANES, GRP_H, GRP_M)
    out_dtypes = (BF16, BF16, BF16, BF16, F32, BF16, F32, BF16)
    return pl.pallas_call(
        _proj_kernel,
        grid=(b, s // t),
        in_specs=[
            tok(D_MODEL), const2((1, D_MODEL)),
            pl.BlockSpec((None, D_MODEL, PROJ_COLS), lambda bi, i: (layer, 0, 0)),
            pl.BlockSpec((1, ROT_DIM, t), lambda bi, i: (bi, 0, i)),
            const2((ROT_DIM, 3 * LANES)), const2((1, LANES)),
            const2((SUBLANES, CONV_DIM)), const2((1, GRP_Q)), const2((1, LANES)),
            const2((GRP_Q, GRP_Q)),
            per_b4((MEM_HEADS, MEM_HEAD_DIM, MEM_TOKENS)), per_b4((MEM_HEADS, MEM_TOKENS, MEM_HEAD_DIM)),
            const2((1, MEM_HEAD_DIM)),
        ],
        out_specs=[tok(w) if n != 3 else pl.BlockSpec((1, LANES, t), lambda bi, i: (bi, 0, i))
                   for n, w in enumerate(out_widths)],
        out_shape=[jax.ShapeDtypeStruct((b, s, w) if n != 3 else (b, LANES, s), d)
                   for n, (w, d) in enumerate(zip(out_widths, out_dtypes))],
        scratch_shapes=[pltpu.VMEM((SUBLANES, CONV_DIM), F32)],
        compiler_params=_params(2),
        name="proj",
    )(x, nmix, w1, cs, rplace, rones, convw, qg, kg, g64, mkt, mv, mqg)


def _bit_transpose32(words):
    a = list(words)
    for j, m in ((16, 0x0000FFFF), (8, 0x00FF00FF), (4, 0x0F0F0F0F), (2, 0x33333333), (1, 0x55555555)):
        for k in range(32):
            if k & j:
                continue
            tmp = (a[k] ^ lax.shift_right_logical(a[k + j], jnp.int32(j))) & m
            a[k] = a[k] ^ tmp
            a[k + j] = a[k + j] ^ lax.shift_left(tmp, jnp.int32(j))
    return a


def _low_bits(n):
    return jnp.where(n >= 32, -1, jnp.where(n <= 0, 0, lax.shift_left(jnp.int32(1), n) - 1))


def _dsa_kernel(fast_ref, q_ref, iq_ref, vw_ref, kkt_ref, v1_ref, o_ref,
                key_ref, plane_ref, cand_ref, sel_ref, m_ref, acc_ref, kbd_ref, ikbd_ref, *, topk, seq_len):
    i = pl.program_id(1)
    t = q_ref.shape[1]
    gt, _, tk = key_ref.shape
    n_groups = plane_ref.shape[0]
    cpt = tk // LANES
    n_pairs = DSA_HEADS // 2
    q0 = i * t
    nj = (q0 + t - 1) // tk + 1
    ng = (nj + gt - 1) // gt

    @pl.when(jnp.logical_and(pl.program_id(0) == 0, i == 0))
    def _():
        plane_ref[...] = jnp.zeros_like(plane_ref)

    @pl.when(i == 0)
    def _():
        kbd_ref[...] = jnp.zeros_like(kbd_ref)
        ikbd_ref[...] = jnp.zeros_like(ikbd_ref)
        d = DSA_HEAD_DIM
        for j in range(kbd_ref.shape[0]):
            kt = kkt_ref[0, :, j * tk:(j + 1) * tk]
            kbd_ref[j, :d, :tk] = kt[:d]
            kbd_ref[j, d:, tk:] = kt[:d]
            ikbd_ref[j, :d, :tk] = kt[d:]
            ikbd_ref[j, d:, tk:] = kt[d:]

    iw = vw_ref[0][:, DSA_HEAD_DIM:DSA_HEAD_DIM + IDX_HEADS]

    def score_group(g, carry):
        nt = jnp.minimum(nj - g * gt, gt)

        def score_tile(jj):
            sc = jnp.zeros((t, tk), F32)
            for p in range(n_pairs):
                r = _mm(iq_ref[0, :, p * LANES:(p + 1) * LANES], ikbd_ref[g * gt + jj])
                sc = sc + jnp.maximum(r[:, :tk], 0.0) * iw[:, 2 * p:2 * p + 1]
                sc = sc + jnp.maximum(r[:, tk:], 0.0) * iw[:, 2 * p + 1:2 * p + 2]
            sc = jnp.where(sc == 0.0, 0.0, sc)
            bits = pltpu.bitcast(sc, jnp.int32)
            key_ref[jj] = bits ^ ((bits >> 31) | INT_MIN)

        def score_two(k, c2):
            score_tile(2 * k)
            score_tile(2 * k + 1)
            return c2

        lax.fori_loop(0, nt // 2, score_two, 0)

        @pl.when(nt % 2 == 1)
        def _():
            score_tile(nt - 1)

        def clear_tile(jj, c2):
            key_ref[jj] = jnp.zeros((t, tk), jnp.int32)
            return c2

        lax.fori_loop(nt, gt, clear_tile, 0)

        def to_planes(rg, c2):
            rows = pl.ds(pl.multiple_of(rg * SUBLANES, SUBLANES), SUBLANES)
            words = []
            for k in range(32):
                ch = 31 - k
                words.append(key_ref[ch // cpt, rows, (ch % cpt) * LANES:(ch % cpt + 1) * LANES])
            for p, w in enumerate(_bit_transpose32(words)):
                plane_ref[g, p, rows, :] = w
            return c2

        lax.fori_loop(0, t // SUBLANES, to_planes, 0)
        return carry

    lax.fori_loop(0, ng, score_group, 0)

    lane = lax.broadcasted_iota(jnp.int32, (t, LANES), 1)
    qpos = lax.broadcasted_iota(jnp.int32, (t, LANES), 0) + q0
    chunks_valid = ((qpos - lane) >> LANE_BITS) + 1

    def count_bits(words):
        tot = lax.population_count(words[0])
        for w in words[1:]:
            tot = tot + lax.population_count(w)
        return jnp.sum(tot.astype(F32), axis=-1, keepdims=True)

    for g in range(n_groups):
        cand_ref[g] = _low_bits(chunks_valid - 32 * g)
        sel_ref[g] = jnp.zeros((t, LANES), jnp.int32)

    def radix_select(groups):
        def radix(p, need):
            ones = [cand_ref[g] & plane_ref[g, p] for g in groups]
            c1 = count_bits(ones)
            take = c1 >= need
            for g, one in zip(groups, ones):
                cand = cand_ref[g]
                cand_ref[g] = jnp.where(take, one, cand ^ one)
                sel_ref[g] = jnp.where(take, sel_ref[g], sel_ref[g] | one)
            return jnp.where(take, need, need - c1)

        def radix_block(k, need):
            for u in range(RADIX_PASSES_PER_TRIP):
                need = radix(k * RADIX_PASSES_PER_TRIP + u, need)
            return need

        return lax.fori_loop(0, 32 // RADIX_PASSES_PER_TRIP, radix_block, jnp.full((t, 1), float(topk), F32))

    if n_groups == 1:
        need = radix_select(range(1))
    else:
        need = lax.switch(ng - 1, [functools.partial(radix_select, range(n)) for n in range(1, n_groups + 1)])

    has_tie = jnp.where(count_bits([cand_ref[g] for g in range(n_groups)]) > need, 1.0, 0.0)
    any_tie = jnp.max(has_tie) > 0.0

    @pl.when(jnp.logical_not(any_tie))
    def _():
        for g in range(n_groups):
            sel_ref[g] = sel_ref[g] | cand_ref[g]

    @pl.when(any_tie)
    def _():
        n_bits = max(1, int(seq_len - 1).bit_length())

        def below(g, cut):
            return cand_ref[g] & _low_bits(((cut - lane + (LANES - 1)) >> LANE_BITS) - 32 * g)

        def bisect_idx(bit, pfx):
            cut = pfx | (jnp.int32(1) << (n_bits - 1 - bit))
            cnt = count_bits([below(g, cut) for g in range(n_groups)])
            return jnp.where(cnt < need, cut, pfx)

        cut = lax.fori_loop(0, n_bits, bisect_idx, jnp.zeros((t, 1), jnp.int32))
        cut = jnp.where(has_tie > 0.0, cut + 1, seq_len)
        for g in range(n_groups):
            sel_ref[g] = sel_ref[g] | below(g, cut)

    neg_bits = int(np.float32(NEG_BIG).view(np.int32))

    def tile_bias(j):
        words = sel_ref[j // gt]
        parts = []
        for c in range(cpt):
            k = (j % gt) * cpt + c
            picked = lax.shift_left(words, 31 - k) >> 31
            parts.append(pltpu.bitcast(neg_bits & ~picked, F32))
        return jnp.concatenate(parts, axis=-1)

    acc_ref[...] = jnp.zeros(acc_ref.shape, F32)
    use_fast = fast_ref[0] > 0

    @pl.when(use_fast)
    def _():
        def attend(j):
            bias = tile_bias(j)
            for p in range(n_pairs):
                lg = _mm(q_ref[0, :, p * LANES:(p + 1) * LANES], kbd_ref[j])
                for e in range(2):
                    pe = jnp.exp2(lg[:, e * tk:(e + 1) * tk] + bias).astype(BF16)
                    acc_ref[2 * p + e] += _mm(pe, v1_ref[0, j])

        def attend_two(k, carry):
            attend(2 * k)
            attend(2 * k + 1)
            return carry

        lax.fori_loop(0, nj // 2, attend_two, 0)

        @pl.when(nj % 2 == 1)
        def _():
            attend(nj - 1)

    @pl.when(jnp.logical_not(use_fast))
    def _():
        m_ref[...] = jnp.full(m_ref.shape, NEG_BIG, F32)

        def attend(j, carry):
            bias = tile_bias(j)
            for p in range(n_pairs):
                lg = _mm(q_ref[0, :, p * LANES:(p + 1) * LANES], kbd_ref[j])
                for e in range(2):
                    h = 2 * p + e
                    s = lg[:, e * tk:(e + 1) * tk] + bias
                    m_old = m_ref[h][:, :1]
                    m_new = jnp.maximum(m_old, jnp.max(s, axis=-1, keepdims=True))
                    pe = jnp.exp2(s - m_new).astype(BF16)
                    acc_ref[h] = acc_ref[h] * jnp.exp2(m_old - m_new) + _mm(pe, v1_ref[0, j])
                    m_ref[h] = jnp.broadcast_to(m_new, (t, LANES))
            return carry

        lax.fori_loop(0, nj, attend, 0)

    first_head = lax.broadcasted_iota(jnp.int32, (t, LANES), 1) < DSA_HEAD_DIM
    for p in range(n_pairs):
        a0, a1 = acc_ref[2 * p], acc_ref[2 * p + 1]
        o0 = a0 / a0[:, DSA_HEAD_DIM:DSA_HEAD_DIM + 1]
        o1 = a1 / a1[:, DSA_HEAD_DIM:DSA_HEAD_DIM + 1]
        o_ref[0, :, p * LANES:(p + 1) * LANES] = jnp.where(
            first_head, o0, pltpu.roll(o1, DSA_HEAD_DIM, 1)).astype(BF16)


def _dsa(fast, q, iq, vw, kkt, v1, topk):
    b, s, _ = q.shape
    t = _token_tile(DSA_Q_TILE, s)
    tk = _token_tile(DSA_K_TILE, s)
    assert (32 * LANES) % tk == 0 and tk % LANES == 0
    nj = s // tk
    gt = 32 * LANES // tk
    n_groups = -(-nj // gt)
    tok = lambda width: pl.BlockSpec((1, t, width), lambda bi, i: (bi, i, 0))
    per_b = lambda shape: pl.BlockSpec((1,) + shape, lambda bi, i: (bi,) + (0,) * len(shape),
                                       pipeline_mode=pl.Buffered(1))
    return pl.pallas_call(
        functools.partial(_dsa_kernel, topk=topk, seq_len=s),
        grid=(b, s // t),
        in_specs=[pl.BlockSpec(memory_space=pltpu.SMEM),
                  tok(GRP_Q), tok(GRP_IQ), tok(LANES),
                  per_b((LANES, s)), per_b((nj, tk, LANES))],
        out_specs=tok(GRP_Q),
        out_shape=jax.ShapeDtypeStruct((b, s, GRP_Q), BF16),
        scratch_shapes=[
            pltpu.VMEM((gt, t, tk), jnp.int32),
            pltpu.VMEM((n_groups, 32, t, LANES), jnp.int32),
            pltpu.VMEM((n_groups, t, LANES), jnp.int32),
            pltpu.VMEM((n_groups, t, LANES), jnp.int32),
            pltpu.VMEM((DSA_HEADS, t, LANES), F32),
            pltpu.VMEM((DSA_HEADS, t, LANES), F32),
            pltpu.VMEM((nj, LANES, 2 * tk), BF16),
            pltpu.VMEM((nj, LANES, 2 * tk), BF16),
        ],
        compiler_params=_params(2),
        name="dsa",
    )(fast, q, iq, vw, kkt, v1.reshape(b, nj, tk, LANES))


def _split3_bf16(x):
    hi = x.astype(BF16)
    r1 = x - hi.astype(F32)
    mid = r1.astype(BF16)
    lo = (r1 - mid.astype(F32)).astype(BF16)
    return hi, mid, lo


def _hgrn_kernel(zh_ref, lb_ref, onorm_ref, o_ref, state_ref):
    i = pl.program_id(1)
    t = zh_ref.shape[1]
    c = HG_CHUNK
    width = HG_HEADS * HG_DK

    @pl.when(i == 0)
    def _():
        state_ref[...] = jnp.zeros_like(state_ref)

    rr = lax.broadcasted_iota(jnp.int32, (c, c), 0)
    cc = lax.broadcasted_iota(jnp.int32, (c, c), 1)
    tril = rr >= cc
    tril_b = jnp.where(tril, 1.0, 0.0).astype(BF16)

    n_chunks = t // c
    chunk = lambda n: slice(n * c, (n + 1) * c)
    head = lambda h: slice(h * HG_DK, (h + 1) * HG_DK)
    zq, zf, v, zg = (zh_ref[0, :, g * width:(g + 1) * width] for g in range(4))
    lb = lb_ref[...]
    log_lb, log1m_lb, one_m_lb = lb[0:1], lb[1:2], lb[2:3]

    e = jnp.exp(-jnp.abs(zf))
    b_ = log1m_lb + jnp.minimum(zf, 0.0) - jnp.log1p(e)
    log_f = jnp.maximum(log_lb, b_) + jnp.log1p(jnp.exp(-jnp.abs(log_lb - b_)))
    hk = one_m_lb * jnp.where(zf >= 0.0, e, 1.0) / (1.0 + e)
    hq = zq * jax.nn.sigmoid(zq)
    vb = v.astype(BF16)
    hi, mid, lo = _split3_bf16(log_f)
    gate = zg * jax.nn.sigmoid(zg)
    heads = range(HG_HEADS)
    states = [state_ref[h] for h in heads]
    q_intra, k_intra, q_inter, k_state, decay, att, o_intra = {}, {}, {}, {}, {}, {}, {}
    for step in range(n_chunks + 3):
        n = step
        if n < n_chunks:
            cum = _mm(tril_b, hi[chunk(n)]) + _mm(tril_b, mid[chunk(n)]) + _mm(tril_b, lo[chunk(n)])
            ref = cum[c // 2 - 1:c // 2]
            last = cum[c - 1:c]
            q_intra[n] = (hq[chunk(n)] * jnp.exp(cum - ref)).astype(BF16)
            k_intra[n] = (hk[chunk(n)] * jnp.exp(ref - cum)).astype(BF16)
            q_inter[n] = (hq[chunk(n)] * jnp.exp(cum)).astype(BF16)
            k_state[n] = (hk[chunk(n)] * jnp.exp(last - cum)).astype(BF16)
            decay[n] = jnp.exp(last)
        n = step - 1
        if 0 <= n < n_chunks:
            att[n] = [_mm_nt(q_intra[n][:, head(h)], k_intra[n][:, head(h)]) for h in heads]
        n = step - 2
        if 0 <= n < n_chunks:
            o_intra[n] = [_mm(jnp.where(tril, att[n][h], 0.0).astype(BF16), vb[chunk(n), head(h)]) for h in heads]
        n = step - 3
        if 0 <= n < n_chunks:
            o_inter = [_mm_nt(q_inter[n][:, head(h)], states[h].astype(BF16)) for h in heads]
            states = [states[h] * decay[n][:, head(h)] + _mm_tn(vb[chunk(n), head(h)], k_state[n][:, head(h)])
                      for h in heads]
            for h in heads:
                o = o_intra[n][h] + o_inter[h]
                o = o * _rms_scale(o) * onorm_ref[...]
                o_ref[0, chunk(n), head(h)] = (o * gate[chunk(n), head(h)]).astype(BF16)
    for h in heads:
        state_ref[h] = states[h]


def _hgrn(zh, lb_rows, onorm):
    b, s, _ = zh.shape
    t = _token_tile(HG_TILE, s)
    return pl.pallas_call(
        _hgrn_kernel,
        grid=(b, s // t),
        in_specs=[pl.BlockSpec((1, t, GRP_H), lambda bi, i: (bi, i, 0)),
                  pl.BlockSpec((SUBLANES, HG_HEADS * HG_DK), lambda bi, i: (0, 0)),
                  pl.BlockSpec((1, HG_DV), lambda bi, i: (0, 0))],
        out_specs=pl.BlockSpec((1, t, HG_HEADS * HG_DV), lambda bi, i: (bi, i, 0)),
        out_shape=jax.ShapeDtypeStruct((b, s, HG_HEADS * HG_DV), BF16),
        scratch_shapes=[pltpu.VMEM((HG_HEADS, HG_DV, HG_DK), F32)],
        compiler_params=_params(2),
        name="hgrn",
    )(zh, lb_rows, onorm)


def _merge_kernel(x_ref, nmix_ref, ya_ref, yb_ref, yc_ref, ym_ref, wg_ref, wl_ref, wo_ref, o_ref):
    x = x_ref[0]
    hb = (x * _rms_scale(x) * nmix_ref[...]).astype(BF16)
    merged = None
    for n, y_ref in enumerate((ya_ref, yb_ref, yc_ref, ym_ref)):
        gate = jax.nn.sigmoid(_mm(hb, wg_ref[:, n * D_MODEL:(n + 1) * D_MODEL]))
        term = gate * _mm(y_ref[0], wl_ref[n])
        merged = term if merged is None else merged + term
    o_ref[0] = x + _mm(merged.astype(BF16), wo_ref[...])


def _merge(layer, x, nmix, ya, yb, yc, ym, wg, wl, wo):
    b, s, _ = x.shape
    t = _token_tile(MERGE_TILE, s)
    tok = lambda width: pl.BlockSpec((1, t, width), lambda bi, i: (bi, i, 0))
    return pl.pallas_call(
        _merge_kernel,
        grid=(b, s // t),
        in_specs=[tok(D_MODEL), pl.BlockSpec((1, D_MODEL), lambda bi, i: (0, 0)),
                  tok(BRANCH_DIM), tok(BRANCH_DIM), tok(BRANCH_DIM), tok(BRANCH_DIM),
                  pl.BlockSpec((None, D_MODEL, N_BRANCH * D_MODEL), lambda bi, i: (layer, 0, 0),
                               pipeline_mode=pl.Buffered(1)),
                  pl.BlockSpec((None, N_BRANCH, BRANCH_DIM, D_MODEL), lambda bi, i: (layer, 0, 0, 0),
                               pipeline_mode=pl.Buffered(1)),
                  pl.BlockSpec((None, D_MODEL, D_MODEL), lambda bi, i: (layer, 0, 0),
                               pipeline_mode=pl.Buffered(1))],
        out_specs=tok(D_MODEL),
        out_shape=jax.ShapeDtypeStruct((b, s, D_MODEL), F32),
        compiler_params=_params(2),
        name="merge",
    )(x, nmix, ya, yb, yc, ym, wg, wl, wo)


def _ffn_kernel(x_ref, nffn_ref, wup_ref, wdn_ref, o_ref):
    x = x_ref[0]
    hb = (x * _rms_scale(x) * nffn_ref[...]).astype(BF16)
    out = x
    for n in range(FFN_DIM // FFN_CHUNK):
        lo = n * FFN_CHUNK
        gate = _mm(hb, wup_ref[:, lo:lo + FFN_CHUNK])
        up = _mm(hb, wup_ref[:, FFN_DIM + lo:FFN_DIM + lo + FFN_CHUNK])
        act = (gate * jax.nn.sigmoid(gate) * up).astype(BF16)
        out = out + _mm(act, wdn_ref[lo:lo + FFN_CHUNK, :])
    o_ref[0] = out


def _ffn(layer, x, nffn, wup, wdn):
    b, s, _ = x.shape
    t = _token_tile(FFN_TILE, s)
    tok = pl.BlockSpec((1, t, D_MODEL), lambda bi, i: (bi, i, 0))
    return pl.pallas_call(
        _ffn_kernel,
        grid=(b, s // t),
        in_specs=[tok, pl.BlockSpec((1, D_MODEL), lambda bi, i: (0, 0)),
                  pl.BlockSpec((None, D_MODEL, 2 * FFN_DIM), lambda bi, i: (layer, 0, 0),
                               pipeline_mode=pl.Buffered(1)),
                  pl.BlockSpec((None, FFN_DIM, D_MODEL), lambda bi, i: (layer, 0, 0),
                               pipeline_mode=pl.Buffered(1))],
        out_specs=tok,
        out_shape=jax.ShapeDtypeStruct((b, s, D_MODEL), F32),
        compiler_params=_params(2),
        name="ffn",
    )(x, nffn, wup, wdn)


def _rope_constants():
    inv_freq = 1.0 / (ROPE_THETA ** (jnp.arange(0, ROT_DIM, 2, dtype=F32) / ROT_DIM))
    invf = jnp.concatenate([inv_freq, inv_freq]).reshape(1, ROT_DIM)
    place = np.zeros((ROT_DIM, 3 * LANES), np.float32)
    ones = np.zeros((1, LANES), np.float32)
    for lane in range(LANES):
        d = lane % DSA_HEAD_DIM
        if d < ROT_HALF:
            place[d, lane] = 1.0
            place[ROT_HALF + d, LANES + lane] = -1.0
        elif d < ROT_DIM:
            place[d - ROT_HALF, lane] = 1.0
            place[d, 2 * LANES + lane] = 1.0
        else:
            ones[0, lane] = 1.0
    return invf, jnp.asarray(place, BF16), jnp.asarray(ones)


W_PREP_ROWS = 128


def _relayout_w_in_kernel(w_ref, w1_ref, wg_ref):
    offs = [0] + [int(v) for v in np.cumsum(SPLIT_SIZES)]
    w = w_ref[...]
    col = lambda n: w[:, offs[n]:offs[n + 1]]
    (a_x, a_b, a_c, d_q, d_k, d_v, i_q, i_k, i_w, g_q, g_f, g_i, g_g, m_q, gates) = (col(n) for n in range(15))
    pad = jnp.zeros((w.shape[0], LANES - DSA_HEAD_DIM - IDX_HEADS), w.dtype)
    dst = 0
    for piece in (a_x, a_b, a_c, d_q, jnp.concatenate([d_k, i_k], axis=1), jnp.concatenate([d_v, i_w, pad], axis=1),
                  i_q, g_q, g_f, g_i, g_g, m_q):
        w1_ref[:, dst:dst + piece.shape[1]] = piece.astype(BF16)
        dst += piece.shape[1]
    wg_ref[...] = gates.astype(BF16)


def _relayout_w_in(w_in):
    depth, d, cols = w_in.shape
    rows = min(W_PREP_ROWS, d)
    blk = lambda width: pl.BlockSpec((None, rows, width), lambda l, r: (l, r, 0))
    return pl.pallas_call(
        _relayout_w_in_kernel,
        grid=(depth, d // rows),
        in_specs=[blk(cols)],
        out_specs=[blk(PROJ_COLS), blk(N_BRANCH * D_MODEL)],
        out_shape=[jax.ShapeDtypeStruct((depth, d, PROJ_COLS), BF16),
                   jax.ShapeDtypeStruct((depth, d, N_BRANCH * D_MODEL), BF16)],
        compiler_params=_params(2),
        name="w_in_relayout",
    )(w_in)


def kernel(x, mem, positions, norm_mix, w_in, conv_w, dsa_q_norm, dsa_k_norm, hgrn_lower_bounds,
           hgrn_out_norm, mem_norm, mem_w_kv, mem_q_norm, mem_k_norm, w_lift, w_out, norm_ffn,
           ffn_w_up, ffn_w_down):
    b, s, d = x.shape
    depth = w_in.shape[0]
    assert d == D_MODEL and w_in.shape[2] == sum(SPLIT_SIZES)
    assert s % HG_CHUNK == 0 and s % min(DSA_K_TILE, s) == 0
    topk = min(TOPK_MAX, s // 4)

    invf, rplace, rones = _rope_constants()
    ang = positions.astype(F32)[:, None, :] * invf.reshape(ROT_DIM, 1)
    cs = jnp.where(jnp.arange(ROT_DIM)[:, None] < ROT_HALF, jnp.cos(ang), jnp.sin(ang))
    lbs = jnp.cumsum(jax.nn.softmax(hgrn_lower_bounds.astype(F32), axis=0), axis=0)
    lbs = lbs - lbs[0:1]
    g64 = jnp.asarray(np.kron(np.eye(DSA_HEADS), np.full((DSA_HEAD_DIM, DSA_HEAD_DIM), 1.0 / DSA_HEAD_DIM)), BF16)
    row = lambda v: v.reshape(1, -1).astype(F32)
    w1, wg = _relayout_w_in(w_in)
    wkv, wl, wo = mem_w_kv.astype(BF16), w_lift.astype(BF16), w_out.astype(BF16)
    wup, wdn = ffn_w_up.astype(BF16), ffn_w_down.astype(BF16)

    for l in range(depth):
        lb = lbs[l]
        lb_rows = jnp.concatenate([jnp.stack([jnp.log(lb), jnp.log1p(-lb), 1.0 - lb]),
                                   jnp.zeros((SUBLANES - 3, lb.shape[0]), F32)])
        convw = jnp.concatenate([conv_w[l], jnp.zeros((SUBLANES - CONV_WIDTH, CONV_DIM), F32)])
        mkt, mv = _mem_kv(l, mem, row(mem_norm), wkv, row(mem_k_norm[l]))
        ya, q, iq, kkt, vw, v1, zh, ym = _proj(
            l, x, row(norm_mix[l]), w1, cs, rplace, rones, convw,
            row(jnp.tile(dsa_q_norm[l], DSA_HEADS)),
            row(jnp.concatenate([dsa_k_norm[l], jnp.ones((LANES - DSA_HEAD_DIM,), F32)])),
            g64, mkt, mv, row(mem_q_norm[l]))
        logit_bound = DSA_HEAD_DIM ** 0.5 * jnp.max(jnp.abs(dsa_q_norm[l])) * jnp.max(jnp.abs(dsa_k_norm[l]))
        fast = (logit_bound <= SAFE_LOGIT_BOUND).astype(jnp.int32).reshape(1)
        yb = _dsa(fast, q, iq, vw, kkt, v1, topk)
        yc = _hgrn(zh, lb_rows, row(hgrn_out_norm[l]))
        x = _merge(l, x, row(norm_mix[l]), ya, yb, yc, ym, wg, wl, wo)
        x = _ffn(l, x, row(norm_ffn[l]), wup, wdn)
    return x
```

```python
import functools

import jax
import jax.numpy as jnp
import numpy as np
from jax import lax
from jax.experimental import pallas as pl
from jax.experimental.pallas import tpu as pltpu

D_MODEL = 1024
MEM_TOKENS = 256
N_BRANCH = 4
BRANCH_DIM = 512
CONV_DIM = 512
CONV_WIDTH = 3
DSA_HEADS = 8
DSA_HEAD_DIM = 64
IDX_HEADS = 8
IDX_DIM = 64
TOPK_MAX = 256
HG_HEADS = 4
HG_DK = 128
HG_DV = 128
HG_CHUNK = 64
MEM_HEADS = 4
MEM_HEAD_DIM = 128
ROPE_THETA = 500000.0
ROT_DIM = DSA_HEAD_DIM // 4
ROT_HALF = ROT_DIM // 2
FFN_DIM = ((8 * D_MODEL // 3 + 255) // 256) * 256
EPS = 1e-6

SPLIT_SIZES = (CONV_DIM, CONV_DIM, CONV_DIM,
               DSA_HEADS * DSA_HEAD_DIM, DSA_HEAD_DIM, DSA_HEAD_DIM,
               IDX_HEADS * IDX_DIM, IDX_DIM, IDX_HEADS,
               HG_HEADS * HG_DK, HG_HEADS * HG_DK, HG_HEADS * HG_DV, HG_HEADS * HG_DV,
               MEM_HEADS * MEM_HEAD_DIM,
               N_BRANCH * D_MODEL)

LANES = 128
LANE_BITS = 7
SUBLANES = 8
VMEM_LIMIT_BYTES = 56 * 1024 * 1024

PROJ_TILE = 512
DSA_Q_TILE = 512
DSA_K_TILE = 512
RADIX_PASSES_PER_TRIP = 8
HG_TILE = 512
MERGE_TILE = 1024
FFN_TILE = 1024
FFN_CHUNK = FFN_DIM // 2

GRP_A = 3 * CONV_DIM
GRP_Q = DSA_HEADS * DSA_HEAD_DIM
GRP_KV = 2 * LANES
GRP_IQ = IDX_HEADS * IDX_DIM
GRP_H = 4 * HG_HEADS * HG_DK
GRP_M = MEM_HEADS * MEM_HEAD_DIM
OFF_A = 0
OFF_Q = OFF_A + GRP_A
OFF_KV = OFF_Q + GRP_Q
OFF_IQ = OFF_KV + GRP_KV
OFF_H = OFF_IQ + GRP_IQ
OFF_M = OFF_H + GRP_H
PROJ_COLS = OFF_M + GRP_M

NEG_BIG = -1e30
INT_MIN = -(2 ** 31)
LOG2E = 1.4426950408889634
Q_SCALE = DSA_HEAD_DIM ** -0.5 * LOG2E
SAFE_LOGIT_BOUND = 40.0

BF16 = jnp.bfloat16
F32 = jnp.float32


def _mm(a, b):
    return jnp.dot(a, b, preferred_element_type=F32)


def _mm_nt(a, b):
    return lax.dot_general(a, b, (((1,), (1,)), ((), ())), preferred_element_type=F32)


def _mm_tn(a, b):
    return lax.dot_general(a, b, (((0,), (0,)), ((), ())), preferred_element_type=F32)


def _rms_scale(x):
    return lax.rsqrt(jnp.mean(x * x, axis=-1, keepdims=True) + EPS)


def _token_tile(preferred, s):
    t = min(preferred, s)
    assert s % t == 0, f"sequence length {s} is not a multiple of the token tile {t}"
    return t


def _params(n_grid):
    return pltpu.CompilerParams(dimension_semantics=("arbitrary",) * n_grid,
                                vmem_limit_bytes=VMEM_LIMIT_BYTES)


def _mem_kv_kernel(mem_ref, mem_norm_ref, wkv_ref, knorm_ref, mkt_ref, mv_ref):
    m = mem_ref[0]
    mn = (m * _rms_scale(m) * mem_norm_ref[...]).astype(BF16)
    kv = _mm(mn, wkv_ref[...])
    for h in range(MEM_HEADS):
        kh = kv[:, h * MEM_HEAD_DIM:(h + 1) * MEM_HEAD_DIM]
        kh = kh * _rms_scale(kh) * knorm_ref[...]
        mkt_ref[0, h] = kh.T.astype(BF16)
        off = MEM_HEADS * MEM_HEAD_DIM + h * MEM_HEAD_DIM
        mv_ref[0, h] = kv[:, off:off + MEM_HEAD_DIM].astype(BF16)


def _mem_kv(layer, mem, mem_norm, wkv, knorm):
    b = mem.shape[0]
    return pl.pallas_call(
        _mem_kv_kernel,
        grid=(b,),
        in_specs=[
            pl.BlockSpec((1, MEM_TOKENS, D_MODEL), lambda i: (i, 0, 0)),
            pl.BlockSpec((1, D_MODEL), lambda i: (0, 0)),
            pl.BlockSpec((None, D_MODEL, 2 * GRP_M), lambda i: (layer, 0, 0)),
            pl.BlockSpec((1, MEM_HEAD_DIM), lambda i: (0, 0)),
        ],
        out_specs=[
            pl.BlockSpec((1, MEM_HEADS, MEM_HEAD_DIM, MEM_TOKENS), lambda i: (i, 0, 0, 0)),
            pl.BlockSpec((1, MEM_HEADS, MEM_TOKENS, MEM_HEAD_DIM), lambda i: (i, 0, 0, 0)),
        ],
        out_shape=[
            jax.ShapeDtypeStruct((b, MEM_HEADS, MEM_HEAD_DIM, MEM_TOKENS), BF16),
            jax.ShapeDtypeStruct((b, MEM_HEADS, MEM_TOKENS, MEM_HEAD_DIM), BF16),
        ],
        compiler_params=_params(1),
        name="mem_kv",
    )(mem, mem_norm, wkv, knorm)


def _rope_slab(s, c, sa, sb):
    return s * c + pltpu.roll(s, LANES - ROT_HALF, 1) * sa + pltpu.roll(s, ROT_HALF, 1) * sb


def _proj_kernel(x_ref, nmix_ref, w_ref, cs_ref, rplace_ref, rones_ref, convw_ref, qg_ref, kg_ref,
                 g64_ref, mkt_ref, mv_ref, mqg_ref,
                 ya_ref, q_ref, iq_ref, kkt_ref, vw_ref, v1_ref, zh_ref, ym_ref,
                 carry_ref):
    t = x_ref.shape[1]

    @pl.when(pl.program_id(1) == 0)
    def _():
        carry_ref[...] = jnp.zeros_like(carry_ref)

    x = x_ref[0]
    hb = (x * _rms_scale(x) * nmix_ref[...]).astype(BF16)

    cs_hi, cs_mid, cs_lo = _split3_bf16(cs_ref[0])
    place = rplace_ref[...]
    tabs = _mm_tn(cs_hi, place) + _mm_tn(cs_mid, place) + _mm_tn(cs_lo, place)
    rc = tabs[:, :LANES] + rones_ref[...]
    rsa = tabs[:, LANES:2 * LANES]
    rsb = tabs[:, 2 * LANES:]

    za = _mm(hb, w_ref[:, OFF_A:OFF_A + GRP_A])
    a_x, a_b, a_c = za[:, :CONV_DIM], za[:, CONV_DIM:2 * CONV_DIM], za[:, 2 * CONV_DIM:]
    u = a_c * a_x
    carry = carry_ref[...]
    row8 = lax.broadcasted_iota(jnp.int32, (SUBLANES, CONV_DIM), 0)
    r1 = pltpu.roll(u, 1, 0)
    r2 = pltpu.roll(u, 2, 0)
    top1 = jnp.where(row8 < 1, pltpu.roll(carry, 1, 0), r1[:SUBLANES])
    top2 = jnp.where(row8 < 2, pltpu.roll(carry, 2, 0), r2[:SUBLANES])
    u1 = jnp.concatenate([top1, r1[SUBLANES:]], axis=0)
    u2 = jnp.concatenate([top2, r2[SUBLANES:]], axis=0)
    carry_ref[...] = u[t - SUBLANES:]
    cw = convw_ref[...]
    ya_ref[0] = (a_b * (u2 * cw[0:1] + u1 * cw[1:2] + u * cw[2:3])).astype(BF16)

    zq = _mm(hb, w_ref[:, OFF_Q:OFF_Q + GRP_Q])
    msq = _mm((zq * zq).astype(BF16), g64_ref[...])
    qn = zq * lax.rsqrt(msq + EPS) * qg_ref[...]
    for p in range(GRP_Q // LANES):
        sl = slice(p * LANES, (p + 1) * LANES)
        q_ref[0, :, sl] = (_rope_slab(qn[:, sl], rc, rsa, rsb) * Q_SCALE).astype(BF16)

    zkv = _mm(hb, w_ref[:, OFF_KV:OFF_KV + GRP_KV])
    s0, s1 = zkv[:, :LANES], zkv[:, LANES:]
    lane = lax.broadcasted_iota(jnp.int32, (t, LANES), 1)
    is_k = lane < DSA_HEAD_DIM
    kms = jnp.sum(jnp.where(is_k, s0 * s0, 0.0), axis=-1, keepdims=True) * (1.0 / DSA_HEAD_DIM)
    s0 = s0 * jnp.where(is_k, lax.rsqrt(kms + EPS) * kg_ref[...], 1.0)
    kkt_ref[0] = _rope_slab(s0, rc, rsa, rsb).T.astype(BF16)
    w_scale = jnp.where(lane >= DSA_HEAD_DIM, jnp.where(lane < DSA_HEAD_DIM + IDX_HEADS, IDX_HEADS ** -0.5, 1.0), 1.0)
    vw_ref[0] = s1 * w_scale
    v1_ref[0] = jnp.where(is_k, s1, jnp.where(lane == DSA_HEAD_DIM, 1.0, 0.0)).astype(BF16)

    ziq = _mm(hb, w_ref[:, OFF_IQ:OFF_IQ + GRP_IQ])
    for p in range(GRP_IQ // LANES):
        sl = slice(p * LANES, (p + 1) * LANES)
        iq_ref[0, :, sl] = (_rope_slab(ziq[:, sl], rc, rsa, rsb) * (IDX_DIM ** -0.5)).astype(BF16)

    zm = _mm(hb, w_ref[:, OFF_M:OFF_M + GRP_M])
    for h in range(MEM_HEADS):
        sl = slice(h * MEM_HEAD_DIM, (h + 1) * MEM_HEAD_DIM)
        mq = zm[:, sl]
        mq = mq * _rms_scale(mq) * mqg_ref[...] * (MEM_HEAD_DIM ** -0.5)
        lg = _mm(mq.astype(BF16), mkt_ref[0, h])
        pe = jnp.exp(lg - jnp.max(lg, axis=-1, keepdims=True))
        den = jnp.sum(pe, axis=-1, keepdims=True)
        ym_ref[0, :, sl] = (_mm(pe.astype(BF16), mv_ref[0, h]) / den).astype(BF16)

    zh_ref[0] = _mm(hb, w_ref[:, OFF_H:OFF_H + GRP_H])


def _proj(layer, x, nmix, w1, cs, rplace, rones, convw, qg, kg, g64, mkt, mv, mqg):
    b, s, _ = x.shape
    t = _token_tile(PROJ_TILE, s)
    tok = lambda width: pl.BlockSpec((1, t, width), lambda bi, i: (bi, i, 0))
    const2 = lambda shape: pl.BlockSpec(shape, lambda bi, i: (0, 0))
    per_b4 = lambda shape: pl.BlockSpec((1,) + shape, lambda bi, i: (bi, 0, 0, 0))
    out_widths = (CONV_DIM, GRP_Q, GRP_IQ, LANES, LANES, LANES, GRP_H, GRP_M)
    out_dtypes = (BF16, BF16, BF16, BF16, F32, BF16, F32, BF16)
    return pl.pallas_call(
        _proj_kernel,
        grid=(b, s // t),
        in_specs=[
            tok(D_MODEL), const2((1, D_MODEL)),
            pl.BlockSpec((None, D_MODEL, PROJ_COLS), lambda bi, i: (layer, 0, 0)),
            pl.BlockSpec((1, ROT_DIM, t), lambda bi, i: (bi, 0, i)),
            const2((ROT_DIM, 3 * LANES)), const2((1, LANES)),
            const2((SUBLANES, CONV_DIM)), const2((1, GRP_Q)), const2((1, LANES)),
            const2((GRP_Q, GRP_Q)),
            per_b4((MEM_HEADS, MEM_HEAD_DIM, MEM_TOKENS)), per_b4((MEM_HEADS, MEM_TOKENS, MEM_HEAD_DIM)),
            const2((1, MEM_HEAD_DIM)),
        ],
        out_specs=[tok(w) if n != 3 else pl.BlockSpec((1, LANES, t), lambda bi, i: (bi, 0, i))
                   for n, w in enumerate(out_widths)],
        out_shape=[jax.ShapeDtypeStruct((b, s, w) if n != 3 else (b, LANES, s), d)
                   for n, (w, d) in enumerate(zip(out_widths, out_dtypes))],
        scratch_shapes=[pltpu.VMEM((SUBLANES, CONV_DIM), F32)],
        compiler_params=_params(2),
        name="proj",
    )(x, nmix, w1, cs, rplace, rones, convw, qg, kg, g64, mkt, mv, mqg)


def _bit_transpose32(words):
    a = list(words)
    for j, m in ((16, 0x0000FFFF), (8, 0x00FF00FF), (4, 0x0F0F0F0F), (2, 0x33333333), (1, 0x55555555)):
        for k in range(32):
            if k & j:
                continue
            tmp = (a[k] ^ lax.shift_right_logical(a[k + j], jnp.int32(j))) & m
            a[k] = a[k] ^ tmp
            a[k + j] = a[k + j] ^ lax.shift_left(tmp, jnp.int32(j))
    return a


def _low_bits(n):
    return jnp.where(n >= 32, -1, jnp.where(n <= 0, 0, lax.shift_left(jnp.int32(1), n) - 1))


def _dsa_kernel(fast_ref, q_ref, iq_ref, vw_ref, kkt_ref, v1_ref, o_ref,
                key_ref, plane_ref, cand_ref, sel_ref, m_ref, acc_ref, kbdt_ref, ikbd_ref, v1t_ref, qt_ref,
                acct_ref, *, topk, seq_len):
    i = pl.program_id(1)
    t = q_ref.shape[1]
    gt, _, tk = key_ref.shape
    n_groups = plane_ref.shape[0]
    cpt = tk // LANES
    n_pairs = DSA_HEADS // 2
    q0 = i * t
    nj = (q0 + t - 1) // tk + 1
    ng = (nj + gt - 1) // gt

    @pl.when(jnp.logical_and(pl.program_id(0) == 0, i == 0))
    def _():
        plane_ref[...] = jnp.zeros_like(plane_ref)

    @pl.when(i == 0)
    def _():
        ikbd_ref[...] = jnp.zeros_like(ikbd_ref)
        d = DSA_HEAD_DIM
        left = lax.broadcasted_iota(jnp.int32, (tk, LANES), 1) < d
        for j in range(ikbd_ref.shape[0]):
            kt = kkt_ref[0, :, j * tk:(j + 1) * tk]
            ikbd_ref[j, :d, :tk] = kt[d:]
            ikbd_ref[j, d:, tk:] = kt[d:]
            k_left = jnp.where(left, kt.astype(F32).T, 0.0)
            kbdt_ref[j, :tk] = k_left.astype(BF16)
            kbdt_ref[j, tk:] = pltpu.roll(k_left, d, 1).astype(BF16)
            v1t_ref[j] = v1_ref[0, j].astype(F32).T.astype(BF16)

    iw = vw_ref[0][:, DSA_HEAD_DIM:DSA_HEAD_DIM + IDX_HEADS]

    def score_group(g, carry):
        nt = jnp.minimum(nj - g * gt, gt)

        def score_tile(jj):
            sc = jnp.zeros((t, tk), F32)
            for p in range(n_pairs):
                r = _mm(iq_ref[0, :, p * LANES:(p + 1) * LANES], ikbd_ref[g * gt + jj])
                sc = sc + jnp.maximum(r[:, :tk], 0.0) * iw[:, 2 * p:2 * p + 1]
                sc = sc + jnp.maximum(r[:, tk:], 0.0) * iw[:, 2 * p + 1:2 * p + 2]
            sc = jnp.where(sc == 0.0, 0.0, sc)
            bits = pltpu.bitcast(sc, jnp.int32)
            key_ref[jj] = bits ^ ((bits >> 31) | INT_MIN)

        def score_two(k, c2):
            score_tile(2 * k)
            score_tile(2 * k + 1)
            return c2

        lax.fori_loop(0, nt // 2, score_two, 0)

        @pl.when(nt % 2 == 1)
        def _():
            score_tile(nt - 1)

        def clear_tile(jj, c2):
            key_ref[jj] = jnp.zeros((t, tk), jnp.int32)
            return c2

        lax.fori_loop(nt, gt, clear_tile, 0)

        def to_planes(rg, c2):
            rows = pl.ds(pl.multiple_of(rg * SUBLANES, SUBLANES), SUBLANES)
            words = []
            for k in range(32):
                ch = 31 - k
                words.append(key_ref[ch // cpt, rows, (ch % cpt) * LANES:(ch % cpt + 1) * LANES])
            for p, w in enumerate(_bit_transpose32(words)):
                plane_ref[g, p, rows, :] = w
            return c2

        lax.fori_loop(0, t // SUBLANES, to_planes, 0)
        return carry

    lax.fori_loop(0, ng, score_group, 0)

    lane = lax.broadcasted_iota(jnp.int32, (t, LANES), 1)
    qpos = lax.broadcasted_iota(jnp.int32, (t, LANES), 0) + q0
    chunks_valid = ((qpos - lane) >> LANE_BITS) + 1

    def count_bits(words):
        tot = lax.population_count(words[0])
        for w in words[1:]:
            tot = tot + lax.population_count(w)
        return jnp.sum(tot.astype(F32), axis=-1, keepdims=True)

    for g in range(n_groups):
        cand_ref[g] = _low_bits(chunks_valid - 32 * g)
        sel_ref[g] = jnp.zeros((t, LANES), jnp.int32)

    def radix_select(groups):
        def radix(p, need):
            ones = [cand_ref[g] & plane_ref[g, p] for g in groups]
            c1 = count_bits(ones)
            take = c1 >= need
            for g, one in zip(groups, ones):
                cand = cand_ref[g]
                cand_ref[g] = jnp.where(take, one, cand ^ one)
                sel_ref[g] = jnp.where(take, sel_ref[g], sel_ref[g] | one)
            return jnp.where(take, need, need - c1)

        def radix_block(k, need):
            for u in range(RADIX_PASSES_PER_TRIP):
                need = radix(k * RADIX_PASSES_PER_TRIP + u, need)
            return need

        return lax.fori_loop(0, 32 // RADIX_PASSES_PER_TRIP, radix_block, jnp.full((t, 1), float(topk), F32))

    if n_groups == 1:
        need = radix_select(range(1))
    else:
        need = lax.switch(ng - 1, [functools.partial(radix_select, range(n)) for n in range(1, n_groups + 1)])

    has_tie = jnp.where(count_bits([cand_ref[g] for g in range(n_groups)]) > need, 1.0, 0.0)
    any_tie = jnp.max(has_tie) > 0.0

    @pl.when(jnp.logical_not(any_tie))
    def _():
        for g in range(n_groups):
            sel_ref[g] = sel_ref[g] | cand_ref[g]

    @pl.when(any_tie)
    def _():
        n_bits = max(1, int(seq_len - 1).bit_length())

        def below(g, cut):
            return cand_ref[g] & _low_bits(((cut - lane + (LANES - 1)) >> LANE_BITS) - 32 * g)

        def bisect_idx(bit, pfx):
            cut = pfx | (jnp.int32(1) << (n_bits - 1 - bit))
            cnt = count_bits([below(g, cut) for g in range(n_groups)])
            return jnp.where(cnt < need, cut, pfx)

        cut = lax.fori_loop(0, n_bits, bisect_idx, jnp.zeros((t, 1), jnp.int32))
        cut = jnp.where(has_tie > 0.0, cut + 1, seq_len)
        for g in range(n_groups):
            sel_ref[g] = sel_ref[g] | below(g, cut)

    neg_bits = int(np.float32(NEG_BIG).view(np.int32))

    def tile_bias(j):
        words = sel_ref[j // gt]
        parts = []
        for c in range(cpt):
            k = (j % gt) * cpt + c
            picked = lax.shift_left(words, 31 - k) >> 31
            parts.append(pltpu.bitcast(neg_bits & ~picked, F32))
        return jnp.concatenate(parts, axis=-1)

    acc_ref[...] = jnp.zeros(acc_ref.shape, F32)
    use_fast = fast_ref[0] > 0

    @pl.when(use_fast)
    def _():
        for p in range(n_pairs):
            qt_ref[p] = q_ref[0, :, p * LANES:(p + 1) * LANES].astype(F32).T.astype(BF16)
        acct_ref[...] = jnp.zeros(acct_ref.shape, F32)

        def attend(j):
            bias_t = tile_bias(j).T
            for p in range(n_pairs):
                lg_t = _mm(kbdt_ref[j], qt_ref[p])
                for e in range(2):
                    pe_t = jnp.exp2(lg_t[e * tk:(e + 1) * tk] + bias_t).astype(BF16)
                    acct_ref[2 * p + e] += _mm(v1t_ref[j], pe_t)

        def attend_two(k, carry):
            attend(2 * k)
            attend(2 * k + 1)
            return carry

        lax.fori_loop(0, nj // 2, attend_two, 0)

        @pl.when(nj % 2 == 1)
        def _():
            attend(nj - 1)

        for h in range(DSA_HEADS):
            acc_ref[h] = acct_ref[h].T

    @pl.when(jnp.logical_not(use_fast))
    def _():
        m_ref[...] = jnp.full(m_ref.shape, NEG_BIG, F32)

        def attend(j, carry):
            bias = tile_bias(j)
            for p in range(n_pairs):
                lg = _mm_nt(q_ref[0, :, p * LANES:(p + 1) * LANES], kbdt_ref[j])
                for e in range(2):
                    h = 2 * p + e
                    s = lg[:, e * tk:(e + 1) * tk] + bias
                    m_old = m_ref[h][:, :1]
                    m_new = jnp.maximum(m_old, jnp.max(s, axis=-1, keepdims=True))
                    pe = jnp.exp2(s - m_new).astype(BF16)
                    acc_ref[h] = acc_ref[h] * jnp.exp2(m_old - m_new) + _mm(pe, v1_ref[0, j])
                    m_ref[h] = jnp.broadcast_to(m_new, (t, LANES))
            return carry

        lax.fori_loop(0, nj, attend, 0)

    first_head = lax.broadcasted_iota(jnp.int32, (t, LANES), 1) < DSA_HEAD_DIM
    for p in range(n_pairs):
        a0, a1 = acc_ref[2 * p], acc_ref[2 * p + 1]
        o0 = a0 / a0[:, DSA_HEAD_DIM:DSA_HEAD_DIM + 1]
        o1 = a1 / a1[:, DSA_HEAD_DIM:DSA_HEAD_DIM + 1]
        o_ref[0, :, p * LANES:(p + 1) * LANES] = jnp.where(
            first_head, o0, pltpu.roll(o1, DSA_HEAD_DIM, 1)).astype(BF16)


def _dsa(fast, q, iq, vw, kkt, v1, topk):
    b, s, _ = q.shape
    t = _token_tile(DSA_Q_TILE, s)
    tk = _token_tile(DSA_K_TILE, s)
    assert (32 * LANES) % tk == 0 and tk % LANES == 0
    nj = s // tk
    gt = 32 * LANES // tk
    n_groups = -(-nj // gt)
    tok = lambda width: pl.BlockSpec((1, t, width), lambda bi, i: (bi, i, 0))
    per_b = lambda shape: pl.BlockSpec((1,) + shape, lambda bi, i: (bi,) + (0,) * len(shape),
                                       pipeline_mode=pl.Buffered(1))
    return pl.pallas_call(
        functools.partial(_dsa_kernel, topk=topk, seq_len=s),
        grid=(b, s // t),
        in_specs=[pl.BlockSpec(memory_space=pltpu.SMEM),
                  tok(GRP_Q), tok(GRP_IQ), tok(LANES),
                  per_b((LANES, s)), per_b((nj, tk, LANES))],
        out_specs=tok(GRP_Q),
        out_shape=jax.ShapeDtypeStruct((b, s, GRP_Q), BF16),
        scratch_shapes=[
            pltpu.VMEM((gt, t, tk), jnp.int32),
            pltpu.VMEM((n_groups, 32, t, LANES), jnp.int32),
            pltpu.VMEM((n_groups, t, LANES), jnp.int32),
            pltpu.VMEM((n_groups, t, LANES), jnp.int32),
            pltpu.VMEM((DSA_HEADS, t, LANES), F32),
            pltpu.VMEM((DSA_HEADS, t, LANES), F32),
            pltpu.VMEM((nj, 2 * tk, LANES), BF16),
            pltpu.VMEM((nj, LANES, 2 * tk), BF16),
            pltpu.VMEM((nj, LANES, tk), BF16),
            pltpu.VMEM((DSA_HEADS // 2, LANES, t), BF16),
            pltpu.VMEM((DSA_HEADS, LANES, t), F32),
        ],
        compiler_params=_params(2),
        name="dsa",
    )(fast, q, iq, vw, kkt, v1.reshape(b, nj, tk, LANES))


def _split3_bf16(x):
    hi = x.astype(BF16)
    r1 = x - hi.astype(F32)
    mid = r1.astype(BF16)
    lo = (r1 - mid.astype(F32)).astype(BF16)
    return hi, mid, lo


def _hgrn_kernel(zh_ref, lb_ref, onorm_ref, o_ref, state_ref):
    i = pl.program_id(1)
    t = zh_ref.shape[1]
    c = HG_CHUNK
    width = HG_HEADS * HG_DK

    @pl.when(i == 0)
    def _():
        state_ref[...] = jnp.zeros_like(state_ref)

    rr = lax.broadcasted_iota(jnp.int32, (c, c), 0)
    cc = lax.broadcasted_iota(jnp.int32, (c, c), 1)
    tril = rr >= cc
    tril_b = jnp.where(tril, 1.0, 0.0).astype(BF16)

    n_chunks = t // c
    chunk = lambda n: slice(n * c, (n + 1) * c)
    head = lambda h: slice(h * HG_DK, (h + 1) * HG_DK)
    zq, zf, v, zg = (zh_ref[0, :, g * width:(g + 1) * width] for g in range(4))
    lb = lb_ref[...]
    log_lb, log1m_lb, one_m_lb = lb[0:1], lb[1:2], lb[2:3]

    e = jnp.exp(-jnp.abs(zf))
    b_ = log1m_lb + jnp.minimum(zf, 0.0) - jnp.log1p(e)
    log_f = jnp.maximum(log_lb, b_) + jnp.log1p(jnp.exp(-jnp.abs(log_lb - b_)))
    hk = one_m_lb * jnp.where(zf >= 0.0, e, 1.0) / (1.0 + e)
    hq = zq * jax.nn.sigmoid(zq)
    vb = v.astype(BF16)
    hi, mid, lo = _split3_bf16(log_f)
    gate = zg * jax.nn.sigmoid(zg)
    heads = range(HG_HEADS)
    states = [state_ref[h] for h in heads]
    q_intra, k_intra, q_inter, k_state, decay, att, o_intra = {}, {}, {}, {}, {}, {}, {}
    for step in range(n_chunks + 3):
        n = step
        if n < n_chunks:
            cum = _mm(tril_b, hi[chunk(n)]) + _mm(tril_b, mid[chunk(n)]) + _mm(tril_b, lo[chunk(n)])
            ref = cum[c // 2 - 1:c // 2]
            last = cum[c - 1:c]
            q_intra[n] = (hq[chunk(n)] * jnp.exp(cum - ref)).astype(BF16)
            k_intra[n] = (hk[chunk(n)] * jnp.exp(ref - cum)).astype(BF16)
            q_inter[n] = (hq[chunk(n)] * jnp.exp(cum)).astype(BF16)
            k_state[n] = (hk[chunk(n)] * jnp.exp(last - cum)).astype(BF16)
            decay[n] = jnp.exp(last)
        n = step - 1
        if 0 <= n < n_chunks:
            att[n] = [_mm_nt(q_intra[n][:, head(h)], k_intra[n][:, head(h)]) for h in heads]
        n = step - 2
        if 0 <= n < n_chunks:
            o_intra[n] = [_mm(jnp.where(tril, att[n][h], 0.0).astype(BF16), vb[chunk(n), head(h)]) for h in heads]
        n = step - 3
        if 0 <= n < n_chunks:
            o_inter = [_mm_nt(q_inter[n][:, head(h)], states[h].astype(BF16)) for h in heads]
            states = [states[h] * decay[n][:, head(h)] + _mm_tn(vb[chunk(n), head(h)], k_state[n][:, head(h)])
                      for h in heads]
            for h in heads:
                o = o_intra[n][h] + o_inter[h]
                o = o * _rms_scale(o) * onorm_ref[...]
                o_ref[0, chunk(n), head(h)] = (o * gate[chunk(n), head(h)]).astype(BF16)
    for h in heads:
        state_ref[h] = states[h]


def _hgrn(zh, lb_rows, onorm):
    b, s, _ = zh.shape
    t = _token_tile(HG_TILE, s)
    return pl.pallas_call(
        _hgrn_kernel,
        grid=(b, s // t),
        in_specs=[pl.BlockSpec((1, t, GRP_H), lambda bi, i: (bi, i, 0)),
                  pl.BlockSpec((SUBLANES, HG_HEADS * HG_DK), lambda bi, i: (0, 0)),
                  pl.BlockSpec((1, HG_DV), lambda bi, i: (0, 0))],
        out_specs=pl.BlockSpec((1, t, HG_HEADS * HG_DV), lambda bi, i: (bi, i, 0)),
        out_shape=jax.ShapeDtypeStruct((b, s, HG_HEADS * HG_DV), BF16),
        scratch_shapes=[pltpu.VMEM((HG_HEADS, HG_DV, HG_DK), F32)],
        compiler_params=_params(2),
        name="hgrn",
    )(zh, lb_rows, onorm)


def _merge_kernel(x_ref, nmix_ref, ya_ref, yb_ref, yc_ref, ym_ref, wg_ref, wl_ref, wo_ref, o_ref):
    x = x_ref[0]
    hb = (x * _rms_scale(x) * nmix_ref[...]).astype(BF16)
    merged = None
    for n, y_ref in enumerate((ya_ref, yb_ref, yc_ref, ym_ref)):
        gate = jax.nn.sigmoid(_mm(hb, wg_ref[:, n * D_MODEL:(n + 1) * D_MODEL]))
        term = gate * _mm(y_ref[0], wl_ref[n])
        merged = term if merged is None else merged + term
    o_ref[0] = x + _mm(merged.astype(BF16), wo_ref[...])


def _merge(layer, x, nmix, ya, yb, yc, ym, wg, wl, wo):
    b, s, _ = x.shape
    t = _token_tile(MERGE_TILE, s)
    tok = lambda width: pl.BlockSpec((1, t, width), lambda bi, i: (bi, i, 0))
    return pl.pallas_call(
        _merge_kernel,
        grid=(b, s // t),
        in_specs=[tok(D_MODEL), pl.BlockSpec((1, D_MODEL), lambda bi, i: (0, 0)),
                  tok(BRANCH_DIM), tok(BRANCH_DIM), tok(BRANCH_DIM), tok(BRANCH_DIM),
                  pl.BlockSpec((None, D_MODEL, N_BRANCH * D_MODEL), lambda bi, i: (layer, 0, 0),
                               pipeline_mode=pl.Buffered(1)),
                  pl.BlockSpec((None, N_BRANCH, BRANCH_DIM, D_MODEL), lambda bi, i: (layer, 0, 0, 0),
                               pipeline_mode=pl.Buffered(1)),
                  pl.BlockSpec((None, D_MODEL, D_MODEL), lambda bi, i: (layer, 0, 0),
                               pipeline_mode=pl.Buffered(1))],
        out_specs=tok(D_MODEL),
        out_shape=jax.ShapeDtypeStruct((b, s, D_MODEL), F32),
        compiler_params=_params(2),
        name="merge",
    )(x, nmix, ya, yb, yc, ym, wg, wl, wo)


def _ffn_kernel(x_ref, nffn_ref, wup_ref, wdn_ref, o_ref):
    x = x_ref[0]
    hb = (x * _rms_scale(x) * nffn_ref[...]).astype(BF16)
    out = x
    for n in range(FFN_DIM // FFN_CHUNK):
        lo = n * FFN_CHUNK
        gate = _mm(hb, wup_ref[:, lo:lo + FFN_CHUNK])
        up = _mm(hb, wup_ref[:, FFN_DIM + lo:FFN_DIM + lo + FFN_CHUNK])
        act = (gate * jax.nn.sigmoid(gate) * up).astype(BF16)
        out = out + _mm(act, wdn_ref[lo:lo + FFN_CHUNK, :])
    o_ref[0] = out


def _ffn(layer, x, nffn, wup, wdn):
    b, s, _ = x.shape
    t = _token_tile(FFN_TILE, s)
    tok = pl.BlockSpec((1, t, D_MODEL), lambda bi, i: (bi, i, 0))
    return pl.pallas_call(
        _ffn_kernel,
        grid=(b, s // t),
        in_specs=[tok, pl.BlockSpec((1, D_MODEL), lambda bi, i: (0, 0)),
                  pl.BlockSpec((None, D_MODEL, 2 * FFN_DIM), lambda bi, i: (layer, 0, 0),
                               pipeline_mode=pl.Buffered(1)),
                  pl.BlockSpec((None, FFN_DIM, D_MODEL), lambda bi, i: (layer, 0, 0),
                               pipeline_mode=pl.Buffered(1))],
        out_specs=tok,
        out_shape=jax.ShapeDtypeStruct((b, s, D_MODEL), F32),
        compiler_params=_params(2),
        name="ffn",
    )(x, nffn, wup, wdn)


def _rope_constants():
    inv_freq = 1.0 / (ROPE_THETA ** (jnp.arange(0, ROT_DIM, 2, dtype=F32) / ROT_DIM))
    invf = jnp.concatenate([inv_freq, inv_freq]).reshape(1, ROT_DIM)
    place = np.zeros((ROT_DIM, 3 * LANES), np.float32)
    ones = np.zeros((1, LANES), np.float32)
    for lane in range(LANES):
        d = lane % DSA_HEAD_DIM
        if d < ROT_HALF:
            place[d, lane] = 1.0
            place[ROT_HALF + d, LANES + lane] = -1.0
        elif d < ROT_DIM:
            place[d - ROT_HALF, lane] = 1.0
            place[d, 2 * LANES + lane] = 1.0
        else:
            ones[0, lane] = 1.0
    return invf, jnp.asarray(place, BF16), jnp.asarray(ones)


W_PREP_ROWS = 128


def _relayout_w_in_kernel(w_ref, w1_ref, wg_ref):
    offs = [0] + [int(v) for v in np.cumsum(SPLIT_SIZES)]
    w = w_ref[...]
    col = lambda n: w[:, offs[n]:offs[n + 1]]
    (a_x, a_b, a_c, d_q, d_k, d_v, i_q, i_k, i_w, g_q, g_f, g_i, g_g, m_q, gates) = (col(n) for n in range(15))
    pad = jnp.zeros((w.shape[0], LANES - DSA_HEAD_DIM - IDX_HEADS), w.dtype)
    dst = 0
    for piece in (a_x, a_b, a_c, d_q, jnp.concatenate([d_k, i_k], axis=1), jnp.concatenate([d_v, i_w, pad], axis=1),
                  i_q, g_q, g_f, g_i, g_g, m_q):
        w1_ref[:, dst:dst + piece.shape[1]] = piece.astype(BF16)
        dst += piece.shape[1]
    wg_ref[...] = gates.astype(BF16)


def _relayout_w_in(w_in):
    depth, d, cols = w_in.shape
    rows = min(W_PREP_ROWS, d)
    blk = lambda width: pl.BlockSpec((None, rows, width), lambda l, r: (l, r, 0))
    return pl.pallas_call(
        _relayout_w_in_kernel,
        grid=(depth, d // rows),
        in_specs=[blk(cols)],
        out_specs=[blk(PROJ_COLS), blk(N_BRANCH * D_MODEL)],
        out_shape=[jax.ShapeDtypeStruct((depth, d, PROJ_COLS), BF16),
                   jax.ShapeDtypeStruct((depth, d, N_BRANCH * D_MODEL), BF16)],
        compiler_params=_params(2),
        name="w_in_relayout",
    )(w_in)


def kernel(x, mem, positions, norm_mix, w_in, conv_w, dsa_q_norm, dsa_k_norm, hgrn_lower_bounds,
           hgrn_out_norm, mem_norm, mem_w_kv, mem_q_norm, mem_k_norm, w_lift, w_out, norm_ffn,
           ffn_w_up, ffn_w_down):
    b, s, d = x.shape
    depth = w_in.shape[0]
    assert d == D_MODEL and w_in.shape[2] == sum(SPLIT_SIZES)
    assert s % HG_CHUNK == 0 and s % min(DSA_K_TILE, s) == 0
    topk = min(TOPK_MAX, s // 4)

    invf, rplace, rones = _rope_constants()
    ang = positions.astype(F32)[:, None, :] * invf.reshape(ROT_DIM, 1)
    cs = jnp.where(jnp.arange(ROT_DIM)[:, None] < ROT_HALF, jnp.cos(ang), jnp.sin(ang))
    lbs = jnp.cumsum(jax.nn.softmax(hgrn_lower_bounds.astype(F32), axis=0), axis=0)
    lbs = lbs - lbs[0:1]
    g64 = jnp.asarray(np.kron(np.eye(DSA_HEADS), np.full((DSA_HEAD_DIM, DSA_HEAD_DIM), 1.0 / DSA_HEAD_DIM)), BF16)
    row = lambda v: v.reshape(1, -1).astype(F32)
    w1, wg = _relayout_w_in(w_in)
    wkv, wl, wo = mem_w_kv.astype(BF16), w_lift.astype(BF16), w_out.astype(BF16)
    wup, wdn = ffn_w_up.astype(BF16), ffn_w_down.astype(BF16)

    for l in range(depth):
        lb = lbs[l]
        lb_rows = jnp.concatenate([jnp.stack([jnp.log(lb), jnp.log1p(-lb), 1.0 - lb]),
                                   jnp.zeros((SUBLANES - 3, lb.shape[0]), F32)])
        convw = jnp.concatenate([conv_w[l], jnp.zeros((SUBLANES - CONV_WIDTH, CONV_DIM), F32)])
        mkt, mv = _mem_kv(l, mem, row(mem_norm), wkv, row(mem_k_norm[l]))
        ya, q, iq, kkt, vw, v1, zh, ym = _proj(
            l, x, row(norm_mix[l]), w1, cs, rplace, rones, convw,
            row(jnp.tile(dsa_q_norm[l], DSA_HEADS)),
            row(jnp.concatenate([dsa_k_norm[l], jnp.ones((LANES - DSA_HEAD_DIM,), F32)])),
            g64, mkt, mv, row(mem_q_norm[l]))
        logit_bound = DSA_HEAD_DIM ** 0.5 * jnp.max(jnp.abs(dsa_q_norm[l])) * jnp.max(jnp.abs(dsa_k_norm[l]))
        fast = (logit_bound <= SAFE_LOGIT_BOUND).astype(jnp.int32).reshape(1)
        yb = _dsa(fast, q, iq, vw, kkt, v1, topk)
        yc = _hgrn(zh, lb_rows, row(hgrn_out_norm[l]))
        x = _merge(l, x, row(norm_mix[l]), ya, yb, yc, ym, wg, wl, wo)
        x = _ffn(l, x, row(norm_ffn[l]), wup, wdn)
    return x
```

```python
import functools

import jax
import jax.numpy as jnp
import numpy as np
from jax import lax
from jax.experimental import pallas as pl
from jax.experimental.pallas import tpu as pltpu

D_MODEL = 1024
MEM_TOKENS = 256
N_BRANCH = 4
BRANCH_DIM = 512
CONV_DIM = 512
CONV_WIDTH = 3
DSA_HEADS = 8
DSA_HEAD_DIM = 64
IDX_HEADS = 8
IDX_DIM = 64
TOPK_MAX = 256
HG_HEADS = 4
HG_DK = 128
HG_DV = 128
HG_CHUNK = 64
MEM_HEADS = 4
MEM_HEAD_DIM = 128
ROPE_THETA = 500000.0
ROT_DIM = DSA_HEAD_DIM // 4
ROT_HALF = ROT_DIM // 2
FFN_DIM = ((8 * D_MODEL // 3 + 255) // 256) * 256
EPS = 1e-6

SPLIT_SIZES = (CONV_DIM, CONV_DIM, CONV_DIM,
               DSA_HEADS * DSA_HEAD_DIM, DSA_HEAD_DIM, DSA_HEAD_DIM,
               IDX_HEADS * IDX_DIM, IDX_DIM, IDX_HEADS,
               HG_HEADS * HG_DK, HG_HEADS * HG_DK, HG_HEADS * HG_DV, HG_HEADS * HG_DV,
               MEM_HEADS * MEM_HEAD_DIM,
               N_BRANCH * D_MODEL)

LANES = 128
LANE_BITS = 7
SUBLANES = 8
VMEM_LIMIT_BYTES = 56 * 1024 * 1024

PROJ_TILE = 512
DSA_Q_TILE = 512
DSA_K_TILE = 512
RADIX_PASSES_PER_TRIP = 8
HG_TILE = 512
MERGE_TILE = 1024
FFN_TILE = 1024
FFN_CHUNK = FFN_DIM // 2

GRP_A = 3 * CONV_DIM
GRP_Q = DSA_HEADS * DSA_HEAD_DIM
GRP_KV = 2 * LANES
GRP_IQ = IDX_HEADS * IDX_DIM
GRP_H = 4 * HG_HEADS * HG_DK
GRP_M = MEM_HEADS * MEM_HEAD_DIM
OFF_A = 0
OFF_Q = OFF_A + GRP_A
OFF_KV = OFF_Q + GRP_Q
OFF_IQ = OFF_KV + GRP_KV
OFF_H = OFF_IQ + GRP_IQ
OFF_M = OFF_H + GRP_H
PROJ_COLS = OFF_M + GRP_M

NEG_BIG = -1e30
INT_MIN = -(2 ** 31)
LOG2E = 1.4426950408889634
Q_SCALE = DSA_HEAD_DIM ** -0.5 * LOG2E
SAFE_LOGIT_BOUND = 40.0

BF16 = jnp.bfloat16
F32 = jnp.float32


def _mm(a, b):
    return jnp.dot(a, b, preferred_element_type=F32)


def _mm_nt(a, b):
    return lax.dot_general(a, b, (((1,), (1,)), ((), ())), preferred_element_type=F32)


def _mm_tn(a, b):
    return lax.dot_general(a, b, (((0,), (0,)), ((), ())), preferred_element_type=F32)


def _rms_scale(x):
    return lax.rsqrt(jnp.mean(x * x, axis=-1, keepdims=True) + EPS)


def _token_tile(preferred, s):
    t = min(preferred, s)
    assert s % t == 0, f"sequence length {s} is not a multiple of the token tile {t}"
    return t


def _params(n_grid):
    return pltpu.CompilerParams(dimension_semantics=("arbitrary",) * n_grid,
                                vmem_limit_bytes=VMEM_LIMIT_BYTES)


def _mem_kv_kernel(mem_ref, mem_norm_ref, wkv_ref, knorm_ref, mkt_ref, mv_ref):
    m = mem_ref[0]
    mn = (m * _rms_scale(m) * mem_norm_ref[...]).astype(BF16)
    kv = _mm(mn, wkv_ref[...])
    for h in range(MEM_HEADS):
        kh = kv[:, h * MEM_HEAD_DIM:(h + 1) * MEM_HEAD_DIM]
        kh = kh * _rms_scale(kh) * knorm_ref[...]
        mkt_ref[0, h] = kh.T.astype(BF16)
        off = MEM_HEADS * MEM_HEAD_DIM + h * MEM_HEAD_DIM
        mv_ref[0, h] = kv[:, off:off + MEM_HEAD_DIM].astype(BF16)


def _mem_kv(layer, mem, mem_norm, wkv, knorm):
    b = mem.shape[0]
    return pl.pallas_call(
        _mem_kv_kernel,
        grid=(b,),
        in_specs=[
            pl.BlockSpec((1, MEM_TOKENS, D_MODEL), lambda i: (i, 0, 0)),
            pl.BlockSpec((1, D_MODEL), lambda i: (0, 0)),
            pl.BlockSpec((None, D_MODEL, 2 * GRP_M), lambda i: (layer, 0, 0)),
            pl.BlockSpec((1, MEM_HEAD_DIM), lambda i: (0, 0)),
        ],
        out_specs=[
            pl.BlockSpec((1, MEM_HEADS, MEM_HEAD_DIM, MEM_TOKENS), lambda i: (i, 0, 0, 0)),
            pl.BlockSpec((1, MEM_HEADS, MEM_TOKENS, MEM_HEAD_DIM), lambda i: (i, 0, 0, 0)),
        ],
        out_shape=[
            jax.ShapeDtypeStruct((b, MEM_HEADS, MEM_HEAD_DIM, MEM_TOKENS), BF16),
            jax.ShapeDtypeStruct((b, MEM_HEADS, MEM_TOKENS, MEM_HEAD_DIM), BF16),
        ],
        compiler_params=_params(1),
        name="mem_kv",
    )(mem, mem_norm, wkv, knorm)


def _rope_slab(s, c, sa, sb):
    return s * c + pltpu.roll(s, LANES - ROT_HALF, 1) * sa + pltpu.roll(s, ROT_HALF, 1) * sb


def _proj_kernel(x_ref, nmix_ref, w_ref, cs_ref, rplace_ref, rones_ref, convw_ref, qg_ref, kg_ref,
                 g64_ref, mkt_ref, mv_ref, mqg_ref,
                 ya_ref, q_ref, iq_ref, kkt_ref, vw_ref, v1_ref, zh_ref, ym_ref,
                 carry_ref):
    t = x_ref.shape[1]

    @pl.when(pl.program_id(1) == 0)
    def _():
        carry_ref[...] = jnp.zeros_like(carry_ref)

    x = x_ref[0]
    hb = (x * _rms_scale(x) * nmix_ref[...]).astype(BF16)

    cs_hi, cs_mid, cs_lo = _split3_bf16(cs_ref[0])
    place = rplace_ref[...]
    tabs = _mm_tn(cs_hi, place) + _mm_tn(cs_mid, place) + _mm_tn(cs_lo, place)
    rc = tabs[:, :LANES] + rones_ref[...]
    rsa = tabs[:, LANES:2 * LANES]
    rsb = tabs[:, 2 * LANES:]

    za = _mm(hb, w_ref[:, OFF_A:OFF_A + GRP_A])
    a_x, a_b, a_c = za[:, :CONV_DIM], za[:, CONV_DIM:2 * CONV_DIM], za[:, 2 * CONV_DIM:]
    u = a_c * a_x
    carry = carry_ref[...]
    row8 = lax.broadcasted_iota(jnp.int32, (SUBLANES, CONV_DIM), 0)
    r1 = pltpu.roll(u, 1, 0)
    r2 = pltpu.roll(u, 2, 0)
    top1 = jnp.where(row8 < 1, pltpu.roll(carry, 1, 0), r1[:SUBLANES])
    top2 = jnp.where(row8 < 2, pltpu.roll(carry, 2, 0), r2[:SUBLANES])
    u1 = jnp.concatenate([top1, r1[SUBLANES:]], axis=0)
    u2 = jnp.concatenate([top2, r2[SUBLANES:]], axis=0)
    carry_ref[...] = u[t - SUBLANES:]
    cw = convw_ref[...]
    ya_ref[0] = (a_b * (u2 * cw[0:1] + u1 * cw[1:2] + u * cw[2:3])).astype(BF16)

    zq = _mm(hb, w_ref[:, OFF_Q:OFF_Q + GRP_Q])
    msq = _mm((zq * zq).astype(BF16), g64_ref[...])
    qn = zq * lax.rsqrt(msq + EPS) * qg_ref[...]
    for p in range(GRP_Q // LANES):
        sl = slice(p * LANES, (p + 1) * LANES)
        q_ref[0, :, sl] = (_rope_slab(qn[:, sl], rc, rsa, rsb) * Q_SCALE).astype(BF16)

    zkv = _mm(hb, w_ref[:, OFF_KV:OFF_KV + GRP_KV])
    s0, s1 = zkv[:, :LANES], zkv[:, LANES:]
    lane = lax.broadcasted_iota(jnp.int32, (t, LANES), 1)
    is_k = lane < DSA_HEAD_DIM
    kms = jnp.sum(jnp.where(is_k, s0 * s0, 0.0), axis=-1, keepdims=True) * (1.0 / DSA_HEAD_DIM)
    s0 = s0 * jnp.where(is_k, lax.rsqrt(kms + EPS) * kg_ref[...], 1.0)
    kkt_ref[0] = _rope_slab(s0, rc, rsa, rsb).T.astype(BF16)
    w_scale = jnp.where(lane >= DSA_HEAD_DIM, jnp.where(lane < DSA_HEAD_DIM + IDX_HEADS, IDX_HEADS ** -0.5, 1.0), 1.0)
    vw_ref[0] = s1 * w_scale
    v1_ref[0] = jnp.where(is_k, s1, jnp.where(lane == DSA_HEAD_DIM, 1.0, 0.0)).astype(BF16)

    ziq = _mm(hb, w_ref[:, OFF_IQ:OFF_IQ + GRP_IQ])
    for p in range(GRP_IQ // LANES):
        sl = slice(p * LANES, (p + 1) * LANES)
        iq_ref[0, :, sl] = (_rope_slab(ziq[:, sl], rc, rsa, rsb) * (IDX_DIM ** -0.5)).astype(BF16)

    zm = _mm(hb, w_ref[:, OFF_M:OFF_M + GRP_M])
    for h in range(MEM_HEADS):
        sl = slice(h * MEM_HEAD_DIM, (h + 1) * MEM_HEAD_DIM)
        mq = zm[:, sl]
        mq = mq * _rms_scale(mq) * mqg_ref[...] * (MEM_HEAD_DIM ** -0.5)
        lg = _mm(mq.astype(BF16), mkt_ref[0, h])
        pe = jnp.exp(lg - jnp.max(lg, axis=-1, keepdims=True))
        den = jnp.sum(pe, axis=-1, keepdims=True)
        ym_ref[0, :, sl] = (_mm(pe.astype(BF16), mv_ref[0, h]) / den).astype(BF16)

    zh_ref[0] = _mm(hb, w_ref[:, OFF_H:OFF_H + GRP_H])


def _proj(layer, x, nmix, w1, cs, rplace, rones, convw, qg, kg, g64, mkt, mv, mqg):
    b, s, _ = x.shape
    t = _token_tile(PROJ_TILE, s)
    tok = lambda width: pl.BlockSpec((1, t, width), lambda bi, i: (bi, i, 0))
    const2 = lambda shape: pl.BlockSpec(shape, lambda bi, i: (0, 0))
    per_b4 = lambda shape: pl.BlockSpec((1,) + shape, lambda bi, i: (bi, 0, 0, 0))
    out_widths = (CONV_DIM, GRP_Q, GRP_IQ, LANES, LANES, LANES, GRP_H, GRP_M)
    out_dtypes = (BF16, BF16, BF16, BF16, F32, BF16, F32, BF16)
    return pl.pallas_call(
        _proj_kernel,
        grid=(b, s // t),
        in_specs=[
            tok(D_MODEL), const2((1, D_MODEL)),
            pl.BlockSpec((None, D_MODEL, PROJ_COLS), lambda bi, i: (layer, 0, 0)),
            pl.BlockSpec((1, ROT_DIM, t), lambda bi, i: (bi, 0, i)),
            const2((ROT_DIM, 3 * LANES)), const2((1, LANES)),
            const2((SUBLANES, CONV_DIM)), const2((1, GRP_Q)), const2((1, LANES)),
            const2((GRP_Q, GRP_Q)),
            per_b4((MEM_HEADS, MEM_HEAD_DIM, MEM_TOKENS)), per_b4((MEM_HEADS, MEM_TOKENS, MEM_HEAD_DIM)),
            const2((1, MEM_HEAD_DIM)),
        ],
        out_specs=[tok(w) if n != 3 else pl.BlockSpec((1, LANES, t), lambda bi, i: (bi, 0, i))
                   for n, w in enumerate(out_widths)],
        out_shape=[jax.ShapeDtypeStruct((b, s, w) if n != 3 else (b, LANES, s), d)
                   for n, (w, d) in enumerate(zip(out_widths, out_dtypes))],
        scratch_shapes=[pltpu.VMEM((SUBLANES, CONV_DIM), F32)],
        compiler_params=_params(2),
        name="proj",
    )(x, nmix, w1, cs, rplace, rones, convw, qg, kg, g64, mkt, mv, mqg)


def _bit_transpose32(words):
    a = list(words)
    for j, m in ((16, 0x0000FFFF), (8, 0x00FF00FF), (4, 0x0F0F0F0F), (2, 0x33333333), (1, 0x55555555)):
        for k in range(32):
            if k & j:
                continue
            tmp = (a[k] ^ lax.shift_right_logical(a[k + j], jnp.int32(j))) & m
            a[k] = a[k] ^ tmp
            a[k + j] = a[k + j] ^ lax.shift_left(tmp, jnp.int32(j))
    return a


def _low_bits(n):
    return jnp.where(n >= 32, -1, jnp.where(n <= 0, 0, lax.shift_left(jnp.int32(1), n) - 1))


def _dsa_kernel(fast_ref, q_ref, iq_ref, vw_ref, kkt_ref, v1_ref, o_ref,
                key_ref, plane_ref, cand_ref, sel_ref, m_ref, acc_ref, kbdt_ref, ikbd_ref, v1t_ref, qt_ref,
                acct_ref, selt_ref, *, topk, seq_len):
    i = pl.program_id(1)
    t = q_ref.shape[1]
    gt, _, tk = key_ref.shape
    n_groups = plane_ref.shape[0]
    cpt = tk // LANES
    n_pairs = DSA_HEADS // 2
    q0 = i * t
    nj = (q0 + t - 1) // tk + 1
    ng = (nj + gt - 1) // gt

    @pl.when(jnp.logical_and(pl.program_id(0) == 0, i == 0))
    def _():
        plane_ref[...] = jnp.zeros_like(plane_ref)

    @pl.when(i == 0)
    def _():
        ikbd_ref[...] = jnp.zeros_like(ikbd_ref)
        d = DSA_HEAD_DIM
        left = lax.broadcasted_iota(jnp.int32, (tk, LANES), 1) < d
        for j in range(ikbd_ref.shape[0]):
            kt = kkt_ref[0, :, j * tk:(j + 1) * tk]
            ikbd_ref[j, :d, :tk] = kt[d:]
            ikbd_ref[j, d:, tk:] = kt[d:]
            k_left = jnp.where(left, kt.astype(F32).T, 0.0)
            kbdt_ref[j, :tk] = k_left.astype(BF16)
            kbdt_ref[j, tk:] = pltpu.roll(k_left, d, 1).astype(BF16)
            v1t_ref[j] = v1_ref[0, j].astype(F32).T.astype(BF16)

    iw = vw_ref[0][:, DSA_HEAD_DIM:DSA_HEAD_DIM + IDX_HEADS]

    def score_group(g, carry):
        nt = jnp.minimum(nj - g * gt, gt)

        def score_tile(jj):
            sc = jnp.zeros((t, tk), F32)
            for p in range(n_pairs):
                r = _mm(iq_ref[0, :, p * LANES:(p + 1) * LANES], ikbd_ref[g * gt + jj])
                sc = sc + jnp.maximum(r[:, :tk], 0.0) * iw[:, 2 * p:2 * p + 1]
                sc = sc + jnp.maximum(r[:, tk:], 0.0) * iw[:, 2 * p + 1:2 * p + 2]
            sc = jnp.where(sc == 0.0, 0.0, sc)
            bits = pltpu.bitcast(sc, jnp.int32)
            key_ref[jj] = bits ^ ((bits >> 31) | INT_MIN)

        def score_two(k, c2):
            score_tile(2 * k)
            score_tile(2 * k + 1)
            return c2

        lax.fori_loop(0, nt // 2, score_two, 0)

        @pl.when(nt % 2 == 1)
        def _():
            score_tile(nt - 1)

        def clear_tile(jj, c2):
            key_ref[jj] = jnp.zeros((t, tk), jnp.int32)
            return c2

        lax.fori_loop(nt, gt, clear_tile, 0)

        def to_planes(rg, c2):
            rows = pl.ds(pl.multiple_of(rg * SUBLANES, SUBLANES), SUBLANES)
            words = []
            for k in range(32):
                ch = 31 - k
                words.append(key_ref[ch // cpt, rows, (ch % cpt) * LANES:(ch % cpt + 1) * LANES])
            for p, w in enumerate(_bit_transpose32(words)):
                plane_ref[g, p, rows, :] = w
            return c2

        lax.fori_loop(0, t // SUBLANES, to_planes, 0)
        return carry

    lax.fori_loop(0, ng, score_group, 0)

    lane = lax.broadcasted_iota(jnp.int32, (t, LANES), 1)
    qpos = lax.broadcasted_iota(jnp.int32, (t, LANES), 0) + q0
    chunks_valid = ((qpos - lane) >> LANE_BITS) + 1

    def count_bits(words):
        tot = lax.population_count(words[0])
        for w in words[1:]:
            tot = tot + lax.population_count(w)
        return jnp.sum(tot.astype(F32), axis=-1, keepdims=True)

    for g in range(n_groups):
        cand_ref[g] = _low_bits(chunks_valid - 32 * g)
        sel_ref[g] = jnp.zeros((t, LANES), jnp.int32)

    def radix_select(groups):
        def radix(p, need):
            ones = [cand_ref[g] & plane_ref[g, p] for g in groups]
            c1 = count_bits(ones)
            take = c1 >= need
            for g, one in zip(groups, ones):
                cand = cand_ref[g]
                cand_ref[g] = jnp.where(take, one, cand ^ one)
                sel_ref[g] = jnp.where(take, sel_ref[g], sel_ref[g] | one)
            return jnp.where(take, need, need - c1)

        def radix_block(k, need):
            for u in range(RADIX_PASSES_PER_TRIP):
                need = radix(k * RADIX_PASSES_PER_TRIP + u, need)
            return need

        return lax.fori_loop(0, 32 // RADIX_PASSES_PER_TRIP, radix_block, jnp.full((t, 1), float(topk), F32))

    if n_groups == 1:
        need = radix_select(range(1))
    else:
        need = lax.switch(ng - 1, [functools.partial(radix_select, range(n)) for n in range(1, n_groups + 1)])

    has_tie = jnp.where(count_bits([cand_ref[g] for g in range(n_groups)]) > need, 1.0, 0.0)
    any_tie = jnp.max(has_tie) > 0.0

    @pl.when(jnp.logical_not(any_tie))
    def _():
        for g in range(n_groups):
            sel_ref[g] = sel_ref[g] | cand_ref[g]

    @pl.when(any_tie)
    def _():
        n_bits = max(1, int(seq_len - 1).bit_length())

        def below(g, cut):
            return cand_ref[g] & _low_bits(((cut - lane + (LANES - 1)) >> LANE_BITS) - 32 * g)

        def bisect_idx(bit, pfx):
            cut = pfx | (jnp.int32(1) << (n_bits - 1 - bit))
            cnt = count_bits([below(g, cut) for g in range(n_groups)])
            return jnp.where(cnt < need, cut, pfx)

        cut = lax.fori_loop(0, n_bits, bisect_idx, jnp.zeros((t, 1), jnp.int32))
        cut = jnp.where(has_tie > 0.0, cut + 1, seq_len)
        for g in range(n_groups):
            sel_ref[g] = sel_ref[g] | below(g, cut)

    neg_bits = int(np.float32(NEG_BIG).view(np.int32))

    def tile_bias(j):
        words = sel_ref[j // gt]
        parts = []
        for c in range(cpt):
            k = (j % gt) * cpt + c
            picked = lax.shift_left(words, 31 - k) >> 31
            parts.append(pltpu.bitcast(neg_bits & ~picked, F32))
        return jnp.concatenate(parts, axis=-1)

    acc_ref[...] = jnp.zeros(acc_ref.shape, F32)
    use_fast = fast_ref[0] > 0

    @pl.when(use_fast)
    def _():
        for p in range(n_pairs):
            qt_ref[p] = q_ref[0, :, p * LANES:(p + 1) * LANES].astype(F32).T.astype(BF16)
        acct_ref[...] = jnp.zeros(acct_ref.shape, F32)
        for g in range(n_groups):
            selt_ref[g] = sel_ref[g].T

        def tile_bias_t(j):
            words = selt_ref[j // gt]
            parts = []
            for c in range(cpt):
                k = (j % gt) * cpt + c
                picked = lax.shift_left(words, 31 - k) >> 31
                parts.append(pltpu.bitcast(neg_bits & ~picked, F32))
            return jnp.concatenate(parts, axis=0)

        def attend(j):
            bias_t = tile_bias_t(j)
            for p in range(n_pairs):
                lg_t = _mm(kbdt_ref[j], qt_ref[p])
                for e in range(2):
                    pe_t = jnp.exp2(lg_t[e * tk:(e + 1) * tk] + bias_t).astype(BF16)
                    acct_ref[2 * p + e] += _mm(v1t_ref[j], pe_t)

        def attend_two(k, carry):
            attend(2 * k)
            attend(2 * k + 1)
            return carry

        lax.fori_loop(0, nj // 2, attend_two, 0)

        @pl.when(nj % 2 == 1)
        def _():
            attend(nj - 1)

        for h in range(DSA_HEADS):
            acc_ref[h] = acct_ref[h].T

    @pl.when(jnp.logical_not(use_fast))
    def _():
        m_ref[...] = jnp.full(m_ref.shape, NEG_BIG, F32)

        def attend(j, carry):
            bias = tile_bias(j)
            for p in range(n_pairs):
                lg = _mm_nt(q_ref[0, :, p * LANES:(p + 1) * LANES], kbdt_ref[j])
                for e in range(2):
                    h = 2 * p + e
                    s = lg[:, e * tk:(e + 1) * tk] + bias
                    m_old = m_ref[h][:, :1]
                    m_new = jnp.maximum(m_old, jnp.max(s, axis=-1, keepdims=True))
                    pe = jnp.exp2(s - m_new).astype(BF16)
                    acc_ref[h] = acc_ref[h] * jnp.exp2(m_old - m_new) + _mm(pe, v1_ref[0, j])
                    m_ref[h] = jnp.broadcast_to(m_new, (t, LANES))
            return carry

        lax.fori_loop(0, nj, attend, 0)

    first_head = lax.broadcasted_iota(jnp.int32, (t, LANES), 1) < DSA_HEAD_DIM
    for p in range(n_pairs):
        a0, a1 = acc_ref[2 * p], acc_ref[2 * p + 1]
        o0 = a0 / a0[:, DSA_HEAD_DIM:DSA_HEAD_DIM + 1]
        o1 = a1 / a1[:, DSA_HEAD_DIM:DSA_HEAD_DIM + 1]
        o_ref[0, :, p * LANES:(p + 1) * LANES] = jnp.where(
            first_head, o0, pltpu.roll(o1, DSA_HEAD_DIM, 1)).astype(BF16)


def _dsa(fast, q, iq, vw, kkt, v1, topk):
    b, s, _ = q.shape
    t = _token_tile(DSA_Q_TILE, s)
    tk = _token_tile(DSA_K_TILE, s)
    assert (32 * LANES) % tk == 0 and tk % LANES == 0
    nj = s // tk
    gt = 32 * LANES // tk
    n_groups = -(-nj // gt)
    tok = lambda width: pl.BlockSpec((1, t, width), lambda bi, i: (bi, i, 0))
    per_b = lambda shape: pl.BlockSpec((1,) + shape, lambda bi, i: (bi,) + (0,) * len(shape),
                                       pipeline_mode=pl.Buffered(1))
    return pl.pallas_call(
        functools.partial(_dsa_kernel, topk=topk, seq_len=s),
        grid=(b, s // t),
        in_specs=[pl.BlockSpec(memory_space=pltpu.SMEM),
                  tok(GRP_Q), tok(GRP_IQ), tok(LANES),
                  per_b((LANES, s)), per_b((nj, tk, LANES))],
        out_specs=tok(GRP_Q),
        out_shape=jax.ShapeDtypeStruct((b, s, GRP_Q), BF16),
        scratch_shapes=[
            pltpu.VMEM((gt, t, tk), jnp.int32),
            pltpu.VMEM((n_groups, 32, t, LANES), jnp.int32),
            pltpu.VMEM((n_groups, t, LANES), jnp.int32),
            pltpu.VMEM((n_groups, t, LANES), jnp.int32),
            pltpu.VMEM((DSA_HEADS, t, LANES), F32),
            pltpu.VMEM((DSA_HEADS, t, LANES), F32),
            pltpu.VMEM((nj, 2 * tk, LANES), BF16),
            pltpu.VMEM((nj, LANES, 2 * tk), BF16),
            pltpu.VMEM((nj, LANES, tk), BF16),
            pltpu.VMEM((DSA_HEADS // 2, LANES, t), BF16),
            pltpu.VMEM((DSA_HEADS, LANES, t), F32),
            pltpu.VMEM((n_groups, LANES, t), jnp.int32),
        ],
        compiler_params=_params(2),
        name="dsa",
    )(fast, q, iq, vw, kkt, v1.reshape(b, nj, tk, LANES))


def _split3_bf16(x):
    hi = x.astype(BF16)
    r1 = x - hi.astype(F32)
    mid = r1.astype(BF16)
    lo = (r1 - mid.astype(F32)).astype(BF16)
    return hi, mid, lo


def _hgrn_kernel(zh_ref, lb_ref, onorm_ref, o_ref, state_ref):
    i = pl.program_id(1)
    t = zh_ref.shape[1]
    c = HG_CHUNK
    width = HG_HEADS * HG_DK

    @pl.when(i == 0)
    def _():
        state_ref[...] = jnp.zeros_like(state_ref)

    rr = lax.broadcasted_iota(jnp.int32, (c, c), 0)
    cc = lax.broadcasted_iota(jnp.int32, (c, c), 1)
    tril = rr >= cc
    tril_b = jnp.where(tril, 1.0, 0.0).astype(BF16)

    n_chunks = t // c
    chunk = lambda n: slice(n * c, (n + 1) * c)
    head = lambda h: slice(h * HG_DK, (h + 1) * HG_DK)
    zq, zf, v, zg = (zh_ref[0, :, g * width:(g + 1) * width] for g in range(4))
    lb = lb_ref[...]
    log_lb, log1m_lb, one_m_lb = lb[0:1], lb[1:2], lb[2:3]

    e = jnp.exp(-jnp.abs(zf))
    b_ = log1m_lb + jnp.minimum(zf, 0.0) - jnp.log1p(e)
    log_f = jnp.maximum(log_lb, b_) + jnp.log1p(jnp.exp(-jnp.abs(log_lb - b_)))
    hk = one_m_lb * jnp.where(zf >= 0.0, e, 1.0) / (1.0 + e)
    hq = zq * jax.nn.sigmoid(zq)
    vb = v.astype(BF16)
    hi, mid, lo = _split3_bf16(log_f)
    gate = zg * jax.nn.sigmoid(zg)
    heads = range(HG_HEADS)
    states = [state_ref[h] for h in heads]
    q_intra, k_intra, q_inter, k_state, decay, att, o_intra = {}, {}, {}, {}, {}, {}, {}
    for step in range(n_chunks + 3):
        n = step
        if n < n_chunks:
            cum = _mm(tril_b, hi[chunk(n)]) + _mm(tril_b, mid[chunk(n)]) + _mm(tril_b, lo[chunk(n)])
            ref = cum[c // 2 - 1:c // 2]
            last = cum[c - 1:c]
            q_intra[n] = (hq[chunk(n)] * jnp.exp(cum - ref)).astype(BF16)
            k_intra[n] = (hk[chunk(n)] * jnp.exp(ref - cum)).astype(BF16)
            q_inter[n] = (hq[chunk(n)] * jnp.exp(cum)).astype(BF16)
            k_state[n] = (hk[chunk(n)] * jnp.exp(last - cum)).astype(BF16)
            decay[n] = jnp.exp(last)
        n = step - 1
        if 0 <= n < n_chunks:
            att[n] = [_mm_nt(q_intra[n][:, head(h)], k_intra[n][:, head(h)]) for h in heads]
        n = step - 2
        if 0 <= n < n_chunks:
            o_intra[n] = [_mm(jnp.where(tril, att[n][h], 0.0).astype(BF16), vb[chunk(n), head(h)]) for h in heads]
        n = step - 3
        if 0 <= n < n_chunks:
            o_inter = [_mm_nt(q_inter[n][:, head(h)], states[h].astype(BF16)) for h in heads]
            states = [states[h] * decay[n][:, head(h)] + _mm_tn(vb[chunk(n), head(h)], k_state[n][:, head(h)])
                      for h in heads]
            for h in heads:
                o = o_intra[n][h] + o_inter[h]
                o = o * _rms_scale(o) * onorm_ref[...]
                o_ref[0, chunk(n), head(h)] = (o * gate[chunk(n), head(h)]).astype(BF16)
    for h in heads:
        state_ref[h] = states[h]


def _hgrn(zh, lb_rows, onorm):
    b, s, _ = zh.shape
    t = _token_tile(HG_TILE, s)
    return pl.pallas_call(
        _hgrn_kernel,
        grid=(b, s // t),
        in_specs=[pl.BlockSpec((1, t, GRP_H), lambda bi, i: (bi, i, 0)),
                  pl.BlockSpec((SUBLANES, HG_HEADS * HG_DK), lambda bi, i: (0, 0)),
                  pl.BlockSpec((1, HG_DV), lambda bi, i: (0, 0))],
        out_specs=pl.BlockSpec((1, t, HG_HEADS * HG_DV), lambda bi, i: (bi, i, 0)),
        out_shape=jax.ShapeDtypeStruct((b, s, HG_HEADS * HG_DV), BF16),
        scratch_shapes=[pltpu.VMEM((HG_HEADS, HG_DV, HG_DK), F32)],
        compiler_params=_params(2),
        name="hgrn",
    )(zh, lb_rows, onorm)


def _merge_kernel(x_ref, nmix_ref, ya_ref, yb_ref, yc_ref, ym_ref, wg_ref, wl_ref, wo_ref, o_ref):
    x = x_ref[0]
    hb = (x * _rms_scale(x) * nmix_ref[...]).astype(BF16)
    merged = None
    for n, y_ref in enumerate((ya_ref, yb_ref, yc_ref, ym_ref)):
        gate = jax.nn.sigmoid(_mm(hb, wg_ref[:, n * D_MODEL:(n + 1) * D_MODEL]))
        term = gate * _mm(y_ref[0], wl_ref[n])
        merged = term if merged is None else merged + term
    o_ref[0] = x + _mm(merged.astype(BF16), wo_ref[...])


def _merge(layer, x, nmix, ya, yb, yc, ym, wg, wl, wo):
    b, s, _ = x.shape
    t = _token_tile(MERGE_TILE, s)
    tok = lambda width: pl.BlockSpec((1, t, width), lambda bi, i: (bi, i, 0))
    return pl.pallas_call(
        _merge_kernel,
        grid=(b, s // t),
        in_specs=[tok(D_MODEL), pl.BlockSpec((1, D_MODEL), lambda bi, i: (0, 0)),
                  tok(BRANCH_DIM), tok(BRANCH_DIM), tok(BRANCH_DIM), tok(BRANCH_DIM),
                  pl.BlockSpec((None, D_MODEL, N_BRANCH * D_MODEL), lambda bi, i: (layer, 0, 0),
                               pipeline_mode=pl.Buffered(1)),
                  pl.BlockSpec((None, N_BRANCH, BRANCH_DIM, D_MODEL), lambda bi, i: (layer, 0, 0, 0),
                               pipeline_mode=pl.Buffered(1)),
                  pl.BlockSpec((None, D_MODEL, D_MODEL), lambda bi, i: (layer, 0, 0),
                               pipeline_mode=pl.Buffered(1))],
        out_specs=tok(D_MODEL),
        out_shape=jax.ShapeDtypeStruct((b, s, D_MODEL), F32),
        compiler_params=_params(2),
        name="merge",
    )(x, nmix, ya, yb, yc, ym, wg, wl, wo)


def _ffn_kernel(x_ref, nffn_ref, wup_ref, wdn_ref, o_ref):
    x = x_ref[0]
    hb = (x * _rms_scale(x) * nffn_ref[...]).astype(BF16)
    out = x
    for n in range(FFN_DIM // FFN_CHUNK):
        lo = n * FFN_CHUNK
        gate = _mm(hb, wup_ref[:, lo:lo + FFN_CHUNK])
        up = _mm(hb, wup_ref[:, FFN_DIM + lo:FFN_DIM + lo + FFN_CHUNK])
        act = (gate * jax.nn.sigmoid(gate) * up).astype(BF16)
        out = out + _mm(act, wdn_ref[lo:lo + FFN_CHUNK, :])
    o_ref[0] = out


def _ffn(layer, x, nffn, wup, wdn):
    b, s, _ = x.shape
    t = _token_tile(FFN_TILE, s)
    tok = pl.BlockSpec((1, t, D_MODEL), lambda bi, i: (bi, i, 0))
    return pl.pallas_call(
        _ffn_kernel,
        grid=(b, s // t),
        in_specs=[tok, pl.BlockSpec((1, D_MODEL), lambda bi, i: (0, 0)),
                  pl.BlockSpec((None, D_MODEL, 2 * FFN_DIM), lambda bi, i: (layer, 0, 0),
                               pipeline_mode=pl.Buffered(1)),
                  pl.BlockSpec((None, FFN_DIM, D_MODEL), lambda bi, i: (layer, 0, 0),
                               pipeline_mode=pl.Buffered(1))],
        out_specs=tok,
        out_shape=jax.ShapeDtypeStruct((b, s, D_MODEL), F32),
        compiler_params=_params(2),
        name="ffn",
    )(x, nffn, wup, wdn)


def _rope_constants():
    inv_freq = 1.0 / (ROPE_THETA ** (jnp.arange(0, ROT_DIM, 2, dtype=F32) / ROT_DIM))
    invf = jnp.concatenate([inv_freq, inv_freq]).reshape(1, ROT_DIM)
    place = np.zeros((ROT_DIM, 3 * LANES), np.float32)
    ones = np.zeros((1, LANES), np.float32)
    for lane in range(LANES):
        d = lane % DSA_HEAD_DIM
        if d < ROT_HALF:
            place[d, lane] = 1.0
            place[ROT_HALF + d, LANES + lane] = -1.0
        elif d < ROT_DIM:
            place[d - ROT_HALF, lane] = 1.0
            place[d, 2 * LANES + lane] = 1.0
        else:
            ones[0, lane] = 1.0
    return invf, jnp.asarray(place, BF16), jnp.asarray(ones)


W_PREP_ROWS = 128


def _relayout_w_in_kernel(w_ref, w1_ref, wg_ref):
    offs = [0] + [int(v) for v in np.cumsum(SPLIT_SIZES)]
    w = w_ref[...]
    col = lambda n: w[:, offs[n]:offs[n + 1]]
    (a_x, a_b, a_c, d_q, d_k, d_v, i_q, i_k, i_w, g_q, g_f, g_i, g_g, m_q, gates) = (col(n) for n in range(15))
    pad = jnp.zeros((w.shape[0], LANES - DSA_HEAD_DIM - IDX_HEADS), w.dtype)
    dst = 0
    for piece in (a_x, a_b, a_c, d_q, jnp.concatenate([d_k, i_k], axis=1), jnp.concatenate([d_v, i_w, pad], axis=1),
                  i_q, g_q, g_f, g_i, g_g, m_q):
        w1_ref[:, dst:dst + piece.shape[1]] = piece.astype(BF16)
        dst += piece.shape[1]
    wg_ref[...] = gates.astype(BF16)


def _relayout_w_in(w_in):
    depth, d, cols = w_in.shape
    rows = min(W_PREP_ROWS, d)
    blk = lambda width: pl.BlockSpec((None, rows, width), lambda l, r: (l, r, 0))
    return pl.pallas_call(
        _relayout_w_in_kernel,
        grid=(depth, d // rows),
        in_specs=[blk(cols)],
        out_specs=[blk(PROJ_COLS), blk(N_BRANCH * D_MODEL)],
        out_shape=[jax.ShapeDtypeStruct((depth, d, PROJ_COLS), BF16),
                   jax.ShapeDtypeStruct((depth, d, N_BRANCH * D_MODEL), BF16)],
        compiler_params=_params(2),
        name="w_in_relayout",
    )(w_in)


def kernel(x, mem, positions, norm_mix, w_in, conv_w, dsa_q_norm, dsa_k_norm, hgrn_lower_bounds,
           hgrn_out_norm, mem_norm, mem_w_kv, mem_q_norm, mem_k_norm, w_lift, w_out, norm_ffn,
           ffn_w_up, ffn_w_down):
    b, s, d = x.shape
    depth = w_in.shape[0]
    assert d == D_MODEL and w_in.shape[2] == sum(SPLIT_SIZES)
    assert s % HG_CHUNK == 0 and s % min(DSA_K_TILE, s) == 0
    topk = min(TOPK_MAX, s // 4)

    invf, rplace, rones = _rope_constants()
    ang = positions.astype(F32)[:, None, :] * invf.reshape(ROT_DIM, 1)
    cs = jnp.where(jnp.arange(ROT_DIM)[:, None] < ROT_HALF, jnp.cos(ang), jnp.sin(ang))
    lbs = jnp.cumsum(jax.nn.softmax(hgrn_lower_bounds.astype(F32), axis=0), axis=0)
    lbs = lbs - lbs[0:1]
    g64 = jnp.asarray(np.kron(np.eye(DSA_HEADS), np.full((DSA_HEAD_DIM, DSA_HEAD_DIM), 1.0 / DSA_HEAD_DIM)), BF16)
    row = lambda v: v.reshape(1, -1).astype(F32)
    w1, wg = _relayout_w_in(w_in)
    wkv, wl, wo = mem_w_kv.astype(BF16), w_lift.astype(BF16), w_out.astype(BF16)
    wup, wdn = ffn_w_up.astype(BF16), ffn_w_down.astype(BF16)

    for l in range(depth):
        lb = lbs[l]
        lb_rows = jnp.concatenate([jnp.stack([jnp.log(lb), jnp.log1p(-lb), 1.0 - lb]),
                                   jnp.zeros((SUBLANES - 3, lb.shape[0]), F32)])
        convw = jnp.concatenate([conv_w[l], jnp.zeros((SUBLANES - CONV_WIDTH, CONV_DIM), F32)])
        mkt, mv = _mem_kv(l, mem, row(mem_norm), wkv, row(mem_k_norm[l]))
        ya, q, iq, kkt, vw, v1, zh, ym = _proj(
            l, x, row(norm_mix[l]), w1, cs, rplace, rones, convw,
            row(jnp.tile(dsa_q_norm[l], DSA_HEADS)),
            row(jnp.concatenate([dsa_k_norm[l], jnp.ones((LANES - DSA_HEAD_DIM,), F32)])),
            g64, mkt, mv, row(mem_q_norm[l]))
        logit_bound = DSA_HEAD_DIM ** 0.5 * jnp.max(jnp.abs(dsa_q_norm[l])) * jnp.max(jnp.abs(dsa_k_norm[l]))
        fast = (logit_bound <= SAFE_LOGIT_BOUND).astype(jnp.int32).reshape(1)
        yb = _dsa(fast, q, iq, vw, kkt, v1, topk)
        yc = _hgrn(zh, lb_rows, row(hgrn_out_norm[l]))
        x = _merge(l, x, row(norm_mix[l]), ya, yb, yc, ym, wg, wl, wo)
        x = _ffn(l, x, row(norm_ffn[l]), wup, wdn)
    return x
```

```python
import functools

import jax
import jax.numpy as jnp
import numpy as np
from jax import lax
from jax.experimental import pallas as pl
from jax.experimental.pallas import tpu as pltpu

D_MODEL = 1024
MEM_TOKENS = 256
N_BRANCH = 4
BRANCH_DIM = 512
CONV_DIM = 512
CONV_WIDTH = 3
DSA_HEADS = 8
DSA_HEAD_DIM = 64
IDX_HEADS = 8
IDX_DIM = 64
TOPK_MAX = 256
HG_HEADS = 4
HG_DK = 128
HG_DV = 128
HG_CHUNK = 64
MEM_HEADS = 4
MEM_HEAD_DIM = 128
ROPE_THETA = 500000.0
ROT_DIM = DSA_HEAD_DIM // 4
ROT_HALF = ROT_DIM // 2
FFN_DIM = ((8 * D_MODEL // 3 + 255) // 256) * 256
EPS = 1e-6

SPLIT_SIZES = (CONV_DIM, CONV_DIM, CONV_DIM,
               DSA_HEADS * DSA_HEAD_DIM, DSA_HEAD_DIM, DSA_HEAD_DIM,
               IDX_HEADS * IDX_DIM, IDX_DIM, IDX_HEADS,
               HG_HEADS * HG_DK, HG_HEADS * HG_DK, HG_HEADS * HG_DV, HG_HEADS * HG_DV,
               MEM_HEADS * MEM_HEAD_DIM,
               N_BRANCH * D_MODEL)

LANES = 128
LANE_BITS = 7
SUBLANES = 8
VMEM_LIMIT_BYTES = 56 * 1024 * 1024

PROJ_TILE = 512
DSA_Q_TILE = 512
DSA_K_TILE = 512
RADIX_PASSES_PER_TRIP = 8
HG_TILE = 512
MERGE_TILE = 1024
FFN_TILE = 1024
FFN_CHUNK = FFN_DIM // 2

GRP_A = 3 * CONV_DIM
GRP_Q = DSA_HEADS * DSA_HEAD_DIM
GRP_KV = 2 * LANES
GRP_IQ = IDX_HEADS * IDX_DIM
GRP_H = 4 * HG_HEADS * HG_DK
GRP_M = MEM_HEADS * MEM_HEAD_DIM
OFF_A = 0
OFF_Q = OFF_A + GRP_A
OFF_KV = OFF_Q + GRP_Q
OFF_IQ = OFF_KV + GRP_KV
OFF_H = OFF_IQ + GRP_IQ
OFF_M = OFF_H + GRP_H
PROJ_COLS = OFF_M + GRP_M

NEG_BIG = -1e30
INT_MIN = -(2 ** 31)
LOG2E = 1.4426950408889634
Q_SCALE = DSA_HEAD_DIM ** -0.5 * LOG2E
SAFE_LOGIT_BOUND = 40.0

BF16 = jnp.bfloat16
F32 = jnp.float32


def _mm(a, b):
    return jnp.dot(a, b, preferred_element_type=F32)


def _mm_nt(a, b):
    return lax.dot_general(a, b, (((1,), (1,)), ((), ())), preferred_element_type=F32)


def _mm_tn(a, b):
    return lax.dot_general(a, b, (((0,), (0,)), ((), ())), preferred_element_type=F32)


def _rms_scale(x):
    return lax.rsqrt(jnp.mean(x * x, axis=-1, keepdims=True) + EPS)


def _token_tile(preferred, s):
    t = min(preferred, s)
    assert s % t == 0, f"sequence length {s} is not a multiple of the token tile {t}"
    return t


def _params(n_grid):
    return pltpu.CompilerParams(dimension_semantics=("arbitrary",) * n_grid,
                                vmem_limit_bytes=VMEM_LIMIT_BYTES)


def _mem_kv_kernel(mem_ref, mem_norm_ref, wkv_ref, knorm_ref, mkt_ref, mv_ref):
    m = mem_ref[0]
    mn = (m * _rms_scale(m) * mem_norm_ref[...]).astype(BF16)
    kv = _mm(mn, wkv_ref[...])
    for h in range(MEM_HEADS):
        kh = kv[:, h * MEM_HEAD_DIM:(h + 1) * MEM_HEAD_DIM]
        kh = kh * _rms_scale(kh) * knorm_ref[...]
        mkt_ref[0, h] = kh.T.astype(BF16)
        off = MEM_HEADS * MEM_HEAD_DIM + h * MEM_HEAD_DIM
        mv_ref[0, h] = kv[:, off:off + MEM_HEAD_DIM].astype(BF16)


def _mem_kv(layer, mem, mem_norm, wkv, knorm):
    b = mem.shape[0]
    return pl.pallas_call(
        _mem_kv_kernel,
        grid=(b,),
        in_specs=[
            pl.BlockSpec((1, MEM_TOKENS, D_MODEL), lambda i: (i, 0, 0)),
            pl.BlockSpec((1, D_MODEL), lambda i: (0, 0)),
            pl.BlockSpec((None, D_MODEL, 2 * GRP_M), lambda i: (layer, 0, 0)),
            pl.BlockSpec((1, MEM_HEAD_DIM), lambda i: (0, 0)),
        ],
        out_specs=[
            pl.BlockSpec((1, MEM_HEADS, MEM_HEAD_DIM, MEM_TOKENS), lambda i: (i, 0, 0, 0)),
            pl.BlockSpec((1, MEM_HEADS, MEM_TOKENS, MEM_HEAD_DIM), lambda i: (i, 0, 0, 0)),
        ],
        out_shape=[
            jax.ShapeDtypeStruct((b, MEM_HEADS, MEM_HEAD_DIM, MEM_TOKENS), BF16),
            jax.ShapeDtypeStruct((b, MEM_HEADS, MEM_TOKENS, MEM_HEAD_DIM), BF16),
        ],
        compiler_params=_params(1),
        name="mem_kv",
    )(mem, mem_norm, wkv, knorm)


def _rope_slab(s, c, sa, sb):
    return s * c + pltpu.roll(s, LANES - ROT_HALF, 1) * sa + pltpu.roll(s, ROT_HALF, 1) * sb


def _proj_kernel(x_ref, nmix_ref, w_ref, cs_ref, rplace_ref, rones_ref, convw_ref, qg_ref, kg_ref,
                 g64_ref, mkt_ref, mv_ref, mqg_ref,
                 ya_ref, q_ref, iq_ref, kkt_ref, vw_ref, v1_ref, zh_ref, ym_ref,
                 carry_ref):
    t = x_ref.shape[1]

    @pl.when(pl.program_id(1) == 0)
    def _():
        carry_ref[...] = jnp.zeros_like(carry_ref)

    x = x_ref[0]
    hb = (x * _rms_scale(x) * nmix_ref[...]).astype(BF16)

    cs_hi, cs_mid, cs_lo = _split3_bf16(cs_ref[0])
    place = rplace_ref[...]
    tabs = _mm_tn(cs_hi, place) + _mm_tn(cs_mid, place) + _mm_tn(cs_lo, place)
    rc = tabs[:, :LANES] + rones_ref[...]
    rsa = tabs[:, LANES:2 * LANES]
    rsb = tabs[:, 2 * LANES:]

    za = _mm(hb, w_ref[:, OFF_A:OFF_A + GRP_A])
    a_x, a_b, a_c = za[:, :CONV_DIM], za[:, CONV_DIM:2 * CONV_DIM], za[:, 2 * CONV_DIM:]
    u = a_c * a_x
    carry = carry_ref[...]
    row8 = lax.broadcasted_iota(jnp.int32, (SUBLANES, CONV_DIM), 0)
    r1 = pltpu.roll(u, 1, 0)
    r2 = pltpu.roll(u, 2, 0)
    top1 = jnp.where(row8 < 1, pltpu.roll(carry, 1, 0), r1[:SUBLANES])
    top2 = jnp.where(row8 < 2, pltpu.roll(carry, 2, 0), r2[:SUBLANES])
    u1 = jnp.concatenate([top1, r1[SUBLANES:]], axis=0)
    u2 = jnp.concatenate([top2, r2[SUBLANES:]], axis=0)
    carry_ref[...] = u[t - SUBLANES:]
    cw = convw_ref[...]
    ya_ref[0] = (a_b * (u2 * cw[0:1] + u1 * cw[1:2] + u * cw[2:3])).astype(BF16)

    zq = _mm(hb, w_ref[:, OFF_Q:OFF_Q + GRP_Q])
    msq = _mm((zq * zq).astype(BF16), g64_ref[...])
    qn = zq * lax.rsqrt(msq + EPS) * qg_ref[...]
    for p in range(GRP_Q // LANES):
        sl = slice(p * LANES, (p + 1) * LANES)
        q_ref[0, :, sl] = (_rope_slab(qn[:, sl], rc, rsa, rsb) * Q_SCALE).astype(BF16)

    zkv = _mm(hb, w_ref[:, OFF_KV:OFF_KV + GRP_KV])
    s0, s1 = zkv[:, :LANES], zkv[:, LANES:]
    lane = lax.broadcasted_iota(jnp.int32, (t, LANES), 1)
    is_k = lane < DSA_HEAD_DIM
    kms = jnp.sum(jnp.where(is_k, s0 * s0, 0.0), axis=-1, keepdims=True) * (1.0 / DSA_HEAD_DIM)
    s0 = s0 * jnp.where(is_k, lax.rsqrt(kms + EPS) * kg_ref[...], 1.0)
    kkt_ref[0] = _rope_slab(s0, rc, rsa, rsb).T.astype(BF16)
    w_scale = jnp.where(lane >= DSA_HEAD_DIM, jnp.where(lane < DSA_HEAD_DIM + IDX_HEADS, IDX_HEADS ** -0.5, 1.0), 1.0)
    vw_ref[0] = s1 * w_scale
    v1_ref[0] = jnp.where(is_k, s1, jnp.where(lane == DSA_HEAD_DIM, 1.0, 0.0)).astype(BF16)

    ziq = _mm(hb, w_ref[:, OFF_IQ:OFF_IQ + GRP_IQ])
    for p in range(GRP_IQ // LANES):
        sl = slice(p * LANES, (p + 1) * LANES)
        iq_ref[0, :, sl] = (_rope_slab(ziq[:, sl], rc, rsa, rsb) * (IDX_DIM ** -0.5)).astype(BF16)

    zm = _mm(hb, w_ref[:, OFF_M:OFF_M + GRP_M])
    for h in range(MEM_HEADS):
        sl = slice(h * MEM_HEAD_DIM, (h + 1) * MEM_HEAD_DIM)
        mq = zm[:, sl]
        mq = mq * _rms_scale(mq) * mqg_ref[...] * (MEM_HEAD_DIM ** -0.5)
        lg = _mm(mq.astype(BF16), mkt_ref[0, h])
        pe = jnp.exp(lg - jnp.max(lg, axis=-1, keepdims=True))
        den = jnp.sum(pe, axis=-1, keepdims=True)
        ym_ref[0, :, sl] = (_mm(pe.astype(BF16), mv_ref[0, h]) / den).astype(BF16)

    zh_ref[0] = _mm(hb, w_ref[:, OFF_H:OFF_H + GRP_H])


def _proj(layer, x, nmix, w1, cs, rplace, rones, convw, qg, kg, g64, mkt, mv, mqg):
    b, s, _ = x.shape
    t = _token_tile(PROJ_TILE, s)
    tok = lambda width: pl.BlockSpec((1, t, width), lambda bi, i: (bi, i, 0))
    const2 = lambda shape: pl.BlockSpec(shape, lambda bi, i: (0, 0))
    per_b4 = lambda shape: pl.BlockSpec((1,) + shape, lambda bi, i: (bi, 0, 0, 0))
    out_widths = (CONV_DIM, GRP_Q, GRP_IQ, LANES, LANES, LANES, GRP_H, GRP_M)
    out_dtypes = (BF16, BF16, BF16, BF16, F32, BF16, F32, BF16)
    return pl.pallas_call(
        _proj_kernel,
        grid=(b, s // t),
        in_specs=[
            tok(D_MODEL), const2((1, D_MODEL)),
            pl.BlockSpec((None, D_MODEL, PROJ_COLS), lambda bi, i: (layer, 0, 0)),
            pl.BlockSpec((1, ROT_DIM, t), lambda bi, i: (bi, 0, i)),
            const2((ROT_DIM, 3 * LANES)), const2((1, LANES)),
            const2((SUBLANES, CONV_DIM)), const2((1, GRP_Q)), const2((1, LANES)),
            const2((GRP_Q, GRP_Q)),
            per_b4((MEM_HEADS, MEM_HEAD_DIM, MEM_TOKENS)), per_b4((MEM_HEADS, MEM_TOKENS, MEM_HEAD_DIM)),
            const2((1, MEM_HEAD_DIM)),
        ],
        out_specs=[tok(w) if n != 3 else pl.BlockSpec((1, LANES, t), lambda bi, i: (bi, 0, i))
                   for n, w in enumerate(out_widths)],
        out_shape=[jax.ShapeDtypeStruct((b, s, w) if n != 3 else (b, LANES, s), d)
                   for n, (w, d) in enumerate(zip(out_widths, out_dtypes))],
        scratch_shapes=[pltpu.VMEM((SUBLANES, CONV_DIM), F32)],
        compiler_params=_params(2),
        name="proj",
    )(x, nmix, w1, cs, rplace, rones, convw, qg, kg, g64, mkt, mv, mqg)


def _bit_transpose32(words):
    a = list(words)
    for j, m in ((16, 0x0000FFFF), (8, 0x00FF00FF), (4, 0x0F0F0F0F), (2, 0x33333333), (1, 0x55555555)):
        for k in range(32):
            if k & j:
                continue
            tmp = (a[k] ^ lax.shift_right_logical(a[k + j], jnp.int32(j))) & m
            a[k] = a[k] ^ tmp
            a[k + j] = a[k + j] ^ lax.shift_left(tmp, jnp.int32(j))
    return a


def _low_bits(n):
    return jnp.where(n >= 32, -1, jnp.where(n <= 0, 0, lax.shift_left(jnp.int32(1), n) - 1))


def _dsa_kernel(fast_ref, q_ref, iq_ref, vw_ref, kkt_ref, v1_ref, o_ref,
                key_ref, plane_ref, cand_ref, sel_ref, m_ref, acc_ref, kbdt_ref, ikbd_ref, v1t_ref, qt_ref,
                acct_ref, selt_ref, *, topk, seq_len):
    i = pl.program_id(1)
    t = q_ref.shape[1]
    gt, _, tk = key_ref.shape
    n_groups = plane_ref.shape[0]
    cpt = tk // LANES
    n_pairs = DSA_HEADS // 2
    q0 = i * t
    nj = (q0 + t - 1) // tk + 1
    ng = (nj + gt - 1) // gt

    @pl.when(jnp.logical_and(pl.program_id(0) == 0, i == 0))
    def _():
        plane_ref[...] = jnp.zeros_like(plane_ref)

    @pl.when(i == 0)
    def _():
        ikbd_ref[...] = jnp.zeros_like(ikbd_ref)
        d = DSA_HEAD_DIM
        left = lax.broadcasted_iota(jnp.int32, (tk, LANES), 1) < d
        for j in range(ikbd_ref.shape[0]):
            kt = kkt_ref[0, :, j * tk:(j + 1) * tk]
            ikbd_ref[j, :d, :tk] = kt[d:]
            ikbd_ref[j, d:, tk:] = kt[d:]
            k_left = jnp.where(left, kt.astype(F32).T, 0.0)
            kbdt_ref[j, :tk] = k_left.astype(BF16)
            kbdt_ref[j, tk:] = pltpu.roll(k_left, d, 1).astype(BF16)
            v1t_ref[j] = v1_ref[0, j].astype(F32).T.astype(BF16)

    iw = vw_ref[0][:, DSA_HEAD_DIM:DSA_HEAD_DIM + IDX_HEADS]

    def score_group(g, carry):
        nt = jnp.minimum(nj - g * gt, gt)

        def score_tile(jj):
            sc = jnp.zeros((t, tk), F32)
            for p in range(n_pairs):
                r = _mm(iq_ref[0, :, p * LANES:(p + 1) * LANES], ikbd_ref[g * gt + jj])
                sc = sc + jnp.maximum(r[:, :tk], 0.0) * iw[:, 2 * p:2 * p + 1]
                sc = sc + jnp.maximum(r[:, tk:], 0.0) * iw[:, 2 * p + 1:2 * p + 2]
            sc = jnp.where(sc == 0.0, 0.0, sc)
            bits = pltpu.bitcast(sc, jnp.int32)
            key_ref[jj] = bits ^ ((bits >> 31) | INT_MIN)

        def score_two(k, c2):
            score_tile(2 * k)
            score_tile(2 * k + 1)
            return c2

        lax.fori_loop(0, nt // 2, score_two, 0)

        @pl.when(nt % 2 == 1)
        def _():
            score_tile(nt - 1)

        def clear_tile(jj, c2):
            key_ref[jj] = jnp.zeros((t, tk), jnp.int32)
            return c2

        lax.fori_loop(nt, gt, clear_tile, 0)

        def to_planes(rg, c2):
            rows = pl.ds(pl.multiple_of(rg * SUBLANES, SUBLANES), SUBLANES)
            words = []
            for k in range(32):
                ch = 31 - k
                words.append(key_ref[ch // cpt, rows, (ch % cpt) * LANES:(ch % cpt + 1) * LANES])
            for p, w in enumerate(_bit_transpose32(words)):
                plane_ref[g, p, rows, :] = w
            return c2

        lax.fori_loop(0, t // SUBLANES, to_planes, 0)
        return carry

    lax.fori_loop(0, ng, score_group, 0)

    lane = lax.broadcasted_iota(jnp.int32, (t, LANES), 1)
    qpos = lax.broadcasted_iota(jnp.int32, (t, LANES), 0) + q0
    chunks_valid = ((qpos - lane) >> LANE_BITS) + 1

    def count_bits(words):
        tot = lax.population_count(words[0])
        for w in words[1:]:
            tot = tot + lax.population_count(w)
        return jnp.sum(tot.astype(F32), axis=-1, keepdims=True)

    for g in range(n_groups):
        cand_ref[g] = _low_bits(chunks_valid - 32 * g)
        sel_ref[g] = jnp.zeros((t, LANES), jnp.int32)

    def radix_select(groups):
        def radix(p, need):
            ones = [cand_ref[g] & plane_ref[g, p] for g in groups]
            c1 = count_bits(ones)
            take = c1 >= need
            for g, one in zip(groups, ones):
                cand = cand_ref[g]
                cand_ref[g] = jnp.where(take, one, cand ^ one)
                sel_ref[g] = jnp.where(take, sel_ref[g], sel_ref[g] | one)
            return jnp.where(take, need, need - c1)

        def radix_block(k, need):
            for u in range(RADIX_PASSES_PER_TRIP):
                need = radix(k * RADIX_PASSES_PER_TRIP + u, need)
            return need

        return lax.fori_loop(0, 32 // RADIX_PASSES_PER_TRIP, radix_block, jnp.full((t, 1), float(topk), F32))

    if n_groups == 1:
        need = radix_select(range(1))
    else:
        need = lax.switch(ng - 1, [functools.partial(radix_select, range(n)) for n in range(1, n_groups + 1)])

    has_tie = jnp.where(count_bits([cand_ref[g] for g in range(n_groups)]) > need, 1.0, 0.0)
    any_tie = jnp.max(has_tie) > 0.0

    @pl.when(jnp.logical_not(any_tie))
    def _():
        for g in range(n_groups):
            sel_ref[g] = sel_ref[g] | cand_ref[g]

    @pl.when(any_tie)
    def _():
        n_bits = max(1, int(seq_len - 1).bit_length())

        def below(g, cut):
            return cand_ref[g] & _low_bits(((cut - lane + (LANES - 1)) >> LANE_BITS) - 32 * g)

        def bisect_idx(bit, pfx):
            cut = pfx | (jnp.int32(1) << (n_bits - 1 - bit))
            cnt = count_bits([below(g, cut) for g in range(n_groups)])
            return jnp.where(cnt < need, cut, pfx)

        cut = lax.fori_loop(0, n_bits, bisect_idx, jnp.zeros((t, 1), jnp.int32))
        cut = jnp.where(has_tie > 0.0, cut + 1, seq_len)
        for g in range(n_groups):
            sel_ref[g] = sel_ref[g] | below(g, cut)

    neg_bits = int(np.float32(NEG_BIG).view(np.int32))

    def tile_bias(j):
        words = sel_ref[j // gt]
        parts = []
        for c in range(cpt):
            k = (j % gt) * cpt + c
            picked = lax.shift_left(words, 31 - k) >> 31
            parts.append(pltpu.bitcast(neg_bits & ~picked, F32))
        return jnp.concatenate(parts, axis=-1)

    use_fast = fast_ref[0] > 0

    @pl.when(use_fast)
    def _():
        for p in range(n_pairs):
            qt_ref[p] = q_ref[0, :, p * LANES:(p + 1) * LANES].astype(F32).T.astype(BF16)
        acct_ref[...] = jnp.zeros(acct_ref.shape, F32)
        for g in range(n_groups):
            selt_ref[g] = sel_ref[g].T

        def tile_bias_t(j):
            words = selt_ref[j // gt]
            parts = []
            for c in range(cpt):
                k = (j % gt) * cpt + c
                picked = lax.shift_left(words, 31 - k) >> 31
                parts.append(pltpu.bitcast(neg_bits & ~picked, F32))
            return jnp.concatenate(parts, axis=0)

        def attend(j):
            bias_t = tile_bias_t(j)
            for p in range(n_pairs):
                lg_t = _mm(kbdt_ref[j], qt_ref[p])
                for e in range(2):
                    pe_t = jnp.exp2(lg_t[e * tk:(e + 1) * tk] + bias_t).astype(BF16)
                    acct_ref[2 * p + e] += _mm(v1t_ref[j], pe_t)

        def attend_two(k, carry):
            attend(2 * k)
            attend(2 * k + 1)
            return carry

        lax.fori_loop(0, nj // 2, attend_two, 0)

        @pl.when(nj % 2 == 1)
        def _():
            attend(nj - 1)

        d = DSA_HEAD_DIM
        for p in range(n_pairs):
            a0, a1 = acct_ref[2 * p], acct_ref[2 * p + 1]
            pair_t = jnp.concatenate([a0[:d] / a0[d:d + 1], a1[:d] / a1[d:d + 1]], axis=0)
            o_ref[0, :, p * LANES:(p + 1) * LANES] = pair_t.T.astype(BF16)

    @pl.when(jnp.logical_not(use_fast))
    def _():
        m_ref[...] = jnp.full(m_ref.shape, NEG_BIG, F32)
        acc_ref[...] = jnp.zeros(acc_ref.shape, F32)

        def attend(j, carry):
            bias = tile_bias(j)
            for p in range(n_pairs):
                lg = _mm_nt(q_ref[0, :, p * LANES:(p + 1) * LANES], kbdt_ref[j])
                for e in range(2):
                    h = 2 * p + e
                    s = lg[:, e * tk:(e + 1) * tk] + bias
                    m_old = m_ref[h][:, :1]
                    m_new = jnp.maximum(m_old, jnp.max(s, axis=-1, keepdims=True))
                    pe = jnp.exp2(s - m_new).astype(BF16)
                    acc_ref[h] = acc_ref[h] * jnp.exp2(m_old - m_new) + _mm(pe, v1_ref[0, j])
                    m_ref[h] = jnp.broadcast_to(m_new, (t, LANES))
            return carry

        lax.fori_loop(0, nj, attend, 0)

        first_head = lax.broadcasted_iota(jnp.int32, (t, LANES), 1) < DSA_HEAD_DIM
        for p in range(n_pairs):
            a0, a1 = acc_ref[2 * p], acc_ref[2 * p + 1]
            o0 = a0 / a0[:, DSA_HEAD_DIM:DSA_HEAD_DIM + 1]
            o1 = a1 / a1[:, DSA_HEAD_DIM:DSA_HEAD_DIM + 1]
            o_ref[0, :, p * LANES:(p + 1) * LANES] = jnp.where(
                first_head, o0, pltpu.roll(o1, DSA_HEAD_DIM, 1)).astype(BF16)


def _dsa(fast, q, iq, vw, kkt, v1, topk):
    b, s, _ = q.shape
    t = _token_tile(DSA_Q_TILE, s)
    tk = _token_tile(DSA_K_TILE, s)
    assert (32 * LANES) % tk == 0 and tk % LANES == 0
    nj = s // tk
    gt = 32 * LANES // tk
    n_groups = -(-nj // gt)
    tok = lambda width: pl.BlockSpec((1, t, width), lambda bi, i: (bi, i, 0))
    per_b = lambda shape: pl.BlockSpec((1,) + shape, lambda bi, i: (bi,) + (0,) * len(shape),
                                       pipeline_mode=pl.Buffered(1))
    return pl.pallas_call(
        functools.partial(_dsa_kernel, topk=topk, seq_len=s),
        grid=(b, s // t),
        in_specs=[pl.BlockSpec(memory_space=pltpu.SMEM),
                  tok(GRP_Q), tok(GRP_IQ), tok(LANES),
                  per_b((LANES, s)), per_b((nj, tk, LANES))],
        out_specs=tok(GRP_Q),
        out_shape=jax.ShapeDtypeStruct((b, s, GRP_Q), BF16),
        scratch_shapes=[
            pltpu.VMEM((gt, t, tk), jnp.int32),
            pltpu.VMEM((n_groups, 32, t, LANES), jnp.int32),
            pltpu.VMEM((n_groups, t, LANES), jnp.int32),
            pltpu.VMEM((n_groups, t, LANES), jnp.int32),
            pltpu.VMEM((DSA_HEADS, t, LANES), F32),
            pltpu.VMEM((DSA_HEADS, t, LANES), F32),
            pltpu.VMEM((nj, 2 * tk, LANES), BF16),
            pltpu.VMEM((nj, LANES, 2 * tk), BF16),
            pltpu.VMEM((nj, LANES, tk), BF16),
            pltpu.VMEM((DSA_HEADS // 2, LANES, t), BF16),
            pltpu.VMEM((DSA_HEADS, LANES, t), F32),
            pltpu.VMEM((n_groups, LANES, t), jnp.int32),
        ],
        compiler_params=_params(2),
        name="dsa",
    )(fast, q, iq, vw, kkt, v1.reshape(b, nj, tk, LANES))


def _split3_bf16(x):
    hi = x.astype(BF16)
    r1 = x - hi.astype(F32)
    mid = r1.astype(BF16)
    lo = (r1 - mid.astype(F32)).astype(BF16)
    return hi, mid, lo


def _hgrn_kernel(zh_ref, lb_ref, onorm_ref, o_ref, state_ref):
    i = pl.program_id(1)
    t = zh_ref.shape[1]
    c = HG_CHUNK
    width = HG_HEADS * HG_DK

    @pl.when(i == 0)
    def _():
        state_ref[...] = jnp.zeros_like(state_ref)

    rr = lax.broadcasted_iota(jnp.int32, (c, c), 0)
    cc = lax.broadcasted_iota(jnp.int32, (c, c), 1)
    tril = rr >= cc
    tril_b = jnp.where(tril, 1.0, 0.0).astype(BF16)

    n_chunks = t // c
    chunk = lambda n: slice(n * c, (n + 1) * c)
    head = lambda h: slice(h * HG_DK, (h + 1) * HG_DK)
    zq, zf, v, zg = (zh_ref[0, :, g * width:(g + 1) * width] for g in range(4))
    lb = lb_ref[...]
    log_lb, log1m_lb, one_m_lb = lb[0:1], lb[1:2], lb[2:3]

    e = jnp.exp(-jnp.abs(zf))
    b_ = log1m_lb + jnp.minimum(zf, 0.0) - jnp.log1p(e)
    log_f = jnp.maximum(log_lb, b_) + jnp.log1p(jnp.exp(-jnp.abs(log_lb - b_)))
    hk = one_m_lb * jnp.where(zf >= 0.0, e, 1.0) / (1.0 + e)
    hq = zq * jax.nn.sigmoid(zq)
    vb = v.astype(BF16)
    hi, mid, lo = _split3_bf16(log_f)
    gate = zg * jax.nn.sigmoid(zg)
    heads = range(HG_HEADS)
    states = [state_ref[h] for h in heads]
    q_intra, k_intra, q_inter, k_state, decay, att, o_intra = {}, {}, {}, {}, {}, {}, {}
    for step in range(n_chunks + 3):
        n = step
        if n < n_chunks:
            cum = _mm(tril_b, hi[chunk(n)]) + _mm(tril_b, mid[chunk(n)]) + _mm(tril_b, lo[chunk(n)])
            ref = cum[c // 2 - 1:c // 2]
            last = cum[c - 1:c]
            q_intra[n] = (hq[chunk(n)] * jnp.exp(cum - ref)).astype(BF16)
            k_intra[n] = (hk[chunk(n)] * jnp.exp(ref - cum)).astype(BF16)
            q_inter[n] = (hq[chunk(n)] * jnp.exp(cum)).astype(BF16)
            k_state[n] = (hk[chunk(n)] * jnp.exp(last - cum)).astype(BF16)
            decay[n] = jnp.exp(last)
        n = step - 1
        if 0 <= n < n_chunks:
            att[n] = [_mm_nt(q_intra[n][:, head(h)], k_intra[n][:, head(h)]) for h in heads]
        n = step - 2
        if 0 <= n < n_chunks:
            o_intra[n] = [_mm(jnp.where(tril, att[n][h], 0.0).astype(BF16), vb[chunk(n), head(h)]) for h in heads]
        n = step - 3
        if 0 <= n < n_chunks:
            o_inter = [_mm_nt(q_inter[n][:, head(h)], states[h].astype(BF16)) for h in heads]
            states = [states[h] * decay[n][:, head(h)] + _mm_tn(vb[chunk(n), head(h)], k_state[n][:, head(h)])
                      for h in heads]
            for h in heads:
                o = o_intra[n][h] + o_inter[h]
                o = o * _rms_scale(o) * onorm_ref[...]
                o_ref[0, chunk(n), head(h)] = (o * gate[chunk(n), head(h)]).astype(BF16)
    for h in heads:
        state_ref[h] = states[h]


def _hgrn(zh, lb_rows, onorm):
    b, s, _ = zh.shape
    t = _token_tile(HG_TILE, s)
    return pl.pallas_call(
        _hgrn_kernel,
        grid=(b, s // t),
        in_specs=[pl.BlockSpec((1, t, GRP_H), lambda bi, i: (bi, i, 0)),
                  pl.BlockSpec((SUBLANES, HG_HEADS * HG_DK), lambda bi, i: (0, 0)),
                  pl.BlockSpec((1, HG_DV), lambda bi, i: (0, 0))],
        out_specs=pl.BlockSpec((1, t, HG_HEADS * HG_DV), lambda bi, i: (bi, i, 0)),
        out_shape=jax.ShapeDtypeStruct((b, s, HG_HEADS * HG_DV), BF16),
        scratch_shapes=[pltpu.VMEM((HG_HEADS, HG_DV, HG_DK), F32)],
        compiler_params=_params(2),
        name="hgrn",
    )(zh, lb_rows, onorm)


def _merge_kernel(x_ref, nmix_ref, ya_ref, yb_ref, yc_ref, ym_ref, wg_ref, wl_ref, wo_ref, o_ref):
    x = x_ref[0]
    hb = (x * _rms_scale(x) * nmix_ref[...]).astype(BF16)
    merged = None
    for n, y_ref in enumerate((ya_ref, yb_ref, yc_ref, ym_ref)):
        gate = jax.nn.sigmoid(_mm(hb, wg_ref[:, n * D_MODEL:(n + 1) * D_MODEL]))
        term = gate * _mm(y_ref[0], wl_ref[n])
        merged = term if merged is None else merged + term
    o_ref[0] = x + _mm(merged.astype(BF16), wo_ref[...])


def _merge(layer, x, nmix, ya, yb, yc, ym, wg, wl, wo):
    b, s, _ = x.shape
    t = _token_tile(MERGE_TILE, s)
    tok = lambda width: pl.BlockSpec((1, t, width), lambda bi, i: (bi, i, 0))
    return pl.pallas_call(
        _merge_kernel,
        grid=(b, s // t),
        in_specs=[tok(D_MODEL), pl.BlockSpec((1, D_MODEL), lambda bi, i: (0, 0)),
                  tok(BRANCH_DIM), tok(BRANCH_DIM), tok(BRANCH_DIM), tok(BRANCH_DIM),
                  pl.BlockSpec((None, D_MODEL, N_BRANCH * D_MODEL), lambda bi, i: (layer, 0, 0),
                               pipeline_mode=pl.Buffered(1)),
                  pl.BlockSpec((None, N_BRANCH, BRANCH_DIM, D_MODEL), lambda bi, i: (layer, 0, 0, 0),
                               pipeline_mode=pl.Buffered(1)),
                  pl.BlockSpec((None, D_MODEL, D_MODEL), lambda bi, i: (layer, 0, 0),
                               pipeline_mode=pl.Buffered(1))],
        out_specs=tok(D_MODEL),
        out_shape=jax.ShapeDtypeStruct((b, s, D_MODEL), F32),
        compiler_params=_params(2),
        name="merge",
    )(x, nmix, ya, yb, yc, ym, wg, wl, wo)


def _ffn_kernel(x_ref, nffn_ref, wup_ref, wdn_ref, o_ref):
    x = x_ref[0]
    hb = (x * _rms_scale(x) * nffn_ref[...]).astype(BF16)
    out = x
    for n in range(FFN_DIM // FFN_CHUNK):
        lo = n * FFN_CHUNK
        gate = _mm(hb, wup_ref[:, lo:lo + FFN_CHUNK])
        up = _mm(hb, wup_ref[:, FFN_DIM + lo:FFN_DIM + lo + FFN_CHUNK])
        act = (gate * jax.nn.sigmoid(gate) * up).astype(BF16)
        out = out + _mm(act, wdn_ref[lo:lo + FFN_CHUNK, :])
    o_ref[0] = out


def _ffn(layer, x, nffn, wup, wdn):
    b, s, _ = x.shape
    t = _token_tile(FFN_TILE, s)
    tok = pl.BlockSpec((1, t, D_MODEL), lambda bi, i: (bi, i, 0))
    return pl.pallas_call(
        _ffn_kernel,
        grid=(b, s // t),
        in_specs=[tok, pl.BlockSpec((1, D_MODEL), lambda bi, i: (0, 0)),
                  pl.BlockSpec((None, D_MODEL, 2 * FFN_DIM), lambda bi, i: (layer, 0, 0),
                               pipeline_mode=pl.Buffered(1)),
                  pl.BlockSpec((None, FFN_DIM, D_MODEL), lambda bi, i: (layer, 0, 0),
                               pipeline_mode=pl.Buffered(1))],
        out_specs=tok,
        out_shape=jax.ShapeDtypeStruct((b, s, D_MODEL), F32),
        compiler_params=_params(2),
        name="ffn",
    )(x, nffn, wup, wdn)


def _rope_constants():
    inv_freq = 1.0 / (ROPE_THETA ** (jnp.arange(0, ROT_DIM, 2, dtype=F32) / ROT_DIM))
    invf = jnp.concatenate([inv_freq, inv_freq]).reshape(1, ROT_DIM)
    place = np.zeros((ROT_DIM, 3 * LANES), np.float32)
    ones = np.zeros((1, LANES), np.float32)
    for lane in range(LANES):
        d = lane % DSA_HEAD_DIM
        if d < ROT_HALF:
            place[d, lane] = 1.0
            place[ROT_HALF + d, LANES + lane] = -1.0
        elif d < ROT_DIM:
            place[d - ROT_HALF, lane] = 1.0
            place[d, 2 * LANES + lane] = 1.0
        else:
            ones[0, lane] = 1.0
    return invf, jnp.asarray(place, BF16), jnp.asarray(ones)


W_PREP_ROWS = 128


def _relayout_w_in_kernel(w_ref, w1_ref, wg_ref):
    offs = [0] + [int(v) for v in np.cumsum(SPLIT_SIZES)]
    w = w_ref[...]
    col = lambda n: w[:, offs[n]:offs[n + 1]]
    (a_x, a_b, a_c, d_q, d_k, d_v, i_q, i_k, i_w, g_q, g_f, g_i, g_g, m_q, gates) = (col(n) for n in range(15))
    pad = jnp.zeros((w.shape[0], LANES - DSA_HEAD_DIM - IDX_HEADS), w.dtype)
    dst = 0
    for piece in (a_x, a_b, a_c, d_q, jnp.concatenate([d_k, i_k], axis=1), jnp.concatenate([d_v, i_w, pad], axis=1),
                  i_q, g_q, g_f, g_i, g_g, m_q):
        w1_ref[:, dst:dst + piece.shape[1]] = piece.astype(BF16)
        dst += piece.shape[1]
    wg_ref[...] = gates.astype(BF16)


def _relayout_w_in(w_in):
    depth, d, cols = w_in.shape
    rows = min(W_PREP_ROWS, d)
    blk = lambda width: pl.BlockSpec((None, rows, width), lambda l, r: (l, r, 0))
    return pl.pallas_call(
        _relayout_w_in_kernel,
        grid=(depth, d // rows),
        in_specs=[blk(cols)],
        out_specs=[blk(PROJ_COLS), blk(N_BRANCH * D_MODEL)],
        out_shape=[jax.ShapeDtypeStruct((depth, d, PROJ_COLS), BF16),
                   jax.ShapeDtypeStruct((depth, d, N_BRANCH * D_MODEL), BF16)],
        compiler_params=_params(2),
        name="w_in_relayout",
    )(w_in)


def kernel(x, mem, positions, norm_mix, w_in, conv_w, dsa_q_norm, dsa_k_norm, hgrn_lower_bounds,
           hgrn_out_norm, mem_norm, mem_w_kv, mem_q_norm, mem_k_norm, w_lift, w_out, norm_ffn,
           ffn_w_up, ffn_w_down):
    b, s, d = x.shape
    depth = w_in.shape[0]
    assert d == D_MODEL and w_in.shape[2] == sum(SPLIT_SIZES)
    assert s % HG_CHUNK == 0 and s % min(DSA_K_TILE, s) == 0
    topk = min(TOPK_MAX, s // 4)

    invf, rplace, rones = _rope_constants()
    ang = positions.astype(F32)[:, None, :] * invf.reshape(ROT_DIM, 1)
    cs = jnp.where(jnp.arange(ROT_DIM)[:, None] < ROT_HALF, jnp.cos(ang), jnp.sin(ang))
    lbs = jnp.cumsum(jax.nn.softmax(hgrn_lower_bounds.astype(F32), axis=0), axis=0)
    lbs = lbs - lbs[0:1]
    g64 = jnp.asarray(np.kron(np.eye(DSA_HEADS), np.full((DSA_HEAD_DIM, DSA_HEAD_DIM), 1.0 / DSA_HEAD_DIM)), BF16)
    row = lambda v: v.reshape(1, -1).astype(F32)
    w1, wg = _relayout_w_in(w_in)
    wkv, wl, wo = mem_w_kv.astype(BF16), w_lift.astype(BF16), w_out.astype(BF16)
    wup, wdn = ffn_w_up.astype(BF16), ffn_w_down.astype(BF16)

    for l in range(depth):
        lb = lbs[l]
        lb_rows = jnp.concatenate([jnp.stack([jnp.log(lb), jnp.log1p(-lb), 1.0 - lb]),
                                   jnp.zeros((SUBLANES - 3, lb.shape[0]), F32)])
        convw = jnp.concatenate([conv_w[l], jnp.zeros((SUBLANES - CONV_WIDTH, CONV_DIM), F32)])
        mkt, mv = _mem_kv(l, mem, row(mem_norm), wkv, row(mem_k_norm[l]))
        ya, q, iq, kkt, vw, v1, zh, ym = _proj(
            l, x, row(norm_mix[l]), w1, cs, rplace, rones, convw,
            row(jnp.tile(dsa_q_norm[l], DSA_HEADS)),
            row(jnp.concatenate([dsa_k_norm[l], jnp.ones((LANES - DSA_HEAD_DIM,), F32)])),
            g64, mkt, mv, row(mem_q_norm[l]))
        logit_bound = DSA_HEAD_DIM ** 0.5 * jnp.max(jnp.abs(dsa_q_norm[l])) * jnp.max(jnp.abs(dsa_k_norm[l]))
        fast = (logit_bound <= SAFE_LOGIT_BOUND).astype(jnp.int32).reshape(1)
        yb = _dsa(fast, q, iq, vw, kkt, v1, topk)
        yc = _hgrn(zh, lb_rows, row(hgrn_out_norm[l]))
        x = _merge(l, x, row(norm_mix[l]), ya, yb, yc, ym, wg, wl, wo)
        x = _ffn(l, x, row(norm_ffn[l]), wup, wdn)
    return x
```
